```python
import jax, jax.numpy as jnp
from jax import lax
import numpy as np

D_MODEL = 1024
BATCH = 8
SEQ = 2048
DEPTH = 2
DEC_BATCH = 128
DEC_SEQ = 4
PAST_LEN = 16384
PAGE_SIZE = 128

BRANCH_WIDTH = D_MODEL // 2
N_BRANCHES = 3
CONV_A_WIDTH = 31
CONV_A_GROUPS = 8
SGU_CHUNK = 128
SGU_HEADS = 8
CONV_C_WIDTH = 3
N_GROUPS = 4
EXPERTS_PER_GROUP = 8
N_EXPERTS = N_GROUPS * EXPERTS_PER_GROUP
TOP_K = 2
D_FF_EXPERT = D_MODEL // 2
MOE_BLOCK = 128
IN_COLS = 7 * BRANCH_WIDTH + N_BRANCHES * D_MODEL
EPS = 1e-6

kernel_name = "hybrid_conv_sgu_shortconv_hmoe_step"


def rms_norm(x, g):
    xf = x.astype(jnp.float32)
    y = xf * lax.rsqrt(jnp.mean(xf * xf, axis=-1, keepdims=True) + EPS)
    return (y * g.astype(jnp.float32)).astype(x.dtype)


def layer_norm(x, g, b, n_groups=1):
    shp = x.shape
    xf = x.astype(jnp.float32).reshape(shp[:-1] + (n_groups, shp[-1] // n_groups))
    mu = jnp.mean(xf, axis=-1, keepdims=True)
    xc = xf - mu
    var = jnp.mean(xc * xc, axis=-1, keepdims=True)
    y = (xc * lax.rsqrt(var + EPS)).reshape(shp)
    return (y * g.astype(jnp.float32) + b.astype(jnp.float32)).astype(x.dtype)


def causal_dwconv(x, hist, w):
    k = w.shape[0]
    xp = jnp.concatenate([hist.astype(x.dtype), x], axis=1)
    y = lax.conv_general_dilated(xp, w[:, None, :].astype(x.dtype), window_strides=(1,),
                                 padding='VALID', dimension_numbers=('NWC', 'WIO', 'NWC'),
                                 feature_group_count=x.shape[-1])
    return y, xp[:, -(k - 1):]


def chunk_spatial_mix(v, w_s, b_s):
    n, t, c = v.shape
    n_chunks = -(-t // SGU_CHUNK)
    pad = n_chunks * SGU_CHUNK - t
    vp = jnp.pad(v, ((0, 0), (0, pad), (0, 0))).reshape(n, n_chunks, SGU_CHUNK, SGU_HEADS, c // SGU_HEADS)
    mask = jnp.tril(jnp.ones((SGU_CHUNK, SGU_CHUNK), dtype=bool))
    w = jnp.where(mask[None], w_s, 0).astype(v.dtype)
    s = jnp.einsum('hts,bnshd->bnthd', w, vp) + b_s.T.astype(v.dtype)[None, None, :, :, None]
    return s.reshape(n, n_chunks * SGU_CHUNK, c)[:, :t]


def hier_moe(h, w_rg, b_rg, w_re, b_re, w1, w3, w2):
    t, d = h.shape
    hf = h.astype(jnp.float32)
    tok = jnp.arange(t, dtype=jnp.int32)
    g_logits = hf @ w_rg.astype(jnp.float32) + b_rg.astype(jnp.float32)
    g_prob = jax.nn.softmax(g_logits, axis=-1)
    g_sel = jnp.argmax(g_logits, axis=-1).astype(jnp.int32)
    e_logits = (hf @ w_re.astype(jnp.float32) + b_re.astype(jnp.float32)).reshape(t, N_GROUPS, EXPERTS_PER_GROUP)
    e_in = e_logits[tok, g_sel]
    top_v, top_i = lax.top_k(e_in, TOP_K)
    gate = jax.nn.softmax(top_v, axis=-1) * g_prob[tok, g_sel][:, None]
    expert = g_sel[:, None] * EXPERTS_PER_GROUP + top_i.astype(jnp.int32)

    a = t * TOP_K
    e_flat = expert.reshape(a)
    tok_flat = jnp.repeat(tok, TOP_K)
    order = jnp.argsort(e_flat)
    e_sorted = e_flat[order]
    tok_sorted = tok_flat[order]
    w_sorted = gate.reshape(a)[order]
    counts = jnp.zeros((N_EXPERTS,), jnp.int32).at[e_flat].add(1)
    starts = jnp.cumsum(counts) - counts
    padded = (counts + MOE_BLOCK - 1) // MOE_BLOCK * MOE_BLOCK
    pends = jnp.cumsum(padded)
    pstarts = pends - padded
    dest = pstarts[e_sorted] + (jnp.arange(a, dtype=jnp.int32) - starts[e_sorted])
    n_blocks = -(-a // MOE_BLOCK) + N_EXPERTS
    rows = jnp.zeros((n_blocks * MOE_BLOCK, d), h.dtype).at[dest].set(h[tok_sorted])
    block_e = jnp.minimum(jnp.searchsorted(pends, jnp.arange(n_blocks, dtype=jnp.int32) * MOE_BLOCK,
                                           side='right'), N_EXPERTS - 1)

    def expert_block(args):
        xb, e = args
        return (jax.nn.silu(xb @ w1[e]) * (xb @ w3[e])) @ w2[e]

    y_rows = lax.map(expert_block, (rows.reshape(n_blocks, MOE_BLOCK, d), block_e)).reshape(n_blocks * MOE_BLOCK, d)
    contrib = y_rows[dest] * w_sorted[:, None].astype(y_rows.dtype)
    return jnp.zeros((t, d), y_rows.dtype).at[tok_sorted].add(contrib)


def trunk_layer(x, c, hist_a, hist_c, norm1_g, norm2_g, w_ada, b_ada, w_in, a_dw_w, a_dw_b,
                a_norm_g, a_norm_b, b_norm_g, b_norm_b, b_spatial_w, b_spatial_b, c_conv_w,
                w_branch, w_out, w_router_g, b_router_g, w_router_e, b_router_e, w1, w3, w2):
    n, t, d = x.shape
    wd = BRANCH_WIDTH
    mod = (jax.nn.silu(c) @ w_ada + b_ada)[:, None, :]
    sh1, sc1, g1, sh2, sc2, g2 = jnp.split(mod, 6, axis=-1)

    h = rms_norm(x, norm1_g) * (1 + sc1) + sh1
    z = h @ w_in
    a_val, a_gt, b_u, b_v, c_b, c_c, c_h, gates = jnp.split(z, [wd * i for i in range(1, 8)], axis=-1)

    a = a_val * jax.nn.sigmoid(a_gt)
    a, new_hist_a = causal_dwconv(a, hist_a, a_dw_w)
    a = jax.nn.silu(layer_norm(a + a_dw_b, a_norm_g, a_norm_b, CONV_A_GROUPS))

    u = jax.nn.gelu(b_u)
    v = layer_norm(jax.nn.gelu(b_v), b_norm_g, b_norm_b)
    bb = u * chunk_spatial_mix(v, b_spatial_w, b_spatial_b)

    ch, new_hist_c = causal_dwconv(c_c * c_h, hist_c, c_conv_w)
    cc = c_b * ch

    br = jnp.einsum('ntgc,gcd->ntgd', jnp.stack([a, bb, cc], axis=2), w_branch)
    gsig = jax.nn.sigmoid(gates.reshape(n, t, N_BRANCHES, d))
    mixed = jnp.sum(gsig * br, axis=2) @ w_out
    x = x + g1 * mixed

    h2 = rms_norm(x, norm2_g) * (1 + sc2) + sh2
    y = hier_moe(h2.reshape(n * t, d), w_router_g, b_router_g, w_router_e, b_router_e, w1, w3, w2)
    x = x + g2 * y.reshape(n, t, d).astype(x.dtype)
    return x, new_hist_a, new_hist_c, v


def setup_inputs(seed: int = 0) -> dict:
    key = jax.random.key(seed)
    k = jax.random.split(key, 30)
    L, D, W, E, F = DEPTH, D_MODEL, BRANCH_WIDTH, N_EXPERTS, D_FF_EXPERT
    f32 = jnp.float32

    def nrm(i, shape, s):
        return jax.random.normal(k[i], shape, f32) * s

    return {
        "x_prompt": nrm(0, (BATCH, SEQ, D), 1.0),
        "x_sample": nrm(1, (DEC_BATCH, DEC_SEQ, D), 1.0),
        "c_prompt": nrm(2, (BATCH, D), 1.0),
        "c_sample": nrm(3, (DEC_BATCH, D), 1.0),
        "state_conv_a": nrm(4, (L, DEC_BATCH, CONV_A_WIDTH - 1, W), 0.5),
        "state_conv_c": nrm(5, (L, DEC_BATCH, CONV_C_WIDTH - 1, W), 1.0),
        "norm1_g": 1.0 + nrm(6, (L, D), 0.02),
        "norm2_g": 1.0 + nrm(7, (L, D), 0.02),
        "w_ada": nrm(8, (L, D, 6 * D), 0.3 * D ** -0.5),
        "b_ada": nrm(9, (L, 6 * D), 0.02),
        "w_in": nrm(10, (L, D, IN_COLS), D ** -0.5),
        "a_dw_w": nrm(11, (L, CONV_A_WIDTH, W), CONV_A_WIDTH ** -0.5),
        "a_dw_b": nrm(12, (L, W), 0.02),
        "a_norm_g": 1.0 + nrm(13, (L, W), 0.02),
        "a_norm_b": nrm(14, (L, W), 0.02),
        "b_norm_g": 1.0 + nrm(15, (L, W), 0.02),
        "b_norm_b": nrm(16, (L, W), 0.02),
        "b_spatial_w": nrm(17, (L, SGU_HEADS, SGU_CHUNK, SGU_CHUNK), SGU_CHUNK ** -0.5),
        "b_spatial_b": 1.0 + nrm(18, (L, SGU_HEADS, SGU_CHUNK), 0.02),
        "c_conv_w": nrm(19, (L, CONV_C_WIDTH, W), CONV_C_WIDTH ** -0.5),
        "w_branch": nrm(20, (L, N_BRANCHES, W, D), W ** -0.5),
        "w_out": nrm(21, (L, D, D), D ** -0.5),
        "w_router_g": nrm(22, (L, D, N_GROUPS), D ** -0.5),
        "b_router_g": nrm(23, (L, N_GROUPS), 0.01),
        "w_router_e": nrm(24, (L, D, E), D ** -0.5),
        "b_router_e": nrm(25, (L, E), 0.01),
        "w1": nrm(26, (L, E, D, F), D ** -0.5),
        "w3": nrm(27, (L, E, D, F), D ** -0.5),
        "w2": nrm(28, (L, E, F, D), F ** -0.5),
        "final_norm_g": 1.0 + nrm(29, (D,), 0.02),
    }


def reference(x_prompt, x_sample, c_prompt, c_sample, state_conv_a, state_conv_c,
              norm1_g, norm2_g, w_ada, b_ada, w_in, a_dw_w, a_dw_b, a_norm_g, a_norm_b,
              b_norm_g, b_norm_b, b_spatial_w, b_spatial_b, c_conv_w, w_branch, w_out,
              w_router_g, b_router_g, w_router_e, b_router_e, w1, w3, w2, final_norm_g):
    xp, xs = x_prompt, x_sample
    pa, pc, sa, sc, sv = [], [], [], [], []
    for l in range(DEPTH):
        lp = [p[l] for p in (norm1_g, norm2_g, w_ada, b_ada, w_in, a_dw_w, a_dw_b, a_norm_g, a_norm_b,
                             b_norm_g, b_norm_b, b_spatial_w, b_spatial_b, c_conv_w, w_branch, w_out,
                             w_router_g, b_router_g, w_router_e, b_router_e, w1, w3, w2)]
        hza = jnp.zeros((xp.shape[0], CONV_A_WIDTH - 1, BRANCH_WIDTH), xp.dtype)
        hzc = jnp.zeros((xp.shape[0], CONV_C_WIDTH - 1, BRANCH_WIDTH), xp.dtype)
        xp, ha, hc, _ = trunk_layer(xp, c_prompt, hza, hzc, *lp)
        pa.append(ha)
        pc.append(hc)
        xs, ha_s, hc_s, v_s = trunk_layer(xs, c_sample, state_conv_a[l], state_conv_c[l], *lp)
        sa.append(ha_s)
        sc.append(hc_s)
        sv.append(v_s)
    y_prompt = rms_norm(xp, final_norm_g)
    y_sample = rms_norm(xs, final_norm_g)
    new_conv_a_prompt = jnp.stack(pa)
    new_conv_c_prompt = jnp.stack(pc)
    new_conv_a_sample = jnp.stack(sa)
    new_conv_c_sample = jnp.stack(sc)
    new_sgu_v_sample = jnp.stack(sv)
    return (y_prompt, y_sample, new_conv_a_prompt, new_conv_c_prompt, new_conv_a_sample, new_conv_c_sample, new_sgu_v_sample)
```

```python
import functools

import jax
import jax.numpy as jnp
from jax import lax
from jax.experimental import pallas as pl
from jax.experimental.pallas import tpu as pltpu

F32 = jnp.float32
BF16 = jnp.bfloat16

D = 1024
W = D // 2
KA = 31
KC = 3
NG_A = 8
CHUNK = 128
HEADS = 8
NGRP = 4
EPG = 8
NEXP = NGRP * EPG
FF = D // 2
EPS = 1e-6
IN_COLS = 7 * W + 3 * D
LANES = 128
ROUTE_COLS = LANES

TT = 256
A_HDR = 32
C_HDR = 8
TP = 512
BM = 256
VMEM_LIMIT = 56 * 1024 * 1024

NEG = -1e30


def _sigmoid(x):
    return 1.0 / (1.0 + jnp.exp(-x))


def _silu(x):
    return x * _sigmoid(x)


def _gelu_tanh(x):
    return 0.5 * x * (1.0 + jnp.tanh(0.7978845608028654 * (x + 0.044715 * x * x * x)))


def _rms(x, g):
    return x * lax.rsqrt(jnp.mean(x * x, axis=-1, keepdims=True) + EPS) * g


def _dot(a, b):
    return jnp.dot(a, b, preferred_element_type=F32)


def _dot_hi(a, b):
    return jnp.dot(a, b, preferred_element_type=F32, precision=lax.Precision.HIGHEST)


def _mod_kernel(c_ref, w_ref, b_ref, o_ref):
    c = c_ref[...]
    s = _silu(c).astype(BF16)
    o_ref[0] = _dot(s, w_ref[0].astype(BF16)) + b_ref[0]


def _adaln(c_all, w_ada, b_ada):
    nl = w_ada.shape[0]
    r = c_all.shape[0]
    tn = D
    return pl.pallas_call(
        _mod_kernel,
        grid=(nl, 6 * D // tn),
        in_specs=[
            pl.BlockSpec((r, D), lambda l, j: (0, 0)),
            pl.BlockSpec((1, D, tn), lambda l, j: (l, 0, j)),
            pl.BlockSpec((1, 1, tn), lambda l, j: (l, 0, j)),
        ],
        out_specs=pl.BlockSpec((1, r, tn), lambda l, j: (l, 0, j)),
        out_shape=jax.ShapeDtypeStruct((nl, r, 6 * D), F32),
        compiler_params=pltpu.CompilerParams(
            dimension_semantics=("arbitrary", "arbitrary"), vmem_limit_bytes=VMEM_LIMIT),
        name="adaln_mod",
    )(c_all, w_ada, b_ada.reshape(nl, 1, 6 * D))


def _group_norm_silu(a, gavg_ref, g, b):
    mu = _dot(a.astype(BF16), gavg_ref[...])
    xc = a - mu
    var = _dot((xc * xc).astype(BF16), gavg_ref[...])
    y = xc * lax.rsqrt(var + EPS) * g + b
    return _silu(y)


def _layer_norm(x, g, b):
    mu = jnp.mean(x, axis=-1, keepdims=True)
    xc = x - mu
    var = jnp.mean(xc * xc, axis=-1, keepdims=True)
    return xc * lax.rsqrt(var + EPS) * g + b


def _merge_and_route(x, a, bb, cc, gates_fn, mod, n2g, wbr_ref, wout_ref, wr_ref, br_ref,
                     x1_ref, h2_ref, route_ref):
    g1, sh2, sc2 = mod
    acc = None
    for gi, br_in in enumerate((a, bb, cc)):
        br = _dot(br_in.astype(BF16), wbr_ref[gi])
        term = _sigmoid(gates_fn(gi)) * br
        acc = term if acc is None else acc + term
    mixed = _dot(acc.astype(BF16), wout_ref[...])
    x1 = x + g1 * mixed
    x1_ref[...] = x1
    h2 = _rms(x1, n2g) * (1.0 + sc2) + sh2
    h2_ref[...] = h2

    logits = _dot_hi(h2, wr_ref[...]) + br_ref[...]
    rows = logits.shape[0]
    lane = lax.broadcasted_iota(jnp.int32, (rows, ROUTE_COLS), 1).astype(F32)
    is_g = lane < NGRP
    gl = jnp.where(is_g, logits, NEG)
    gmax = jnp.max(gl, axis=-1, keepdims=True)
    gsel = jnp.min(jnp.where(gl == gmax, lane, float(ROUTE_COLS)), axis=-1, keepdims=True)
    gden = jnp.sum(jnp.where(is_g, jnp.exp(gl - gmax), 0.0), axis=-1, keepdims=True)
    gprob = 1.0 / gden
    lo = NGRP + EPG * gsel
    in_grp = jnp.logical_and(lane >= lo, lane < lo + EPG)
    el = jnp.where(in_grp, logits, NEG)
    v1 = jnp.max(el, axis=-1, keepdims=True)
    i1 = jnp.min(jnp.where(el == v1, lane, float(ROUTE_COLS)), axis=-1, keepdims=True)
    el2 = jnp.where(lane == i1, NEG, el)
    v2 = jnp.max(el2, axis=-1, keepdims=True)
    i2 = jnp.min(jnp.where(el2 == v2, lane, float(ROUTE_COLS)), axis=-1, keepdims=True)
    p2 = jnp.exp(v2 - v1)
    w_a = gprob / (1.0 + p2)
    w_b = gprob * p2 / (1.0 + p2)
    route = jnp.where(lane == 0.0, i1 - NGRP,
                      jnp.where(lane == 1.0, i2 - NGRP,
                                jnp.where(lane == 2.0, w_a,
                                          jnp.where(lane == 3.0, w_b, 0.0))))
    route_ref[...] = route


def _mixer_prompt_kernel(x_ref, mod_ref, n1g_ref, n2g_ref, win_ref, adw_ref, adb_ref, ang_ref,
                         anb_ref, bng_ref, bnb_ref, wsp_ref, bsp_ref, ccw_ref, wbr_ref, wout_ref,
                         gavg_ref, wr_ref, br_ref,
                         x1_ref, h2_ref, route_ref, ha_ref, hc_ref,
                         abuf, cbuf, aconv):
    t = pl.program_id(1)
    nt = pl.num_programs(1)

    @pl.when(t == 0)
    def _():
        abuf[0:A_HDR, :] = jnp.zeros((A_HDR, W), F32)
        cbuf[0:C_HDR, :] = jnp.zeros((C_HDR, W), F32)

    x = x_ref[...]
    sh1 = mod_ref[0, 0:1, :]
    sc1 = mod_ref[0, 1:2, :]
    g1 = mod_ref[0, 2:3, :]
    sh2 = mod_ref[0, 3:4, :]
    sc2 = mod_ref[0, 4:5, :]
    h = (_rms(x, n1g_ref[...]) * (1.0 + sc1) + sh1).astype(BF16)

    def zsec(lo, hi):
        return _dot(h, win_ref[:, lo:hi])

    a_glu = zsec(0, W) * _sigmoid(zsec(W, 2 * W))
    abuf[A_HDR:A_HDR + TT, :] = a_glu
    off = A_HDR - (KA - 1)
    rc = 32
    for r0 in range(0, TT, rc):
        acc = jnp.zeros((rc, W), F32)
        for k in range(KA):
            acc = acc + adw_ref[k:k + 1, :] * abuf[off + r0 + k:off + r0 + k + rc, :]
        aconv[r0:r0 + rc, :] = acc + adb_ref[...]
    a = _group_norm_silu(aconv[...], gavg_ref, ang_ref[...], anb_ref[...])

    @pl.when(t == nt - 1)
    def _():
        ha_ref[0] = abuf[A_HDR + TT - (KA - 1):A_HDR + TT, :]

    abuf[0:A_HDR, :] = abuf[TT:TT + A_HDR, :]

    u = _gelu_tanh(zsec(2 * W, 3 * W))
    v = _layer_norm(_gelu_tanh(zsec(3 * W, 4 * W)), bng_ref[...], bnb_ref[...])
    col_head = lax.broadcasted_iota(jnp.int32, (CHUNK, W), 1) // (W // HEADS)
    s_parts = []
    for c in range(TT // CHUNK):
        vc = v[c * CHUNK:(c + 1) * CHUNK, :]
        s = bsp_ref[...]
        for hh in range(HEADS):
            vm = jnp.where(col_head == hh, vc, 0.0).astype(BF16)
            s = s + _dot(wsp_ref[hh], vm)
        s_parts.append(s)
    bb = u * jnp.concatenate(s_parts, axis=0)

    cch = zsec(5 * W, 6 * W) * zsec(6 * W, 7 * W)
    cbuf[C_HDR:C_HDR + TT, :] = cch
    coff = C_HDR - (KC - 1)
    ych = (ccw_ref[0:1, :] * cbuf[coff:coff + TT, :]
           + ccw_ref[1:2, :] * cbuf[coff + 1:coff + 1 + TT, :]
           + ccw_ref[2:3, :] * cbuf[coff + 2:coff + 2 + TT, :])
    cc = zsec(4 * W, 5 * W) * ych

    @pl.when(t == nt - 1)
    def _():
        hc_ref[0] = cbuf[C_HDR + TT - (KC - 1):C_HDR + TT, :]

    cbuf[0:C_HDR, :] = cbuf[TT:TT + C_HDR, :]

    def gates_fn(gi):
        return zsec(7 * W + gi * D, 7 * W + (gi + 1) * D)

    _merge_and_route(x, a, bb, cc, gates_fn, (g1, sh2, sc2), n2g_ref[...], wbr_ref, wout_ref,
                     wr_ref, br_ref, x1_ref, h2_ref, route_ref)


def _const_spec(shape):
    nd = len(shape)
    return pl.BlockSpec(shape, lambda *_: (0,) * nd, pipeline_mode=pl.Buffered(1))


def _mixer_prompt(x, mod, lw):
    n = mod.shape[0]
    t = x.shape[0] // n
    nt = t // TT
    row_spec = lambda cols: pl.BlockSpec((TT, cols), lambda i, j: (i * nt + j, 0))
    consts = [lw["n1g"], lw["n2g"], lw["w_in"], lw["a_dw_w"], lw["a_dw_b"], lw["a_norm_g"],
              lw["a_norm_b"], lw["b_norm_g"], lw["b_norm_b"], lw["w_sp"], lw["b_sp"], lw["c_conv_w"],
              lw["w_branch"], lw["w_out"], lw["gavg"], lw["w_router"], lw["b_router"]]
    return pl.pallas_call(
        _mixer_prompt_kernel,
        grid=(n, nt),
        in_specs=[row_spec(D), pl.BlockSpec((1, 6, D), lambda i, j: (i, 0, 0))]
        + [_const_spec(c.shape) for c in consts],
        out_specs=[row_spec(D), row_spec(D), row_spec(ROUTE_COLS),
                   pl.BlockSpec((1, KA - 1, W), lambda i, j: (i, 0, 0)),
                   pl.BlockSpec((1, KC - 1, W), lambda i, j: (i, 0, 0))],
        out_shape=[jax.ShapeDtypeStruct((n * t, D), F32),
                   jax.ShapeDtypeStruct((n * t, D), F32),
                   jax.ShapeDtypeStruct((n * t, ROUTE_COLS), F32),
                   jax.ShapeDtypeStruct((n, KA - 1, W), F32),
                   jax.ShapeDtypeStruct((n, KC - 1, W), F32)],
        scratch_shapes=[pltpu.VMEM((A_HDR + TT, W), F32),
                        pltpu.VMEM((C_HDR + TT, W), F32),
                        pltpu.VMEM((TT, W), F32)],
        compiler_params=pltpu.CompilerParams(
            dimension_semantics=("arbitrary", "arbitrary"), vmem_limit_bytes=VMEM_LIMIT),
        name="mixer_prompt",
    )(x, mod, *consts)


def _mixer_sample_kernel(x_ref, mod_ref, hista_ref, histc_ref, n1g_ref, n2g_ref, win_ref, adw_ref,
                         adb_ref, ang_ref, anb_ref, bng_ref, bnb_ref, wsp4_ref, bsp4_ref,
                         ccw_ref, wbr_ref, wout_ref, gavg_ref, wr_ref, br_ref,
                         x1_ref, h2_ref, route_ref, ha_ref, hc_ref, v_ref):
    ts, nb = x_ref.shape[0], x_ref.shape[1]

    def rows(fn):
        return jnp.concatenate([fn(j) for j in range(ts)], axis=0)

    x = rows(lambda j: x_ref[j])

    def modrow(i):
        return jnp.concatenate([mod_ref[i]] * ts, axis=0)

    h = (_rms(x, n1g_ref[...]) * (1.0 + modrow(1)) + modrow(0)).astype(BF16)

    def zsec(lo, hi):
        return _dot(h, win_ref[:, lo:hi])

    def tslab(arr, j):
        return arr[j * nb:(j + 1) * nb, :]

    a_glu = zsec(0, W) * _sigmoid(zsec(W, 2 * W))
    kh = KA - 1
    a_conv = []
    for tq in range(ts):
        acc = jnp.zeros((nb, W), F32) + adb_ref[...]
        for r in range(tq, kh):
            acc = acc + adw_ref[r - tq:r - tq + 1, :] * hista_ref[r]
        for j in range(tq + 1):
            acc = acc + adw_ref[kh + j - tq:kh + j - tq + 1, :] * tslab(a_glu, j)
        a_conv.append(acc)
    a = _group_norm_silu(jnp.concatenate(a_conv, axis=0), gavg_ref, ang_ref[...], anb_ref[...])
    for r in range(kh - ts):
        ha_ref[r] = hista_ref[r + ts]
    for j in range(ts):
        ha_ref[kh - ts + j] = tslab(a_glu, j)

    u = _gelu_tanh(zsec(2 * W, 3 * W))
    v = _layer_norm(_gelu_tanh(zsec(3 * W, 4 * W)), bng_ref[...], bnb_ref[...])
    for j in range(ts):
        v_ref[j] = tslab(v, j)
    s_rows = []
    for tq in range(ts):
        s = jnp.zeros((nb, W), F32) + bsp4_ref[tq:tq + 1, :]
        for sq in range(tq + 1):
            s = s + wsp4_ref[tq * ts + sq:tq * ts + sq + 1, :] * tslab(v, sq)
        s_rows.append(s)
    bb = u * jnp.concatenate(s_rows, axis=0)

    cch = zsec(5 * W, 6 * W) * zsec(6 * W, 7 * W)
    xp = [histc_ref[r] for r in range(KC - 1)] + [tslab(cch, j) for j in range(ts)]
    ych = jnp.concatenate(
        [sum(ccw_ref[k:k + 1, :] * xp[tq + k] for k in range(KC)) for tq in range(ts)], axis=0)
    cc = zsec(4 * W, 5 * W) * ych
    for r in range(KC - 1):
        hc_ref[r] = xp[ts + r]

    def gates_fn(gi):
        return zsec(7 * W + gi * D, 7 * W + (gi + 1) * D)

    _merge_and_route(x, a, bb, cc, gates_fn, (modrow(2), modrow(3), modrow(4)), n2g_ref[...],
                     wbr_ref, wout_ref, wr_ref, br_ref, x1_ref, h2_ref, route_ref)


def _mixer_sample(x_tm, mod_tm, hist_a_tm, hist_c_tm, lw, nsplit=2):
    ts, n, _ = x_tm.shape
    nb = n // nsplit
    rows = ts * nb
    consts = [lw["n1g"], lw["n2g"], lw["w_in"], lw["a_dw_w"], lw["a_dw_b"],
              lw["a_norm_g"], lw["a_norm_b"], lw["b_norm_g"], lw["b_norm_b"], lw["w_sp4"], lw["b_sp4"],
              lw["c_conv_w"], lw["w_branch"], lw["w_out"], lw["gavg"], lw["w_router"], lw["b_router"]]
    seq3 = lambda k, cols: pl.BlockSpec((k, nb, cols), lambda i: (0, i, 0))
    row_spec = lambda cols: pl.BlockSpec((rows, cols), lambda i: (i, 0))
    return pl.pallas_call(
        _mixer_sample_kernel,
        grid=(nsplit,),
        in_specs=[seq3(ts, D), seq3(6, D), seq3(KA - 1, W), seq3(KC - 1, W)]
        + [_const_spec(c.shape) for c in consts],
        out_specs=[row_spec(D), row_spec(D), row_spec(ROUTE_COLS),
                   seq3(KA - 1, W), seq3(KC - 1, W), seq3(ts, W)],
        out_shape=[jax.ShapeDtypeStruct((ts * n, D), F32),
                   jax.ShapeDtypeStruct((ts * n, D), F32),
                   jax.ShapeDtypeStruct((ts * n, ROUTE_COLS), F32),
                   jax.ShapeDtypeStruct((KA - 1, n, W), F32),
                   jax.ShapeDtypeStruct((KC - 1, n, W), F32),
                   jax.ShapeDtypeStruct((ts, n, W), F32)],
        compiler_params=pltpu.CompilerParams(
            dimension_semantics=("arbitrary",), vmem_limit_bytes=VMEM_LIMIT),
        name="mixer_sample",
    )(x_tm, mod_tm, hist_a_tm, hist_c_tm, *consts)


def _plan_kernel(rp_ref, rs_ref, tri_ref, dest_ref, tbl_ref, carry, pstart, *, ntp):
    ph = pl.program_id(0)
    i = pl.program_id(1)
    nt = pl.num_programs(1)
    lane = lax.broadcasted_iota(jnp.int32, (TP, ROUTE_COLS), 1).astype(F32)

    @pl.when(jnp.logical_and(ph == 0, i == 0))
    def _():
        carry[...] = jnp.zeros_like(carry)

    r = jnp.where(i < ntp, rp_ref[...], rs_ref[...])
    e_a = r[:, 0:1]
    e_b = r[:, 1:2]
    hot_a = lane == e_a
    hot_b = lane == e_b
    s = jnp.where(jnp.logical_or(hot_a, hot_b), 1.0, 0.0)
    colsum = jnp.sum(s, axis=0, keepdims=True)

    @pl.when(ph == 0)
    def _():
        carry[...] = carry[...] + colsum

    @pl.when(jnp.logical_and(ph == 0, i == nt - 1))
    def _():
        cnt = carry[...]
        nblk = jnp.floor((cnt + (BM - 1)) * (1.0 / BM))
        ri = lax.broadcasted_iota(jnp.int32, (ROUTE_COLS, ROUTE_COLS), 0)
        ci = lax.broadcasted_iota(jnp.int32, (ROUTE_COLS, ROUTE_COLS), 1)
        upper = jnp.where(ri <= ci, 1.0, 0.0)
        cum = _dot_hi(jnp.broadcast_to(nblk, (8, ROUTE_COLS)), upper)[0:1, :]
        cum_ex = cum - nblk
        pstart[...] = cum_ex * BM
        nbp = tbl_ref.shape[0]
        b = lax.broadcasted_iota(jnp.int32, (nbp, ROUTE_COLS), 0).astype(F32)
        ln = lax.broadcasted_iota(jnp.int32, (nbp, ROUTE_COLS), 1).astype(F32)
        is_e = ln < NEXP
        n_used = jnp.max(jnp.where(is_e, cum, 0.0), axis=-1, keepdims=True)
        be = jnp.sum(jnp.where(jnp.logical_and(is_e, cum <= b), 1.0, 0.0), axis=-1, keepdims=True)
        be = jnp.minimum(be, NEXP - 1.0)
        hot = ln == be
        cnt_b = jnp.sum(jnp.where(hot, cnt, 0.0), axis=-1, keepdims=True)
        cex_b = jnp.sum(jnp.where(hot, cum_ex, 0.0), axis=-1, keepdims=True)
        b0 = b[:, 0:1]
        used = b0 < n_used
        nvalid = jnp.where(used, jnp.clip(cnt_b - (b0 - cex_b) * BM, 0.0, float(BM)), 0.0)
        first = jnp.where(jnp.logical_and(used, b0 == cex_b), 1.0, 0.0)
        rowblk = jnp.minimum(b0, n_used - 1.0)
        last_e = jnp.sum(jnp.where(jnp.logical_and(is_e, cum <= n_used - 1.0), 1.0, 0.0),
                         axis=-1, keepdims=True)
        be = jnp.where(used, be, jnp.minimum(last_e, NEXP - 1.0))
        tbl = jnp.where(ln == 0.0, be,
                        jnp.where(ln == 1.0, nvalid,
                                  jnp.where(ln == 2.0, first,
                                            jnp.where(ln == 3.0, rowblk, 0.0))))
        tbl_ref[...] = tbl.astype(jnp.int32)
        carry[...] = jnp.zeros_like(carry)

    @pl.when(ph == 1)
    def _():
        pre = _dot(tri_ref[...], s.astype(BF16)) + carry[...] + pstart[...]
        d_a = jnp.sum(jnp.where(hot_a, pre, 0.0), axis=-1, keepdims=True)
        d_b = jnp.sum(jnp.where(hot_b, pre, 0.0), axis=-1, keepdims=True)
        l2 = lax.broadcasted_iota(jnp.int32, (TP, 2), 1)
        dest_ref[...] = jnp.where(l2 == 0, d_a, d_b).astype(jnp.int32)
        carry[...] = carry[...] + colsum


def _plan(route_p, route_s, nblocks):
    ntp = route_p.shape[0] // TP
    nts = route_s.shape[0] // TP
    nt = ntp + nts
    nbp = -(-nblocks // 8) * 8
    ri = lax.broadcasted_iota(jnp.int32, (TP, TP), 0)
    ci = lax.broadcasted_iota(jnp.int32, (TP, TP), 1)
    tri = jnp.where(ci < ri, 1.0, 0.0).astype(BF16)
    dest, tbl = pl.pallas_call(
        functools.partial(_plan_kernel, ntp=ntp),
        grid=(2, nt),
        in_specs=[pl.BlockSpec((TP, ROUTE_COLS), lambda p, i: (jnp.minimum(i, ntp - 1), 0)),
                  pl.BlockSpec((TP, ROUTE_COLS), lambda p, i: (jnp.maximum(i - ntp, 0), 0)),
                  pl.BlockSpec((TP, TP), lambda p, i: (0, 0))],
        out_specs=[pl.BlockSpec((TP, 2), lambda p, i: (i * p, 0)),
                   pl.BlockSpec((nbp, ROUTE_COLS), lambda p, i: (0, 0))],
        out_shape=[jax.ShapeDtypeStruct((nt * TP, 2), jnp.int32),
                   jax.ShapeDtypeStruct((nbp, ROUTE_COLS), jnp.int32)],
        scratch_shapes=[pltpu.VMEM((1, ROUTE_COLS), F32), pltpu.VMEM((1, ROUTE_COLS), F32)],
        compiler_params=pltpu.CompilerParams(
            dimension_semantics=("arbitrary", "arbitrary"), vmem_limit_bytes=VMEM_LIMIT),
        name="moe_plan",
    )(route_p, route_s, tri)
    return dest, tbl


def _scatter_kernel(dest_ref, h_ref, *rest):
    rows_ref, sem = rest[-2:]
    tp = h_ref.shape[0]

    def row_copy(r, k):
        d = dest_ref[2 * r + k]
        return pltpu.make_async_copy(h_ref.at[pl.ds(r, 1)], rows_ref.at[pl.ds(d, 1)], sem)

    def issue(r, c):
        row_copy(r, 0).start()
        row_copy(r, 1).start()
        return c

    lax.fori_loop(0, tp, issue, 0, unroll=8)

    def drain(r, c):
        row_copy(r, 0).wait()
        row_copy(r, 1).wait()
        return c

    lax.fori_loop(0, tp, drain, 0, unroll=8)


def _scatter_rows(dest_flat, h2, rows_buf, n_rows, tp, tile0):
    nt = h2.shape[0] // tp
    in_specs = [pl.BlockSpec((2 * tp,), lambda i: (i + tile0,), memory_space=pltpu.SMEM),
                pl.BlockSpec((tp, D), lambda i: (i, 0))]
    args = [dest_flat, h2]
    aliases = {}
    if rows_buf is not None:
        in_specs.append(pl.BlockSpec(memory_space=pl.ANY))
        args.append(rows_buf)
        aliases = {2: 0}
    return pl.pallas_call(
        _scatter_kernel,
        grid=(nt,),
        in_specs=in_specs,
        out_specs=pl.BlockSpec(memory_space=pl.ANY),
        out_shape=jax.ShapeDtypeStruct((n_rows, D), F32),
        scratch_shapes=[pltpu.SemaphoreType.DMA(())],
        input_output_aliases=aliases,
        compiler_params=pltpu.CompilerParams(
            dimension_semantics=("arbitrary",), vmem_limit_bytes=VMEM_LIMIT,
            has_side_effects=True),
        name="moe_scatter",
    )(*args)


def _expert_kernel(be_ref, nv_ref, first_ref, rb_ref, x_ref, w1_ref, w3_ref, w2_ref, y_ref,
                   w1b, w3b, w2b):
    del be_ref, rb_ref
    b = pl.program_id(0)

    @pl.when(first_ref[b] == 1)
    def _():
        w1b[...] = w1_ref[0, 0].astype(BF16)
        w3b[...] = w3_ref[0, 0].astype(BF16)
        w2b[...] = w2_ref[0, 0].astype(BF16)

    nv = nv_ref[b]

    @pl.when(nv > 0)
    def _():
        row = lax.broadcasted_iota(jnp.int32, (BM, D), 0)
        x = jnp.where(row < nv, x_ref[...], 0.0).astype(BF16)
        h1 = _dot(x, w1b[...])
        h3 = _dot(x, w3b[...])
        act = (_silu(h1) * h3).astype(BF16)
        y_ref[...] = _dot(act, w2b[...])


def _experts(tbl, rows_buf, w1, w3, w2, layer, nblocks):
    be, nv, first, rb = (tbl[:nblocks, k] for k in range(4))
    wspec = lambda shape: pl.BlockSpec((1, 1) + shape,
                                       lambda b, be, nv, fi, rb: (layer, be[b], 0, 0))
    xspec = pl.BlockSpec((BM, D), lambda b, be, nv, fi, rb: (rb[b], 0))
    return pl.pallas_call(
        _expert_kernel,
        grid_spec=pltpu.PrefetchScalarGridSpec(
            num_scalar_prefetch=4,
            grid=(nblocks,),
            in_specs=[xspec, wspec((D, FF)), wspec((D, FF)), wspec((FF, D))],
            out_specs=xspec,
            scratch_shapes=[pltpu.VMEM((D, FF), BF16), pltpu.VMEM((D, FF), BF16),
                            pltpu.VMEM((FF, D), BF16)],
        ),
        out_shape=jax.ShapeDtypeStruct(rows_buf.shape, F32),
        compiler_params=pltpu.CompilerParams(
            dimension_semantics=("arbitrary",), vmem_limit_bytes=VMEM_LIMIT),
        name="moe_experts",
    )(be, nv, first, rb, rows_buf, w1, w3, w2)


def _combine_kernel(dest_ref, x1_ref, route_ref, g2_ref, fng_ref, y_ref, o_ref, ybuf, sem,
                    *, final_norm, g2_rep):
    tp = x1_ref.shape[0]

    def row_copy(r, k):
        d = dest_ref[2 * r + k]
        return pltpu.make_async_copy(y_ref.at[pl.ds(d, 1)], ybuf.at[k, pl.ds(r, 1)], sem)

    def issue(r, c):
        row_copy(r, 0).start()
        row_copy(r, 1).start()
        return c

    lax.fori_loop(0, tp, issue, 0, unroll=8)

    def drain(r, c):
        row_copy(r, 0).wait()
        row_copy(r, 1).wait()
        return c

    lax.fori_loop(0, tp, drain, 0, unroll=8)

    route = route_ref[...]
    w_a = route[:, 2:3]
    w_b = route[:, 3:4]
    g2 = g2_ref[0]
    if g2_rep > 1:
        g2 = jnp.concatenate([g2] * g2_rep, axis=0)
    x2 = x1_ref[...] + g2 * (w_a * ybuf[0] + w_b * ybuf[1])
    if final_norm:
        x2 = _rms(x2, fng_ref[...])
    o_ref[...] = x2


def _combine(dest_flat, x1, route, g2, fng, y_rows, tp, tile0, *, final_norm, tiles_per_g2=1):
    nt = x1.shape[0] // tp
    r = g2.shape[1]
    if r == 1:
        g2_spec = pl.BlockSpec((1, 1, D), lambda i: (i // tiles_per_g2, 0, 0))
        rep = 1
    else:
        g2_spec = pl.BlockSpec((1, r, D), lambda i: (i, 0, 0))
        rep = tp // r
    return pl.pallas_call(
        functools.partial(_combine_kernel, final_norm=final_norm, g2_rep=rep),
        grid=(nt,),
        in_specs=[pl.BlockSpec((2 * tp,), lambda i: (i + tile0,), memory_space=pltpu.SMEM),
                  pl.BlockSpec((tp, D), lambda i: (i, 0)),
                  pl.BlockSpec((tp, ROUTE_COLS), lambda i: (i, 0)),
                  g2_spec,
                  pl.BlockSpec((1, D), lambda i: (0, 0)),
                  pl.BlockSpec(memory_space=pl.ANY)],
        out_specs=pl.BlockSpec((tp, D), lambda i: (i, 0)),
        out_shape=jax.ShapeDtypeStruct(x1.shape, F32),
        scratch_shapes=[pltpu.VMEM((2, tp, D), F32), pltpu.SemaphoreType.DMA(())],
        compiler_params=pltpu.CompilerParams(
            dimension_semantics=("arbitrary",), vmem_limit_bytes=VMEM_LIMIT),
        name="moe_combine",
    )(dest_flat, x1, route, g2, fng, y_rows)


def _layer_weights(l, p):
    hd = W // HEADS
    gi = jnp.arange(W) // (W // NG_A)
    gavg = jnp.where(gi[:, None] == gi[None, :], 1.0 / (W // NG_A), 0.0).astype(BF16)
    tril = jnp.tril(jnp.ones((CHUNK, CHUNK), dtype=bool))
    w_sp = jnp.where(tril[None], p["b_spatial_w"][l], 0.0)
    w_router = jnp.concatenate(
        [p["w_router_g"][l], p["w_router_e"][l],
         jnp.zeros((D, ROUTE_COLS - NGRP - NEXP), F32)], axis=1)
    b_router = jnp.concatenate(
        [p["b_router_g"][l], p["b_router_e"][l],
         jnp.zeros((ROUTE_COLS - NGRP - NEXP,), F32)])[None, :]
    return {
        "n1g": p["norm1_g"][l][None, :], "n2g": p["norm2_g"][l][None, :],
        "w_in": p["w_in"][l].astype(BF16),
        "a_dw_w": p["a_dw_w"][l], "a_dw_b": p["a_dw_b"][l][None, :],
        "a_norm_g": p["a_norm_g"][l][None, :], "a_norm_b": p["a_norm_b"][l][None, :],
        "b_norm_g": p["b_norm_g"][l][None, :], "b_norm_b": p["b_norm_b"][l][None, :],
        "w_sp": w_sp.astype(BF16),
        "b_sp": jnp.repeat(p["b_spatial_b"][l].T, hd, axis=1),
        "c_conv_w": p["c_conv_w"][l],
        "w_branch": p["w_branch"][l].astype(BF16), "w_out": p["w_out"][l].astype(BF16),
        "gavg": gavg, "w_router": w_router, "b_router": b_router,
    }


def _sample_spatial(l, p, ts):
    hd = W // HEADS
    w = p["b_spatial_w"][l][:, :ts, :ts]
    w = jnp.where(jnp.tril(jnp.ones((ts, ts), dtype=bool))[None], w, 0.0)
    w4 = jnp.repeat(jnp.transpose(w, (1, 2, 0)).reshape(ts * ts, HEADS), hd, axis=1)
    b4 = jnp.repeat(p["b_spatial_b"][l][:, :ts].T, hd, axis=1)
    return w4, b4


def kernel(x_prompt, x_sample, c_prompt, c_sample, state_conv_a, state_conv_c, norm1_g, norm2_g, w_ada, b_ada, w_in, a_dw_w, a_dw_b, a_norm_g, a_norm_b, b_norm_g, b_norm_b, b_spatial_w, b_spatial_b, c_conv_w, w_branch, w_out, w_router_g, b_router_g, w_router_e, b_router_e, w1, w3, w2, final_norm_g):
    p = dict(norm1_g=norm1_g, norm2_g=norm2_g, w_in=w_in, a_dw_w=a_dw_w, a_dw_b=a_dw_b,
             a_norm_g=a_norm_g, a_norm_b=a_norm_b, b_norm_g=b_norm_g, b_norm_b=b_norm_b,
             b_spatial_w=b_spatial_w, b_spatial_b=b_spatial_b, c_conv_w=c_conv_w,
             w_branch=w_branch, w_out=w_out, w_router_g=w_router_g, b_router_g=b_router_g,
             w_router_e=w_router_e, b_router_e=b_router_e)
    depth = w_in.shape[0]
    nb_, seq, _ = x_prompt.shape
    ns, ts, _ = x_sample.shape
    n_tok = nb_ * seq + ns * ts
    nblocks = -(-(2 * n_tok) // BM) + NEXP
    nsplit = 2
    assert seq % TT == 0 and TT % CHUNK == 0 and (nb_ * seq) % TP == 0 and ts * ns == TP

    mod = _adaln(jnp.concatenate([c_prompt, c_sample], axis=0), w_ada, b_ada)
    mod = mod.reshape(depth, nb_ + ns, 6, D)

    xp = x_prompt.reshape(nb_ * seq, D)
    xs = jnp.transpose(x_sample, (1, 0, 2))
    fng = final_norm_g[None, :]
    pa, pc, sa, sc, sv = [], [], [], [], []
    for l in range(depth):
        lw = _layer_weights(l, p)
        lw["w_sp4"], lw["b_sp4"] = _sample_spatial(l, p, ts)
        mod_p = mod[l, :nb_]
        mod_s = jnp.transpose(mod[l, nb_:], (1, 0, 2))

        x1p, h2p, route_p, ha, hc = _mixer_prompt(xp, mod_p, lw)
        pa.append(ha)
        pc.append(hc)
        x1s, h2s, route_s, ha_s, hc_s, v_s = _mixer_sample(
            xs, mod_s, jnp.transpose(state_conv_a[l], (1, 0, 2)),
            jnp.transpose(state_conv_c[l], (1, 0, 2)), lw, nsplit)
        sa.append(jnp.transpose(ha_s, (1, 0, 2)))
        sc.append(jnp.transpose(hc_s, (1, 0, 2)))
        sv.append(jnp.transpose(v_s, (1, 0, 2)))

        dest, tbl = _plan(route_p, route_s, nblocks)
        dest_flat = dest.reshape(-1)
        tps = ts * ns // nsplit
        rows_buf = _scatter_rows(dest_flat, h2p, None, nblocks * BM, TP, 0)
        rows_buf = _scatter_rows(dest_flat, h2s, rows_buf, nblocks * BM, tps, h2p.shape[0] // tps)
        y_rows = _experts(tbl, rows_buf, w1, w3, w2, l, nblocks)

        last = l == depth - 1
        g2p = mod_p[:, 5:6, :]
        g2s = mod[l, nb_:, 5, :].reshape(nsplit, ns // nsplit, D)
        xp = _combine(dest_flat, x1p, route_p, g2p, fng, y_rows, TP, 0,
                      final_norm=last, tiles_per_g2=seq // TP)
        xs_rows = _combine(dest_flat, x1s, route_s, g2s, fng, y_rows, tps, h2p.shape[0] // tps,
                           final_norm=last)
        xs = jnp.transpose(xs_rows.reshape(nsplit, ts, ns // nsplit, D), (1, 0, 2, 3)).reshape(ts, ns, D)

    y_prompt = xp.reshape(nb_, seq, D)
    y_sample = jnp.transpose(xs, (1, 0, 2))
    return (y_prompt, y_sample, jnp.stack(pa), jnp.stack(pc), jnp.stack(sa), jnp.stack(sc),
            jnp.stack(sv))
```

```python
import functools

import jax
import jax.numpy as jnp
from jax import lax
from jax.experimental import pallas as pl
from jax.experimental.pallas import tpu as pltpu

F32 = jnp.float32
BF16 = jnp.bfloat16

D = 1024
W = D // 2
KA = 31
KC = 3
NG_A = 8
CHUNK = 128
HEADS = 8
NGRP = 4
EPG = 8
NEXP = NGRP * EPG
FF = D // 2
EPS = 1e-6
IN_COLS = 7 * W + 3 * D
LANES = 128
ROUTE_COLS = LANES

TT = 256
SUBLANES = 8
A_HDR = 32
A_PAD = SUBLANES
A_ROWS = 64
C_HDR = 8
TP = 512
BM = 256
VMEM_LIMIT = 56 * 1024 * 1024

NEG = -1e30


def _sigmoid(x):
    return 1.0 / (1.0 + jnp.exp(-x))


def _silu(x):
    return x * _sigmoid(x)


def _gelu_tanh(x):
    return 0.5 * x * (1.0 + jnp.tanh(0.7978845608028654 * (x + 0.044715 * x * x * x)))


def _rms(x, g):
    return x * lax.rsqrt(jnp.mean(x * x, axis=-1, keepdims=True) + EPS) * g


def _dot(a, b):
    return jnp.dot(a, b, preferred_element_type=F32)


def _dot_hi(a, b):
    return jnp.dot(a, b, preferred_element_type=F32, precision=lax.Precision.HIGHEST)


ROW_TILE = D // LANES


def _store_row_tiles(ref, val):
    r = val.shape[0]
    for j in range(ROW_TILE):
        ref[pl.ds(j, r, stride=ROW_TILE), :] = val[:, j * LANES:(j + 1) * LANES]


def _load_row_tiles(ref):
    r = ref.shape[0] // ROW_TILE
    return jnp.concatenate([ref[pl.ds(j, r, stride=ROW_TILE), :] for j in range(ROW_TILE)], axis=1)


def _row_tile(ref, r):
    return ref.at[pl.ds(pl.multiple_of(r * ROW_TILE, ROW_TILE), ROW_TILE)]


def _mod_kernel(c_ref, w_ref, b_ref, o_ref):
    c = c_ref[...]
    s = _silu(c).astype(BF16)
    o_ref[0] = _dot(s, w_ref[0].astype(BF16)) + b_ref[0]


def _adaln(c_all, w_ada, b_ada):
    nl = w_ada.shape[0]
    r = c_all.shape[0]
    tn = D
    return pl.pallas_call(
        _mod_kernel,
        grid=(nl, 6 * D // tn),
        in_specs=[
            pl.BlockSpec((r, D), lambda l, j: (0, 0)),
            pl.BlockSpec((1, D, tn), lambda l, j: (l, 0, j)),
            pl.BlockSpec((1, 1, tn), lambda l, j: (l, 0, j)),
        ],
        out_specs=pl.BlockSpec((1, r, tn), lambda l, j: (l, 0, j)),
        out_shape=jax.ShapeDtypeStruct((nl, r, 6 * D), F32),
        compiler_params=pltpu.CompilerParams(
            dimension_semantics=("arbitrary", "arbitrary"), vmem_limit_bytes=VMEM_LIMIT),
        name="adaln_mod",
    )(c_all, w_ada, b_ada.reshape(nl, 1, 6 * D))


def _group_norm_silu(a, gavg_ref, g, b):
    mu = _dot(a.astype(BF16), gavg_ref[...])
    xc = a - mu
    var = _dot((xc * xc).astype(BF16), gavg_ref[...])
    y = xc * lax.rsqrt(var + EPS) * g + b
    return _silu(y)


def _layer_norm(x, g, b):
    mu = jnp.mean(x, axis=-1, keepdims=True)
    xc = x - mu
    var = jnp.mean(xc * xc, axis=-1, keepdims=True)
    return xc * lax.rsqrt(var + EPS) * g + b


def _merge_and_route(x, a, bb, cc, gates_fn, mod, n2g, wbr_ref, wout_ref, wr_ref, br_ref,
                     x1_ref, h2_ref, route_ref):
    g1, sh2, sc2 = mod
    acc = None
    for gi, br_in in enumerate((a, bb, cc)):
        br = _dot(br_in.astype(BF16), wbr_ref[gi])
        term = _sigmoid(gates_fn(gi)) * br
        acc = term if acc is None else acc + term
    mixed = _dot(acc.astype(BF16), wout_ref[...])
    x1 = x + g1 * mixed
    x1_ref[...] = x1
    h2 = _rms(x1, n2g) * (1.0 + sc2) + sh2
    _store_row_tiles(h2_ref, h2)

    logits = _dot(h2.astype(BF16), wr_ref[...]) + br_ref[...]
    rows = logits.shape[0]
    lane = lax.broadcasted_iota(jnp.int32, (rows, ROUTE_COLS), 1).astype(F32)
    is_g = lane < NGRP
    gl = jnp.where(is_g, logits, NEG)
    gmax = jnp.max(gl, axis=-1, keepdims=True)
    gsel = jnp.min(jnp.where(gl == gmax, lane, float(ROUTE_COLS)), axis=-1, keepdims=True)
    gden = jnp.sum(jnp.where(is_g, jnp.exp(gl - gmax), 0.0), axis=-1, keepdims=True)
    gprob = 1.0 / gden
    lo = NGRP + EPG * gsel
    in_grp = jnp.logical_and(lane >= lo, lane < lo + EPG)
    el = jnp.where(in_grp, logits, NEG)
    v1 = jnp.max(el, axis=-1, keepdims=True)
    i1 = jnp.min(jnp.where(el == v1, lane, float(ROUTE_COLS)), axis=-1, keepdims=True)
    el2 = jnp.where(lane == i1, NEG, el)
    v2 = jnp.max(el2, axis=-1, keepdims=True)
    i2 = jnp.min(jnp.where(el2 == v2, lane, float(ROUTE_COLS)), axis=-1, keepdims=True)
    p2 = jnp.exp(v2 - v1)
    w_a = gprob / (1.0 + p2)
    w_b = gprob * p2 / (1.0 + p2)
    route = jnp.where(lane == 0.0, i1 - NGRP,
                      jnp.where(lane == 1.0, i2 - NGRP,
                                jnp.where(lane == 2.0, w_a,
                                          jnp.where(lane == 3.0, w_b, 0.0))))
    route_ref[...] = route


def _mixer_prompt_kernel(x_ref, mod_ref, n1g_ref, n2g_ref, win_ref, adw_ref, adb_ref, ang_ref,
                         anb_ref, bng_ref, bnb_ref, wsp_ref, bsp_ref, ccw_ref, wbr_ref, wout_ref,
                         gavg_ref, wr_ref, br_ref,
                         x1_ref, h2_ref, route_ref, ha_ref, hc_ref,
                         abuf, cbuf, aconv):
    first = pl.program_id(1) == 0
    abuf[0:A_HDR, :] = jnp.where(first, 0.0, abuf[TT:TT + A_HDR, :])
    abuf[A_HDR + TT:A_HDR + TT + A_PAD, :] = jnp.zeros((A_PAD, W), F32)
    cbuf[0:C_HDR, :] = jnp.where(first, 0.0, cbuf[TT:TT + C_HDR, :])

    x = x_ref[...]
    sh1 = mod_ref[0, 0:1, :]
    sc1 = mod_ref[0, 1:2, :]
    g1 = mod_ref[0, 2:3, :]
    sh2 = mod_ref[0, 3:4, :]
    sc2 = mod_ref[0, 4:5, :]
    h = (_rms(x, n1g_ref[...]) * (1.0 + sc1) + sh1).astype(BF16)

    def zsec(lo, hi):
        return _dot(h, win_ref[:, lo:hi])

    a_glu = zsec(0, W) * _sigmoid(zsec(W, 2 * W))
    abuf[A_HDR:A_HDR + TT, :] = a_glu
    off = A_HDR - (KA - 1)
    rc = A_ROWS

    def conv_lane_tile(lt):
        ls = slice(lt * LANES, (lt + 1) * LANES)
        for r0 in range(0, TT, rc):
            y = None
            for r in range(SUBLANES):
                z = None
                for q in range((KA - 1 + off) // SUBLANES + 1):
                    k = SUBLANES * q + r - off
                    if 0 <= k < KA:
                        rows = slice(r0 + SUBLANES * q, r0 + SUBLANES * q + rc + SUBLANES)
                        term = adw_ref[k:k + 1, ls] * abuf[rows, ls]
                        z = term if z is None else z + term
                part = z[r:r + rc, :]
                y = part if y is None else y + part
            aconv[r0:r0 + rc, ls] = y + adb_ref[:, ls]

    conv_lane_tile(0)
    u = _gelu_tanh(zsec(2 * W, 3 * W))
    conv_lane_tile(1)
    v = _layer_norm(_gelu_tanh(zsec(3 * W, 4 * W)), bng_ref[...], bnb_ref[...])
    conv_lane_tile(2)

    cch = zsec(5 * W, 6 * W) * zsec(6 * W, 7 * W)
    cbuf[C_HDR:C_HDR + TT, :] = cch
    coff = C_HDR - (KC - 1)
    ych = (ccw_ref[0:1, :] * cbuf[coff:coff + TT, :]
           + ccw_ref[1:2, :] * cbuf[coff + 1:coff + 1 + TT, :]
           + ccw_ref[2:3, :] * cbuf[coff + 2:coff + 2 + TT, :])
    cc = zsec(4 * W, 5 * W) * ych
    hc_ref[0] = cbuf[C_HDR + TT - (KC - 1):C_HDR + TT, :]
    conv_lane_tile(3)
    ha_ref[0] = abuf[A_HDR + TT - (KA - 1):A_HDR + TT, :]

    vb = v.astype(BF16)
    low_head = lax.broadcasted_iota(jnp.int32, (CHUNK, LANES), 1) < (W // HEADS)
    zero_b = jnp.zeros((CHUNK, LANES), BF16)
    s_parts = []
    for c in range(TT // CHUNK):
        tiles = []
        for lt in range(W // LANES):
            vt = vb[c * CHUNK:(c + 1) * CHUNK, lt * LANES:(lt + 1) * LANES]
            stacked = jnp.concatenate([jnp.where(low_head, vt, zero_b),
                                       jnp.where(low_head, zero_b, vt)], axis=0)
            tiles.append(_dot(wsp_ref[lt], stacked))
        s_parts.append(jnp.concatenate(tiles, axis=1) + bsp_ref[...])
    bb = u * jnp.concatenate(s_parts, axis=0)
    a = _group_norm_silu(aconv[...], gavg_ref, ang_ref[...], anb_ref[...])

    def gates_fn(gi):
        return zsec(7 * W + gi * D, 7 * W + (gi + 1) * D)

    _merge_and_route(x, a, bb, cc, gates_fn, (g1, sh2, sc2), n2g_ref[...], wbr_ref, wout_ref,
                     wr_ref, br_ref, x1_ref, h2_ref, route_ref)


def _const_spec(shape):
    nd = len(shape)
    return pl.BlockSpec(shape, lambda *_: (0,) * nd, pipeline_mode=pl.Buffered(1))


def _mixer_prompt(x, mod, lw):
    n = mod.shape[0]
    t = x.shape[0] // n
    nt = t // TT
    row_spec = lambda cols: pl.BlockSpec((TT, cols), lambda i, j: (i * nt + j, 0))
    consts = [lw["n1g"], lw["n2g"], lw["w_in"], lw["a_dw_w"], lw["a_dw_b"], lw["a_norm_g"],
              lw["a_norm_b"], lw["b_norm_g"], lw["b_norm_b"], lw["w_sp"], lw["b_sp"], lw["c_conv_w"],
              lw["w_branch"], lw["w_out"], lw["gavg"], lw["w_router"], lw["b_router"]]
    return pl.pallas_call(
        _mixer_prompt_kernel,
        grid=(n, nt),
        in_specs=[row_spec(D), pl.BlockSpec((1, 6, D), lambda i, j: (i, 0, 0))]
        + [_const_spec(c.shape) for c in consts],
        out_specs=[row_spec(D),
                   pl.BlockSpec((TT * ROW_TILE, LANES), lambda i, j: (i * nt + j, 0)),
                   row_spec(ROUTE_COLS),
                   pl.BlockSpec((1, KA - 1, W), lambda i, j: (i, 0, 0)),
                   pl.BlockSpec((1, KC - 1, W), lambda i, j: (i, 0, 0))],
        out_shape=[jax.ShapeDtypeStruct((n * t, D), F32),
                   jax.ShapeDtypeStruct((n * t * ROW_TILE, LANES), F32),
                   jax.ShapeDtypeStruct((n * t, ROUTE_COLS), F32),
                   jax.ShapeDtypeStruct((n, KA - 1, W), F32),
                   jax.ShapeDtypeStruct((n, KC - 1, W), F32)],
        scratch_shapes=[pltpu.VMEM((A_HDR + TT + A_PAD, W), F32),
                        pltpu.VMEM((C_HDR + TT, W), F32),
                        pltpu.VMEM((TT, W), F32)],
        compiler_params=pltpu.CompilerParams(
            dimension_semantics=("arbitrary", "arbitrary"), vmem_limit_bytes=VMEM_LIMIT),
        name="mixer_prompt",
    )(x, mod, *consts)


def _mixer_sample_kernel(x_ref, mod_ref, hista_ref, histc_ref, n1g_ref, n2g_ref, win_ref, adw_ref,
                         adb_ref, ang_ref, anb_ref, bng_ref, bnb_ref, wsp4_ref, bsp4_ref,
                         ccw_ref, wbr_ref, wout_ref, gavg_ref, wr_ref, br_ref,
                         x1_ref, h2_ref, route_ref, ha_ref, hc_ref, v_ref):
    ts, nb = x_ref.shape[0], x_ref.shape[1]

    def rows(fn):
        return jnp.concatenate([fn(j) for j in range(ts)], axis=0)

    x = rows(lambda j: x_ref[j])

    def modrow(i):
        return jnp.concatenate([mod_ref[i]] * ts, axis=0)

    h = (_rms(x, n1g_ref[...]) * (1.0 + modrow(1)) + modrow(0)).astype(BF16)

    def zsec(lo, hi):
        return _dot(h, win_ref[:, lo:hi])

    def tslab(arr, j):
        return arr[j * nb:(j + 1) * nb, :]

    a_glu = zsec(0, W) * _sigmoid(zsec(W, 2 * W))
    kh = KA - 1
    a_conv = []
    for tq in range(ts):
        acc = jnp.zeros((nb, W), F32) + adb_ref[...]
        for r in range(tq, kh):
            acc = acc + adw_ref[r - tq:r - tq + 1, :] * hista_ref[r]
        for j in range(tq + 1):
            acc = acc + adw_ref[kh + j - tq:kh + j - tq + 1, :] * tslab(a_glu, j)
        a_conv.append(acc)
    a = _group_norm_silu(jnp.concatenate(a_conv, axis=0), gavg_ref, ang_ref[...], anb_ref[...])
    for r in range(kh - ts):
        ha_ref[r] = hista_ref[r + ts]
    for j in range(ts):
        ha_ref[kh - ts + j] = tslab(a_glu, j)

    u = _gelu_tanh(zsec(2 * W, 3 * W))
    v = _layer_norm(_gelu_tanh(zsec(3 * W, 4 * W)), bng_ref[...], bnb_ref[...])
    for j in range(ts):
        v_ref[j] = tslab(v, j)
    s_rows = []
    for tq in range(ts):
        s = jnp.zeros((nb, W), F32) + bsp4_ref[tq:tq + 1, :]
        for sq in range(tq + 1):
            s = s + wsp4_ref[tq * ts + sq:tq * ts + sq + 1, :] * tslab(v, sq)
        s_rows.append(s)
    bb = u * jnp.concatenate(s_rows, axis=0)

    cch = zsec(5 * W, 6 * W) * zsec(6 * W, 7 * W)
    xp = [histc_ref[r] for r in range(KC - 1)] + [tslab(cch, j) for j in range(ts)]
    ych = jnp.concatenate(
        [sum(ccw_ref[k:k + 1, :] * xp[tq + k] for k in range(KC)) for tq in range(ts)], axis=0)
    cc = zsec(4 * W, 5 * W) * ych
    for r in range(KC - 1):
        hc_ref[r] = xp[ts + r]

    def gates_fn(gi):
        return zsec(7 * W + gi * D, 7 * W + (gi + 1) * D)

    _merge_and_route(x, a, bb, cc, gates_fn, (modrow(2), modrow(3), modrow(4)), n2g_ref[...],
                     wbr_ref, wout_ref, wr_ref, br_ref, x1_ref, h2_ref, route_ref)


def _mixer_sample(x_tm, mod_tm, hist_a_tm, hist_c_tm, lw, nsplit=2):
    ts, n, _ = x_tm.shape
    nb = n // nsplit
    rows = ts * nb
    consts = [lw["n1g"], lw["n2g"], lw["w_in"], lw["a_dw_w"], lw["a_dw_b"],
              lw["a_norm_g"], lw["a_norm_b"], lw["b_norm_g"], lw["b_norm_b"], lw["w_sp4"], lw["b_sp4"],
              lw["c_conv_w"], lw["w_branch"], lw["w_out"], lw["gavg"], lw["w_router"], lw["b_router"]]
    seq3 = lambda k, cols: pl.BlockSpec((k, nb, cols), lambda i: (0, i, 0))
    row_spec = lambda cols: pl.BlockSpec((rows, cols), lambda i: (i, 0))
    return pl.pallas_call(
        _mixer_sample_kernel,
        grid=(nsplit,),
        in_specs=[seq3(ts, D), seq3(6, D), seq3(KA - 1, W), seq3(KC - 1, W)]
        + [_const_spec(c.shape) for c in consts],
        out_specs=[row_spec(D),
                   pl.BlockSpec((rows * ROW_TILE, LANES), lambda i: (i, 0)),
                   row_spec(ROUTE_COLS),
                   seq3(KA - 1, W), seq3(KC - 1, W), seq3(ts, W)],
        out_shape=[jax.ShapeDtypeStruct((ts * n, D), F32),
                   jax.ShapeDtypeStruct((ts * n * ROW_TILE, LANES), F32),
                   jax.ShapeDtypeStruct((ts * n, ROUTE_COLS), F32),
                   jax.ShapeDtypeStruct((KA - 1, n, W), F32),
                   jax.ShapeDtypeStruct((KC - 1, n, W), F32),
                   jax.ShapeDtypeStruct((ts, n, W), F32)],
        compiler_params=pltpu.CompilerParams(
            dimension_semantics=("arbitrary",), vmem_limit_bytes=VMEM_LIMIT),
        name="mixer_sample",
    )(x_tm, mod_tm, hist_a_tm, hist_c_tm, *consts)


def _plan_kernel(rp_ref, rs_ref, tri_ref, dest_ref, tbl_ref, carry, pstart, *, ntp):
    ph = pl.program_id(0)
    i = pl.program_id(1)
    nt = pl.num_programs(1)
    lane = lax.broadcasted_iota(jnp.int32, (TP, ROUTE_COLS), 1).astype(F32)

    @pl.when(jnp.logical_and(ph == 0, i == 0))
    def _():
        carry[...] = jnp.zeros_like(carry)

    r = jnp.where(i < ntp, rp_ref[...], rs_ref[...])
    e_a = r[:, 0:1]
    e_b = r[:, 1:2]
    hot_a = lane == e_a
    hot_b = lane == e_b
    s = jnp.where(jnp.logical_or(hot_a, hot_b), 1.0, 0.0)
    colsum = jnp.sum(s, axis=0, keepdims=True)

    @pl.when(ph == 0)
    def _():
        carry[...] = carry[...] + colsum

    @pl.when(jnp.logical_and(ph == 0, i == nt - 1))
    def _():
        cnt = carry[...]
        nblk = jnp.floor((cnt + (BM - 1)) * (1.0 / BM))
        ri = lax.broadcasted_iota(jnp.int32, (ROUTE_COLS, ROUTE_COLS), 0)
        ci = lax.broadcasted_iota(jnp.int32, (ROUTE_COLS, ROUTE_COLS), 1)
        upper = jnp.where(ri <= ci, 1.0, 0.0)
        cum = _dot_hi(jnp.broadcast_to(nblk, (8, ROUTE_COLS)), upper)[0:1, :]
        cum_ex = cum - nblk
        pstart[...] = cum_ex * BM
        nbp = tbl_ref.shape[0]
        b = lax.broadcasted_iota(jnp.int32, (nbp, ROUTE_COLS), 0).astype(F32)
        ln = lax.broadcasted_iota(jnp.int32, (nbp, ROUTE_COLS), 1).astype(F32)
        is_e = ln < NEXP
        n_used = jnp.max(jnp.where(is_e, cum, 0.0), axis=-1, keepdims=True)
        be = jnp.sum(jnp.where(jnp.logical_and(is_e, cum <= b), 1.0, 0.0), axis=-1, keepdims=True)
        be = jnp.minimum(be, NEXP - 1.0)
        hot = ln == be
        cnt_b = jnp.sum(jnp.where(hot, cnt, 0.0), axis=-1, keepdims=True)
        cex_b = jnp.sum(jnp.where(hot, cum_ex, 0.0), axis=-1, keepdims=True)
        b0 = b[:, 0:1]
        used = b0 < n_used
        nvalid = jnp.where(used, jnp.clip(cnt_b - (b0 - cex_b) * BM, 0.0, float(BM)), 0.0)
        first = jnp.where(jnp.logical_and(used, b0 == cex_b), 1.0, 0.0)
        rowblk = jnp.minimum(b0, n_used - 1.0)
        last_e = jnp.sum(jnp.where(jnp.logical_and(is_e, cum <= n_used - 1.0), 1.0, 0.0),
                         axis=-1, keepdims=True)
        be = jnp.where(used, be, jnp.minimum(last_e, NEXP - 1.0))
        tbl = jnp.where(ln == 0.0, be,
                        jnp.where(ln == 1.0, nvalid,
                                  jnp.where(ln == 2.0, first,
                                            jnp.where(ln == 3.0, rowblk, 0.0))))
        tbl_ref[...] = tbl.astype(jnp.int32)
        carry[...] = jnp.zeros_like(carry)

    @pl.when(ph == 1)
    def _():
        pre = _dot(tri_ref[...], s.astype(BF16)) + carry[...] + pstart[...]
        d_a = jnp.sum(jnp.where(hot_a, pre, 0.0), axis=-1, keepdims=True)
        d_b = jnp.sum(jnp.where(hot_b, pre, 0.0), axis=-1, keepdims=True)
        l2 = lax.broadcasted_iota(jnp.int32, (TP, 2), 1)
        dest_ref[...] = jnp.where(l2 == 0, d_a, d_b).astype(jnp.int32)
        carry[...] = carry[...] + colsum


def _plan(route_p, route_s, nblocks):
    ntp = route_p.shape[0] // TP
    nts = route_s.shape[0] // TP
    nt = ntp + nts
    nbp = -(-nblocks // 8) * 8
    ri = lax.broadcasted_iota(jnp.int32, (TP, TP), 0)
    ci = lax.broadcasted_iota(jnp.int32, (TP, TP), 1)
    tri = jnp.where(ci < ri, 1.0, 0.0).astype(BF16)
    dest, tbl = pl.pallas_call(
        functools.partial(_plan_kernel, ntp=ntp),
        grid=(2, nt),
        in_specs=[pl.BlockSpec((TP, ROUTE_COLS), lambda p, i: (jnp.minimum(i, ntp - 1), 0)),
                  pl.BlockSpec((TP, ROUTE_COLS), lambda p, i: (jnp.maximum(i - ntp, 0), 0)),
                  pl.BlockSpec((TP, TP), lambda p, i: (0, 0))],
        out_specs=[pl.BlockSpec((TP, 2), lambda p, i: (i * p, 0)),
                   pl.BlockSpec((nbp, ROUTE_COLS), lambda p, i: (0, 0))],
        out_shape=[jax.ShapeDtypeStruct((nt * TP, 2), jnp.int32),
                   jax.ShapeDtypeStruct((nbp, ROUTE_COLS), jnp.int32)],
        scratch_shapes=[pltpu.VMEM((1, ROUTE_COLS), F32), pltpu.VMEM((1, ROUTE_COLS), F32)],
        compiler_params=pltpu.CompilerParams(
            dimension_semantics=("arbitrary", "arbitrary"), vmem_limit_bytes=VMEM_LIMIT),
        name="moe_plan",
    )(route_p, route_s, tri)
    return dest, tbl


def _scatter_kernel(dest_ref, h_ref, *rest):
    rows_ref, sem = rest[-2:]
    tp = h_ref.shape[0] // ROW_TILE

    def row_copy(r, k):
        d = dest_ref[2 * r + k]
        return pltpu.make_async_copy(_row_tile(h_ref, r), _row_tile(rows_ref, d), sem)

    def issue(r, c):
        row_copy(r, 0).start(priority=0)
        row_copy(r, 1).start(priority=1)
        return c

    lax.fori_loop(0, tp, issue, 0, unroll=8)

    def drain(r, c):
        row_copy(r, 0).wait()
        row_copy(r, 1).wait()
        return c

    lax.fori_loop(0, tp, drain, 0, unroll=8)


def _scatter_rows(dest_flat, h2, rows_buf, n_rows, tp, tile0):
    nt = h2.shape[0] // (tp * ROW_TILE)
    in_specs = [pl.BlockSpec((2 * tp,), lambda i: (i + tile0,), memory_space=pltpu.SMEM),
                pl.BlockSpec((tp * ROW_TILE, LANES), lambda i: (i, 0))]
    args = [dest_flat, h2]
    aliases = {}
    if rows_buf is not None:
        in_specs.append(pl.BlockSpec(memory_space=pl.ANY))
        args.append(rows_buf)
        aliases = {2: 0}
    return pl.pallas_call(
        _scatter_kernel,
        grid=(nt,),
        in_specs=in_specs,
        out_specs=pl.BlockSpec(memory_space=pl.ANY),
        out_shape=jax.ShapeDtypeStruct((n_rows * ROW_TILE, LANES), F32),
        scratch_shapes=[pltpu.SemaphoreType.DMA(())],
        input_output_aliases=aliases,
        compiler_params=pltpu.CompilerParams(
            dimension_semantics=("arbitrary",), vmem_limit_bytes=VMEM_LIMIT,
            has_side_effects=True),
        name="moe_scatter",
    )(*args)


def _expert_kernel(be_ref, nv_ref, first_ref, rb_ref, x_ref, w1_ref, w3_ref, w2_ref, y_ref,
                   w1b, w3b, w2b):
    del be_ref, rb_ref
    b = pl.program_id(0)

    @pl.when(first_ref[b] == 1)
    def _():
        w1b[...] = w1_ref[0, 0].astype(BF16)
        w3b[...] = w3_ref[0, 0].astype(BF16)
        w2b[...] = w2_ref[0, 0].astype(BF16)

    nv = nv_ref[b]

    @pl.when(nv > 0)
    def _():
        row = lax.broadcasted_iota(jnp.int32, (BM, D), 0)
        x = jnp.where(row < nv, _load_row_tiles(x_ref), 0.0).astype(BF16)
        h1 = _dot(x, w1b[...])
        h3 = _dot(x, w3b[...])
        act = (_silu(h1) * h3).astype(BF16)
        _store_row_tiles(y_ref, _dot(act, w2b[...]))


def _experts(tbl, rows_buf, w1, w3, w2, layer, nblocks):
    be, nv, first, rb = (tbl[:nblocks, k] for k in range(4))
    wspec = lambda shape: pl.BlockSpec((1, 1) + shape,
                                       lambda b, be, nv, fi, rb: (layer, be[b], 0, 0))
    xspec = pl.BlockSpec((BM * ROW_TILE, LANES), lambda b, be, nv, fi, rb: (rb[b], 0))
    return pl.pallas_call(
        _expert_kernel,
        grid_spec=pltpu.PrefetchScalarGridSpec(
            num_scalar_prefetch=4,
            grid=(nblocks,),
            in_specs=[xspec, wspec((D, FF)), wspec((D, FF)), wspec((FF, D))],
            out_specs=xspec,
            scratch_shapes=[pltpu.VMEM((D, FF), BF16), pltpu.VMEM((D, FF), BF16),
                            pltpu.VMEM((FF, D), BF16)],
        ),
        out_shape=jax.ShapeDtypeStruct(rows_buf.shape, F32),
        compiler_params=pltpu.CompilerParams(
            dimension_semantics=("arbitrary",), vmem_limit_bytes=VMEM_LIMIT),
        name="moe_experts",
    )(be, nv, first, rb, rows_buf, w1, w3, w2)


def _combine_kernel(dest_ref, x1_ref, route_ref, g2_ref, fng_ref, y_ref, o_ref, ybuf, sem,
                    *, final_norm, g2_rep):
    tp = x1_ref.shape[0]

    def row_copy(r, k):
        d = dest_ref[2 * r + k]
        return pltpu.make_async_copy(_row_tile(y_ref, d), _row_tile(ybuf.at[k], r), sem)

    def issue(r, c):
        row_copy(r, 0).start(priority=0)
        row_copy(r, 1).start(priority=1)
        return c

    lax.fori_loop(0, tp, issue, 0, unroll=8)

    def drain(r, c):
        row_copy(r, 0).wait()
        row_copy(r, 1).wait()
        return c

    lax.fori_loop(0, tp, drain, 0, unroll=8)

    route = route_ref[...]
    w_a = route[:, 2:3]
    w_b = route[:, 3:4]
    g2 = g2_ref[0]
    if g2_rep > 1:
        g2 = jnp.concatenate([g2] * g2_rep, axis=0)
    x2 = x1_ref[...] + g2 * (w_a * _load_row_tiles(ybuf.at[0]) + w_b * _load_row_tiles(ybuf.at[1]))
    if final_norm:
        x2 = _rms(x2, fng_ref[...])
    o_ref[...] = x2


def _combine(dest_flat, x1, route, g2, fng, y_rows, tp, tile0, *, final_norm, tiles_per_g2=1):
    nt = x1.shape[0] // tp
    r = g2.shape[1]
    if r == 1:
        g2_spec = pl.BlockSpec((1, 1, D), lambda i: (i // tiles_per_g2, 0, 0))
        rep = 1
    else:
        g2_spec = pl.BlockSpec((1, r, D), lambda i: (i, 0, 0))
        rep = tp // r
    return pl.pallas_call(
        functools.partial(_combine_kernel, final_norm=final_norm, g2_rep=rep),
        grid=(nt,),
        in_specs=[pl.BlockSpec((2 * tp,), lambda i: (i + tile0,), memory_space=pltpu.SMEM),
                  pl.BlockSpec((tp, D), lambda i: (i, 0)),
                  pl.BlockSpec((tp, ROUTE_COLS), lambda i: (i, 0)),
                  g2_spec,
                  pl.BlockSpec((1, D), lambda i: (0, 0)),
                  pl.BlockSpec(memory_space=pl.ANY)],
        out_specs=pl.BlockSpec((tp, D), lambda i: (i, 0)),
        out_shape=jax.ShapeDtypeStruct(x1.shape, F32),
        scratch_shapes=[pltpu.VMEM((2, tp * ROW_TILE, LANES), F32), pltpu.SemaphoreType.DMA(())],
        compiler_params=pltpu.CompilerParams(
            dimension_semantics=("arbitrary",), vmem_limit_bytes=VMEM_LIMIT),
        name="moe_combine",
    )(dest_flat, x1, route, g2, fng, y_rows)


def _layer_weights(l, p):
    hd = W // HEADS
    gi = jnp.arange(W) // (W // NG_A)
    gavg = jnp.where(gi[:, None] == gi[None, :], 1.0 / (W // NG_A), 0.0).astype(BF16)
    tril = jnp.tril(jnp.ones((CHUNK, CHUNK), dtype=bool))
    w_sp = jnp.where(tril[None], p["b_spatial_w"][l], 0.0)
    w_router = jnp.concatenate(
        [p["w_router_g"][l], p["w_router_e"][l],
         jnp.zeros((D, ROUTE_COLS - NGRP - NEXP), F32)], axis=1)
    b_router = jnp.concatenate(
        [p["b_router_g"][l], p["b_router_e"][l],
         jnp.zeros((ROUTE_COLS - NGRP - NEXP,), F32)])[None, :]
    return {
        "n1g": p["norm1_g"][l][None, :], "n2g": p["norm2_g"][l][None, :],
        "w_in": p["w_in"][l].astype(BF16),
        "a_dw_w": p["a_dw_w"][l], "a_dw_b": p["a_dw_b"][l][None, :],
        "a_norm_g": p["a_norm_g"][l][None, :], "a_norm_b": p["a_norm_b"][l][None, :],
        "b_norm_g": p["b_norm_g"][l][None, :], "b_norm_b": p["b_norm_b"][l][None, :],
        "w_sp": jnp.concatenate([w_sp[0::2], w_sp[1::2]], axis=2).astype(BF16),
        "b_sp": jnp.repeat(p["b_spatial_b"][l].T, hd, axis=1),
        "c_conv_w": p["c_conv_w"][l],
        "w_branch": p["w_branch"][l].astype(BF16), "w_out": p["w_out"][l].astype(BF16),
        "gavg": gavg, "w_router": w_router.astype(BF16), "b_router": b_router,
    }


def _sample_spatial(l, p, ts):
    hd = W // HEADS
    w = p["b_spatial_w"][l][:, :ts, :ts]
    w = jnp.where(jnp.tril(jnp.ones((ts, ts), dtype=bool))[None], w, 0.0)
    w4 = jnp.repeat(jnp.transpose(w, (1, 2, 0)).reshape(ts * ts, HEADS), hd, axis=1)
    b4 = jnp.repeat(p["b_spatial_b"][l][:, :ts].T, hd, axis=1)
    return w4, b4


def kernel(x_prompt, x_sample, c_prompt, c_sample, state_conv_a, state_conv_c, norm1_g, norm2_g, w_ada, b_ada, w_in, a_dw_w, a_dw_b, a_norm_g, a_norm_b, b_norm_g, b_norm_b, b_spatial_w, b_spatial_b, c_conv_w, w_branch, w_out, w_router_g, b_router_g, w_router_e, b_router_e, w1, w3, w2, final_norm_g):
    p = dict(norm1_g=norm1_g, norm2_g=norm2_g, w_in=w_in, a_dw_w=a_dw_w, a_dw_b=a_dw_b,
             a_norm_g=a_norm_g, a_norm_b=a_norm_b, b_norm_g=b_norm_g, b_norm_b=b_norm_b,
             b_spatial_w=b_spatial_w, b_spatial_b=b_spatial_b, c_conv_w=c_conv_w,
             w_branch=w_branch, w_out=w_out, w_router_g=w_router_g, b_router_g=b_router_g,
             w_router_e=w_router_e, b_router_e=b_router_e)
    depth = w_in.shape[0]
    nb_, seq, _ = x_prompt.shape
    ns, ts, _ = x_sample.shape
    n_tok = nb_ * seq + ns * ts
    nblocks = -(-(2 * n_tok) // BM) + NEXP
    nsplit = 2
    assert seq % TT == 0 and TT % CHUNK == 0 and (nb_ * seq) % TP == 0 and ts * ns == TP

    mod = _adaln(jnp.concatenate([c_prompt, c_sample], axis=0), w_ada, b_ada)
    mod = mod.reshape(depth, nb_ + ns, 6, D)

    xp = x_prompt.reshape(nb_ * seq, D)
    xs = jnp.transpose(x_sample, (1, 0, 2))
    fng = final_norm_g[None, :]
    pa, pc, sa, sc, sv = [], [], [], [], []
    for l in range(depth):
        lw = _layer_weights(l, p)
        lw["w_sp4"], lw["b_sp4"] = _sample_spatial(l, p, ts)
        mod_p = mod[l, :nb_]
        mod_s = jnp.transpose(mod[l, nb_:], (1, 0, 2))

        x1p, h2p, route_p, ha, hc = _mixer_prompt(xp, mod_p, lw)
        pa.append(ha)
        pc.append(hc)
        x1s, h2s, route_s, ha_s, hc_s, v_s = _mixer_sample(
            xs, mod_s, jnp.transpose(state_conv_a[l], (1, 0, 2)),
            jnp.transpose(state_conv_c[l], (1, 0, 2)), lw, nsplit)
        sa.append(jnp.transpose(ha_s, (1, 0, 2)))
        sc.append(jnp.transpose(hc_s, (1, 0, 2)))
        sv.append(jnp.transpose(v_s, (1, 0, 2)))

        dest, tbl = _plan(route_p, route_s, nblocks)
        dest_flat = dest.reshape(-1)
        tps = ts * ns // nsplit
        rows_buf = _scatter_rows(dest_flat, h2p, None, nblocks * BM, TP, 0)
        rows_buf = _scatter_rows(dest_flat, h2s, rows_buf, nblocks * BM, tps, x1p.shape[0] // tps)
        y_rows = _experts(tbl, rows_buf, w1, w3, w2, l, nblocks)

        last = l == depth - 1
        g2p = mod_p[:, 5:6, :]
        g2s = mod[l, nb_:, 5, :].reshape(nsplit, ns // nsplit, D)
        xp = _combine(dest_flat, x1p, route_p, g2p, fng, y_rows, TP, 0,
                      final_norm=last, tiles_per_g2=seq // TP)
        xs_rows = _combine(dest_flat, x1s, route_s, g2s, fng, y_rows, tps, x1p.shape[0] // tps,
                           final_norm=last)
        xs = jnp.transpose(xs_rows.reshape(nsplit, ts, ns // nsplit, D), (1, 0, 2, 3)).reshape(ts, ns, D)

    y_prompt = xp.reshape(nb_, seq, D)
    y_sample = jnp.transpose(xs, (1, 0, 2))
    return (y_prompt, y_sample, jnp.stack(pa), jnp.stack(pc), jnp.stack(sa), jnp.stack(sc),
            jnp.stack(sv))
```

```python
import functools

import jax
import jax.numpy as jnp
from jax import lax
from jax.experimental import pallas as pl
from jax.experimental.pallas import tpu as pltpu

F32 = jnp.float32
BF16 = jnp.bfloat16

D = 1024
W = D // 2
KA = 31
KC = 3
NG_A = 8
CHUNK = 128
HEADS = 8
NGRP = 4
EPG = 8
NEXP = NGRP * EPG
FF = D // 2
EPS = 1e-6
IN_COLS = 7 * W + 3 * D
LANES = 128
ROUTE_COLS = LANES

TT = 512
SUBLANES = 8
A_HDR = 32
A_PAD = SUBLANES
A_ROWS = 64
C_HDR = 8
TP = 512
TPP = 2048
BM = 256
VMEM_LIMIT = 56 * 1024 * 1024

NEG = -1e30


def _sigmoid(x):
    return 1.0 / (1.0 + jnp.exp(-x))


def _silu(x):
    return x * _sigmoid(x)


def _gelu_tanh(x):
    return 0.5 * x * (1.0 + jnp.tanh(0.7978845608028654 * (x + 0.044715 * x * x * x)))


def _rms(x, g):
    return x * lax.rsqrt(jnp.mean(x * x, axis=-1, keepdims=True) + EPS) * g


def _dot(a, b):
    return jnp.dot(a, b, preferred_element_type=F32)


def _dot_hi(a, b):
    return jnp.dot(a, b, preferred_element_type=F32, precision=lax.Precision.HIGHEST)


ROW_TILE = D // LANES


def _store_row_tiles(ref, val):
    r = val.shape[0]
    for j in range(ROW_TILE):
        ref[pl.ds(j, r, stride=ROW_TILE), :] = val[:, j * LANES:(j + 1) * LANES]


def _load_row_tiles(ref):
    r = ref.shape[0] // ROW_TILE
    return jnp.concatenate([ref[pl.ds(j, r, stride=ROW_TILE), :] for j in range(ROW_TILE)], axis=1)


def _row_tile(ref, r):
    return ref.at[pl.ds(pl.multiple_of(r * ROW_TILE, ROW_TILE), ROW_TILE)]


def _mod_kernel(c_ref, w_ref, b_ref, o_ref):
    c = c_ref[...]
    s = _silu(c).astype(BF16)
    o_ref[0, 0] = _dot(s, w_ref[0].astype(BF16)) + b_ref[0]


def _adaln(c_all, w_ada, b_ada):
    nl = w_ada.shape[0]
    r = c_all.shape[0]
    return pl.pallas_call(
        _mod_kernel,
        grid=(nl, 6),
        in_specs=[
            pl.BlockSpec((r, D), lambda l, j: (0, 0)),
            pl.BlockSpec((1, D, D), lambda l, j: (l, 0, j)),
            pl.BlockSpec((1, 1, D), lambda l, j: (l, 0, j)),
        ],
        out_specs=pl.BlockSpec((1, 1, r, D), lambda l, j: (l, j, 0, 0)),
        out_shape=jax.ShapeDtypeStruct((nl, 6, r, D), F32),
        compiler_params=pltpu.CompilerParams(
            dimension_semantics=("arbitrary", "arbitrary"), vmem_limit_bytes=VMEM_LIMIT),
        name="adaln_mod",
    )(c_all, w_ada, b_ada.reshape(nl, 1, 6 * D))


def _group_norm_silu(a, gavg_ref, g, b):
    mu = _dot(a.astype(BF16), gavg_ref[...])
    xc = a - mu
    var = _dot((xc * xc).astype(BF16), gavg_ref[...])
    y = xc * lax.rsqrt(var + EPS) * g + b
    return _silu(y)


def _layer_norm(x, g, b):
    mu = jnp.mean(x, axis=-1, keepdims=True)
    xc = x - mu
    var = jnp.mean(xc * xc, axis=-1, keepdims=True)
    return xc * lax.rsqrt(var + EPS) * g + b


def _merge_and_route(x, a, bb, cc, gates_fn, mod, n2g, wbr_ref, wout_ref, wr_ref, br_ref,
                     x1_ref, h2_ref, route_ref):
    g1, sh2, sc2 = mod
    acc = None
    for gi, br_in in enumerate((a, bb, cc)):
        br = _dot(br_in.astype(BF16), wbr_ref[gi])
        term = _sigmoid(gates_fn(gi)) * br
        acc = term if acc is None else acc + term
    mixed = _dot(acc.astype(BF16), wout_ref[...])
    x1 = x + g1 * mixed
    x1_ref[...] = x1
    h2 = _rms(x1, n2g) * (1.0 + sc2) + sh2
    _store_row_tiles(h2_ref, h2)

    logits = _dot(h2.astype(BF16), wr_ref[...]) + br_ref[...]
    rows = logits.shape[0]
    lane = lax.broadcasted_iota(jnp.int32, (rows, ROUTE_COLS), 1).astype(F32)
    is_g = lane < NGRP
    gl = jnp.where(is_g, logits, NEG)
    gmax = jnp.max(gl, axis=-1, keepdims=True)
    gsel = jnp.min(jnp.where(gl == gmax, lane, float(ROUTE_COLS)), axis=-1, keepdims=True)
    gden = jnp.sum(jnp.where(is_g, jnp.exp(gl - gmax), 0.0), axis=-1, keepdims=True)
    gprob = 1.0 / gden
    lo = NGRP + EPG * gsel
    in_grp = jnp.logical_and(lane >= lo, lane < lo + EPG)
    el = jnp.where(in_grp, logits, NEG)
    v1 = jnp.max(el, axis=-1, keepdims=True)
    i1 = jnp.min(jnp.where(el == v1, lane, float(ROUTE_COLS)), axis=-1, keepdims=True)
    el2 = jnp.where(lane == i1, NEG, el)
    v2 = jnp.max(el2, axis=-1, keepdims=True)
    i2 = jnp.min(jnp.where(el2 == v2, lane, float(ROUTE_COLS)), axis=-1, keepdims=True)
    p2 = jnp.exp(v2 - v1)
    w_a = gprob / (1.0 + p2)
    w_b = gprob * p2 / (1.0 + p2)
    route = jnp.where(lane == 0.0, i1 - NGRP,
                      jnp.where(lane == 1.0, i2 - NGRP,
                                jnp.where(lane == 2.0, w_a,
                                          jnp.where(lane == 3.0, w_b, 0.0))))
    route_ref[...] = route


def _mixer_prompt_kernel(x_ref, mod_ref, n1g_ref, n2g_ref, win_ref, adw_ref, adb_ref, ang_ref,
                         anb_ref, bng_ref, bnb_ref, wsp_ref, bsp_ref, ccw_ref, wbr_ref, wout_ref,
                         gavg_ref, wr_ref, br_ref,
                         x1_ref, h2_ref, route_ref, ha_ref, hc_ref,
                         abuf, cbuf, aconv, *, mod_row0):
    first = pl.program_id(1) == 0
    abuf[0:A_HDR, :] = jnp.where(first, 0.0, abuf[TT:TT + A_HDR, :])
    abuf[A_HDR + TT:A_HDR + TT + A_PAD, :] = jnp.zeros((A_PAD, W), F32)
    cbuf[0:C_HDR, :] = jnp.where(first, 0.0, cbuf[TT:TT + C_HDR, :])

    x = x_ref[...]
    mrow = mod_row0 + pl.program_id(0)
    sh1, sc1, g1, sh2, sc2 = (mod_ref[0, s, pl.ds(mrow, 1), :] for s in range(5))
    h = (_rms(x, n1g_ref[...]) * (1.0 + sc1) + sh1).astype(BF16)

    def zsec(lo, hi):
        return _dot(h, win_ref[:, lo:hi])

    a_glu = zsec(0, W) * _sigmoid(zsec(W, 2 * W))
    abuf[A_HDR:A_HDR + TT, :] = a_glu
    off = A_HDR - (KA - 1)
    rc = A_ROWS

    def conv_pass(p):
        lt, half = p // 2, p % 2
        ls = slice(lt * LANES, (lt + 1) * LANES)
        for r0 in range(half * (TT // 2), (half + 1) * (TT // 2), rc):
            y = None
            for r in range(SUBLANES):
                z = None
                for q in range((KA - 1 + off) // SUBLANES + 1):
                    k = SUBLANES * q + r - off
                    if 0 <= k < KA:
                        rows = slice(r0 + SUBLANES * q, r0 + SUBLANES * q + rc + SUBLANES)
                        term = adw_ref[k:k + 1, ls] * abuf[rows, ls]
                        z = term if z is None else z + term
                part = z[r:r + rc, :]
                y = part if y is None else y + part
            aconv[r0:r0 + rc, ls] = y + adb_ref[:, ls]

    conv_pass(0)
    u = _gelu_tanh(zsec(2 * W, 3 * W))
    conv_pass(1)
    v = _layer_norm(_gelu_tanh(zsec(3 * W, 4 * W)), bng_ref[...], bnb_ref[...])
    conv_pass(2)

    cch = zsec(5 * W, 6 * W) * zsec(6 * W, 7 * W)
    cbuf[C_HDR:C_HDR + TT, :] = cch
    coff = C_HDR - (KC - 1)
    ych = (ccw_ref[0:1, :] * cbuf[coff:coff + TT, :]
           + ccw_ref[1:2, :] * cbuf[coff + 1:coff + 1 + TT, :]
           + ccw_ref[2:3, :] * cbuf[coff + 2:coff + 2 + TT, :])
    conv_pass(3)
    cc = zsec(4 * W, 5 * W) * ych
    hc_ref[0] = cbuf[C_HDR + TT - (KC - 1):C_HDR + TT, :]
    conv_pass(4)
    gate_z = [zsec(7 * W, 7 * W + D)]
    conv_pass(5)
    gate_z.append(zsec(7 * W + D, 7 * W + 2 * D))
    conv_pass(6)
    gate_z.append(zsec(7 * W + 2 * D, 7 * W + 3 * D))
    conv_pass(7)
    ha_ref[0] = abuf[A_HDR + TT - (KA - 1):A_HDR + TT, :]

    vb = v.astype(BF16)
    low_head = lax.broadcasted_iota(jnp.int32, (CHUNK, LANES), 1) < (W // HEADS)
    zero_b = jnp.zeros((CHUNK, LANES), BF16)
    s_parts = []
    for c in range(TT // CHUNK):
        tiles = []
        for lt in range(W // LANES):
            vt = vb[c * CHUNK:(c + 1) * CHUNK, lt * LANES:(lt + 1) * LANES]
            stacked = jnp.concatenate([jnp.where(low_head, vt, zero_b),
                                       jnp.where(low_head, zero_b, vt)], axis=0)
            tiles.append(_dot(wsp_ref[lt], stacked))
        s_parts.append(jnp.concatenate(tiles, axis=1) + bsp_ref[...])
    bb = u * jnp.concatenate(s_parts, axis=0)
    a = _group_norm_silu(aconv[...], gavg_ref, ang_ref[...], anb_ref[...])

    _merge_and_route(x, a, bb, cc, lambda gi: gate_z[gi], (g1, sh2, sc2), n2g_ref[...], wbr_ref,
                     wout_ref, wr_ref, br_ref, x1_ref, h2_ref, route_ref)


def _const_spec(shape):
    nd = len(shape)
    return pl.BlockSpec(shape, lambda *_: (0,) * nd, pipeline_mode=pl.Buffered(1))


def _mixer_prompt(x, mod, layer, n, mod_row0, lw):
    t = x.shape[0] // n
    nt = t // TT
    mod_rows = mod.shape[2]
    row_spec = lambda cols: pl.BlockSpec((TT, cols), lambda i, j: (i * nt + j, 0))
    consts = [lw["n1g"], lw["n2g"], lw["w_in"], lw["a_dw_w"], lw["a_dw_b"], lw["a_norm_g"],
              lw["a_norm_b"], lw["b_norm_g"], lw["b_norm_b"], lw["w_sp"], lw["b_sp"], lw["c_conv_w"],
              lw["w_branch"], lw["w_out"], lw["gavg"], lw["w_router"], lw["b_router"]]
    return pl.pallas_call(
        functools.partial(_mixer_prompt_kernel, mod_row0=mod_row0),
        grid=(n, nt),
        in_specs=[row_spec(D),
                  pl.BlockSpec((1, 6, mod_rows, D), lambda i, j: (layer, 0, 0, 0),
                               pipeline_mode=pl.Buffered(1))]
        + [_const_spec(c.shape) for c in consts],
        out_specs=[row_spec(D),
                   pl.BlockSpec((TT * ROW_TILE, LANES), lambda i, j: (i * nt + j, 0)),
                   row_spec(ROUTE_COLS),
                   pl.BlockSpec((1, KA - 1, W), lambda i, j: (i, 0, 0)),
                   pl.BlockSpec((1, KC - 1, W), lambda i, j: (i, 0, 0))],
        out_shape=[jax.ShapeDtypeStruct((n * t, D), F32),
                   jax.ShapeDtypeStruct((n * t * ROW_TILE, LANES), F32),
                   jax.ShapeDtypeStruct((n * t, ROUTE_COLS), F32),
                   jax.ShapeDtypeStruct((n, KA - 1, W), F32),
                   jax.ShapeDtypeStruct((n, KC - 1, W), F32)],
        scratch_shapes=[pltpu.VMEM((A_HDR + TT + A_PAD, W), F32),
                        pltpu.VMEM((C_HDR + TT, W), F32),
                        pltpu.VMEM((TT, W), F32)],
        compiler_params=pltpu.CompilerParams(
            dimension_semantics=("arbitrary", "arbitrary"), vmem_limit_bytes=VMEM_LIMIT),
        name="mixer_prompt",
    )(x, mod, *consts)


def _mixer_sample_kernel(x_ref, mod_ref, hista_ref, histc_ref, n1g_ref, n2g_ref, win_ref, adw_ref,
                         adb_ref, ang_ref, anb_ref, bng_ref, bnb_ref, wsp4_ref, bsp4_ref,
                         ccw_ref, wbr_ref, wout_ref, gavg_ref, wr_ref, br_ref,
                         x1_ref, h2_ref, route_ref, ha_ref, hc_ref, v_ref):
    ts, nb = v_ref.shape[0], v_ref.shape[1]
    x = x_ref[...]

    def modrow(i):
        return jnp.concatenate([mod_ref[0, i]] * ts, axis=0)

    h = (_rms(x, n1g_ref[...]) * (1.0 + modrow(1)) + modrow(0)).astype(BF16)

    def zsec(lo, hi):
        return _dot(h, win_ref[:, lo:hi])

    def tslab(arr, j):
        return arr[j * nb:(j + 1) * nb, :]

    a_glu = zsec(0, W) * _sigmoid(zsec(W, 2 * W))
    kh = KA - 1
    a_conv = []
    for tq in range(ts):
        acc = jnp.zeros((nb, W), F32) + adb_ref[...]
        for r in range(tq, kh):
            acc = acc + adw_ref[r - tq:r - tq + 1, :] * hista_ref[r]
        for j in range(tq + 1):
            acc = acc + adw_ref[kh + j - tq:kh + j - tq + 1, :] * tslab(a_glu, j)
        a_conv.append(acc)
    a = _group_norm_silu(jnp.concatenate(a_conv, axis=0), gavg_ref, ang_ref[...], anb_ref[...])
    for r in range(kh - ts):
        ha_ref[r] = hista_ref[r + ts]
    for j in range(ts):
        ha_ref[kh - ts + j] = tslab(a_glu, j)

    u = _gelu_tanh(zsec(2 * W, 3 * W))
    v = _layer_norm(_gelu_tanh(zsec(3 * W, 4 * W)), bng_ref[...], bnb_ref[...])
    for j in range(ts):
        v_ref[j] = tslab(v, j)
    s_rows = []
    for tq in range(ts):
        s = jnp.zeros((nb, W), F32) + bsp4_ref[tq:tq + 1, :]
        for sq in range(tq + 1):
            s = s + wsp4_ref[tq * ts + sq:tq * ts + sq + 1, :] * tslab(v, sq)
        s_rows.append(s)
    bb = u * jnp.concatenate(s_rows, axis=0)

    cch = zsec(5 * W, 6 * W) * zsec(6 * W, 7 * W)
    xp = [histc_ref[r] for r in range(KC - 1)] + [tslab(cch, j) for j in range(ts)]
    ych = jnp.concatenate(
        [sum(ccw_ref[k:k + 1, :] * xp[tq + k] for k in range(KC)) for tq in range(ts)], axis=0)
    cc = zsec(4 * W, 5 * W) * ych
    for r in range(KC - 1):
        hc_ref[r] = xp[ts + r]

    def gates_fn(gi):
        return zsec(7 * W + gi * D, 7 * W + (gi + 1) * D)

    _merge_and_route(x, a, bb, cc, gates_fn, (modrow(2), modrow(3), modrow(4)), n2g_ref[...],
                     wbr_ref, wout_ref, wr_ref, br_ref, x1_ref, h2_ref, route_ref)


def _mixer_sample(x_rows, mod, layer, hist_a_tm, hist_c_tm, lw, nsplit):
    n = hist_a_tm.shape[1]
    ts = x_rows.shape[0] // n
    nb = n // nsplit
    rows = ts * nb
    consts = [lw["n1g"], lw["n2g"], lw["w_in"], lw["a_dw_w"], lw["a_dw_b"],
              lw["a_norm_g"], lw["a_norm_b"], lw["b_norm_g"], lw["b_norm_b"], lw["w_sp4"], lw["b_sp4"],
              lw["c_conv_w"], lw["w_branch"], lw["w_out"], lw["gavg"], lw["w_router"], lw["b_router"]]
    seq3 = lambda k, cols: pl.BlockSpec((k, nb, cols), lambda i: (0, i, 0))
    row_spec = lambda cols: pl.BlockSpec((rows, cols), lambda i: (i, 0))
    return pl.pallas_call(
        _mixer_sample_kernel,
        grid=(nsplit,),
        in_specs=[row_spec(D), pl.BlockSpec((1, 6, nb, D), lambda i: (layer, 0, i, 0)),
                  seq3(KA - 1, W), seq3(KC - 1, W)]
        + [_const_spec(c.shape) for c in consts],
        out_specs=[row_spec(D),
                   pl.BlockSpec((rows * ROW_TILE, LANES), lambda i: (i, 0)),
                   row_spec(ROUTE_COLS),
                   seq3(KA - 1, W), seq3(KC - 1, W), seq3(ts, W)],
        out_shape=[jax.ShapeDtypeStruct((ts * n, D), F32),
                   jax.ShapeDtypeStruct((ts * n * ROW_TILE, LANES), F32),
                   jax.ShapeDtypeStruct((ts * n, ROUTE_COLS), F32),
                   jax.ShapeDtypeStruct((KA - 1, n, W), F32),
                   jax.ShapeDtypeStruct((KC - 1, n, W), F32),
                   jax.ShapeDtypeStruct((ts, n, W), F32)],
        compiler_params=pltpu.CompilerParams(
            dimension_semantics=("arbitrary",), vmem_limit_bytes=VMEM_LIMIT),
        name="mixer_sample",
    )(x_rows, mod, hist_a_tm, hist_c_tm, *consts)


def _plan_kernel(rp_ref, rs_ref, tri_ref, destp_ref, dests_ref, tbl_ref, carry, pstart, *, ntp):
    ph = pl.program_id(0)
    i = pl.program_id(1)
    nt = pl.num_programs(1)
    lane = lax.broadcasted_iota(jnp.int32, (TP, ROUTE_COLS), 1).astype(F32)

    @pl.when(jnp.logical_and(ph == 0, i == 0))
    def _():
        carry[...] = jnp.zeros_like(carry)

    def sub_tile(r, dest_ref, row0):
        hot_a = lane == r[:, 0:1]
        hot_b = lane == r[:, 1:2]
        s = jnp.where(jnp.logical_or(hot_a, hot_b), 1.0, 0.0)

        @pl.when(ph == 1)
        def _():
            pre = _dot(tri_ref[...], s.astype(BF16)) + carry[...] + pstart[...]
            d_a = jnp.sum(jnp.where(hot_a, pre, 0.0), axis=-1, keepdims=True)
            d_b = jnp.sum(jnp.where(hot_b, pre, 0.0), axis=-1, keepdims=True)
            l2 = lax.broadcasted_iota(jnp.int32, (TP, 2), 1)
            dest_ref[row0:row0 + TP, :] = jnp.where(l2 == 0, d_a, d_b).astype(jnp.int32)

        carry[...] = carry[...] + jnp.sum(s, axis=0, keepdims=True)

    @pl.when(i < ntp)
    def _():
        for sub in range(rp_ref.shape[0] // TP):
            sub_tile(rp_ref[sub * TP:(sub + 1) * TP, :], destp_ref, sub * TP)

    @pl.when(i == ntp)
    def _():
        for sub in range(rs_ref.shape[0] // TP):
            sub_tile(rs_ref[sub * TP:(sub + 1) * TP, :], dests_ref, sub * TP)

    @pl.when(jnp.logical_and(ph == 0, i == nt - 1))
    def _():
        cnt = carry[...]
        nblk = jnp.floor((cnt + (BM - 1)) * (1.0 / BM))
        ri = lax.broadcasted_iota(jnp.int32, (ROUTE_COLS, ROUTE_COLS), 0)
        ci = lax.broadcasted_iota(jnp.int32, (ROUTE_COLS, ROUTE_COLS), 1)
        upper = jnp.where(ri <= ci, 1.0, 0.0)
        cum = _dot_hi(jnp.broadcast_to(nblk, (8, ROUTE_COLS)), upper)[0:1, :]
        cum_ex = cum - nblk
        pstart[...] = cum_ex * BM
        nbp = tbl_ref.shape[0]
        b = lax.broadcasted_iota(jnp.int32, (nbp, ROUTE_COLS), 0).astype(F32)
        ln = lax.broadcasted_iota(jnp.int32, (nbp, ROUTE_COLS), 1).astype(F32)
        is_e = ln < NEXP
        n_used = jnp.max(jnp.where(is_e, cum, 0.0), axis=-1, keepdims=True)
        be = jnp.sum(jnp.where(jnp.logical_and(is_e, cum <= b), 1.0, 0.0), axis=-1, keepdims=True)
        be = jnp.minimum(be, NEXP - 1.0)
        hot = ln == be
        cnt_b = jnp.sum(jnp.where(hot, cnt, 0.0), axis=-1, keepdims=True)
        cex_b = jnp.sum(jnp.where(hot, cum_ex, 0.0), axis=-1, keepdims=True)
        b0 = b[:, 0:1]
        used = b0 < n_used
        nvalid = jnp.where(used, jnp.clip(cnt_b - (b0 - cex_b) * BM, 0.0, float(BM)), 0.0)
        first = jnp.where(jnp.logical_and(used, b0 == cex_b), 1.0, 0.0)
        rowblk = jnp.minimum(b0, n_used - 1.0)
        last_e = jnp.sum(jnp.where(jnp.logical_and(is_e, cum <= n_used - 1.0), 1.0, 0.0),
                         axis=-1, keepdims=True)
        be = jnp.where(used, be, jnp.minimum(last_e, NEXP - 1.0))
        tbl = jnp.where(ln == 0.0, be,
                        jnp.where(ln == 1.0, nvalid,
                                  jnp.where(ln == 2.0, first,
                                            jnp.where(ln == 3.0, rowblk, 0.0))))
        tbl_ref[...] = tbl.astype(jnp.int32)
        carry[...] = jnp.zeros_like(carry)


def _plan(route_p, route_s, nblocks):
    rows_p, rows_s = route_p.shape[0], route_s.shape[0]
    ntp = rows_p // TPP
    nbp = -(-nblocks // 8) * 8
    ri = lax.broadcasted_iota(jnp.int32, (TP, TP), 0)
    ci = lax.broadcasted_iota(jnp.int32, (TP, TP), 1)
    tri = jnp.where(ci < ri, 1.0, 0.0).astype(BF16)
    return pl.pallas_call(
        functools.partial(_plan_kernel, ntp=ntp),
        grid=(2, ntp + 1),
        in_specs=[pl.BlockSpec((TPP, ROUTE_COLS), lambda p, i: (jnp.minimum(i, ntp - 1), 0)),
                  pl.BlockSpec((rows_s, ROUTE_COLS), lambda p, i: (0, 0)),
                  pl.BlockSpec((TP, TP), lambda p, i: (0, 0))],
        out_specs=[pl.BlockSpec((TPP, 2), lambda p, i: (jnp.minimum(i, ntp - 1) * p, 0)),
                   pl.BlockSpec((rows_s, 2), lambda p, i: (0, 0)),
                   pl.BlockSpec((nbp, ROUTE_COLS), lambda p, i: (0, 0))],
        out_shape=[jax.ShapeDtypeStruct((rows_p, 2), jnp.int32),
                   jax.ShapeDtypeStruct((rows_s, 2), jnp.int32),
                   jax.ShapeDtypeStruct((nbp, ROUTE_COLS), jnp.int32)],
        scratch_shapes=[pltpu.VMEM((1, ROUTE_COLS), F32), pltpu.VMEM((1, ROUTE_COLS), F32)],
        compiler_params=pltpu.CompilerParams(
            dimension_semantics=("arbitrary", "arbitrary"), vmem_limit_bytes=VMEM_LIMIT),
        name="moe_plan",
    )(route_p, route_s, tri)


def _scatter_kernel(dest_ref, h_ref, *rest):
    rows_ref, sem = rest[-2:]
    tp = h_ref.shape[0] // ROW_TILE

    def row_copy(r, k):
        d = dest_ref[2 * r + k]
        return pltpu.make_async_copy(_row_tile(h_ref, r), _row_tile(rows_ref, d), sem)

    def issue(r, c):
        row_copy(r, 0).start(priority=0)
        row_copy(r, 1).start(priority=1)
        return c

    lax.fori_loop(0, tp, issue, 0, unroll=8)

    def drain(r, c):
        row_copy(r, 0).wait()
        row_copy(r, 1).wait()
        return c

    lax.fori_loop(0, tp, drain, 0, unroll=8)


def _scatter_rows(dest_flat, h2, rows_buf, n_rows, tp):
    nt = h2.shape[0] // (tp * ROW_TILE)
    in_specs = [pl.BlockSpec((2 * tp,), lambda i: (i,), memory_space=pltpu.SMEM),
                pl.BlockSpec((tp * ROW_TILE, LANES), lambda i: (i, 0))]
    args = [dest_flat, h2]
    aliases = {}
    if rows_buf is not None:
        in_specs.append(pl.BlockSpec(memory_space=pl.ANY))
        args.append(rows_buf)
        aliases = {2: 0}
    return pl.pallas_call(
        _scatter_kernel,
        grid=(nt,),
        in_specs=in_specs,
        out_specs=pl.BlockSpec(memory_space=pl.ANY),
        out_shape=jax.ShapeDtypeStruct((n_rows * ROW_TILE, LANES), F32),
        scratch_shapes=[pltpu.SemaphoreType.DMA(())],
        input_output_aliases=aliases,
        compiler_params=pltpu.CompilerParams(
            dimension_semantics=("arbitrary",), vmem_limit_bytes=VMEM_LIMIT,
            has_side_effects=True),
        name="moe_scatter",
    )(*args)


def _expert_kernel(be_ref, nv_ref, first_ref, rb_ref, x_ref, w1_ref, w3_ref, w2_ref, y_ref,
                   w1b, w3b, w2b):
    del be_ref, rb_ref
    b = pl.program_id(0)

    @pl.when(first_ref[b] == 1)
    def _():
        w1b[...] = w1_ref[0, 0].astype(BF16)
        w3b[...] = w3_ref[0, 0].astype(BF16)
        w2b[...] = w2_ref[0, 0].astype(BF16)

    nv = nv_ref[b]

    @pl.when(nv > 0)
    def _():
        row = lax.broadcasted_iota(jnp.int32, (BM, D), 0)
        x = jnp.where(row < nv, _load_row_tiles(x_ref), 0.0).astype(BF16)
        h1 = _dot(x, w1b[...])
        h3 = _dot(x, w3b[...])
        act = (_silu(h1) * h3).astype(BF16)
        _store_row_tiles(y_ref, _dot(act, w2b[...]))


def _experts(tbl, rows_buf, w1, w3, w2, layer, nblocks):
    be, nv, first, rb = (tbl[:nblocks, k] for k in range(4))
    wspec = lambda shape: pl.BlockSpec((1, 1) + shape,
                                       lambda b, be, nv, fi, rb: (layer, be[b], 0, 0))
    xspec = pl.BlockSpec((BM * ROW_TILE, LANES), lambda b, be, nv, fi, rb: (rb[b], 0))
    return pl.pallas_call(
        _expert_kernel,
        grid_spec=pltpu.PrefetchScalarGridSpec(
            num_scalar_prefetch=4,
            grid=(nblocks,),
            in_specs=[xspec, wspec((D, FF)), wspec((D, FF)), wspec((FF, D))],
            out_specs=xspec,
            scratch_shapes=[pltpu.VMEM((D, FF), BF16), pltpu.VMEM((D, FF), BF16),
                            pltpu.VMEM((FF, D), BF16)],
        ),
        out_shape=jax.ShapeDtypeStruct(rows_buf.shape, F32),
        compiler_params=pltpu.CompilerParams(
            dimension_semantics=("arbitrary",), vmem_limit_bytes=VMEM_LIMIT),
        name="moe_experts",
    )(be, nv, first, rb, rows_buf, w1, w3, w2)


def _combine_kernel(dest_ref, dest_next_ref, x1_ref, route_ref, g2_ref, fng_ref, y_ref, o_ref,
                    ybuf, sem, *, final_norm, g2_row0, tiles_per_g2):
    tp = x1_ref.shape[0]
    i = pl.program_id(0)
    n = pl.num_programs(0)
    slot = lax.rem(i, 2)

    def gather(d_ref, s):
        def row_copy(r, k):
            return pltpu.make_async_copy(_row_tile(y_ref, d_ref[2 * r + k]),
                                         _row_tile(ybuf.at[s, k], r), sem.at[s])

        def issue(r, c):
            row_copy(r, 0).start(priority=0)
            row_copy(r, 1).start(priority=1)
            return c

        lax.fori_loop(0, tp, issue, 0, unroll=8)

    @pl.when(i == 0)
    def _():
        gather(dest_ref, 0)

    @pl.when(i + 1 < n)
    def _():
        gather(dest_next_ref, 1 - slot)

    for k in range(2):
        pltpu.make_async_copy(y_ref.at[pl.ds(0, tp * ROW_TILE)], ybuf.at[slot, k], sem.at[slot]).wait()

    route = route_ref[...]
    w_a = route[:, 2:3]
    w_b = route[:, 3:4]
    if tiles_per_g2:
        g2 = g2_ref[0, 0, pl.ds(g2_row0 + i // tiles_per_g2, 1), :]
    else:
        g2 = jnp.concatenate([g2_ref[0, 0]] * (tp // g2_ref.shape[2]), axis=0)
    y = (w_a * _load_row_tiles(ybuf.at[slot, 0]) + w_b * _load_row_tiles(ybuf.at[slot, 1]))
    x2 = x1_ref[...] + g2 * y
    if final_norm:
        x2 = _rms(x2, fng_ref[...])
    o_ref[...] = x2


def _combine(dest_flat, x1, route, mod, layer, fng, y_rows, tp, *, final_norm, g2_row0=0,
             tiles_per_g2=0, seqs_per_tile=0):
    nt = x1.shape[0] // tp
    if tiles_per_g2:
        g2_spec = pl.BlockSpec((1, 1, mod.shape[2], D), lambda i: (layer, 5, 0, 0))
    else:
        g2_spec = pl.BlockSpec((1, 1, seqs_per_tile, D), lambda i: (layer, 5, i, 0))
    return pl.pallas_call(
        functools.partial(_combine_kernel, final_norm=final_norm, g2_row0=g2_row0,
                          tiles_per_g2=tiles_per_g2),
        grid=(nt,),
        in_specs=[pl.BlockSpec((2 * tp,), lambda i: (i,), memory_space=pltpu.SMEM),
                  pl.BlockSpec((2 * tp,), lambda i: (jnp.minimum(i + 1, nt - 1),),
                               memory_space=pltpu.SMEM),
                  pl.BlockSpec((tp, D), lambda i: (i, 0)),
                  pl.BlockSpec((tp, ROUTE_COLS), lambda i: (i, 0)),
                  g2_spec,
                  pl.BlockSpec((1, D), lambda i: (0, 0)),
                  pl.BlockSpec(memory_space=pl.ANY)],
        out_specs=pl.BlockSpec((tp, D), lambda i: (i, 0)),
        out_shape=jax.ShapeDtypeStruct(x1.shape, F32),
        scratch_shapes=[pltpu.VMEM((2, 2, tp * ROW_TILE, LANES), F32),
                        pltpu.SemaphoreType.DMA((2,))],
        compiler_params=pltpu.CompilerParams(
            dimension_semantics=("arbitrary",), vmem_limit_bytes=VMEM_LIMIT),
        name="moe_combine",
    )(dest_flat, dest_flat, x1, route, mod, fng, y_rows)


def _layer_weights(l, p):
    hd = W // HEADS
    gi = jnp.arange(W) // (W // NG_A)
    gavg = jnp.where(gi[:, None] == gi[None, :], 1.0 / (W // NG_A), 0.0).astype(BF16)
    tril = jnp.tril(jnp.ones((CHUNK, CHUNK), dtype=bool))
    w_sp = jnp.where(tril[None], p["b_spatial_w"][l], 0.0)
    w_router = jnp.concatenate(
        [p["w_router_g"][l], p["w_router_e"][l],
         jnp.zeros((D, ROUTE_COLS - NGRP - NEXP), F32)], axis=1)
    b_router = jnp.concatenate(
        [p["b_router_g"][l], p["b_router_e"][l],
         jnp.zeros((ROUTE_COLS - NGRP - NEXP,), F32)])[None, :]
    return {
        "n1g": p["norm1_g"][l][None, :], "n2g": p["norm2_g"][l][None, :],
        "w_in": p["w_in"][l].astype(BF16),
        "a_dw_w": p["a_dw_w"][l], "a_dw_b": p["a_dw_b"][l][None, :],
        "a_norm_g": p["a_norm_g"][l][None, :], "a_norm_b": p["a_norm_b"][l][None, :],
        "b_norm_g": p["b_norm_g"][l][None, :], "b_norm_b": p["b_norm_b"][l][None, :],
        "w_sp": jnp.concatenate([w_sp[0::2], w_sp[1::2]], axis=2).astype(BF16),
        "b_sp": jnp.repeat(p["b_spatial_b"][l].T, hd, axis=1),
        "c_conv_w": p["c_conv_w"][l],
        "w_branch": p["w_branch"][l].astype(BF16), "w_out": p["w_out"][l].astype(BF16),
        "gavg": gavg, "w_router": w_router.astype(BF16), "b_router": b_router,
    }


def _sample_spatial(l, p, ts):
    hd = W // HEADS
    w = p["b_spatial_w"][l][:, :ts, :ts]
    w = jnp.where(jnp.tril(jnp.ones((ts, ts), dtype=bool))[None], w, 0.0)
    w4 = jnp.repeat(jnp.transpose(w, (1, 2, 0)).reshape(ts * ts, HEADS), hd, axis=1)
    b4 = jnp.repeat(p["b_spatial_b"][l][:, :ts].T, hd, axis=1)
    return w4, b4


def kernel(x_prompt, x_sample, c_prompt, c_sample, state_conv_a, state_conv_c, norm1_g, norm2_g, w_ada, b_ada, w_in, a_dw_w, a_dw_b, a_norm_g, a_norm_b, b_norm_g, b_norm_b, b_spatial_w, b_spatial_b, c_conv_w, w_branch, w_out, w_router_g, b_router_g, w_router_e, b_router_e, w1, w3, w2, final_norm_g):
    p = dict(norm1_g=norm1_g, norm2_g=norm2_g, w_in=w_in, a_dw_w=a_dw_w, a_dw_b=a_dw_b,
             a_norm_g=a_norm_g, a_norm_b=a_norm_b, b_norm_g=b_norm_g, b_norm_b=b_norm_b,
             b_spatial_w=b_spatial_w, b_spatial_b=b_spatial_b, c_conv_w=c_conv_w,
             w_branch=w_branch, w_out=w_out, w_router_g=w_router_g, b_router_g=b_router_g,
             w_router_e=w_router_e, b_router_e=b_router_e)
    depth = w_in.shape[0]
    nb_, seq, _ = x_prompt.shape
    ns, ts, _ = x_sample.shape
    n_tok = nb_ * seq + ns * ts
    nblocks = -(-(2 * n_tok) // BM) + NEXP
    nsplit = 2
    nbs = ns // nsplit
    tps = ts * nbs
    assert seq % TT == 0 and TT % CHUNK == 0 and seq % TP == 0 and (nb_ * seq) % TPP == 0
    assert (ts * ns) % TP == 0 and ns % SUBLANES == 0

    mod = _adaln(jnp.concatenate([c_sample, c_prompt], axis=0), w_ada, b_ada)

    xp = x_prompt.reshape(nb_ * seq, D)
    xs = jnp.transpose(x_sample.reshape(nsplit, nbs, ts, D), (0, 2, 1, 3)).reshape(ts * ns, D)
    fng = final_norm_g[None, :]
    pa, pc, sa, sc, sv = [], [], [], [], []
    for l in range(depth):
        lw = _layer_weights(l, p)
        lw["w_sp4"], lw["b_sp4"] = _sample_spatial(l, p, ts)

        x1p, h2p, route_p, ha, hc = _mixer_prompt(xp, mod, l, nb_, ns, lw)
        pa.append(ha)
        pc.append(hc)
        x1s, h2s, route_s, ha_s, hc_s, v_s = _mixer_sample(
            xs, mod, l, jnp.transpose(state_conv_a[l], (1, 0, 2)),
            jnp.transpose(state_conv_c[l], (1, 0, 2)), lw, nsplit)
        sa.append(jnp.transpose(ha_s, (1, 0, 2)))
        sc.append(jnp.transpose(hc_s, (1, 0, 2)))
        sv.append(jnp.transpose(v_s, (1, 0, 2)))

        dest_p, dest_s, tbl = _plan(route_p, route_s, nblocks)
        dest_p, dest_s = dest_p.reshape(-1), dest_s.reshape(-1)
        rows_buf = _scatter_rows(dest_p, h2p, None, nblocks * BM, TP)
        rows_buf = _scatter_rows(dest_s, h2s, rows_buf, nblocks * BM, tps)
        y_rows = _experts(tbl, rows_buf, w1, w3, w2, l, nblocks)

        last = l == depth - 1
        xp = _combine(dest_p, x1p, route_p, mod, l, fng, y_rows, TP, final_norm=last,
                      g2_row0=ns, tiles_per_g2=seq // TP)
        xs = _combine(dest_s, x1s, route_s, mod, l, fng, y_rows, tps, final_norm=last,
                      seqs_per_tile=nbs)

    y_prompt = xp.reshape(nb_, seq, D)
    y_sample = jnp.transpose(xs.reshape(nsplit, ts, nbs, D), (0, 2, 1, 3)).reshape(ns, ts, D)
    return (y_prompt, y_sample, jnp.stack(pa), jnp.stack(pc), jnp.stack(sa), jnp.stack(sc),
            jnp.stack(sv))
```

```python
import functools

import jax
import jax.numpy as jnp
from jax import lax
from jax.experimental import pallas as pl
from jax.experimental.pallas import tpu as pltpu

F32 = jnp.float32
BF16 = jnp.bfloat16

D = 1024
W = D // 2
KA = 31
KC = 3
NG_A = 8
CHUNK = 128
HEADS = 8
NGRP = 4
EPG = 8
NEXP = NGRP * EPG
FF = D // 2
EPS = 1e-6
IN_COLS = 7 * W + 3 * D
LANES = 128
ROUTE_COLS = LANES

TT = 512
SUBLANES = 8
A_HDR = 32
A_PAD = SUBLANES
A_ROWS = 64
C_HDR = 8
TP = 512
TPP = 2048
BM = 256
VMEM_LIMIT = 56 * 1024 * 1024

NEG = -1e30


def _sigmoid(x):
    return 1.0 / (1.0 + jnp.exp(-x))


def _silu(x):
    return x * _sigmoid(x)


def _gelu_tanh(x):
    return 0.5 * x * (1.0 + jnp.tanh(0.7978845608028654 * (x + 0.044715 * x * x * x)))


def _rms(x, g):
    return x * lax.rsqrt(jnp.mean(x * x, axis=-1, keepdims=True) + EPS) * g


def _dot(a, b):
    return jnp.dot(a, b, preferred_element_type=F32)


def _dot_hi(a, b):
    return jnp.dot(a, b, preferred_element_type=F32, precision=lax.Precision.HIGHEST)


ROW_TILE = D // LANES


def _store_row_tiles(ref, val):
    r = val.shape[0]
    for j in range(ROW_TILE):
        ref[pl.ds(j, r, stride=ROW_TILE), :] = val[:, j * LANES:(j + 1) * LANES]


def _load_row_tiles(ref):
    r = ref.shape[0] // ROW_TILE
    return jnp.concatenate([ref[pl.ds(j, r, stride=ROW_TILE), :] for j in range(ROW_TILE)], axis=1)


def _row_tile(ref, r):
    return ref.at[pl.ds(pl.multiple_of(r * ROW_TILE, ROW_TILE), ROW_TILE)]


def _mod_kernel(c_ref, w_ref, b_ref, o_ref):
    c = c_ref[...]
    s = _silu(c).astype(BF16)
    o_ref[0, 0] = _dot(s, w_ref[0].astype(BF16)) + b_ref[0]


def _adaln(c_all, w_ada, b_ada):
    nl = w_ada.shape[0]
    r = c_all.shape[0]
    return pl.pallas_call(
        _mod_kernel,
        grid=(nl, 6),
        in_specs=[
            pl.BlockSpec((r, D), lambda l, j: (0, 0)),
            pl.BlockSpec((1, D, D), lambda l, j: (l, 0, j)),
            pl.BlockSpec((1, 1, D), lambda l, j: (l, 0, j)),
        ],
        out_specs=pl.BlockSpec((1, 1, r, D), lambda l, j: (l, j, 0, 0)),
        out_shape=jax.ShapeDtypeStruct((nl, 6, r, D), F32),
        compiler_params=pltpu.CompilerParams(
            dimension_semantics=("arbitrary", "arbitrary"), vmem_limit_bytes=VMEM_LIMIT),
        name="adaln_mod",
    )(c_all, w_ada, b_ada.reshape(nl, 1, 6 * D))


def _group_norm_silu(a, gavg_ref, g, b):
    mu = _dot(a.astype(BF16), gavg_ref[...])
    xc = a - mu
    var = _dot((xc * xc).astype(BF16), gavg_ref[...])
    y = xc * lax.rsqrt(var + EPS) * g + b
    return _silu(y)


def _layer_norm(x, g, b):
    mu = jnp.mean(x, axis=-1, keepdims=True)
    xc = x - mu
    var = jnp.mean(xc * xc, axis=-1, keepdims=True)
    return xc * lax.rsqrt(var + EPS) * g + b


def _merge_and_route(x, a, bb, cc, gates_fn, mod, n2g, wbr_ref, wout_ref, wr_ref, br_ref,
                     x1_ref, h2_ref, route_ref):
    g1, sh2, sc2 = mod
    acc = None
    for gi, br_in in enumerate((a, bb, cc)):
        br = _dot(br_in.astype(BF16), wbr_ref[gi])
        term = _sigmoid(gates_fn(gi)) * br
        acc = term if acc is None else acc + term
    mixed = _dot(acc.astype(BF16), wout_ref[...])
    x1 = x + g1 * mixed
    x1_ref[...] = x1
    h2 = _rms(x1, n2g) * (1.0 + sc2) + sh2
    _store_row_tiles(h2_ref, h2)

    logits = _dot(h2.astype(BF16), wr_ref[...]) + br_ref[...]
    rows = logits.shape[0]
    lane = lax.broadcasted_iota(jnp.int32, (rows, ROUTE_COLS), 1).astype(F32)
    is_g = lane < NGRP
    gl = jnp.where(is_g, logits, NEG)
    gmax = jnp.max(gl, axis=-1, keepdims=True)
    gsel = jnp.min(jnp.where(gl == gmax, lane, float(ROUTE_COLS)), axis=-1, keepdims=True)
    gden = jnp.sum(jnp.where(is_g, jnp.exp(gl - gmax), 0.0), axis=-1, keepdims=True)
    gprob = 1.0 / gden
    lo = NGRP + EPG * gsel
    in_grp = jnp.logical_and(lane >= lo, lane < lo + EPG)
    el = jnp.where(in_grp, logits, NEG)
    v1 = jnp.max(el, axis=-1, keepdims=True)
    i1 = jnp.min(jnp.where(el == v1, lane, float(ROUTE_COLS)), axis=-1, keepdims=True)
    el2 = jnp.where(lane == i1, NEG, el)
    v2 = jnp.max(el2, axis=-1, keepdims=True)
    i2 = jnp.min(jnp.where(el2 == v2, lane, float(ROUTE_COLS)), axis=-1, keepdims=True)
    p2 = jnp.exp(v2 - v1)
    w_a = gprob / (1.0 + p2)
    w_b = gprob * p2 / (1.0 + p2)
    route = jnp.where(lane == 0.0, i1 - NGRP,
                      jnp.where(lane == 1.0, i2 - NGRP,
                                jnp.where(lane == 2.0, w_a,
                                          jnp.where(lane == 3.0, w_b, 0.0))))
    route_ref[...] = route


def _mixer_prompt_kernel(x_ref, mod_ref, n1g_ref, n2g_ref, win_ref, adw_ref, adb_ref, ang_ref,
                         anb_ref, bng_ref, bnb_ref, wsp_ref, bsp_ref, ccw_ref, wbr_ref, wout_ref,
                         gavg_ref, wr_ref, br_ref,
                         x1_ref, h2_ref, route_ref, ha_ref, hc_ref,
                         abuf, cbuf, aconv, *, mod_row0):
    first = pl.program_id(1) == 0
    abuf[0:A_HDR, :] = jnp.where(first, 0.0, abuf[TT:TT + A_HDR, :])
    abuf[A_HDR + TT:A_HDR + TT + A_PAD, :] = jnp.zeros((A_PAD, W), F32)
    cbuf[0:C_HDR, :] = jnp.where(first, 0.0, cbuf[TT:TT + C_HDR, :])

    x = x_ref[...]
    mrow = mod_row0 + pl.program_id(0)
    sh1, sc1, g1, sh2, sc2 = (mod_ref[0, s, pl.ds(mrow, 1), :] for s in range(5))
    h = (_rms(x, n1g_ref[...]) * (1.0 + sc1) + sh1).astype(BF16)

    def zsec(lo, hi):
        return _dot(h, win_ref[:, lo:hi])

    a_glu = zsec(0, W) * _sigmoid(zsec(W, 2 * W))
    abuf[A_HDR:A_HDR + TT, :] = a_glu
    off = A_HDR - (KA - 1)
    rc = A_ROWS

    def conv_pass(p):
        lt, half = p // 2, p % 2
        ls = slice(lt * LANES, (lt + 1) * LANES)
        for r0 in range(half * (TT // 2), (half + 1) * (TT // 2), rc):
            y = None
            for r in range(SUBLANES):
                z = None
                for q in range((KA - 1 + off) // SUBLANES + 1):
                    k = SUBLANES * q + r - off
                    if 0 <= k < KA:
                        rows = slice(r0 + SUBLANES * q, r0 + SUBLANES * q + rc + SUBLANES)
                        term = adw_ref[k:k + 1, ls] * abuf[rows, ls]
                        z = term if z is None else z + term
                part = z[r:r + rc, :]
                y = part if y is None else y + part
            aconv[r0:r0 + rc, ls] = y + adb_ref[:, ls]

    conv_pass(0)
    u = _gelu_tanh(zsec(2 * W, 3 * W))
    conv_pass(1)
    v = _layer_norm(_gelu_tanh(zsec(3 * W, 4 * W)), bng_ref[...], bnb_ref[...])
    conv_pass(2)

    cch = zsec(5 * W, 6 * W) * zsec(6 * W, 7 * W)
    cbuf[C_HDR:C_HDR + TT, :] = cch
    coff = C_HDR - (KC - 1)
    ych = (ccw_ref[0:1, :] * cbuf[coff:coff + TT, :]
           + ccw_ref[1:2, :] * cbuf[coff + 1:coff + 1 + TT, :]
           + ccw_ref[2:3, :] * cbuf[coff + 2:coff + 2 + TT, :])
    conv_pass(3)
    cc = zsec(4 * W, 5 * W) * ych
    hc_ref[0] = cbuf[C_HDR + TT - (KC - 1):C_HDR + TT, :]
    conv_pass(4)
    gate_z = [zsec(7 * W, 7 * W + D)]
    conv_pass(5)
    gate_z.append(zsec(7 * W + D, 7 * W + 2 * D))
    conv_pass(6)
    gate_z.append(zsec(7 * W + 2 * D, 7 * W + 3 * D))
    conv_pass(7)
    ha_ref[0] = abuf[A_HDR + TT - (KA - 1):A_HDR + TT, :]

    vb = v.astype(BF16)
    low_head = lax.broadcasted_iota(jnp.int32, (CHUNK, LANES), 1) < (W // HEADS)
    zero_b = jnp.zeros((CHUNK, LANES), BF16)
    s_parts = []
    for c in range(TT // CHUNK):
        tiles = []
        for lt in range(W // LANES):
            vt = vb[c * CHUNK:(c + 1) * CHUNK, lt * LANES:(lt + 1) * LANES]
            stacked = jnp.concatenate([jnp.where(low_head, vt, zero_b),
                                       jnp.where(low_head, zero_b, vt)], axis=0)
            tiles.append(_dot(wsp_ref[lt], stacked))
        s_parts.append(jnp.concatenate(tiles, axis=1) + bsp_ref[...])
    bb = u * jnp.concatenate(s_parts, axis=0)
    a = _group_norm_silu(aconv[...], gavg_ref, ang_ref[...], anb_ref[...])

    _merge_and_route(x, a, bb, cc, lambda gi: gate_z[gi], (g1, sh2, sc2), n2g_ref[...], wbr_ref,
                     wout_ref, wr_ref, br_ref, x1_ref, h2_ref, route_ref)


def _const_spec(shape):
    nd = len(shape)
    return pl.BlockSpec(shape, lambda *_: (0,) * nd, pipeline_mode=pl.Buffered(1))


def _mixer_prompt(x, mod, layer, n, mod_row0, lw):
    t = x.shape[0] // n
    nt = t // TT
    mod_rows = mod.shape[2]
    row_spec = lambda cols: pl.BlockSpec((TT, cols), lambda i, j: (i * nt + j, 0))
    consts = [lw["n1g"], lw["n2g"], lw["w_in"], lw["a_dw_w"], lw["a_dw_b"], lw["a_norm_g"],
              lw["a_norm_b"], lw["b_norm_g"], lw["b_norm_b"], lw["w_sp"], lw["b_sp"], lw["c_conv_w"],
              lw["w_branch"], lw["w_out"], lw["gavg"], lw["w_router"], lw["b_router"]]
    return pl.pallas_call(
        functools.partial(_mixer_prompt_kernel, mod_row0=mod_row0),
        grid=(n, nt),
        in_specs=[row_spec(D),
                  pl.BlockSpec((1, 6, mod_rows, D), lambda i, j: (layer, 0, 0, 0),
                               pipeline_mode=pl.Buffered(1))]
        + [_const_spec(c.shape) for c in consts],
        out_specs=[row_spec(D),
                   pl.BlockSpec((TT * ROW_TILE, LANES), lambda i, j: (i * nt + j, 0)),
                   row_spec(ROUTE_COLS),
                   pl.BlockSpec((1, KA - 1, W), lambda i, j: (i, 0, 0)),
                   pl.BlockSpec((1, KC - 1, W), lambda i, j: (i, 0, 0))],
        out_shape=[jax.ShapeDtypeStruct((n * t, D), F32),
                   jax.ShapeDtypeStruct((n * t * ROW_TILE, LANES), F32),
                   jax.ShapeDtypeStruct((n * t, ROUTE_COLS), F32),
                   jax.ShapeDtypeStruct((n, KA - 1, W), F32),
                   jax.ShapeDtypeStruct((n, KC - 1, W), F32)],
        scratch_shapes=[pltpu.VMEM((A_HDR + TT + A_PAD, W), F32),
                        pltpu.VMEM((C_HDR + TT, W), F32),
                        pltpu.VMEM((TT, W), F32)],
        compiler_params=pltpu.CompilerParams(
            dimension_semantics=("arbitrary", "arbitrary"), vmem_limit_bytes=VMEM_LIMIT),
        name="mixer_prompt",
    )(x, mod, *consts)


def _mixer_sample_kernel(x_ref, mod_ref, hista_ref, histc_ref, n1g_ref, n2g_ref, win_ref, adw_ref,
                         adb_ref, ang_ref, anb_ref, bng_ref, bnb_ref, wsp4_ref, bsp4_ref,
                         ccw_ref, wbr_ref, wout_ref, gavg_ref, wr_ref, br_ref,
                         x1_ref, h2_ref, route_ref, ha_ref, hc_ref, v_ref):
    ts, nb = v_ref.shape[0], v_ref.shape[1]
    x = x_ref[...]

    def modrow(i):
        return jnp.concatenate([mod_ref[0, i]] * ts, axis=0)

    h = (_rms(x, n1g_ref[...]) * (1.0 + modrow(1)) + modrow(0)).astype(BF16)

    def zsec(lo, hi):
        return _dot(h, win_ref[:, lo:hi])

    def tslab(arr, j):
        return arr[j * nb:(j + 1) * nb, :]

    a_glu = zsec(0, W) * _sigmoid(zsec(W, 2 * W))
    kh = KA - 1
    a_conv = []
    for tq in range(ts):
        acc = jnp.zeros((nb, W), F32) + adb_ref[...]
        for r in range(tq, kh):
            acc = acc + adw_ref[r - tq:r - tq + 1, :] * hista_ref[r]
        for j in range(tq + 1):
            acc = acc + adw_ref[kh + j - tq:kh + j - tq + 1, :] * tslab(a_glu, j)
        a_conv.append(acc)
    a = _group_norm_silu(jnp.concatenate(a_conv, axis=0), gavg_ref, ang_ref[...], anb_ref[...])
    for r in range(kh - ts):
        ha_ref[r] = hista_ref[r + ts]
    for j in range(ts):
        ha_ref[kh - ts + j] = tslab(a_glu, j)

    u = _gelu_tanh(zsec(2 * W, 3 * W))
    v = _layer_norm(_gelu_tanh(zsec(3 * W, 4 * W)), bng_ref[...], bnb_ref[...])
    for j in range(ts):
        v_ref[j] = tslab(v, j)
    s_rows = []
    for tq in range(ts):
        s = jnp.zeros((nb, W), F32) + bsp4_ref[tq:tq + 1, :]
        for sq in range(tq + 1):
            s = s + wsp4_ref[tq * ts + sq:tq * ts + sq + 1, :] * tslab(v, sq)
        s_rows.append(s)
    bb = u * jnp.concatenate(s_rows, axis=0)

    cch = zsec(5 * W, 6 * W) * zsec(6 * W, 7 * W)
    xp = [histc_ref[r] for r in range(KC - 1)] + [tslab(cch, j) for j in range(ts)]
    ych = jnp.concatenate(
        [sum(ccw_ref[k:k + 1, :] * xp[tq + k] for k in range(KC)) for tq in range(ts)], axis=0)
    cc = zsec(4 * W, 5 * W) * ych
    for r in range(KC - 1):
        hc_ref[r] = xp[ts + r]

    def gates_fn(gi):
        return zsec(7 * W + gi * D, 7 * W + (gi + 1) * D)

    _merge_and_route(x, a, bb, cc, gates_fn, (modrow(2), modrow(3), modrow(4)), n2g_ref[...],
                     wbr_ref, wout_ref, wr_ref, br_ref, x1_ref, h2_ref, route_ref)


def _mixer_sample(x_rows, mod, layer, hist_a_tm, hist_c_tm, lw, nsplit):
    n = hist_a_tm.shape[1]
    ts = x_rows.shape[0] // n
    nb = n // nsplit
    rows = ts * nb
    consts = [lw["n1g"], lw["n2g"], lw["w_in"], lw["a_dw_w"], lw["a_dw_b"],
              lw["a_norm_g"], lw["a_norm_b"], lw["b_norm_g"], lw["b_norm_b"], lw["w_sp4"], lw["b_sp4"],
              lw["c_conv_w"], lw["w_branch"], lw["w_out"], lw["gavg"], lw["w_router"], lw["b_router"]]
    seq3 = lambda k, cols: pl.BlockSpec((k, nb, cols), lambda i: (0, i, 0))
    row_spec = lambda cols: pl.BlockSpec((rows, cols), lambda i: (i, 0))
    return pl.pallas_call(
        _mixer_sample_kernel,
        grid=(nsplit,),
        in_specs=[row_spec(D), pl.BlockSpec((1, 6, nb, D), lambda i: (layer, 0, i, 0)),
                  seq3(KA - 1, W), seq3(KC - 1, W)]
        + [_const_spec(c.shape) for c in consts],
        out_specs=[row_spec(D),
                   pl.BlockSpec((rows * ROW_TILE, LANES), lambda i: (i, 0)),
                   row_spec(ROUTE_COLS),
                   seq3(KA - 1, W), seq3(KC - 1, W), seq3(ts, W)],
        out_shape=[jax.ShapeDtypeStruct((ts * n, D), F32),
                   jax.ShapeDtypeStruct((ts * n * ROW_TILE, LANES), F32),
                   jax.ShapeDtypeStruct((ts * n, ROUTE_COLS), F32),
                   jax.ShapeDtypeStruct((KA - 1, n, W), F32),
                   jax.ShapeDtypeStruct((KC - 1, n, W), F32),
                   jax.ShapeDtypeStruct((ts, n, W), F32)],
        compiler_params=pltpu.CompilerParams(
            dimension_semantics=("arbitrary",), vmem_limit_bytes=VMEM_LIMIT),
        name="mixer_sample",
    )(x_rows, mod, hist_a_tm, hist_c_tm, *consts)


def _plan_kernel(rp_ref, rs_ref, tri_ref, destp_ref, dests_ref, tbl_ref, carry, pstart, *, ntp):
    ph = pl.program_id(0)
    i = pl.program_id(1)
    nt = pl.num_programs(1)
    lane = lax.broadcasted_iota(jnp.int32, (TP, ROUTE_COLS), 1).astype(F32)

    @pl.when(jnp.logical_and(ph == 0, i == 0))
    def _():
        carry[...] = jnp.zeros_like(carry)

    def sub_tile(r, dest_ref, row0):
        hot_a = lane == r[:, 0:1]
        hot_b = lane == r[:, 1:2]
        s = jnp.where(jnp.logical_or(hot_a, hot_b), 1.0, 0.0)

        @pl.when(ph == 1)
        def _():
            pre = _dot(tri_ref[...], s.astype(BF16)) + carry[...] + pstart[...]
            d_a = jnp.sum(jnp.where(hot_a, pre, 0.0), axis=-1, keepdims=True)
            d_b = jnp.sum(jnp.where(hot_b, pre, 0.0), axis=-1, keepdims=True)
            l2 = lax.broadcasted_iota(jnp.int32, (TP, 2), 1)
            dest_ref[row0:row0 + TP, :] = jnp.where(l2 == 0, d_a, d_b).astype(jnp.int32)

        carry[...] = carry[...] + jnp.sum(s, axis=0, keepdims=True)

    @pl.when(i < ntp)
    def _():
        for sub in range(rp_ref.shape[0] // TP):
            sub_tile(rp_ref[sub * TP:(sub + 1) * TP, :], destp_ref, sub * TP)

    @pl.when(i == ntp)
    def _():
        for sub in range(rs_ref.shape[0] // TP):
            sub_tile(rs_ref[sub * TP:(sub + 1) * TP, :], dests_ref, sub * TP)

    @pl.when(jnp.logical_and(ph == 0, i == nt - 1))
    def _():
        cnt = carry[...]
        nblk = jnp.floor((cnt + (BM - 1)) * (1.0 / BM))
        ri = lax.broadcasted_iota(jnp.int32, (ROUTE_COLS, ROUTE_COLS), 0)
        ci = lax.broadcasted_iota(jnp.int32, (ROUTE_COLS, ROUTE_COLS), 1)
        upper = jnp.where(ri <= ci, 1.0, 0.0)
        cum = _dot_hi(jnp.broadcast_to(nblk, (8, ROUTE_COLS)), upper)[0:1, :]
        cum_ex = cum - nblk
        pstart[...] = cum_ex * BM
        row = lax.broadcasted_iota(jnp.int32, (SUBLANES, ROUTE_COLS), 0)
        tbl = jnp.where(row == 0, cum_ex, jnp.where(row == 1, cnt, 0.0))
        tbl_ref[...] = tbl.astype(jnp.int32)
        carry[...] = jnp.zeros_like(carry)


def _plan(route_p, route_s):
    rows_p, rows_s = route_p.shape[0], route_s.shape[0]
    ntp = rows_p // TPP
    nbp = SUBLANES
    ri = lax.broadcasted_iota(jnp.int32, (TP, TP), 0)
    ci = lax.broadcasted_iota(jnp.int32, (TP, TP), 1)
    tri = jnp.where(ci < ri, 1.0, 0.0).astype(BF16)
    return pl.pallas_call(
        functools.partial(_plan_kernel, ntp=ntp),
        grid=(2, ntp + 1),
        in_specs=[pl.BlockSpec((TPP, ROUTE_COLS), lambda p, i: (jnp.minimum(i, ntp - 1), 0)),
                  pl.BlockSpec((rows_s, ROUTE_COLS), lambda p, i: (0, 0)),
                  pl.BlockSpec((TP, TP), lambda p, i: (0, 0))],
        out_specs=[pl.BlockSpec((TPP, 2), lambda p, i: (jnp.minimum(i, ntp - 1) * p, 0)),
                   pl.BlockSpec((rows_s, 2), lambda p, i: (0, 0)),
                   pl.BlockSpec((nbp, ROUTE_COLS), lambda p, i: (0, 0))],
        out_shape=[jax.ShapeDtypeStruct((rows_p, 2), jnp.int32),
                   jax.ShapeDtypeStruct((rows_s, 2), jnp.int32),
                   jax.ShapeDtypeStruct((nbp, ROUTE_COLS), jnp.int32)],
        scratch_shapes=[pltpu.VMEM((1, ROUTE_COLS), F32), pltpu.VMEM((1, ROUTE_COLS), F32)],
        compiler_params=pltpu.CompilerParams(
            dimension_semantics=("arbitrary", "arbitrary"), vmem_limit_bytes=VMEM_LIMIT),
        name="moe_plan",
    )(route_p, route_s, tri)


def _scatter_kernel(dest_ref, h_ref, *rest):
    rows_ref, sem = rest[-2:]
    tp = h_ref.shape[0] // ROW_TILE

    def row_copy(r, k):
        d = dest_ref[2 * r + k]
        return pltpu.make_async_copy(_row_tile(h_ref, r), _row_tile(rows_ref, d), sem)

    def issue(r, c):
        row_copy(r, 0).start(priority=0)
        row_copy(r, 1).start(priority=1)
        return c

    lax.fori_loop(0, tp, issue, 0, unroll=8)

    def drain(r, c):
        row_copy(r, 0).wait()
        row_copy(r, 1).wait()
        return c

    lax.fori_loop(0, tp, drain, 0, unroll=8)


def _scatter_rows(dest_flat, h2, rows_buf, n_rows, tp):
    nt = h2.shape[0] // (tp * ROW_TILE)
    in_specs = [pl.BlockSpec((2 * tp,), lambda i: (i,), memory_space=pltpu.SMEM),
                pl.BlockSpec((tp * ROW_TILE, LANES), lambda i: (i, 0))]
    args = [dest_flat, h2]
    aliases = {}
    if rows_buf is not None:
        in_specs.append(pl.BlockSpec(memory_space=pl.ANY))
        args.append(rows_buf)
        aliases = {2: 0}
    return pl.pallas_call(
        _scatter_kernel,
        grid=(nt,),
        in_specs=in_specs,
        out_specs=pl.BlockSpec(memory_space=pl.ANY),
        out_shape=jax.ShapeDtypeStruct((n_rows * ROW_TILE, LANES), F32),
        scratch_shapes=[pltpu.SemaphoreType.DMA(())],
        input_output_aliases=aliases,
        compiler_params=pltpu.CompilerParams(
            dimension_semantics=("arbitrary",), vmem_limit_bytes=VMEM_LIMIT,
            has_side_effects=True),
        name="moe_scatter",
    )(*args)


def _expert_kernel(blk0_ref, cnt_ref, x_hbm, w1_ref, w3_ref, w2_ref, y_hbm,
                   xbuf, ybuf, w1b, w3b, w2b, sem_in, sem_out):
    e = pl.program_id(0)
    cnt = cnt_ref[e]
    nblk = lax.shift_right_logical(cnt + (BM - 1), BM.bit_length() - 1)
    blk0 = blk0_ref[e]
    blk_rows = BM * ROW_TILE

    def block_rows(ref, j):
        return ref.at[pl.ds(pl.multiple_of((blk0 + j) * blk_rows, blk_rows), blk_rows)]

    def in_copy(j, slot):
        return pltpu.make_async_copy(block_rows(x_hbm, j), xbuf.at[slot], sem_in.at[slot])

    def out_copy(j, slot):
        return pltpu.make_async_copy(ybuf.at[slot], block_rows(y_hbm, j), sem_out.at[slot])

    @pl.when(nblk > 0)
    def _():
        in_copy(0, 0).start()
        w1b[...] = w1_ref[0, 0].astype(BF16)
        w3b[...] = w3_ref[0, 0].astype(BF16)
        w2b[...] = w2_ref[0, 0].astype(BF16)

        def body(j, carry):
            slot = lax.rem(j, 2)

            @pl.when(j + 1 < nblk)
            def _():
                in_copy(j + 1, 1 - slot).start()

            in_copy(j, slot).wait()

            @pl.when(j >= 2)
            def _():
                out_copy(j - 2, slot).wait()

            row = lax.broadcasted_iota(jnp.int32, (BM, D), 0)
            x = jnp.where(row < cnt - j * BM, _load_row_tiles(xbuf.at[slot]), 0.0).astype(BF16)
            h1 = _dot(x, w1b[...])
            h3 = _dot(x, w3b[...])
            act = (_silu(h1) * h3).astype(BF16)
            _store_row_tiles(ybuf.at[slot], _dot(act, w2b[...]))
            out_copy(j, slot).start()
            return carry

        lax.fori_loop(0, nblk, body, 0)

        @pl.when(nblk >= 2)
        def _():
            out_copy(nblk - 2, lax.rem(nblk, 2)).wait()

        out_copy(nblk - 1, lax.rem(nblk - 1, 2)).wait()


def _experts(tbl, rows_buf, w1, w3, w2, layer):
    blk0, cnt = tbl[0, :NEXP], tbl[1, :NEXP]
    wspec = lambda shape: pl.BlockSpec((1, 1) + shape, lambda e, blk0, cnt: (layer, e, 0, 0))
    return pl.pallas_call(
        _expert_kernel,
        grid_spec=pltpu.PrefetchScalarGridSpec(
            num_scalar_prefetch=2,
            grid=(NEXP,),
            in_specs=[pl.BlockSpec(memory_space=pl.ANY),
                      wspec((D, FF)), wspec((D, FF)), wspec((FF, D))],
            out_specs=pl.BlockSpec(memory_space=pl.ANY),
            scratch_shapes=[pltpu.VMEM((2, BM * ROW_TILE, LANES), F32),
                            pltpu.VMEM((2, BM * ROW_TILE, LANES), F32),
                            pltpu.VMEM((D, FF), BF16), pltpu.VMEM((D, FF), BF16),
                            pltpu.VMEM((FF, D), BF16),
                            pltpu.SemaphoreType.DMA((2,)), pltpu.SemaphoreType.DMA((2,))],
        ),
        out_shape=jax.ShapeDtypeStruct(rows_buf.shape, F32),
        compiler_params=pltpu.CompilerParams(
            dimension_semantics=("arbitrary",), vmem_limit_bytes=VMEM_LIMIT,
            has_side_effects=True),
        name="moe_experts",
    )(blk0, cnt, rows_buf, w1, w3, w2)


def _combine_kernel(dest_ref, dest_next_ref, x1_ref, route_ref, g2_ref, fng_ref, y_ref, o_ref,
                    ybuf, sem, *, final_norm, g2_row0, tiles_per_g2):
    tp = x1_ref.shape[0]
    i = pl.program_id(0)
    n = pl.num_programs(0)
    slot = lax.rem(i, 2)

    def gather(d_ref, s):
        def row_copy(r, k):
            return pltpu.make_async_copy(_row_tile(y_ref, d_ref[2 * r + k]),
                                         _row_tile(ybuf.at[s, k], r), sem.at[s])

        def issue(r, c):
            row_copy(r, 0).start(priority=0)
            row_copy(r, 1).start(priority=1)
            return c

        lax.fori_loop(0, tp, issue, 0, unroll=8)

    @pl.when(i == 0)
    def _():
        gather(dest_ref, 0)

    @pl.when(i + 1 < n)
    def _():
        gather(dest_next_ref, 1 - slot)

    for k in range(2):
        pltpu.make_async_copy(y_ref.at[pl.ds(0, tp * ROW_TILE)], ybuf.at[slot, k], sem.at[slot]).wait()

    route = route_ref[...]
    w_a = route[:, 2:3]
    w_b = route[:, 3:4]
    if tiles_per_g2:
        g2 = g2_ref[0, 0, pl.ds(g2_row0 + i // tiles_per_g2, 1), :]
    else:
        g2 = jnp.concatenate([g2_ref[0, 0]] * (tp // g2_ref.shape[2]), axis=0)
    y = (w_a * _load_row_tiles(ybuf.at[slot, 0]) + w_b * _load_row_tiles(ybuf.at[slot, 1]))
    x2 = x1_ref[...] + g2 * y
    if final_norm:
        x2 = _rms(x2, fng_ref[...])
    o_ref[...] = x2


def _combine(dest_flat, x1, route, mod, layer, fng, y_rows, tp, *, final_norm, g2_row0=0,
             tiles_per_g2=0, seqs_per_tile=0):
    nt = x1.shape[0] // tp
    if tiles_per_g2:
        g2_spec = pl.BlockSpec((1, 1, mod.shape[2], D), lambda i: (layer, 5, 0, 0))
    else:
        g2_spec = pl.BlockSpec((1, 1, seqs_per_tile, D), lambda i: (layer, 5, i, 0))
    return pl.pallas_call(
        functools.partial(_combine_kernel, final_norm=final_norm, g2_row0=g2_row0,
                          tiles_per_g2=tiles_per_g2),
        grid=(nt,),
        in_specs=[pl.BlockSpec((2 * tp,), lambda i: (i,), memory_space=pltpu.SMEM),
                  pl.BlockSpec((2 * tp,), lambda i: (jnp.minimum(i + 1, nt - 1),),
                               memory_space=pltpu.SMEM),
                  pl.BlockSpec((tp, D), lambda i: (i, 0)),
                  pl.BlockSpec((tp, ROUTE_COLS), lambda i: (i, 0)),
                  g2_spec,
                  pl.BlockSpec((1, D), lambda i: (0, 0)),
                  pl.BlockSpec(memory_space=pl.ANY)],
        out_specs=pl.BlockSpec((tp, D), lambda i: (i, 0)),
        out_shape=jax.ShapeDtypeStruct(x1.shape, F32),
        scratch_shapes=[pltpu.VMEM((2, 2, tp * ROW_TILE, LANES), F32),
                        pltpu.SemaphoreType.DMA((2,))],
        compiler_params=pltpu.CompilerParams(
            dimension_semantics=("arbitrary",), vmem_limit_bytes=VMEM_LIMIT),
        name="moe_combine",
    )(dest_flat, dest_flat, x1, route, mod, fng, y_rows)


def _layer_weights(l, p):
    hd = W // HEADS
    gi = jnp.arange(W) // (W // NG_A)
    gavg = jnp.where(gi[:, None] == gi[None, :], 1.0 / (W // NG_A), 0.0).astype(BF16)
    tril = jnp.tril(jnp.ones((CHUNK, CHUNK), dtype=bool))
    w_sp = jnp.where(tril[None], p["b_spatial_w"][l], 0.0)
    w_router = jnp.concatenate(
        [p["w_router_g"][l], p["w_router_e"][l],
         jnp.zeros((D, ROUTE_COLS - NGRP - NEXP), F32)], axis=1)
    b_router = jnp.concatenate(
        [p["b_router_g"][l], p["b_router_e"][l],
         jnp.zeros((ROUTE_COLS - NGRP - NEXP,), F32)])[None, :]
    return {
        "n1g": p["norm1_g"][l][None, :], "n2g": p["norm2_g"][l][None, :],
        "w_in": p["w_in"][l].astype(BF16),
        "a_dw_w": p["a_dw_w"][l], "a_dw_b": p["a_dw_b"][l][None, :],
        "a_norm_g": p["a_norm_g"][l][None, :], "a_norm_b": p["a_norm_b"][l][None, :],
        "b_norm_g": p["b_norm_g"][l][None, :], "b_norm_b": p["b_norm_b"][l][None, :],
        "w_sp": jnp.concatenate([w_sp[0::2], w_sp[1::2]], axis=2).astype(BF16),
        "b_sp": jnp.repeat(p["b_spatial_b"][l].T, hd, axis=1),
        "c_conv_w": p["c_conv_w"][l],
        "w_branch": p["w_branch"][l].astype(BF16), "w_out": p["w_out"][l].astype(BF16),
        "gavg": gavg, "w_router": w_router.astype(BF16), "b_router": b_router,
    }


def _sample_spatial(l, p, ts):
    hd = W // HEADS
    w = p["b_spatial_w"][l][:, :ts, :ts]
    w = jnp.where(jnp.tril(jnp.ones((ts, ts), dtype=bool))[None], w, 0.0)
    w4 = jnp.repeat(jnp.transpose(w, (1, 2, 0)).reshape(ts * ts, HEADS), hd, axis=1)
    b4 = jnp.repeat(p["b_spatial_b"][l][:, :ts].T, hd, axis=1)
    return w4, b4


def kernel(x_prompt, x_sample, c_prompt, c_sample, state_conv_a, state_conv_c, norm1_g, norm2_g, w_ada, b_ada, w_in, a_dw_w, a_dw_b, a_norm_g, a_norm_b, b_norm_g, b_norm_b, b_spatial_w, b_spatial_b, c_conv_w, w_branch, w_out, w_router_g, b_router_g, w_router_e, b_router_e, w1, w3, w2, final_norm_g):
    p = dict(norm1_g=norm1_g, norm2_g=norm2_g, w_in=w_in, a_dw_w=a_dw_w, a_dw_b=a_dw_b,
             a_norm_g=a_norm_g, a_norm_b=a_norm_b, b_norm_g=b_norm_g, b_norm_b=b_norm_b,
             b_spatial_w=b_spatial_w, b_spatial_b=b_spatial_b, c_conv_w=c_conv_w,
             w_branch=w_branch, w_out=w_out, w_router_g=w_router_g, b_router_g=b_router_g,
             w_router_e=w_router_e, b_router_e=b_router_e)
    depth = w_in.shape[0]
    nb_, seq, _ = x_prompt.shape
    ns, ts, _ = x_sample.shape
    n_tok = nb_ * seq + ns * ts
    nblocks = -(-(2 * n_tok) // BM) + NEXP
    nsplit = 2
    nbs = ns // nsplit
    tps = ts * nbs
    assert seq % TT == 0 and TT % CHUNK == 0 and seq % TP == 0 and (nb_ * seq) % TPP == 0
    assert (ts * ns) % TP == 0 and ns % SUBLANES == 0

    mod = _adaln(jnp.concatenate([c_sample, c_prompt], axis=0), w_ada, b_ada)

    xp = x_prompt.reshape(nb_ * seq, D)
    xs = jnp.transpose(x_sample.reshape(nsplit, nbs, ts, D), (0, 2, 1, 3)).reshape(ts * ns, D)
    fng = final_norm_g[None, :]
    pa, pc, sa, sc, sv = [], [], [], [], []
    for l in range(depth):
        lw = _layer_weights(l, p)
        lw["w_sp4"], lw["b_sp4"] = _sample_spatial(l, p, ts)

        x1p, h2p, route_p, ha, hc = _mixer_prompt(xp, mod, l, nb_, ns, lw)
        pa.append(ha)
        pc.append(hc)
        x1s, h2s, route_s, ha_s, hc_s, v_s = _mixer_sample(
            xs, mod, l, jnp.transpose(state_conv_a[l], (1, 0, 2)),
            jnp.transpose(state_conv_c[l], (1, 0, 2)), lw, nsplit)
        sa.append(jnp.transpose(ha_s, (1, 0, 2)))
        sc.append(jnp.transpose(hc_s, (1, 0, 2)))
        sv.append(jnp.transpose(v_s, (1, 0, 2)))

        dest_p, dest_s, tbl = _plan(route_p, route_s)
        dest_p, dest_s = dest_p.reshape(-1), dest_s.reshape(-1)
        rows_buf = _scatter_rows(dest_p, h2p, None, nblocks * BM, TP)
        rows_buf = _scatter_rows(dest_s, h2s, rows_buf, nblocks * BM, tps)
        y_rows = _experts(tbl, rows_buf, w1, w3, w2, l)

        last = l == depth - 1
        xp = _combine(dest_p, x1p, route_p, mod, l, fng, y_rows, TP, final_norm=last,
                      g2_row0=ns, tiles_per_g2=seq // TP)
        xs = _combine(dest_s, x1s, route_s, mod, l, fng, y_rows, tps, final_norm=last,
                      seqs_per_tile=nbs)

    y_prompt = xp.reshape(nb_, seq, D)
    y_sample = jnp.transpose(xs.reshape(nsplit, ts, nbs, D), (0, 2, 1, 3)).reshape(ns, ts, D)
    return (y_prompt, y_sample, jnp.stack(pa), jnp.stack(pc), jnp.stack(sa), jnp.stack(sc),
            jnp.stack(sv))
```

```python
import functools

import jax
import jax.numpy as jnp
from jax import lax
from jax.experimental import pallas as pl
from jax.experimental.pallas import tpu as pltpu

F32 = jnp.float32
BF16 = jnp.bfloat16

D = 1024
W = D // 2
KA = 31
KC = 3
NG_A = 8
CHUNK = 128
HEADS = 8
NGRP = 4
EPG = 8
NEXP = NGRP * EPG
FF = D // 2
EPS = 1e-6
IN_COLS = 7 * W + 3 * D
LANES = 128
ROUTE_COLS = LANES

TT = 512
SUBLANES = 8
A_HDR = 32
A_PAD = SUBLANES
A_ROWS = 64
C_HDR = 8
TP = 512
TPP = 2048
BM = 256
VMEM_LIMIT = 56 * 1024 * 1024

NEG = -1e30


def _sigmoid(x):
    return 1.0 / (1.0 + jnp.exp(-x))


def _silu(x):
    return x * _sigmoid(x)


def _gelu_tanh(x):
    return 0.5 * x * (1.0 + jnp.tanh(0.7978845608028654 * (x + 0.044715 * x * x * x)))


def _rms(x, g):
    return x * lax.rsqrt(jnp.mean(x * x, axis=-1, keepdims=True) + EPS) * g


def _dot(a, b):
    return jnp.dot(a, b, preferred_element_type=F32)


def _dot_hi(a, b):
    return jnp.dot(a, b, preferred_element_type=F32, precision=lax.Precision.HIGHEST)


ROW_TILE = D // LANES


def _store_row_tiles(ref, val):
    r = val.shape[0]
    for j in range(ROW_TILE):
        ref[pl.ds(j, r, stride=ROW_TILE), :] = val[:, j * LANES:(j + 1) * LANES]


def _load_row_tiles(ref):
    r = ref.shape[0] // ROW_TILE
    return jnp.concatenate([ref[pl.ds(j, r, stride=ROW_TILE), :] for j in range(ROW_TILE)], axis=1)


def _row_tile(ref, r):
    return ref.at[pl.ds(pl.multiple_of(r * ROW_TILE, ROW_TILE), ROW_TILE)]


def _mod_kernel(c_ref, w_ref, b_ref, o_ref):
    c = c_ref[...]
    s = _silu(c).astype(BF16)
    o_ref[0, 0] = _dot(s, w_ref[0].astype(BF16)) + b_ref[0]


def _adaln(c_all, w_ada, b_ada):
    nl = w_ada.shape[0]
    r = c_all.shape[0]
    return pl.pallas_call(
        _mod_kernel,
        grid=(nl, 6),
        in_specs=[
            pl.BlockSpec((r, D), lambda l, j: (0, 0)),
            pl.BlockSpec((1, D, D), lambda l, j: (l, 0, j)),
            pl.BlockSpec((1, 1, D), lambda l, j: (l, 0, j)),
        ],
        out_specs=pl.BlockSpec((1, 1, r, D), lambda l, j: (l, j, 0, 0)),
        out_shape=jax.ShapeDtypeStruct((nl, 6, r, D), F32),
        compiler_params=pltpu.CompilerParams(
            dimension_semantics=("arbitrary", "arbitrary"), vmem_limit_bytes=VMEM_LIMIT),
        name="adaln_mod",
    )(c_all, w_ada, b_ada.reshape(nl, 1, 6 * D))


def _group_norm_silu(a, gavg_ref, g, b):
    mu = _dot(a.astype(BF16), gavg_ref[...])
    xc = a - mu
    var = _dot((xc * xc).astype(BF16), gavg_ref[...])
    y = xc * lax.rsqrt(var + EPS) * g + b
    return _silu(y)


def _layer_norm(x, g, b):
    mu = jnp.mean(x, axis=-1, keepdims=True)
    xc = x - mu
    var = jnp.mean(xc * xc, axis=-1, keepdims=True)
    return xc * lax.rsqrt(var + EPS) * g + b


def _merge_and_route(x, a, bb, cc, gates_fn, mod, n2g, wbr_ref, wout_ref, wr_ref, br_ref,
                     x1_ref, h2_ref, route_ref):
    g1, sh2, sc2 = mod
    acc = None
    for gi, br_in in enumerate((a, bb, cc)):
        br = _dot(br_in.astype(BF16), wbr_ref[gi])
        term = _sigmoid(gates_fn(gi)) * br
        acc = term if acc is None else acc + term
    mixed = _dot(acc.astype(BF16), wout_ref[...])
    x1 = x + g1 * mixed
    x1_ref[...] = x1
    h2 = _rms(x1, n2g) * (1.0 + sc2) + sh2
    _store_row_tiles(h2_ref, h2)

    logits = _dot(h2.astype(BF16), wr_ref[...]) + br_ref[...]
    rows = logits.shape[0]
    lane = lax.broadcasted_iota(jnp.int32, (rows, ROUTE_COLS), 1).astype(F32)
    is_g = lane < NGRP
    gl = jnp.where(is_g, logits, NEG)
    gmax = jnp.max(gl, axis=-1, keepdims=True)
    gsel = jnp.min(jnp.where(gl == gmax, lane, float(ROUTE_COLS)), axis=-1, keepdims=True)
    gden = jnp.sum(jnp.where(is_g, jnp.exp(gl - gmax), 0.0), axis=-1, keepdims=True)
    gprob = 1.0 / gden
    lo = NGRP + EPG * gsel
    in_grp = jnp.logical_and(lane >= lo, lane < lo + EPG)
    el = jnp.where(in_grp, logits, NEG)
    v1 = jnp.max(el, axis=-1, keepdims=True)
    i1 = jnp.min(jnp.where(el == v1, lane, float(ROUTE_COLS)), axis=-1, keepdims=True)
    el2 = jnp.where(lane == i1, NEG, el)
    v2 = jnp.max(el2, axis=-1, keepdims=True)
    i2 = jnp.min(jnp.where(el2 == v2, lane, float(ROUTE_COLS)), axis=-1, keepdims=True)
    p2 = jnp.exp(v2 - v1)
    w_a = gprob / (1.0 + p2)
    w_b = gprob * p2 / (1.0 + p2)
    route = jnp.where(lane == 0.0, i1 - NGRP,
                      jnp.where(lane == 1.0, i2 - NGRP,
                                jnp.where(lane == 2.0, w_a,
                                          jnp.where(lane == 3.0, w_b, 0.0))))
    route_ref[...] = route


def _mixer_prompt_kernel(x_ref, mod_ref, n1g_ref, n2g_ref, win_ref, adw_ref, adb_ref, ang_ref,
                         anb_ref, bng_ref, bnb_ref, wsp_ref, bsp_ref, ccw_ref, wbr_ref, wout_ref,
                         gavg_ref, wr_ref, br_ref,
                         x1_ref, h2_ref, route_ref, ha_ref, hc_ref,
                         abuf, cbuf, aconv, *, mod_row0):
    first = pl.program_id(1) == 0
    abuf[0:A_HDR, :] = jnp.where(first, 0.0, abuf[TT:TT + A_HDR, :])
    abuf[A_HDR + TT:A_HDR + TT + A_PAD, :] = jnp.zeros((A_PAD, W), F32)
    cbuf[0:C_HDR, :] = jnp.where(first, 0.0, cbuf[TT:TT + C_HDR, :])

    x = x_ref[...]
    mrow = mod_row0 + pl.program_id(0)
    sh1, sc1, g1, sh2, sc2 = (mod_ref[0, s, pl.ds(mrow, 1), :] for s in range(5))
    h = (_rms(x, n1g_ref[...]) * (1.0 + sc1) + sh1).astype(BF16)

    def zsec(lo, hi):
        return _dot(h, win_ref[:, lo:hi])

    a_glu = zsec(0, W) * _sigmoid(zsec(W, 2 * W))
    abuf[A_HDR:A_HDR + TT, :] = a_glu
    off = A_HDR - (KA - 1)
    rc = A_ROWS

    def conv_pass(p):
        lt, half = p // 2, p % 2
        ls = slice(lt * LANES, (lt + 1) * LANES)
        for r0 in range(half * (TT // 2), (half + 1) * (TT // 2), rc):
            y = None
            for r in range(SUBLANES):
                z = None
                for q in range((KA - 1 + off) // SUBLANES + 1):
                    k = SUBLANES * q + r - off
                    if 0 <= k < KA:
                        rows = slice(r0 + SUBLANES * q, r0 + SUBLANES * q + rc + SUBLANES)
                        term = adw_ref[k:k + 1, ls] * abuf[rows, ls]
                        z = term if z is None else z + term
                part = z[r:r + rc, :]
                y = part if y is None else y + part
            aconv[r0:r0 + rc, ls] = y + adb_ref[:, ls]

    conv_pass(0)
    u = _gelu_tanh(zsec(2 * W, 3 * W))
    conv_pass(1)
    v = _layer_norm(_gelu_tanh(zsec(3 * W, 4 * W)), bng_ref[...], bnb_ref[...])
    conv_pass(2)

    cch = zsec(5 * W, 6 * W) * zsec(6 * W, 7 * W)
    cbuf[C_HDR:C_HDR + TT, :] = cch
    coff = C_HDR - (KC - 1)
    ych = (ccw_ref[0:1, :] * cbuf[coff:coff + TT, :]
           + ccw_ref[1:2, :] * cbuf[coff + 1:coff + 1 + TT, :]
           + ccw_ref[2:3, :] * cbuf[coff + 2:coff + 2 + TT, :])
    conv_pass(3)
    cc = zsec(4 * W, 5 * W) * ych
    hc_ref[0] = cbuf[C_HDR + TT - (KC - 1):C_HDR + TT, :]
    conv_pass(4)
    gate_z = [zsec(7 * W, 7 * W + D)]
    conv_pass(5)
    gate_z.append(zsec(7 * W + D, 7 * W + 2 * D))
    conv_pass(6)
    gate_z.append(zsec(7 * W + 2 * D, 7 * W + 3 * D))
    conv_pass(7)
    ha_ref[0] = abuf[A_HDR + TT - (KA - 1):A_HDR + TT, :]

    vb = v.astype(BF16)
    low_head = lax.broadcasted_iota(jnp.int32, (CHUNK, LANES), 1) < (W // HEADS)
    zero_b = jnp.zeros((CHUNK, LANES), BF16)
    s_parts = []
    for c in range(TT // CHUNK):
        tiles = []
        for lt in range(W // LANES):
            vt = vb[c * CHUNK:(c + 1) * CHUNK, lt * LANES:(lt + 1) * LANES]
            stacked = jnp.concatenate([jnp.where(low_head, vt, zero_b),
                                       jnp.where(low_head, zero_b, vt)], axis=0)
            tiles.append(_dot(wsp_ref[lt], stacked))
        s_parts.append(jnp.concatenate(tiles, axis=1) + bsp_ref[...])
    bb = u * jnp.concatenate(s_parts, axis=0)
    a = _group_norm_silu(aconv[...], gavg_ref, ang_ref[...], anb_ref[...])

    _merge_and_route(x, a, bb, cc, lambda gi: gate_z[gi], (g1, sh2, sc2), n2g_ref[...], wbr_ref,
                     wout_ref, wr_ref, br_ref, x1_ref, h2_ref, route_ref)


def _const_spec(shape):
    nd = len(shape)
    return pl.BlockSpec(shape, lambda *_: (0,) * nd, pipeline_mode=pl.Buffered(1))


def _mixer_prompt(x, mod, layer, n, mod_row0, lw):
    t = x.shape[0] // n
    nt = t // TT
    mod_rows = mod.shape[2]
    row_spec = lambda cols: pl.BlockSpec((TT, cols), lambda i, j: (i * nt + j, 0))
    consts = [lw["n1g"], lw["n2g"], lw["w_in"], lw["a_dw_w"], lw["a_dw_b"], lw["a_norm_g"],
              lw["a_norm_b"], lw["b_norm_g"], lw["b_norm_b"], lw["w_sp"], lw["b_sp"], lw["c_conv_w"],
              lw["w_branch"], lw["w_out"], lw["gavg"], lw["w_router"], lw["b_router"]]
    return pl.pallas_call(
        functools.partial(_mixer_prompt_kernel, mod_row0=mod_row0),
        grid=(n, nt),
        in_specs=[row_spec(D),
                  pl.BlockSpec((1, 6, mod_rows, D), lambda i, j: (layer, 0, 0, 0),
                               pipeline_mode=pl.Buffered(1))]
        + [_const_spec(c.shape) for c in consts],
        out_specs=[row_spec(D),
                   pl.BlockSpec((TT * ROW_TILE, LANES), lambda i, j: (i * nt + j, 0)),
                   row_spec(ROUTE_COLS),
                   pl.BlockSpec((1, KA - 1, W), lambda i, j: (i, 0, 0)),
                   pl.BlockSpec((1, KC - 1, W), lambda i, j: (i, 0, 0))],
        out_shape=[jax.ShapeDtypeStruct((n * t, D), F32),
                   jax.ShapeDtypeStruct((n * t * ROW_TILE, LANES), F32),
                   jax.ShapeDtypeStruct((n * t, ROUTE_COLS), F32),
                   jax.ShapeDtypeStruct((n, KA - 1, W), F32),
                   jax.ShapeDtypeStruct((n, KC - 1, W), F32)],
        scratch_shapes=[pltpu.VMEM((A_HDR + TT + A_PAD, W), F32),
                        pltpu.VMEM((C_HDR + TT, W), F32),
                        pltpu.VMEM((TT, W), F32)],
        compiler_params=pltpu.CompilerParams(
            dimension_semantics=("arbitrary", "arbitrary"), vmem_limit_bytes=VMEM_LIMIT),
        name="mixer_prompt",
    )(x, mod, *consts)


def _mixer_sample_kernel(x_ref, mod_ref, hista_ref, histc_ref, n1g_ref, n2g_ref, win_ref, adw_ref,
                         adb_ref, ang_ref, anb_ref, bng_ref, bnb_ref, wsp4_ref, bsp4_ref,
                         ccw_ref, wbr_ref, wout_ref, gavg_ref, wr_ref, br_ref,
                         x1_ref, h2_ref, route_ref, ha_ref, hc_ref, v_ref):
    ts, nb = v_ref.shape[0], v_ref.shape[1]
    x = x_ref[...]

    def modrow(i):
        return jnp.concatenate([mod_ref[0, i]] * ts, axis=0)

    h = (_rms(x, n1g_ref[...]) * (1.0 + modrow(1)) + modrow(0)).astype(BF16)

    def zsec(lo, hi):
        return _dot(h, win_ref[:, lo:hi])

    def tslab(arr, j):
        return arr[j * nb:(j + 1) * nb, :]

    a_glu = zsec(0, W) * _sigmoid(zsec(W, 2 * W))
    kh = KA - 1
    a_conv = []
    for tq in range(ts):
        acc = jnp.zeros((nb, W), F32) + adb_ref[...]
        for r in range(tq, kh):
            acc = acc + adw_ref[r - tq:r - tq + 1, :] * hista_ref[r]
        for j in range(tq + 1):
            acc = acc + adw_ref[kh + j - tq:kh + j - tq + 1, :] * tslab(a_glu, j)
        a_conv.append(acc)
    a = _group_norm_silu(jnp.concatenate(a_conv, axis=0), gavg_ref, ang_ref[...], anb_ref[...])
    for r in range(kh - ts):
        ha_ref[r] = hista_ref[r + ts]
    for j in range(ts):
        ha_ref[kh - ts + j] = tslab(a_glu, j)

    u = _gelu_tanh(zsec(2 * W, 3 * W))
    v = _layer_norm(_gelu_tanh(zsec(3 * W, 4 * W)), bng_ref[...], bnb_ref[...])
    for j in range(ts):
        v_ref[j] = tslab(v, j)
    s_rows = []
    for tq in range(ts):
        s = jnp.zeros((nb, W), F32) + bsp4_ref[tq:tq + 1, :]
        for sq in range(tq + 1):
            s = s + wsp4_ref[tq * ts + sq:tq * ts + sq + 1, :] * tslab(v, sq)
        s_rows.append(s)
    bb = u * jnp.concatenate(s_rows, axis=0)

    cch = zsec(5 * W, 6 * W) * zsec(6 * W, 7 * W)
    xp = [histc_ref[r] for r in range(KC - 1)] + [tslab(cch, j) for j in range(ts)]
    ych = jnp.concatenate(
        [sum(ccw_ref[k:k + 1, :] * xp[tq + k] for k in range(KC)) for tq in range(ts)], axis=0)
    cc = zsec(4 * W, 5 * W) * ych
    for r in range(KC - 1):
        hc_ref[r] = xp[ts + r]

    def gates_fn(gi):
        return zsec(7 * W + gi * D, 7 * W + (gi + 1) * D)

    _merge_and_route(x, a, bb, cc, gates_fn, (modrow(2), modrow(3), modrow(4)), n2g_ref[...],
                     wbr_ref, wout_ref, wr_ref, br_ref, x1_ref, h2_ref, route_ref)


def _mixer_sample(x_rows, mod, layer, hist_a_tm, hist_c_tm, lw, nsplit):
    n = hist_a_tm.shape[1]
    ts = x_rows.shape[0] // n
    nb = n // nsplit
    rows = ts * nb
    consts = [lw["n1g"], lw["n2g"], lw["w_in"], lw["a_dw_w"], lw["a_dw_b"],
              lw["a_norm_g"], lw["a_norm_b"], lw["b_norm_g"], lw["b_norm_b"], lw["w_sp4"], lw["b_sp4"],
              lw["c_conv_w"], lw["w_branch"], lw["w_out"], lw["gavg"], lw["w_router"], lw["b_router"]]
    seq3 = lambda k, cols: pl.BlockSpec((k, nb, cols), lambda i: (0, i, 0))
    row_spec = lambda cols: pl.BlockSpec((rows, cols), lambda i: (i, 0))
    return pl.pallas_call(
        _mixer_sample_kernel,
        grid=(nsplit,),
        in_specs=[row_spec(D), pl.BlockSpec((1, 6, nb, D), lambda i: (layer, 0, i, 0)),
                  seq3(KA - 1, W), seq3(KC - 1, W)]
        + [_const_spec(c.shape) for c in consts],
        out_specs=[row_spec(D),
                   pl.BlockSpec((rows * ROW_TILE, LANES), lambda i: (i, 0)),
                   row_spec(ROUTE_COLS),
                   seq3(KA - 1, W), seq3(KC - 1, W), seq3(ts, W)],
        out_shape=[jax.ShapeDtypeStruct((ts * n, D), F32),
                   jax.ShapeDtypeStruct((ts * n * ROW_TILE, LANES), F32),
                   jax.ShapeDtypeStruct((ts * n, ROUTE_COLS), F32),
                   jax.ShapeDtypeStruct((KA - 1, n, W), F32),
                   jax.ShapeDtypeStruct((KC - 1, n, W), F32),
                   jax.ShapeDtypeStruct((ts, n, W), F32)],
        compiler_params=pltpu.CompilerParams(
            dimension_semantics=("arbitrary",), vmem_limit_bytes=VMEM_LIMIT),
        name="mixer_sample",
    )(x_rows, mod, hist_a_tm, hist_c_tm, *consts)


def _plan_kernel(rp_ref, rs_ref, tri_ref, destp_ref, dests_ref, tbl_ref, carry, pstart, *, ntp):
    ph = pl.program_id(0)
    i = pl.program_id(1)
    nt = pl.num_programs(1)
    lane = lax.broadcasted_iota(jnp.int32, (TP, ROUTE_COLS), 1).astype(F32)

    @pl.when(jnp.logical_and(ph == 0, i == 0))
    def _():
        carry[...] = jnp.zeros_like(carry)

    def sub_tile(r, dest_ref, row0):
        hot_a = lane == r[:, 0:1]
        hot_b = lane == r[:, 1:2]
        s = jnp.where(jnp.logical_or(hot_a, hot_b), 1.0, 0.0)

        @pl.when(ph == 1)
        def _():
            pre = _dot(tri_ref[...], s.astype(BF16)) + carry[...] + pstart[...]
            d_a = jnp.sum(jnp.where(hot_a, pre, 0.0), axis=-1, keepdims=True)
            d_b = jnp.sum(jnp.where(hot_b, pre, 0.0), axis=-1, keepdims=True)
            l2 = lax.broadcasted_iota(jnp.int32, (TP, 2), 1)
            dest_ref[row0:row0 + TP, :] = jnp.where(l2 == 0, d_a, d_b).astype(jnp.int32)

        carry[...] = carry[...] + jnp.sum(s, axis=0, keepdims=True)

    @pl.when(i < ntp)
    def _():
        for sub in range(rp_ref.shape[0] // TP):
            sub_tile(rp_ref[sub * TP:(sub + 1) * TP, :], destp_ref, sub * TP)

    @pl.when(i == ntp)
    def _():
        for sub in range(rs_ref.shape[0] // TP):
            sub_tile(rs_ref[sub * TP:(sub + 1) * TP, :], dests_ref, sub * TP)

    @pl.when(jnp.logical_and(ph == 0, i == nt - 1))
    def _():
        cnt = carry[...]
        nblk = jnp.floor((cnt + (BM - 1)) * (1.0 / BM))
        ri = lax.broadcasted_iota(jnp.int32, (ROUTE_COLS, ROUTE_COLS), 0)
        ci = lax.broadcasted_iota(jnp.int32, (ROUTE_COLS, ROUTE_COLS), 1)
        upper = jnp.where(ri <= ci, 1.0, 0.0)
        cum = _dot_hi(jnp.broadcast_to(nblk, (8, ROUTE_COLS)), upper)[0:1, :]
        cum_ex = cum - nblk
        pstart[...] = cum_ex * BM
        row = lax.broadcasted_iota(jnp.int32, (SUBLANES, ROUTE_COLS), 0)
        tbl = jnp.where(row == 0, cum_ex, jnp.where(row == 1, cnt, 0.0))
        tbl_ref[...] = tbl.astype(jnp.int32)
        carry[...] = jnp.zeros_like(carry)


def _plan(route_p, route_s):
    rows_p, rows_s = route_p.shape[0], route_s.shape[0]
    ntp = rows_p // TPP
    nbp = SUBLANES
    ri = lax.broadcasted_iota(jnp.int32, (TP, TP), 0)
    ci = lax.broadcasted_iota(jnp.int32, (TP, TP), 1)
    tri = jnp.where(ci < ri, 1.0, 0.0).astype(BF16)
    return pl.pallas_call(
        functools.partial(_plan_kernel, ntp=ntp),
        grid=(2, ntp + 1),
        in_specs=[pl.BlockSpec((TPP, ROUTE_COLS), lambda p, i: (jnp.minimum(i, ntp - 1), 0)),
                  pl.BlockSpec((rows_s, ROUTE_COLS), lambda p, i: (0, 0)),
                  pl.BlockSpec((TP, TP), lambda p, i: (0, 0))],
        out_specs=[pl.BlockSpec((TPP, 2), lambda p, i: (jnp.minimum(i, ntp - 1) * p, 0)),
                   pl.BlockSpec((rows_s, 2), lambda p, i: (0, 0)),
                   pl.BlockSpec((nbp, ROUTE_COLS), lambda p, i: (0, 0))],
        out_shape=[jax.ShapeDtypeStruct((rows_p, 2), jnp.int32),
                   jax.ShapeDtypeStruct((rows_s, 2), jnp.int32),
                   jax.ShapeDtypeStruct((nbp, ROUTE_COLS), jnp.int32)],
        scratch_shapes=[pltpu.VMEM((1, ROUTE_COLS), F32), pltpu.VMEM((1, ROUTE_COLS), F32)],
        compiler_params=pltpu.CompilerParams(
            dimension_semantics=("arbitrary", "arbitrary"), vmem_limit_bytes=VMEM_LIMIT),
        name="moe_plan",
    )(route_p, route_s, tri)


def _scatter_kernel(dest_ref, h_ref, *rest):
    rows_ref, sem = rest[-2:]
    tp = h_ref.shape[0] // ROW_TILE

    def row_copy(r, k):
        d = dest_ref[2 * r + k]
        return pltpu.make_async_copy(_row_tile(h_ref, r), _row_tile(rows_ref, d), sem)

    def issue(r, c):
        row_copy(r, 0).start(priority=0)
        row_copy(r, 1).start(priority=1)
        return c

    lax.fori_loop(0, tp, issue, 0, unroll=8)

    def drain(r, c):
        row_copy(r, 0).wait()
        row_copy(r, 1).wait()
        return c

    lax.fori_loop(0, tp, drain, 0, unroll=8)


def _scatter_rows(dest_flat, h2, rows_buf, n_rows, tp):
    nt = h2.shape[0] // (tp * ROW_TILE)
    in_specs = [pl.BlockSpec((2 * tp,), lambda i: (i,), memory_space=pltpu.SMEM),
                pl.BlockSpec((tp * ROW_TILE, LANES), lambda i: (i, 0))]
    args = [dest_flat, h2]
    aliases = {}
    if rows_buf is not None:
        in_specs.append(pl.BlockSpec(memory_space=pl.ANY))
        args.append(rows_buf)
        aliases = {2: 0}
    return pl.pallas_call(
        _scatter_kernel,
        grid=(nt,),
        in_specs=in_specs,
        out_specs=pl.BlockSpec(memory_space=pl.ANY),
        out_shape=jax.ShapeDtypeStruct((n_rows * ROW_TILE, LANES), F32),
        scratch_shapes=[pltpu.SemaphoreType.DMA(())],
        input_output_aliases=aliases,
        compiler_params=pltpu.CompilerParams(
            dimension_semantics=("arbitrary",), vmem_limit_bytes=VMEM_LIMIT,
            has_side_effects=True),
        name="moe_scatter",
    )(*args)


ROW_DMA_PRIORITY = 1


def _expert_kernel(blk0_ref, cnt_ref, x_hbm, w1_ref, w3_ref, w2_ref, y_hbm,
                   xbuf, ybuf, w1b, w3b, w2b, sem_in, sem_out):
    e = pl.program_id(0)
    cnt = cnt_ref[e]
    nblk = lax.shift_right_logical(cnt + (BM - 1), BM.bit_length() - 1)
    blk0 = blk0_ref[e]
    blk_rows = BM * ROW_TILE

    def block_rows(ref, j):
        return ref.at[pl.ds(pl.multiple_of((blk0 + j) * blk_rows, blk_rows), blk_rows)]

    def in_copy(j, slot):
        return pltpu.make_async_copy(block_rows(x_hbm, j), xbuf.at[slot], sem_in.at[slot])

    def out_copy(j, slot):
        return pltpu.make_async_copy(ybuf.at[slot], block_rows(y_hbm, j), sem_out.at[slot])

    @pl.when(nblk > 0)
    def _():
        in_copy(0, 0).start(priority=ROW_DMA_PRIORITY)
        w1b[...] = w1_ref[0, 0].astype(BF16)
        w3b[...] = w3_ref[0, 0].astype(BF16)
        w2b[...] = w2_ref[0, 0].astype(BF16)

        def body(j, carry):
            slot = lax.rem(j, 2)

            @pl.when(j + 1 < nblk)
            def _():
                in_copy(j + 1, 1 - slot).start(priority=ROW_DMA_PRIORITY)

            in_copy(j, slot).wait()

            @pl.when(j >= 2)
            def _():
                out_copy(j - 2, slot).wait()

            row = lax.broadcasted_iota(jnp.int32, (BM, D), 0)
            x = jnp.where(row < cnt - j * BM, _load_row_tiles(xbuf.at[slot]), 0.0).astype(BF16)
            h1 = _dot(x, w1b[...])
            h3 = _dot(x, w3b[...])
            act = (_silu(h1) * h3).astype(BF16)
            _store_row_tiles(ybuf.at[slot], _dot(act, w2b[...]))
            out_copy(j, slot).start(priority=ROW_DMA_PRIORITY)
            return carry

        lax.fori_loop(0, nblk, body, 0)

        @pl.when(nblk >= 2)
        def _():
            out_copy(nblk - 2, lax.rem(nblk, 2)).wait()

        out_copy(nblk - 1, lax.rem(nblk - 1, 2)).wait()


def _experts(tbl, rows_buf, w1, w3, w2, layer):
    blk0, cnt = tbl[0, :NEXP], tbl[1, :NEXP]
    wspec = lambda shape: pl.BlockSpec((1, 1) + shape, lambda e, blk0, cnt: (layer, e, 0, 0))
    return pl.pallas_call(
        _expert_kernel,
        grid_spec=pltpu.PrefetchScalarGridSpec(
            num_scalar_prefetch=2,
            grid=(NEXP,),
            in_specs=[pl.BlockSpec(memory_space=pl.ANY),
                      wspec((D, FF)), wspec((D, FF)), wspec((FF, D))],
            out_specs=pl.BlockSpec(memory_space=pl.ANY),
            scratch_shapes=[pltpu.VMEM((2, BM * ROW_TILE, LANES), F32),
                            pltpu.VMEM((2, BM * ROW_TILE, LANES), F32),
                            pltpu.VMEM((D, FF), BF16), pltpu.VMEM((D, FF), BF16),
                            pltpu.VMEM((FF, D), BF16),
                            pltpu.SemaphoreType.DMA((2,)), pltpu.SemaphoreType.DMA((2,))],
        ),
        out_shape=jax.ShapeDtypeStruct(rows_buf.shape, F32),
        compiler_params=pltpu.CompilerParams(
            dimension_semantics=("arbitrary",), vmem_limit_bytes=VMEM_LIMIT,
            has_side_effects=True),
        name="moe_experts",
    )(blk0, cnt, rows_buf, w1, w3, w2)


def _combine_kernel(dest_ref, dest_next_ref, x1_ref, route_ref, g2_ref, fng_ref, y_ref, o_ref,
                    ybuf, sem, *, final_norm, g2_row0, tiles_per_g2):
    tp = x1_ref.shape[0]
    i = pl.program_id(0)
    n = pl.num_programs(0)
    slot = lax.rem(i, 2)

    def gather(d_ref, s):
        def row_copy(r, k):
            return pltpu.make_async_copy(_row_tile(y_ref, d_ref[2 * r + k]),
                                         _row_tile(ybuf.at[s, k], r), sem.at[s])

        def issue(r, c):
            row_copy(r, 0).start(priority=0)
            row_copy(r, 1).start(priority=1)
            return c

        lax.fori_loop(0, tp, issue, 0, unroll=8)

    @pl.when(i == 0)
    def _():
        gather(dest_ref, 0)

    @pl.when(i + 1 < n)
    def _():
        gather(dest_next_ref, 1 - slot)

    for k in range(2):
        pltpu.make_async_copy(y_ref.at[pl.ds(0, tp * ROW_TILE)], ybuf.at[slot, k], sem.at[slot]).wait()

    route = route_ref[...]
    w_a = route[:, 2:3]
    w_b = route[:, 3:4]
    if tiles_per_g2:
        g2 = g2_ref[0, 0, pl.ds(g2_row0 + i // tiles_per_g2, 1), :]
    else:
        g2 = jnp.concatenate([g2_ref[0, 0]] * (tp // g2_ref.shape[2]), axis=0)
    y = (w_a * _load_row_tiles(ybuf.at[slot, 0]) + w_b * _load_row_tiles(ybuf.at[slot, 1]))
    x2 = x1_ref[...] + g2 * y
    if final_norm:
        x2 = _rms(x2, fng_ref[...])
    o_ref[...] = x2


def _combine(dest_flat, x1, route, mod, layer, fng, y_rows, tp, *, final_norm, g2_row0=0,
             tiles_per_g2=0, seqs_per_tile=0):
    nt = x1.shape[0] // tp
    if tiles_per_g2:
        g2_spec = pl.BlockSpec((1, 1, mod.shape[2], D), lambda i: (layer, 5, 0, 0))
    else:
        g2_spec = pl.BlockSpec((1, 1, seqs_per_tile, D), lambda i: (layer, 5, i, 0))
    return pl.pallas_call(
        functools.partial(_combine_kernel, final_norm=final_norm, g2_row0=g2_row0,
                          tiles_per_g2=tiles_per_g2),
        grid=(nt,),
        in_specs=[pl.BlockSpec((2 * tp,), lambda i: (i,), memory_space=pltpu.SMEM),
                  pl.BlockSpec((2 * tp,), lambda i: (jnp.minimum(i + 1, nt - 1),),
                               memory_space=pltpu.SMEM),
                  pl.BlockSpec((tp, D), lambda i: (i, 0)),
                  pl.BlockSpec((tp, ROUTE_COLS), lambda i: (i, 0)),
                  g2_spec,
                  pl.BlockSpec((1, D), lambda i: (0, 0)),
                  pl.BlockSpec(memory_space=pl.ANY)],
        out_specs=pl.BlockSpec((tp, D), lambda i: (i, 0)),
        out_shape=jax.ShapeDtypeStruct(x1.shape, F32),
        scratch_shapes=[pltpu.VMEM((2, 2, tp * ROW_TILE, LANES), F32),
                        pltpu.SemaphoreType.DMA((2,))],
        compiler_params=pltpu.CompilerParams(
            dimension_semantics=("arbitrary",), vmem_limit_bytes=VMEM_LIMIT),
        name="moe_combine",
    )(dest_flat, dest_flat, x1, route, mod, fng, y_rows)


def _layer_weights(l, p):
    hd = W // HEADS
    gi = jnp.arange(W) // (W // NG_A)
    gavg = jnp.where(gi[:, None] == gi[None, :], 1.0 / (W // NG_A), 0.0).astype(BF16)
    tril = jnp.tril(jnp.ones((CHUNK, CHUNK), dtype=bool))
    w_sp = jnp.where(tril[None], p["b_spatial_w"][l], 0.0)
    w_router = jnp.concatenate(
        [p["w_router_g"][l], p["w_router_e"][l],
         jnp.zeros((D, ROUTE_COLS - NGRP - NEXP), F32)], axis=1)
    b_router = jnp.concatenate(
        [p["b_router_g"][l], p["b_router_e"][l],
         jnp.zeros((ROUTE_COLS - NGRP - NEXP,), F32)])[None, :]
    return {
        "n1g": p["norm1_g"][l][None, :], "n2g": p["norm2_g"][l][None, :],
        "w_in": p["w_in"][l].astype(BF16),
        "a_dw_w": p["a_dw_w"][l], "a_dw_b": p["a_dw_b"][l][None, :],
        "a_norm_g": p["a_norm_g"][l][None, :], "a_norm_b": p["a_norm_b"][l][None, :],
        "b_norm_g": p["b_norm_g"][l][None, :], "b_norm_b": p["b_norm_b"][l][None, :],
        "w_sp": jnp.concatenate([w_sp[0::2], w_sp[1::2]], axis=2).astype(BF16),
        "b_sp": jnp.repeat(p["b_spatial_b"][l].T, hd, axis=1),
        "c_conv_w": p["c_conv_w"][l],
        "w_branch": p["w_branch"][l].astype(BF16), "w_out": p["w_out"][l].astype(BF16),
        "gavg": gavg, "w_router": w_router.astype(BF16), "b_router": b_router,
    }


def _sample_spatial(l, p, ts):
    hd = W // HEADS
    w = p["b_spatial_w"][l][:, :ts, :ts]
    w = jnp.where(jnp.tril(jnp.ones((ts, ts), dtype=bool))[None], w, 0.0)
    w4 = jnp.repeat(jnp.transpose(w, (1, 2, 0)).reshape(ts * ts, HEADS), hd, axis=1)
    b4 = jnp.repeat(p["b_spatial_b"][l][:, :ts].T, hd, axis=1)
    return w4, b4


def kernel(x_prompt, x_sample, c_prompt, c_sample, state_conv_a, state_conv_c, norm1_g, norm2_g, w_ada, b_ada, w_in, a_dw_w, a_dw_b, a_norm_g, a_norm_b, b_norm_g, b_norm_b, b_spatial_w, b_spatial_b, c_conv_w, w_branch, w_out, w_router_g, b_router_g, w_router_e, b_router_e, w1, w3, w2, final_norm_g):
    p = dict(norm1_g=norm1_g, norm2_g=norm2_g, w_in=w_in, a_dw_w=a_dw_w, a_dw_b=a_dw_b,
             a_norm_g=a_norm_g, a_norm_b=a_norm_b, b_norm_g=b_norm_g, b_norm_b=b_norm_b,
             b_spatial_w=b_spatial_w, b_spatial_b=b_spatial_b, c_conv_w=c_conv_w,
             w_branch=w_branch, w_out=w_out, w_router_g=w_router_g, b_router_g=b_router_g,
             w_router_e=w_router_e, b_router_e=b_router_e)
    depth = w_in.shape[0]
    nb_, seq, _ = x_prompt.shape
    ns, ts, _ = x_sample.shape
    n_tok = nb_ * seq + ns * ts
    nblocks = -(-(2 * n_tok) // BM) + NEXP
    nsplit = 2
    nbs = ns // nsplit
    tps = ts * nbs
    assert seq % TT == 0 and TT % CHUNK == 0 and seq % TP == 0 and (nb_ * seq) % TPP == 0
    assert (ts * ns) % TP == 0 and ns % SUBLANES == 0

    mod = _adaln(jnp.concatenate([c_sample, c_prompt], axis=0), w_ada, b_ada)

    xp = x_prompt.reshape(nb_ * seq, D)
    xs = jnp.transpose(x_sample.reshape(nsplit, nbs, ts, D), (0, 2, 1, 3)).reshape(ts * ns, D)
    fng = final_norm_g[None, :]
    pa, pc, sa, sc, sv = [], [], [], [], []
    for l in range(depth):
        lw = _layer_weights(l, p)
        lw["w_sp4"], lw["b_sp4"] = _sample_spatial(l, p, ts)

        x1p, h2p, route_p, ha, hc = _mixer_prompt(xp, mod, l, nb_, ns, lw)
        pa.append(ha)
        pc.append(hc)
        x1s, h2s, route_s, ha_s, hc_s, v_s = _mixer_sample(
            xs, mod, l, jnp.transpose(state_conv_a[l], (1, 0, 2)),
            jnp.transpose(state_conv_c[l], (1, 0, 2)), lw, nsplit)
        sa.append(jnp.transpose(ha_s, (1, 0, 2)))
        sc.append(jnp.transpose(hc_s, (1, 0, 2)))
        sv.append(jnp.transpose(v_s, (1, 0, 2)))

        dest_p, dest_s, tbl = _plan(route_p, route_s)
        dest_p, dest_s = dest_p.reshape(-1), dest_s.reshape(-1)
        rows_buf = _scatter_rows(dest_p, h2p, None, nblocks * BM, TP)
        rows_buf = _scatter_rows(dest_s, h2s, rows_buf, nblocks * BM, tps)
        y_rows = _experts(tbl, rows_buf, w1, w3, w2, l)

        last = l == depth - 1
        xp = _combine(dest_p, x1p, route_p, mod, l, fng, y_rows, TP, final_norm=last,
                      g2_row0=ns, tiles_per_g2=seq // TP)
        xs = _combine(dest_s, x1s, route_s, mod, l, fng, y_rows, tps, final_norm=last,
                      seqs_per_tile=nbs)

    y_prompt = xp.reshape(nb_, seq, D)
    y_sample = jnp.transpose(xs.reshape(nsplit, ts, nbs, D), (0, 2, 1, 3)).reshape(ns, ts, D)
    return (y_prompt, y_sample, jnp.stack(pa), jnp.stack(pc), jnp.stack(sa), jnp.stack(sc),
            jnp.stack(sv))
```

```python
import functools

import jax
import jax.numpy as jnp
from jax import lax
from jax.experimental import pallas as pl
from jax.experimental.pallas import tpu as pltpu

F32 = jnp.float32
BF16 = jnp.bfloat16

D = 1024
W = D // 2
KA = 31
KC = 3
NG_A = 8
CHUNK = 128
HEADS = 8
NGRP = 4
EPG = 8
NEXP = NGRP * EPG
FF = D // 2
EPS = 1e-6
IN_COLS = 7 * W + 3 * D
LANES = 128
ROUTE_COLS = LANES

TT = 512
SUBLANES = 8
A_HDR = 32
A_PAD = SUBLANES
A_ROWS = 64
C_HDR = 8
TP = 512
TPP = 2048
BM = 256
VMEM_LIMIT = 56 * 1024 * 1024

NEG = -1e30


def _sigmoid(x):
    return 1.0 / (1.0 + jnp.exp(-x))


def _silu(x):
    return x * _sigmoid(x)


def _gelu_tanh(x):
    return 0.5 * x * (1.0 + jnp.tanh(0.7978845608028654 * (x + 0.044715 * x * x * x)))


def _rms(x, g):
    return x * lax.rsqrt(jnp.mean(x * x, axis=-1, keepdims=True) + EPS) * g


def _dot(a, b):
    return jnp.dot(a, b, preferred_element_type=F32)


def _dot_hi(a, b):
    return jnp.dot(a, b, preferred_element_type=F32, precision=lax.Precision.HIGHEST)


U32 = jnp.uint32
ROW_TILE = D // (2 * LANES)


def _pack_bf16_pair(lo, hi):
    def rne(x):
        b = pltpu.bitcast(x, U32)
        return b + (U32(0x7FFF) + ((b >> 16) & U32(1)))
    return (rne(lo) >> 16) | (rne(hi) & U32(0xFFFF0000))


def _store_row_tiles(ref, val):
    r = val.shape[0]
    for j in range(ROW_TILE):
        lo = val[:, 2 * j * LANES:(2 * j + 1) * LANES]
        hi = val[:, (2 * j + 1) * LANES:(2 * j + 2) * LANES]
        ref[pl.ds(j, r, stride=ROW_TILE), :] = _pack_bf16_pair(lo, hi)


def _load_row_tiles(ref):
    r = ref.shape[0] // ROW_TILE
    parts = []
    for j in range(ROW_TILE):
        w = ref[pl.ds(j, r, stride=ROW_TILE), :]
        parts.append(pltpu.bitcast(w << 16, F32))
        parts.append(pltpu.bitcast(w & U32(0xFFFF0000), F32))
    return jnp.concatenate(parts, axis=1)


def _row_tile(ref, r):
    return ref.at[pl.ds(pl.multiple_of(r * ROW_TILE, ROW_TILE), ROW_TILE)]


def _mod_kernel(c_ref, w_ref, b_ref, o_ref):
    c = c_ref[...]
    s = _silu(c).astype(BF16)
    o_ref[0, 0] = _dot(s, w_ref[0].astype(BF16)) + b_ref[0]


def _adaln(c_all, w_ada, b_ada):
    nl = w_ada.shape[0]
    r = c_all.shape[0]
    return pl.pallas_call(
        _mod_kernel,
        grid=(nl, 6),
        in_specs=[
            pl.BlockSpec((r, D), lambda l, j: (0, 0)),
            pl.BlockSpec((1, D, D), lambda l, j: (l, 0, j)),
            pl.BlockSpec((1, 1, D), lambda l, j: (l, 0, j)),
        ],
        out_specs=pl.BlockSpec((1, 1, r, D), lambda l, j: (l, j, 0, 0)),
        out_shape=jax.ShapeDtypeStruct((nl, 6, r, D), F32),
        compiler_params=pltpu.CompilerParams(
            dimension_semantics=("arbitrary", "arbitrary"), vmem_limit_bytes=VMEM_LIMIT),
        name="adaln_mod",
    )(c_all, w_ada, b_ada.reshape(nl, 1, 6 * D))


def _group_norm_silu(a, gavg_ref, g, b):
    mu = _dot(a.astype(BF16), gavg_ref[...])
    xc = a - mu
    var = _dot((xc * xc).astype(BF16), gavg_ref[...])
    y = xc * lax.rsqrt(var + EPS) * g + b
    return _silu(y)


def _layer_norm(x, g, b):
    mu = jnp.mean(x, axis=-1, keepdims=True)
    xc = x - mu
    var = jnp.mean(xc * xc, axis=-1, keepdims=True)
    return xc * lax.rsqrt(var + EPS) * g + b


def _merge_and_route(x, a, bb, cc, gates_fn, mod, n2g, wbr_ref, wout_ref, wr_ref, br_ref,
                     x1_ref, h2_ref, route_ref):
    g1, sh2, sc2 = mod
    acc = None
    for gi, br_in in enumerate((a, bb, cc)):
        br = _dot(br_in.astype(BF16), wbr_ref[gi])
        term = _sigmoid(gates_fn(gi)) * br
        acc = term if acc is None else acc + term
    mixed = _dot(acc.astype(BF16), wout_ref[...])
    x1 = x + g1 * mixed
    x1_ref[...] = x1
    h2 = _rms(x1, n2g) * (1.0 + sc2) + sh2
    _store_row_tiles(h2_ref, h2)

    logits = _dot(h2.astype(BF16), wr_ref[...]) + br_ref[...]
    rows = logits.shape[0]
    lane = lax.broadcasted_iota(jnp.int32, (rows, ROUTE_COLS), 1).astype(F32)
    is_g = lane < NGRP
    gl = jnp.where(is_g, logits, NEG)
    gmax = jnp.max(gl, axis=-1, keepdims=True)
    gsel = jnp.min(jnp.where(gl == gmax, lane, float(ROUTE_COLS)), axis=-1, keepdims=True)
    gden = jnp.sum(jnp.where(is_g, jnp.exp(gl - gmax), 0.0), axis=-1, keepdims=True)
    gprob = 1.0 / gden
    lo = NGRP + EPG * gsel
    in_grp = jnp.logical_and(lane >= lo, lane < lo + EPG)
    el = jnp.where(in_grp, logits, NEG)
    v1 = jnp.max(el, axis=-1, keepdims=True)
    i1 = jnp.min(jnp.where(el == v1, lane, float(ROUTE_COLS)), axis=-1, keepdims=True)
    el2 = jnp.where(lane == i1, NEG, el)
    v2 = jnp.max(el2, axis=-1, keepdims=True)
    i2 = jnp.min(jnp.where(el2 == v2, lane, float(ROUTE_COLS)), axis=-1, keepdims=True)
    p2 = jnp.exp(v2 - v1)
    w_a = gprob / (1.0 + p2)
    w_b = gprob * p2 / (1.0 + p2)
    route = jnp.where(lane == 0.0, i1 - NGRP,
                      jnp.where(lane == 1.0, i2 - NGRP,
                                jnp.where(lane == 2.0, w_a,
                                          jnp.where(lane == 3.0, w_b, 0.0))))
    route_ref[...] = route


def _mixer_prompt_kernel(x_ref, mod_ref, n1g_ref, n2g_ref, win_ref, adw_ref, adb_ref, ang_ref,
                         anb_ref, bng_ref, bnb_ref, wsp_ref, bsp_ref, ccw_ref, wbr_ref, wout_ref,
                         gavg_ref, wr_ref, br_ref,
                         x1_ref, h2_ref, route_ref, ha_ref, hc_ref,
                         abuf, cbuf, aconv, *, mod_row0):
    first = pl.program_id(1) == 0
    abuf[0:A_HDR, :] = jnp.where(first, 0.0, abuf[TT:TT + A_HDR, :])
    abuf[A_HDR + TT:A_HDR + TT + A_PAD, :] = jnp.zeros((A_PAD, W), F32)
    cbuf[0:C_HDR, :] = jnp.where(first, 0.0, cbuf[TT:TT + C_HDR, :])

    x = x_ref[...]
    mrow = mod_row0 + pl.program_id(0)
    sh1, sc1, g1, sh2, sc2 = (mod_ref[0, s, pl.ds(mrow, 1), :] for s in range(5))
    h = (_rms(x, n1g_ref[...]) * (1.0 + sc1) + sh1).astype(BF16)

    def zsec(lo, hi):
        return _dot(h, win_ref[:, lo:hi])

    a_glu = zsec(0, W) * _sigmoid(zsec(W, 2 * W))
    abuf[A_HDR:A_HDR + TT, :] = a_glu
    off = A_HDR - (KA - 1)
    rc = A_ROWS

    def conv_pass(p):
        lt, half = p // 2, p % 2
        ls = slice(lt * LANES, (lt + 1) * LANES)
        for r0 in range(half * (TT // 2), (half + 1) * (TT // 2), rc):
            y = None
            for r in range(SUBLANES):
                z = None
                for q in range((KA - 1 + off) // SUBLANES + 1):
                    k = SUBLANES * q + r - off
                    if 0 <= k < KA:
                        rows = slice(r0 + SUBLANES * q, r0 + SUBLANES * q + rc + SUBLANES)
                        term = adw_ref[k:k + 1, ls] * abuf[rows, ls]
                        z = term if z is None else z + term
                part = z[r:r + rc, :]
                y = part if y is None else y + part
            aconv[r0:r0 + rc, ls] = y + adb_ref[:, ls]

    conv_pass(0)
    u = _gelu_tanh(zsec(2 * W, 3 * W))
    conv_pass(1)
    v = _layer_norm(_gelu_tanh(zsec(3 * W, 4 * W)), bng_ref[...], bnb_ref[...])
    conv_pass(2)

    cch = zsec(5 * W, 6 * W) * zsec(6 * W, 7 * W)
    cbuf[C_HDR:C_HDR + TT, :] = cch
    coff = C_HDR - (KC - 1)
    ych = (ccw_ref[0:1, :] * cbuf[coff:coff + TT, :]
           + ccw_ref[1:2, :] * cbuf[coff + 1:coff + 1 + TT, :]
           + ccw_ref[2:3, :] * cbuf[coff + 2:coff + 2 + TT, :])
    conv_pass(3)
    cc = zsec(4 * W, 5 * W) * ych
    hc_ref[0] = cbuf[C_HDR + TT - (KC - 1):C_HDR + TT, :]
    conv_pass(4)
    gate_z = [zsec(7 * W, 7 * W + D)]
    conv_pass(5)
    gate_z.append(zsec(7 * W + D, 7 * W + 2 * D))
    conv_pass(6)
    gate_z.append(zsec(7 * W + 2 * D, 7 * W + 3 * D))
    conv_pass(7)
    ha_ref[0] = abuf[A_HDR + TT - (KA - 1):A_HDR + TT, :]

    vb = v.astype(BF16)
    low_head = lax.broadcasted_iota(jnp.int32, (CHUNK, LANES), 1) < (W // HEADS)
    zero_b = jnp.zeros((CHUNK, LANES), BF16)
    s_parts = []
    for c in range(TT // CHUNK):
        tiles = []
        for lt in range(W // LANES):
            vt = vb[c * CHUNK:(c + 1) * CHUNK, lt * LANES:(lt + 1) * LANES]
            stacked = jnp.concatenate([jnp.where(low_head, vt, zero_b),
                                       jnp.where(low_head, zero_b, vt)], axis=0)
            tiles.append(_dot(wsp_ref[lt], stacked))
        s_parts.append(jnp.concatenate(tiles, axis=1) + bsp_ref[...])
    bb = u * jnp.concatenate(s_parts, axis=0)
    a = _group_norm_silu(aconv[...], gavg_ref, ang_ref[...], anb_ref[...])

    _merge_and_route(x, a, bb, cc, lambda gi: gate_z[gi], (g1, sh2, sc2), n2g_ref[...], wbr_ref,
                     wout_ref, wr_ref, br_ref, x1_ref, h2_ref, route_ref)


def _const_spec(shape):
    nd = len(shape)
    return pl.BlockSpec(shape, lambda *_: (0,) * nd, pipeline_mode=pl.Buffered(1))


def _mixer_prompt(x, mod, layer, n, mod_row0, lw):
    t = x.shape[0] // n
    nt = t // TT
    mod_rows = mod.shape[2]
    row_spec = lambda cols: pl.BlockSpec((TT, cols), lambda i, j: (i * nt + j, 0))
    consts = [lw["n1g"], lw["n2g"], lw["w_in"], lw["a_dw_w"], lw["a_dw_b"], lw["a_norm_g"],
              lw["a_norm_b"], lw["b_norm_g"], lw["b_norm_b"], lw["w_sp"], lw["b_sp"], lw["c_conv_w"],
              lw["w_branch"], lw["w_out"], lw["gavg"], lw["w_router"], lw["b_router"]]
    return pl.pallas_call(
        functools.partial(_mixer_prompt_kernel, mod_row0=mod_row0),
        grid=(n, nt),
        in_specs=[row_spec(D),
                  pl.BlockSpec((1, 6, mod_rows, D), lambda i, j: (layer, 0, 0, 0),
                               pipeline_mode=pl.Buffered(1))]
        + [_const_spec(c.shape) for c in consts],
        out_specs=[row_spec(D),
                   pl.BlockSpec((TT * ROW_TILE, LANES), lambda i, j: (i * nt + j, 0)),
                   row_spec(ROUTE_COLS),
                   pl.BlockSpec((1, KA - 1, W), lambda i, j: (i, 0, 0)),
                   pl.BlockSpec((1, KC - 1, W), lambda i, j: (i, 0, 0))],
        out_shape=[jax.ShapeDtypeStruct((n * t, D), F32),
                   jax.ShapeDtypeStruct((n * t * ROW_TILE, LANES), U32),
                   jax.ShapeDtypeStruct((n * t, ROUTE_COLS), F32),
                   jax.ShapeDtypeStruct((n, KA - 1, W), F32),
                   jax.ShapeDtypeStruct((n, KC - 1, W), F32)],
        scratch_shapes=[pltpu.VMEM((A_HDR + TT + A_PAD, W), F32),
                        pltpu.VMEM((C_HDR + TT, W), F32),
                        pltpu.VMEM((TT, W), F32)],
        compiler_params=pltpu.CompilerParams(
            dimension_semantics=("arbitrary", "arbitrary"), vmem_limit_bytes=VMEM_LIMIT),
        name="mixer_prompt",
    )(x, mod, *consts)


def _mixer_sample_kernel(x_ref, mod_ref, hista_ref, histc_ref, n1g_ref, n2g_ref, win_ref, adw_ref,
                         adb_ref, ang_ref, anb_ref, bng_ref, bnb_ref, wsp4_ref, bsp4_ref,
                         ccw_ref, wbr_ref, wout_ref, gavg_ref, wr_ref, br_ref,
                         x1_ref, h2_ref, route_ref, ha_ref, hc_ref, v_ref):
    ts, nb = v_ref.shape[0], v_ref.shape[1]
    x = x_ref[...]

    def modrow(i):
        return jnp.concatenate([mod_ref[0, i]] * ts, axis=0)

    h = (_rms(x, n1g_ref[...]) * (1.0 + modrow(1)) + modrow(0)).astype(BF16)

    def zsec(lo, hi):
        return _dot(h, win_ref[:, lo:hi])

    def tslab(arr, j):
        return arr[j * nb:(j + 1) * nb, :]

    a_glu = zsec(0, W) * _sigmoid(zsec(W, 2 * W))
    kh = KA - 1
    a_conv = []
    for tq in range(ts):
        acc = jnp.zeros((nb, W), F32) + adb_ref[...]
        for r in range(tq, kh):
            acc = acc + adw_ref[r - tq:r - tq + 1, :] * hista_ref[r]
        for j in range(tq + 1):
            acc = acc + adw_ref[kh + j - tq:kh + j - tq + 1, :] * tslab(a_glu, j)
        a_conv.append(acc)
    a = _group_norm_silu(jnp.concatenate(a_conv, axis=0), gavg_ref, ang_ref[...], anb_ref[...])
    for r in range(kh - ts):
        ha_ref[r] = hista_ref[r + ts]
    for j in range(ts):
        ha_ref[kh - ts + j] = tslab(a_glu, j)

    u = _gelu_tanh(zsec(2 * W, 3 * W))
    v = _layer_norm(_gelu_tanh(zsec(3 * W, 4 * W)), bng_ref[...], bnb_ref[...])
    for j in range(ts):
        v_ref[j] = tslab(v, j)
    s_rows = []
    for tq in range(ts):
        s = jnp.zeros((nb, W), F32) + bsp4_ref[tq:tq + 1, :]
        for sq in range(tq + 1):
            s = s + wsp4_ref[tq * ts + sq:tq * ts + sq + 1, :] * tslab(v, sq)
        s_rows.append(s)
    bb = u * jnp.concatenate(s_rows, axis=0)

    cch = zsec(5 * W, 6 * W) * zsec(6 * W, 7 * W)
    xp = [histc_ref[r] for r in range(KC - 1)] + [tslab(cch, j) for j in range(ts)]
    ych = jnp.concatenate(
        [sum(ccw_ref[k:k + 1, :] * xp[tq + k] for k in range(KC)) for tq in range(ts)], axis=0)
    cc = zsec(4 * W, 5 * W) * ych
    for r in range(KC - 1):
        hc_ref[r] = xp[ts + r]

    def gates_fn(gi):
        return zsec(7 * W + gi * D, 7 * W + (gi + 1) * D)

    _merge_and_route(x, a, bb, cc, gates_fn, (modrow(2), modrow(3), modrow(4)), n2g_ref[...],
                     wbr_ref, wout_ref, wr_ref, br_ref, x1_ref, h2_ref, route_ref)


def _mixer_sample(x_rows, mod, layer, hist_a_tm, hist_c_tm, lw, nsplit):
    n = hist_a_tm.shape[1]
    ts = x_rows.shape[0] // n
    nb = n // nsplit
    rows = ts * nb
    consts = [lw["n1g"], lw["n2g"], lw["w_in"], lw["a_dw_w"], lw["a_dw_b"],
              lw["a_norm_g"], lw["a_norm_b"], lw["b_norm_g"], lw["b_norm_b"], lw["w_sp4"], lw["b_sp4"],
              lw["c_conv_w"], lw["w_branch"], lw["w_out"], lw["gavg"], lw["w_router"], lw["b_router"]]
    seq3 = lambda k, cols: pl.BlockSpec((k, nb, cols), lambda i: (0, i, 0))
    row_spec = lambda cols: pl.BlockSpec((rows, cols), lambda i: (i, 0))
    return pl.pallas_call(
        _mixer_sample_kernel,
        grid=(nsplit,),
        in_specs=[row_spec(D), pl.BlockSpec((1, 6, nb, D), lambda i: (layer, 0, i, 0)),
                  seq3(KA - 1, W), seq3(KC - 1, W)]
        + [_const_spec(c.shape) for c in consts],
        out_specs=[row_spec(D),
                   pl.BlockSpec((rows * ROW_TILE, LANES), lambda i: (i, 0)),
                   row_spec(ROUTE_COLS),
                   seq3(KA - 1, W), seq3(KC - 1, W), seq3(ts, W)],
        out_shape=[jax.ShapeDtypeStruct((ts * n, D), F32),
                   jax.ShapeDtypeStruct((ts * n * ROW_TILE, LANES), U32),
                   jax.ShapeDtypeStruct((ts * n, ROUTE_COLS), F32),
                   jax.ShapeDtypeStruct((KA - 1, n, W), F32),
                   jax.ShapeDtypeStruct((KC - 1, n, W), F32),
                   jax.ShapeDtypeStruct((ts, n, W), F32)],
        compiler_params=pltpu.CompilerParams(
            dimension_semantics=("arbitrary",), vmem_limit_bytes=VMEM_LIMIT),
        name="mixer_sample",
    )(x_rows, mod, hist_a_tm, hist_c_tm, *consts)


def _plan_kernel(rp_ref, rs_ref, tri_ref, destp_ref, dests_ref, tbl_ref, carry, pstart, *, ntp):
    ph = pl.program_id(0)
    i = pl.program_id(1)
    nt = pl.num_programs(1)
    lane = lax.broadcasted_iota(jnp.int32, (TP, ROUTE_COLS), 1).astype(F32)

    @pl.when(jnp.logical_and(ph == 0, i == 0))
    def _():
        carry[...] = jnp.zeros_like(carry)

    def sub_tile(r, dest_ref, row0):
        hot_a = lane == r[:, 0:1]
        hot_b = lane == r[:, 1:2]
        s = jnp.where(jnp.logical_or(hot_a, hot_b), 1.0, 0.0)

        @pl.when(ph == 1)
        def _():
            pre = _dot(tri_ref[...], s.astype(BF16)) + carry[...] + pstart[...]
            d_a = jnp.sum(jnp.where(hot_a, pre, 0.0), axis=-1, keepdims=True)
            d_b = jnp.sum(jnp.where(hot_b, pre, 0.0), axis=-1, keepdims=True)
            l2 = lax.broadcasted_iota(jnp.int32, (TP, 2), 1)
            dest_ref[row0:row0 + TP, :] = jnp.where(l2 == 0, d_a, d_b).astype(jnp.int32)

        carry[...] = carry[...] + jnp.sum(s, axis=0, keepdims=True)

    @pl.when(i < ntp)
    def _():
        for sub in range(rp_ref.shape[0] // TP):
            sub_tile(rp_ref[sub * TP:(sub + 1) * TP, :], destp_ref, sub * TP)

    @pl.when(i == ntp)
    def _():
        for sub in range(rs_ref.shape[0] // TP):
            sub_tile(rs_ref[sub * TP:(sub + 1) * TP, :], dests_ref, sub * TP)

    @pl.when(jnp.logical_and(ph == 0, i == nt - 1))
    def _():
        cnt = carry[...]
        nblk = jnp.floor((cnt + (BM - 1)) * (1.0 / BM))
        ri = lax.broadcasted_iota(jnp.int32, (ROUTE_COLS, ROUTE_COLS), 0)
        ci = lax.broadcasted_iota(jnp.int32, (ROUTE_COLS, ROUTE_COLS), 1)
        upper = jnp.where(ri <= ci, 1.0, 0.0)
        cum = _dot_hi(jnp.broadcast_to(nblk, (8, ROUTE_COLS)), upper)[0:1, :]
        cum_ex = cum - nblk
        pstart[...] = cum_ex * BM
        row = lax.broadcasted_iota(jnp.int32, (SUBLANES, ROUTE_COLS), 0)
        tbl = jnp.where(row == 0, cum_ex, jnp.where(row == 1, cnt, 0.0))
        tbl_ref[...] = tbl.astype(jnp.int32)
        carry[...] = jnp.zeros_like(carry)


def _plan(route_p, route_s):
    rows_p, rows_s = route_p.shape[0], route_s.shape[0]
    ntp = rows_p // TPP
    nbp = SUBLANES
    ri = lax.broadcasted_iota(jnp.int32, (TP, TP), 0)
    ci = lax.broadcasted_iota(jnp.int32, (TP, TP), 1)
    tri = jnp.where(ci < ri, 1.0, 0.0).astype(BF16)
    return pl.pallas_call(
        functools.partial(_plan_kernel, ntp=ntp),
        grid=(2, ntp + 1),
        in_specs=[pl.BlockSpec((TPP, ROUTE_COLS), lambda p, i: (jnp.minimum(i, ntp - 1), 0)),
                  pl.BlockSpec((rows_s, ROUTE_COLS), lambda p, i: (0, 0)),
                  pl.BlockSpec((TP, TP), lambda p, i: (0, 0))],
        out_specs=[pl.BlockSpec((TPP, 2), lambda p, i: (jnp.minimum(i, ntp - 1) * p, 0)),
                   pl.BlockSpec((rows_s, 2), lambda p, i: (0, 0)),
                   pl.BlockSpec((nbp, ROUTE_COLS), lambda p, i: (0, 0))],
        out_shape=[jax.ShapeDtypeStruct((rows_p, 2), jnp.int32),
                   jax.ShapeDtypeStruct((rows_s, 2), jnp.int32),
                   jax.ShapeDtypeStruct((nbp, ROUTE_COLS), jnp.int32)],
        scratch_shapes=[pltpu.VMEM((1, ROUTE_COLS), F32), pltpu.VMEM((1, ROUTE_COLS), F32)],
        compiler_params=pltpu.CompilerParams(
            dimension_semantics=("arbitrary", "arbitrary"), vmem_limit_bytes=VMEM_LIMIT),
        name="moe_plan",
    )(route_p, route_s, tri)


def _scatter_kernel(dest_ref, h_ref, *rest):
    rows_ref, sem = rest[-2:]
    tp = h_ref.shape[0] // ROW_TILE

    def row_copy(r, k):
        d = dest_ref[2 * r + k]
        return pltpu.make_async_copy(_row_tile(h_ref, r), _row_tile(rows_ref, d), sem)

    def issue(r, c):
        row_copy(r, 0).start(priority=0)
        row_copy(r, 1).start(priority=1)
        return c

    lax.fori_loop(0, tp, issue, 0, unroll=8)

    def drain(r, c):
        row_copy(r, 0).wait()
        row_copy(r, 1).wait()
        return c

    lax.fori_loop(0, tp, drain, 0, unroll=8)


def _scatter_rows(dest_flat, h2, rows_buf, n_rows, tp):
    nt = h2.shape[0] // (tp * ROW_TILE)
    in_specs = [pl.BlockSpec((2 * tp,), lambda i: (i,), memory_space=pltpu.SMEM),
                pl.BlockSpec((tp * ROW_TILE, LANES), lambda i: (i, 0))]
    args = [dest_flat, h2]
    aliases = {}
    if rows_buf is not None:
        in_specs.append(pl.BlockSpec(memory_space=pl.ANY))
        args.append(rows_buf)
        aliases = {2: 0}
    return pl.pallas_call(
        _scatter_kernel,
        grid=(nt,),
        in_specs=in_specs,
        out_specs=pl.BlockSpec(memory_space=pl.ANY),
        out_shape=jax.ShapeDtypeStruct((n_rows * ROW_TILE, LANES), U32),
        scratch_shapes=[pltpu.SemaphoreType.DMA(())],
        input_output_aliases=aliases,
        compiler_params=pltpu.CompilerParams(
            dimension_semantics=("arbitrary",), vmem_limit_bytes=VMEM_LIMIT,
            has_side_effects=True),
        name="moe_scatter",
    )(*args)


ROW_DMA_PRIORITY = 1


def _expert_kernel(blk0_ref, cnt_ref, x_hbm, w1_ref, w3_ref, w2_ref, y_hbm,
                   xbuf, ybuf, w1b, w3b, w2b, sem_in, sem_out):
    e = pl.program_id(0)
    cnt = cnt_ref[e]
    nblk = lax.shift_right_logical(cnt + (BM - 1), BM.bit_length() - 1)
    blk0 = blk0_ref[e]
    blk_rows = BM * ROW_TILE

    def block_rows(ref, j):
        return ref.at[pl.ds(pl.multiple_of((blk0 + j) * blk_rows, blk_rows), blk_rows)]

    def in_copy(j, slot):
        return pltpu.make_async_copy(block_rows(x_hbm, j), xbuf.at[slot], sem_in.at[slot])

    def out_copy(j, slot):
        return pltpu.make_async_copy(ybuf.at[slot], block_rows(y_hbm, j), sem_out.at[slot])

    @pl.when(nblk > 0)
    def _():
        in_copy(0, 0).start(priority=ROW_DMA_PRIORITY)
        w1b[...] = w1_ref[0, 0].astype(BF16)
        w3b[...] = w3_ref[0, 0].astype(BF16)
        w2b[...] = w2_ref[0, 0].astype(BF16)

        def body(j, carry):
            slot = lax.rem(j, 2)

            @pl.when(j + 1 < nblk)
            def _():
                in_copy(j + 1, 1 - slot).start(priority=ROW_DMA_PRIORITY)

            in_copy(j, slot).wait()

            @pl.when(j >= 2)
            def _():
                out_copy(j - 2, slot).wait()

            row = lax.broadcasted_iota(jnp.int32, (BM, D), 0)
            x = jnp.where(row < cnt - j * BM, _load_row_tiles(xbuf.at[slot]), 0.0).astype(BF16)
            h1 = _dot(x, w1b[...])
            h3 = _dot(x, w3b[...])
            act = (_silu(h1) * h3).astype(BF16)
            _store_row_tiles(ybuf.at[slot], _dot(act, w2b[...]))
            out_copy(j, slot).start(priority=ROW_DMA_PRIORITY)
            return carry

        lax.fori_loop(0, nblk, body, 0)

        @pl.when(nblk >= 2)
        def _():
            out_copy(nblk - 2, lax.rem(nblk, 2)).wait()

        out_copy(nblk - 1, lax.rem(nblk - 1, 2)).wait()


def _experts(tbl, rows_buf, w1, w3, w2, layer):
    blk0, cnt = tbl[0, :NEXP], tbl[1, :NEXP]
    wspec = lambda shape: pl.BlockSpec((1, 1) + shape, lambda e, blk0, cnt: (layer, e, 0, 0))
    return pl.pallas_call(
        _expert_kernel,
        grid_spec=pltpu.PrefetchScalarGridSpec(
            num_scalar_prefetch=2,
            grid=(NEXP,),
            in_specs=[pl.BlockSpec(memory_space=pl.ANY),
                      wspec((D, FF)), wspec((D, FF)), wspec((FF, D))],
            out_specs=pl.BlockSpec(memory_space=pl.ANY),
            scratch_shapes=[pltpu.VMEM((2, BM * ROW_TILE, LANES), U32),
                            pltpu.VMEM((2, BM * ROW_TILE, LANES), U32),
                            pltpu.VMEM((D, FF), BF16), pltpu.VMEM((D, FF), BF16),
                            pltpu.VMEM((FF, D), BF16),
                            pltpu.SemaphoreType.DMA((2,)), pltpu.SemaphoreType.DMA((2,))],
        ),
        out_shape=jax.ShapeDtypeStruct(rows_buf.shape, U32),
        compiler_params=pltpu.CompilerParams(
            dimension_semantics=("arbitrary",), vmem_limit_bytes=VMEM_LIMIT,
            has_side_effects=True),
        name="moe_experts",
    )(blk0, cnt, rows_buf, w1, w3, w2)


def _combine_kernel(dest_ref, dest_next_ref, x1_ref, route_ref, g2_ref, fng_ref, y_ref, o_ref,
                    ybuf, sem, *, final_norm, g2_row0, tiles_per_g2):
    tp = x1_ref.shape[0]
    i = pl.program_id(0)
    n = pl.num_programs(0)
    slot = lax.rem(i, 2)

    def gather(d_ref, s):
        def row_copy(r, k):
            return pltpu.make_async_copy(_row_tile(y_ref, d_ref[2 * r + k]),
                                         _row_tile(ybuf.at[s, k], r), sem.at[s])

        def issue(r, c):
            row_copy(r, 0).start(priority=0)
            row_copy(r, 1).start(priority=1)
            return c

        lax.fori_loop(0, tp, issue, 0, unroll=8)

    @pl.when(i == 0)
    def _():
        gather(dest_ref, 0)

    @pl.when(i + 1 < n)
    def _():
        gather(dest_next_ref, 1 - slot)

    for k in range(2):
        pltpu.make_async_copy(y_ref.at[pl.ds(0, tp * ROW_TILE)], ybuf.at[slot, k], sem.at[slot]).wait()

    route = route_ref[...]
    w_a = route[:, 2:3]
    w_b = route[:, 3:4]
    if tiles_per_g2:
        g2 = g2_ref[0, 0, pl.ds(g2_row0 + i // tiles_per_g2, 1), :]
    else:
        g2 = jnp.concatenate([g2_ref[0, 0]] * (tp // g2_ref.shape[2]), axis=0)
    y = (w_a * _load_row_tiles(ybuf.at[slot, 0]) + w_b * _load_row_tiles(ybuf.at[slot, 1]))
    x2 = x1_ref[...] + g2 * y
    if final_norm:
        x2 = _rms(x2, fng_ref[...])
    o_ref[...] = x2


def _combine(dest_flat, x1, route, mod, layer, fng, y_rows, tp, *, final_norm, g2_row0=0,
             tiles_per_g2=0, seqs_per_tile=0):
    nt = x1.shape[0] // tp
    if tiles_per_g2:
        g2_spec = pl.BlockSpec((1, 1, mod.shape[2], D), lambda i: (layer, 5, 0, 0))
    else:
        g2_spec = pl.BlockSpec((1, 1, seqs_per_tile, D), lambda i: (layer, 5, i, 0))
    return pl.pallas_call(
        functools.partial(_combine_kernel, final_norm=final_norm, g2_row0=g2_row0,
                          tiles_per_g2=tiles_per_g2),
        grid=(nt,),
        in_specs=[pl.BlockSpec((2 * tp,), lambda i: (i,), memory_space=pltpu.SMEM),
                  pl.BlockSpec((2 * tp,), lambda i: (jnp.minimum(i + 1, nt - 1),),
                               memory_space=pltpu.SMEM),
                  pl.BlockSpec((tp, D), lambda i: (i, 0)),
                  pl.BlockSpec((tp, ROUTE_COLS), lambda i: (i, 0)),
                  g2_spec,
                  pl.BlockSpec((1, D), lambda i: (0, 0)),
                  pl.BlockSpec(memory_space=pl.ANY)],
        out_specs=pl.BlockSpec((tp, D), lambda i: (i, 0)),
        out_shape=jax.ShapeDtypeStruct(x1.shape, F32),
        scratch_shapes=[pltpu.VMEM((2, 2, tp * ROW_TILE, LANES), U32),
                        pltpu.SemaphoreType.DMA((2,))],
        compiler_params=pltpu.CompilerParams(
            dimension_semantics=("arbitrary",), vmem_limit_bytes=VMEM_LIMIT),
        name="moe_combine",
    )(dest_flat, dest_flat, x1, route, mod, fng, y_rows)


def _layer_weights(l, p):
    hd = W // HEADS
    gi = jnp.arange(W) // (W // NG_A)
    gavg = jnp.where(gi[:, None] == gi[None, :], 1.0 / (W // NG_A), 0.0).astype(BF16)
    tril = jnp.tril(jnp.ones((CHUNK, CHUNK), dtype=bool))
    w_sp = jnp.where(tril[None], p["b_spatial_w"][l], 0.0)
    w_router = jnp.concatenate(
        [p["w_router_g"][l], p["w_router_e"][l],
         jnp.zeros((D, ROUTE_COLS - NGRP - NEXP), F32)], axis=1)
    b_router = jnp.concatenate(
        [p["b_router_g"][l], p["b_router_e"][l],
         jnp.zeros((ROUTE_COLS - NGRP - NEXP,), F32)])[None, :]
    return {
        "n1g": p["norm1_g"][l][None, :], "n2g": p["norm2_g"][l][None, :],
        "w_in": p["w_in"][l].astype(BF16),
        "a_dw_w": p["a_dw_w"][l], "a_dw_b": p["a_dw_b"][l][None, :],
        "a_norm_g": p["a_norm_g"][l][None, :], "a_norm_b": p["a_norm_b"][l][None, :],
        "b_norm_g": p["b_norm_g"][l][None, :], "b_norm_b": p["b_norm_b"][l][None, :],
        "w_sp": jnp.concatenate([w_sp[0::2], w_sp[1::2]], axis=2).astype(BF16),
        "b_sp": jnp.repeat(p["b_spatial_b"][l].T, hd, axis=1),
        "c_conv_w": p["c_conv_w"][l],
        "w_branch": p["w_branch"][l].astype(BF16), "w_out": p["w_out"][l].astype(BF16),
        "gavg": gavg, "w_router": w_router.astype(BF16), "b_router": b_router,
    }


def _sample_spatial(l, p, ts):
    hd = W // HEADS
    w = p["b_spatial_w"][l][:, :ts, :ts]
    w = jnp.where(jnp.tril(jnp.ones((ts, ts), dtype=bool))[None], w, 0.0)
    w4 = jnp.repeat(jnp.transpose(w, (1, 2, 0)).reshape(ts * ts, HEADS), hd, axis=1)
    b4 = jnp.repeat(p["b_spatial_b"][l][:, :ts].T, hd, axis=1)
    return w4, b4


def kernel(x_prompt, x_sample, c_prompt, c_sample, state_conv_a, state_conv_c, norm1_g, norm2_g, w_ada, b_ada, w_in, a_dw_w, a_dw_b, a_norm_g, a_norm_b, b_norm_g, b_norm_b, b_spatial_w, b_spatial_b, c_conv_w, w_branch, w_out, w_router_g, b_router_g, w_router_e, b_router_e, w1, w3, w2, final_norm_g):
    p = dict(norm1_g=norm1_g, norm2_g=norm2_g, w_in=w_in, a_dw_w=a_dw_w, a_dw_b=a_dw_b,
             a_norm_g=a_norm_g, a_norm_b=a_norm_b, b_norm_g=b_norm_g, b_norm_b=b_norm_b,
             b_spatial_w=b_spatial_w, b_spatial_b=b_spatial_b, c_conv_w=c_conv_w,
             w_branch=w_branch, w_out=w_out, w_router_g=w_router_g, b_router_g=b_router_g,
             w_router_e=w_router_e, b_router_e=b_router_e)
    depth = w_in.shape[0]
    nb_, seq, _ = x_prompt.shape
    ns, ts, _ = x_sample.shape
    n_tok = nb_ * seq + ns * ts
    nblocks = -(-(2 * n_tok) // BM) + NEXP
    nsplit = 2
    nbs = ns // nsplit
    tps = ts * nbs
    assert seq % TT == 0 and TT % CHUNK == 0 and seq % TP == 0 and (nb_ * seq) % TPP == 0
    assert (ts * ns) % TP == 0 and ns % SUBLANES == 0

    mod = _adaln(jnp.concatenate([c_sample, c_prompt], axis=0), w_ada, b_ada)

    xp = x_prompt.reshape(nb_ * seq, D)
    xs = jnp.transpose(x_sample.reshape(nsplit, nbs, ts, D), (0, 2, 1, 3)).reshape(ts * ns, D)
    fng = final_norm_g[None, :]
    pa, pc, sa, sc, sv = [], [], [], [], []
    for l in range(depth):
        lw = _layer_weights(l, p)
        lw["w_sp4"], lw["b_sp4"] = _sample_spatial(l, p, ts)

        x1p, h2p, route_p, ha, hc = _mixer_prompt(xp, mod, l, nb_, ns, lw)
        pa.append(ha)
        pc.append(hc)
        x1s, h2s, route_s, ha_s, hc_s, v_s = _mixer_sample(
            xs, mod, l, jnp.transpose(state_conv_a[l], (1, 0, 2)),
            jnp.transpose(state_conv_c[l], (1, 0, 2)), lw, nsplit)
        sa.append(jnp.transpose(ha_s, (1, 0, 2)))
        sc.append(jnp.transpose(hc_s, (1, 0, 2)))
        sv.append(jnp.transpose(v_s, (1, 0, 2)))

        dest_p, dest_s, tbl = _plan(route_p, route_s)
        dest_p, dest_s = dest_p.reshape(-1), dest_s.reshape(-1)
        rows_buf = _scatter_rows(dest_p, h2p, None, nblocks * BM, TP)
        rows_buf = _scatter_rows(dest_s, h2s, rows_buf, nblocks * BM, tps)
        y_rows = _experts(tbl, rows_buf, w1, w3, w2, l)

        last = l == depth - 1
        xp = _combine(dest_p, x1p, route_p, mod, l, fng, y_rows, TP, final_norm=last,
                      g2_row0=ns, tiles_per_g2=seq // TP)
        xs = _combine(dest_s, x1s, route_s, mod, l, fng, y_rows, tps, final_norm=last,
                      seqs_per_tile=nbs)

    y_prompt = xp.reshape(nb_, seq, D)
    y_sample = jnp.transpose(xs.reshape(nsplit, ts, nbs, D), (0, 2, 1, 3)).reshape(ns, ts, D)
    return (y_prompt, y_sample, jnp.stack(pa), jnp.stack(pc), jnp.stack(sa), jnp.stack(sc),
            jnp.stack(sv))
```

```python
import functools

import jax
import jax.numpy as jnp
from jax import lax
from jax.experimental import pallas as pl
from jax.experimental.pallas import tpu as pltpu

F32 = jnp.float32
BF16 = jnp.bfloat16

D = 1024
W = D // 2
KA = 31
KC = 3
NG_A = 8
CHUNK = 128
HEADS = 8
NGRP = 4
EPG = 8
NEXP = NGRP * EPG
FF = D // 2
EPS = 1e-6
IN_COLS = 7 * W + 3 * D
LANES = 128
ROUTE_COLS = LANES

TT = 512
SUBLANES = 8
A_HDR = 32
A_PAD = SUBLANES
A_ROWS = 64
C_HDR = 8
TP = 512
TPP = 2048
BM = 256
VMEM_LIMIT = 56 * 1024 * 1024

NEG = -1e30


def _sigmoid(x):
    return 1.0 / (1.0 + jnp.exp(-x))


def _silu(x):
    return x * _sigmoid(x)


def _gelu_tanh(x):
    return 0.5 * x * (1.0 + jnp.tanh(0.7978845608028654 * (x + 0.044715 * x * x * x)))


def _rms(x, g):
    return x * lax.rsqrt(jnp.mean(x * x, axis=-1, keepdims=True) + EPS) * g


def _dot(a, b):
    return jnp.dot(a, b, preferred_element_type=F32)


def _dot_hi(a, b):
    return jnp.dot(a, b, preferred_element_type=F32, precision=lax.Precision.HIGHEST)


U32 = jnp.uint32
ROW_TILE = D // (2 * LANES)


def _pack_bf16_pair(lo, hi):
    def rne(x):
        b = pltpu.bitcast(x, U32)
        return b + (U32(0x7FFF) + ((b >> 16) & U32(1)))
    return (rne(lo) >> 16) | (rne(hi) & U32(0xFFFF0000))


def _store_row_tiles(ref, val):
    r = val.shape[0]
    for j in range(ROW_TILE):
        lo = val[:, 2 * j * LANES:(2 * j + 1) * LANES]
        hi = val[:, (2 * j + 1) * LANES:(2 * j + 2) * LANES]
        ref[pl.ds(j, r, stride=ROW_TILE), :] = _pack_bf16_pair(lo, hi)


def _load_row_tiles(ref):
    r = ref.shape[0] // ROW_TILE
    parts = []
    for j in range(ROW_TILE):
        w = ref[pl.ds(j, r, stride=ROW_TILE), :]
        parts.append(pltpu.bitcast(w << 16, F32))
        parts.append(pltpu.bitcast(w & U32(0xFFFF0000), F32))
    return jnp.concatenate(parts, axis=1)


def _row_tile(ref, r):
    return ref.at[pl.ds(pl.multiple_of(r * ROW_TILE, ROW_TILE), ROW_TILE)]


def _mod_kernel(c_ref, w_ref, b_ref, o_ref):
    c = c_ref[...]
    s = _silu(c).astype(BF16)
    o_ref[0, 0] = _dot(s, w_ref[0].astype(BF16)) + b_ref[0]


def _adaln(c_all, w_ada, b_ada):
    nl = w_ada.shape[0]
    r = c_all.shape[0]
    return pl.pallas_call(
        _mod_kernel,
        grid=(nl, 6),
        in_specs=[
            pl.BlockSpec((r, D), lambda l, j: (0, 0)),
            pl.BlockSpec((1, D, D), lambda l, j: (l, 0, j)),
            pl.BlockSpec((1, 1, D), lambda l, j: (l, 0, j)),
        ],
        out_specs=pl.BlockSpec((1, 1, r, D), lambda l, j: (l, j, 0, 0)),
        out_shape=jax.ShapeDtypeStruct((nl, 6, r, D), F32),
        compiler_params=pltpu.CompilerParams(
            dimension_semantics=("arbitrary", "arbitrary"), vmem_limit_bytes=VMEM_LIMIT),
        name="adaln_mod",
    )(c_all, w_ada, b_ada.reshape(nl, 1, 6 * D))


def _group_norm_silu(a, gavg_ref, g, b):
    mu = _dot(a.astype(BF16), gavg_ref[...])
    xc = a - mu
    var = _dot((xc * xc).astype(BF16), gavg_ref[...])
    y = xc * lax.rsqrt(var + EPS) * g + b
    return _silu(y)


def _layer_norm(x, g, b):
    mu = jnp.mean(x, axis=-1, keepdims=True)
    xc = x - mu
    var = jnp.mean(xc * xc, axis=-1, keepdims=True)
    return xc * lax.rsqrt(var + EPS) * g + b


def _merge_and_route(x, a, bb, cc, gates_fn, mod, n2g, wbr_ref, wout_ref, wr_ref, br_ref,
                     x1_ref, h2_ref, route_ref):
    g1, sh2, sc2 = mod
    acc = None
    for gi, br_in in enumerate((a, bb, cc)):
        br = _dot(br_in.astype(BF16), wbr_ref[gi])
        term = _sigmoid(gates_fn(gi)) * br
        acc = term if acc is None else acc + term
    mixed = _dot(acc.astype(BF16), wout_ref[...])
    x1 = x + g1 * mixed
    x1_ref[...] = x1
    h2 = _rms(x1, n2g) * (1.0 + sc2) + sh2
    _store_row_tiles(h2_ref, h2)

    logits = _dot(h2.astype(BF16), wr_ref[...]) + br_ref[...]
    rows = logits.shape[0]
    lane = lax.broadcasted_iota(jnp.int32, (rows, ROUTE_COLS), 1).astype(F32)
    is_g = lane < NGRP
    gl = jnp.where(is_g, logits, NEG)
    gmax = jnp.max(gl, axis=-1, keepdims=True)
    gsel = jnp.min(jnp.where(gl == gmax, lane, float(ROUTE_COLS)), axis=-1, keepdims=True)
    gden = jnp.sum(jnp.where(is_g, jnp.exp(gl - gmax), 0.0), axis=-1, keepdims=True)
    gprob = 1.0 / gden
    lo = NGRP + EPG * gsel
    in_grp = jnp.logical_and(lane >= lo, lane < lo + EPG)
    el = jnp.where(in_grp, logits, NEG)
    v1 = jnp.max(el, axis=-1, keepdims=True)
    i1 = jnp.min(jnp.where(el == v1, lane, float(ROUTE_COLS)), axis=-1, keepdims=True)
    el2 = jnp.where(lane == i1, NEG, el)
    v2 = jnp.max(el2, axis=-1, keepdims=True)
    i2 = jnp.min(jnp.where(el2 == v2, lane, float(ROUTE_COLS)), axis=-1, keepdims=True)
    p2 = jnp.exp(v2 - v1)
    w_a = gprob / (1.0 + p2)
    w_b = gprob * p2 / (1.0 + p2)
    route = jnp.where(lane == 0.0, i1 - NGRP,
                      jnp.where(lane == 1.0, i2 - NGRP,
                                jnp.where(lane == 2.0, w_a,
                                          jnp.where(lane == 3.0, w_b, 0.0))))
    route_ref[...] = route


def _mixer_prompt_kernel(x_ref, mod_ref, n1g_ref, n2g_ref, win_ref, adw_ref, adb_ref, ang_ref,
                         anb_ref, bng_ref, bnb_ref, wsp_ref, bsp_ref, ccw_ref, wbr_ref, wout_ref,
                         gavg_ref, wr_ref, br_ref,
                         x1_ref, h2_ref, route_ref, ha_ref, hc_ref,
                         abuf, cbuf, aconv, *, mod_row0):
    first = pl.program_id(1) == 0
    abuf[0:A_HDR, :] = jnp.where(first, 0.0, abuf[TT:TT + A_HDR, :])
    abuf[A_HDR + TT:A_HDR + TT + A_PAD, :] = jnp.zeros((A_PAD, W), F32)
    cbuf[0:C_HDR, :] = jnp.where(first, 0.0, cbuf[TT:TT + C_HDR, :])

    x = x_ref[...]
    mrow = mod_row0 + pl.program_id(0)
    sh1, sc1, g1, sh2, sc2 = (mod_ref[0, s, pl.ds(mrow, 1), :] for s in range(5))
    h = (_rms(x, n1g_ref[...]) * (1.0 + sc1) + sh1).astype(BF16)

    def zsec(lo, hi):
        return _dot(h, win_ref[:, lo:hi])

    a_glu = zsec(0, W) * _sigmoid(zsec(W, 2 * W))
    abuf[A_HDR:A_HDR + TT, :] = a_glu
    off = A_HDR - (KA - 1)
    rc = A_ROWS

    def conv_pass(p):
        lt, half = p // 2, p % 2
        ls = slice(lt * LANES, (lt + 1) * LANES)
        for r0 in range(half * (TT // 2), (half + 1) * (TT // 2), rc):
            y = None
            for r in range(SUBLANES):
                z = None
                for q in range((KA - 1 + off) // SUBLANES + 1):
                    k = SUBLANES * q + r - off
                    if 0 <= k < KA:
                        rows = slice(r0 + SUBLANES * q, r0 + SUBLANES * q + rc + SUBLANES)
                        term = adw_ref[k:k + 1, ls] * abuf[rows, ls]
                        z = term if z is None else z + term
                part = z[r:r + rc, :]
                y = part if y is None else y + part
            aconv[r0:r0 + rc, ls] = y + adb_ref[:, ls]

    conv_pass(0)
    u = _gelu_tanh(zsec(2 * W, 3 * W))
    conv_pass(1)
    v = _layer_norm(_gelu_tanh(zsec(3 * W, 4 * W)), bng_ref[...], bnb_ref[...])
    conv_pass(2)

    cch = zsec(5 * W, 6 * W) * zsec(6 * W, 7 * W)
    cbuf[C_HDR:C_HDR + TT, :] = cch
    coff = C_HDR - (KC - 1)
    ych = (ccw_ref[0:1, :] * cbuf[coff:coff + TT, :]
           + ccw_ref[1:2, :] * cbuf[coff + 1:coff + 1 + TT, :]
           + ccw_ref[2:3, :] * cbuf[coff + 2:coff + 2 + TT, :])
    conv_pass(3)
    cc = zsec(4 * W, 5 * W) * ych
    hc_ref[0] = cbuf[C_HDR + TT - (KC - 1):C_HDR + TT, :]
    conv_pass(4)
    gate_z = [zsec(7 * W, 7 * W + D)]
    conv_pass(5)
    gate_z.append(zsec(7 * W + D, 7 * W + 2 * D))
    conv_pass(6)
    gate_z.append(zsec(7 * W + 2 * D, 7 * W + 3 * D))
    conv_pass(7)
    ha_ref[0] = abuf[A_HDR + TT - (KA - 1):A_HDR + TT, :]

    vb = v.astype(BF16)
    low_head = lax.broadcasted_iota(jnp.int32, (CHUNK, LANES), 1) < (W // HEADS)
    zero_b = jnp.zeros((CHUNK, LANES), BF16)
    s_parts = []
    for c in range(TT // CHUNK):
        tiles = []
        for lt in range(W // LANES):
            vt = vb[c * CHUNK:(c + 1) * CHUNK, lt * LANES:(lt + 1) * LANES]
            stacked = jnp.concatenate([jnp.where(low_head, vt, zero_b),
                                       jnp.where(low_head, zero_b, vt)], axis=0)
            tiles.append(_dot(wsp_ref[lt], stacked))
        s_parts.append(jnp.concatenate(tiles, axis=1) + bsp_ref[...])
    bb = u * jnp.concatenate(s_parts, axis=0)
    a = _group_norm_silu(aconv[...], gavg_ref, ang_ref[...], anb_ref[...])

    _merge_and_route(x, a, bb, cc, lambda gi: gate_z[gi], (g1, sh2, sc2), n2g_ref[...], wbr_ref,
                     wout_ref, wr_ref, br_ref, x1_ref, h2_ref, route_ref)


def _const_spec(shape):
    nd = len(shape)
    return pl.BlockSpec(shape, lambda *_: (0,) * nd, pipeline_mode=pl.Buffered(1))


def _mixer_prompt(x, mod, layer, n, mod_row0, lw):
    t = x.shape[0] // n
    nt = t // TT
    mod_rows = mod.shape[2]
    row_spec = lambda cols: pl.BlockSpec((TT, cols), lambda i, j: (i * nt + j, 0))
    consts = [lw["n1g"], lw["n2g"], lw["w_in"], lw["a_dw_w"], lw["a_dw_b"], lw["a_norm_g"],
              lw["a_norm_b"], lw["b_norm_g"], lw["b_norm_b"], lw["w_sp"], lw["b_sp"], lw["c_conv_w"],
              lw["w_branch"], lw["w_out"], lw["gavg"], lw["w_router"], lw["b_router"]]
    return pl.pallas_call(
        functools.partial(_mixer_prompt_kernel, mod_row0=mod_row0),
        grid=(n, nt),
        in_specs=[row_spec(D),
                  pl.BlockSpec((1, 6, mod_rows, D), lambda i, j: (layer, 0, 0, 0),
                               pipeline_mode=pl.Buffered(1))]
        + [_const_spec(c.shape) for c in consts],
        out_specs=[row_spec(D),
                   pl.BlockSpec((TT * ROW_TILE, LANES), lambda i, j: (i * nt + j, 0)),
                   row_spec(ROUTE_COLS),
                   pl.BlockSpec((1, KA - 1, W), lambda i, j: (i, 0, 0)),
                   pl.BlockSpec((1, KC - 1, W), lambda i, j: (i, 0, 0))],
        out_shape=[jax.ShapeDtypeStruct((n * t, D), F32),
                   jax.ShapeDtypeStruct((n * t * ROW_TILE, LANES), U32),
                   jax.ShapeDtypeStruct((n * t, ROUTE_COLS), F32),
                   jax.ShapeDtypeStruct((n, KA - 1, W), F32),
                   jax.ShapeDtypeStruct((n, KC - 1, W), F32)],
        scratch_shapes=[pltpu.VMEM((A_HDR + TT + A_PAD, W), F32),
                        pltpu.VMEM((C_HDR + TT, W), F32),
                        pltpu.VMEM((TT, W), F32)],
        compiler_params=pltpu.CompilerParams(
            dimension_semantics=("arbitrary", "arbitrary"), vmem_limit_bytes=VMEM_LIMIT),
        name="mixer_prompt",
    )(x, mod, *consts)


def _mixer_sample_kernel(x_ref, mod_ref, hista_ref, histc_ref, n1g_ref, n2g_ref, win_ref, adw_ref,
                         adb_ref, ang_ref, anb_ref, bng_ref, bnb_ref, wsp4_ref, bsp4_ref,
                         ccw_ref, wbr_ref, wout_ref, gavg_ref, wr_ref, br_ref,
                         x1_ref, h2_ref, route_ref, ha_ref, hc_ref, v_ref):
    ts, nb = v_ref.shape[0], v_ref.shape[1]
    x = x_ref[...]

    def modrow(i):
        return jnp.concatenate([mod_ref[0, i]] * ts, axis=0)

    h = (_rms(x, n1g_ref[...]) * (1.0 + modrow(1)) + modrow(0)).astype(BF16)

    def zsec(lo, hi):
        return _dot(h, win_ref[:, lo:hi])

    def tslab(arr, j):
        return arr[j * nb:(j + 1) * nb, :]

    a_glu = zsec(0, W) * _sigmoid(zsec(W, 2 * W))
    kh = KA - 1
    a_conv = []
    for tq in range(ts):
        acc = jnp.zeros((nb, W), F32) + adb_ref[...]
        for r in range(tq, kh):
            acc = acc + adw_ref[r - tq:r - tq + 1, :] * hista_ref[r]
        for j in range(tq + 1):
            acc = acc + adw_ref[kh + j - tq:kh + j - tq + 1, :] * tslab(a_glu, j)
        a_conv.append(acc)
    a = _group_norm_silu(jnp.concatenate(a_conv, axis=0), gavg_ref, ang_ref[...], anb_ref[...])
    for r in range(kh - ts):
        ha_ref[r] = hista_ref[r + ts]
    for j in range(ts):
        ha_ref[kh - ts + j] = tslab(a_glu, j)

    u = _gelu_tanh(zsec(2 * W, 3 * W))
    v = _layer_norm(_gelu_tanh(zsec(3 * W, 4 * W)), bng_ref[...], bnb_ref[...])
    for j in range(ts):
        v_ref[j] = tslab(v, j)
    s_rows = []
    for tq in range(ts):
        s = jnp.zeros((nb, W), F32) + bsp4_ref[tq:tq + 1, :]
        for sq in range(tq + 1):
            s = s + wsp4_ref[tq * ts + sq:tq * ts + sq + 1, :] * tslab(v, sq)
        s_rows.append(s)
    bb = u * jnp.concatenate(s_rows, axis=0)

    cch = zsec(5 * W, 6 * W) * zsec(6 * W, 7 * W)
    xp = [histc_ref[r] for r in range(KC - 1)] + [tslab(cch, j) for j in range(ts)]
    ych = jnp.concatenate(
        [sum(ccw_ref[k:k + 1, :] * xp[tq + k] for k in range(KC)) for tq in range(ts)], axis=0)
    cc = zsec(4 * W, 5 * W) * ych
    for r in range(KC - 1):
        hc_ref[r] = xp[ts + r]

    def gates_fn(gi):
        return zsec(7 * W + gi * D, 7 * W + (gi + 1) * D)

    _merge_and_route(x, a, bb, cc, gates_fn, (modrow(2), modrow(3), modrow(4)), n2g_ref[...],
                     wbr_ref, wout_ref, wr_ref, br_ref, x1_ref, h2_ref, route_ref)


def _mixer_sample(x_rows, mod, layer, hist_a_tm, hist_c_tm, lw, nsplit):
    n = hist_a_tm.shape[1]
    ts = x_rows.shape[0] // n
    nb = n // nsplit
    rows = ts * nb
    consts = [lw["n1g"], lw["n2g"], lw["w_in"], lw["a_dw_w"], lw["a_dw_b"],
              lw["a_norm_g"], lw["a_norm_b"], lw["b_norm_g"], lw["b_norm_b"], lw["w_sp4"], lw["b_sp4"],
              lw["c_conv_w"], lw["w_branch"], lw["w_out"], lw["gavg"], lw["w_router"], lw["b_router"]]
    seq3 = lambda k, cols: pl.BlockSpec((k, nb, cols), lambda i: (0, i, 0))
    row_spec = lambda cols: pl.BlockSpec((rows, cols), lambda i: (i, 0))
    return pl.pallas_call(
        _mixer_sample_kernel,
        grid=(nsplit,),
        in_specs=[row_spec(D), pl.BlockSpec((1, 6, nb, D), lambda i: (layer, 0, i, 0)),
                  seq3(KA - 1, W), seq3(KC - 1, W)]
        + [_const_spec(c.shape) for c in consts],
        out_specs=[row_spec(D),
                   pl.BlockSpec((rows * ROW_TILE, LANES), lambda i: (i, 0)),
                   row_spec(ROUTE_COLS),
                   seq3(KA - 1, W), seq3(KC - 1, W), seq3(ts, W)],
        out_shape=[jax.ShapeDtypeStruct((ts * n, D), F32),
                   jax.ShapeDtypeStruct((ts * n * ROW_TILE, LANES), U32),
                   jax.ShapeDtypeStruct((ts * n, ROUTE_COLS), F32),
                   jax.ShapeDtypeStruct((KA - 1, n, W), F32),
                   jax.ShapeDtypeStruct((KC - 1, n, W), F32),
                   jax.ShapeDtypeStruct((ts, n, W), F32)],
        compiler_params=pltpu.CompilerParams(
            dimension_semantics=("arbitrary",), vmem_limit_bytes=VMEM_LIMIT),
        name="mixer_sample",
    )(x_rows, mod, hist_a_tm, hist_c_tm, *consts)


def _plan_kernel(rp_ref, rs_ref, tri_ref, destp_ref, dests_ref, tbl_ref, carry, pstart, *, ntp):
    ph = pl.program_id(0)
    i = pl.program_id(1)
    nt = pl.num_programs(1)
    lane = lax.broadcasted_iota(jnp.int32, (TP, ROUTE_COLS), 1).astype(F32)

    @pl.when(jnp.logical_and(ph == 0, i == 0))
    def _():
        carry[...] = jnp.zeros_like(carry)

    def sub_tile(r, dest_ref, row0):
        hot_a = lane == r[:, 0:1]
        hot_b = lane == r[:, 1:2]
        s = jnp.where(jnp.logical_or(hot_a, hot_b), 1.0, 0.0)

        @pl.when(ph == 1)
        def _():
            pre = _dot(tri_ref[...], s.astype(BF16)) + carry[...] + pstart[...]
            d_a = jnp.sum(jnp.where(hot_a, pre, 0.0), axis=-1, keepdims=True)
            d_b = jnp.sum(jnp.where(hot_b, pre, 0.0), axis=-1, keepdims=True)
            l2 = lax.broadcasted_iota(jnp.int32, (TP, 2), 1)
            dest_ref[row0:row0 + TP, :] = jnp.where(l2 == 0, d_a, d_b).astype(jnp.int32)

        carry[...] = carry[...] + jnp.sum(s, axis=0, keepdims=True)

    @pl.when(i < ntp)
    def _():
        for sub in range(rp_ref.shape[0] // TP):
            sub_tile(rp_ref[sub * TP:(sub + 1) * TP, :], destp_ref, sub * TP)

    @pl.when(i == ntp)
    def _():
        for sub in range(rs_ref.shape[0] // TP):
            sub_tile(rs_ref[sub * TP:(sub + 1) * TP, :], dests_ref, sub * TP)

    @pl.when(jnp.logical_and(ph == 0, i == nt - 1))
    def _():
        cnt = carry[...]
        nblk = jnp.floor((cnt + (BM - 1)) * (1.0 / BM))
        ri = lax.broadcasted_iota(jnp.int32, (ROUTE_COLS, ROUTE_COLS), 0)
        ci = lax.broadcasted_iota(jnp.int32, (ROUTE_COLS, ROUTE_COLS), 1)
        upper = jnp.where(ri <= ci, 1.0, 0.0)
        cum = _dot_hi(jnp.broadcast_to(nblk, (8, ROUTE_COLS)), upper)[0:1, :]
        cum_ex = cum - nblk
        pstart[...] = cum_ex * BM
        row = lax.broadcasted_iota(jnp.int32, (SUBLANES, ROUTE_COLS), 0)
        tbl = jnp.where(row == 0, cum_ex, jnp.where(row == 1, cnt, 0.0))
        tbl_ref[...] = tbl.astype(jnp.int32)
        carry[...] = jnp.zeros_like(carry)


def _plan(route_p, route_s):
    rows_p, rows_s = route_p.shape[0], route_s.shape[0]
    ntp = rows_p // TPP
    nbp = SUBLANES
    ri = lax.broadcasted_iota(jnp.int32, (TP, TP), 0)
    ci = lax.broadcasted_iota(jnp.int32, (TP, TP), 1)
    tri = jnp.where(ci < ri, 1.0, 0.0).astype(BF16)
    return pl.pallas_call(
        functools.partial(_plan_kernel, ntp=ntp),
        grid=(2, ntp + 1),
        in_specs=[pl.BlockSpec((TPP, ROUTE_COLS), lambda p, i: (jnp.minimum(i, ntp - 1), 0)),
                  pl.BlockSpec((rows_s, ROUTE_COLS), lambda p, i: (0, 0)),
                  pl.BlockSpec((TP, TP), lambda p, i: (0, 0))],
        out_specs=[pl.BlockSpec((TPP, 2), lambda p, i: (jnp.minimum(i, ntp - 1) * p, 0)),
                   pl.BlockSpec((rows_s, 2), lambda p, i: (0, 0)),
                   pl.BlockSpec((nbp, ROUTE_COLS), lambda p, i: (0, 0))],
        out_shape=[jax.ShapeDtypeStruct((rows_p, 2), jnp.int32),
                   jax.ShapeDtypeStruct((rows_s, 2), jnp.int32),
                   jax.ShapeDtypeStruct((nbp, ROUTE_COLS), jnp.int32)],
        scratch_shapes=[pltpu.VMEM((1, ROUTE_COLS), F32), pltpu.VMEM((1, ROUTE_COLS), F32)],
        compiler_params=pltpu.CompilerParams(
            dimension_semantics=("arbitrary", "arbitrary"), vmem_limit_bytes=VMEM_LIMIT),
        name="moe_plan",
    )(route_p, route_s, tri)


def _scatter_kernel(dest_ref, h_ref, *rest):
    rows_ref, sem = rest[-2:]
    tp = h_ref.shape[0] // ROW_TILE

    def row_copy(r, k):
        d = dest_ref[2 * r + k]
        return pltpu.make_async_copy(_row_tile(h_ref, r), _row_tile(rows_ref, d), sem)

    def issue(r, c):
        row_copy(r, 0).start(priority=0)
        row_copy(r, 1).start(priority=1)
        return c

    lax.fori_loop(0, tp, issue, 0, unroll=8)

    def drain(r, c):
        row_copy(r, 0).wait()
        row_copy(r, 1).wait()
        return c

    lax.fori_loop(0, tp, drain, 0, unroll=8)


def _scatter_rows(dest_flat, h2, rows_buf, n_rows, tp):
    nt = h2.shape[0] // (tp * ROW_TILE)
    in_specs = [pl.BlockSpec((2 * tp,), lambda i: (i,), memory_space=pltpu.SMEM),
                pl.BlockSpec((tp * ROW_TILE, LANES), lambda i: (i, 0))]
    args = [dest_flat, h2]
    aliases = {}
    if rows_buf is not None:
        in_specs.append(pl.BlockSpec(memory_space=pl.ANY))
        args.append(rows_buf)
        aliases = {2: 0}
    return pl.pallas_call(
        _scatter_kernel,
        grid=(nt,),
        in_specs=in_specs,
        out_specs=pl.BlockSpec(memory_space=pl.ANY),
        out_shape=jax.ShapeDtypeStruct((n_rows * ROW_TILE, LANES), U32),
        scratch_shapes=[pltpu.SemaphoreType.DMA(())],
        input_output_aliases=aliases,
        compiler_params=pltpu.CompilerParams(
            dimension_semantics=("arbitrary",), vmem_limit_bytes=VMEM_LIMIT,
            has_side_effects=True),
        name="moe_scatter",
    )(*args)


ROW_DMA_PRIORITY = 1


def _expert_kernel(blk0_ref, cnt_ref, x_hbm, w1_ref, w3_ref, w2_ref, y_hbm,
                   xbuf, ybuf, w1b, w3b, w2b, sem_in, sem_out):
    e = pl.program_id(0)
    cnt = cnt_ref[e]
    nblk = lax.shift_right_logical(cnt + (BM - 1), BM.bit_length() - 1)
    blk0 = blk0_ref[e]
    last_e = NEXP - 1
    n_used = blk0_ref[last_e] + lax.shift_right_logical(cnt_ref[last_e] + (BM - 1),
                                                        BM.bit_length() - 1)
    blk_rows = BM * ROW_TILE

    def block_rows(ref, g):
        return ref.at[pl.ds(pl.multiple_of(g * blk_rows, blk_rows), blk_rows)]

    def in_copy(g, slot):
        return pltpu.make_async_copy(block_rows(x_hbm, g), xbuf.at[slot], sem_in.at[slot])

    def out_copy(g, slot):
        return pltpu.make_async_copy(ybuf.at[slot], block_rows(y_hbm, g), sem_out.at[slot])

    @pl.when(nblk > 0)
    def _():
        @pl.when(blk0 == 0)
        def _():
            in_copy(0, 0).start(priority=ROW_DMA_PRIORITY)

        w1b[...] = w1_ref[0, 0].astype(BF16)
        w3b[...] = w3_ref[0, 0].astype(BF16)
        w2b[...] = w2_ref[0, 0].astype(BF16)

        def body(j, carry):
            g = blk0 + j
            slot = lax.rem(g, 2)

            @pl.when(g + 1 < n_used)
            def _():
                in_copy(g + 1, 1 - slot).start(priority=ROW_DMA_PRIORITY)

            in_copy(g, slot).wait()

            @pl.when(g >= 2)
            def _():
                out_copy(g - 2, slot).wait()

            row = lax.broadcasted_iota(jnp.int32, (BM, D), 0)
            x = jnp.where(row < cnt - j * BM, _load_row_tiles(xbuf.at[slot]), 0.0).astype(BF16)
            h1 = _dot(x, w1b[...])
            h3 = _dot(x, w3b[...])
            act = (_silu(h1) * h3).astype(BF16)
            _store_row_tiles(ybuf.at[slot], _dot(act, w2b[...]))
            out_copy(g, slot).start(priority=ROW_DMA_PRIORITY)
            return carry

        lax.fori_loop(0, nblk, body, 0)

    @pl.when(e == last_e)
    def _():
        @pl.when(n_used >= 2)
        def _():
            out_copy(n_used - 2, lax.rem(n_used, 2)).wait()

        out_copy(n_used - 1, lax.rem(n_used - 1, 2)).wait()


def _experts(tbl, rows_buf, w1, w3, w2, layer):
    blk0, cnt = tbl[0, :NEXP], tbl[1, :NEXP]
    wspec = lambda shape: pl.BlockSpec((1, 1) + shape, lambda e, blk0, cnt: (layer, e, 0, 0))
    return pl.pallas_call(
        _expert_kernel,
        grid_spec=pltpu.PrefetchScalarGridSpec(
            num_scalar_prefetch=2,
            grid=(NEXP,),
            in_specs=[pl.BlockSpec(memory_space=pl.ANY),
                      wspec((D, FF)), wspec((D, FF)), wspec((FF, D))],
            out_specs=pl.BlockSpec(memory_space=pl.ANY),
            scratch_shapes=[pltpu.VMEM((2, BM * ROW_TILE, LANES), U32),
                            pltpu.VMEM((2, BM * ROW_TILE, LANES), U32),
                            pltpu.VMEM((D, FF), BF16), pltpu.VMEM((D, FF), BF16),
                            pltpu.VMEM((FF, D), BF16),
                            pltpu.SemaphoreType.DMA((2,)), pltpu.SemaphoreType.DMA((2,))],
        ),
        out_shape=jax.ShapeDtypeStruct(rows_buf.shape, U32),
        compiler_params=pltpu.CompilerParams(
            dimension_semantics=("arbitrary",), vmem_limit_bytes=VMEM_LIMIT,
            has_side_effects=True),
        name="moe_experts",
    )(blk0, cnt, rows_buf, w1, w3, w2)


def _combine_kernel(dest_ref, dest_next_ref, x1_ref, route_ref, g2_ref, fng_ref, y_ref, o_ref,
                    ybuf, sem, *, final_norm, g2_row0, tiles_per_g2):
    tp = x1_ref.shape[0]
    i = pl.program_id(0)
    n = pl.num_programs(0)
    slot = lax.rem(i, 2)

    def gather(d_ref, s):
        def row_copy(r, k):
            return pltpu.make_async_copy(_row_tile(y_ref, d_ref[2 * r + k]),
                                         _row_tile(ybuf.at[s, k], r), sem.at[s])

        def issue(r, c):
            row_copy(r, 0).start(priority=0)
            row_copy(r, 1).start(priority=1)
            return c

        lax.fori_loop(0, tp, issue, 0, unroll=8)

    @pl.when(i == 0)
    def _():
        gather(dest_ref, 0)

    @pl.when(i + 1 < n)
    def _():
        gather(dest_next_ref, 1 - slot)

    for k in range(2):
        pltpu.make_async_copy(y_ref.at[pl.ds(0, tp * ROW_TILE)], ybuf.at[slot, k], sem.at[slot]).wait()

    route = route_ref[...]
    w_a = route[:, 2:3]
    w_b = route[:, 3:4]
    if tiles_per_g2:
        g2 = g2_ref[0, 0, pl.ds(g2_row0 + i // tiles_per_g2, 1), :]
    else:
        g2 = jnp.concatenate([g2_ref[0, 0]] * (tp // g2_ref.shape[2]), axis=0)
    y = (w_a * _load_row_tiles(ybuf.at[slot, 0]) + w_b * _load_row_tiles(ybuf.at[slot, 1]))
    x2 = x1_ref[...] + g2 * y
    if final_norm:
        x2 = _rms(x2, fng_ref[...])
    o_ref[...] = x2


def _combine(dest_flat, x1, route, mod, layer, fng, y_rows, tp, *, final_norm, g2_row0=0,
             tiles_per_g2=0, seqs_per_tile=0):
    nt = x1.shape[0] // tp
    if tiles_per_g2:
        g2_spec = pl.BlockSpec((1, 1, mod.shape[2], D), lambda i: (layer, 5, 0, 0))
    else:
        g2_spec = pl.BlockSpec((1, 1, seqs_per_tile, D), lambda i: (layer, 5, i, 0))
    return pl.pallas_call(
        functools.partial(_combine_kernel, final_norm=final_norm, g2_row0=g2_row0,
                          tiles_per_g2=tiles_per_g2),
        grid=(nt,),
        in_specs=[pl.BlockSpec((2 * tp,), lambda i: (i,), memory_space=pltpu.SMEM),
                  pl.BlockSpec((2 * tp,), lambda i: (jnp.minimum(i + 1, nt - 1),),
                               memory_space=pltpu.SMEM),
                  pl.BlockSpec((tp, D), lambda i: (i, 0)),
                  pl.BlockSpec((tp, ROUTE_COLS), lambda i: (i, 0)),
                  g2_spec,
                  pl.BlockSpec((1, D), lambda i: (0, 0)),
                  pl.BlockSpec(memory_space=pl.ANY)],
        out_specs=pl.BlockSpec((tp, D), lambda i: (i, 0)),
        out_shape=jax.ShapeDtypeStruct(x1.shape, F32),
        scratch_shapes=[pltpu.VMEM((2, 2, tp * ROW_TILE, LANES), U32),
                        pltpu.SemaphoreType.DMA((2,))],
        compiler_params=pltpu.CompilerParams(
            dimension_semantics=("arbitrary",), vmem_limit_bytes=VMEM_LIMIT),
        name="moe_combine",
    )(dest_flat, dest_flat, x1, route, mod, fng, y_rows)


def _layer_weights(l, p):
    hd = W // HEADS
    gi = jnp.arange(W) // (W // NG_A)
    gavg = jnp.where(gi[:, None] == gi[None, :], 1.0 / (W // NG_A), 0.0).astype(BF16)
    tril = jnp.tril(jnp.ones((CHUNK, CHUNK), dtype=bool))
    w_sp = jnp.where(tril[None], p["b_spatial_w"][l], 0.0)
    w_router = jnp.concatenate(
        [p["w_router_g"][l], p["w_router_e"][l],
         jnp.zeros((D, ROUTE_COLS - NGRP - NEXP), F32)], axis=1)
    b_router = jnp.concatenate(
        [p["b_router_g"][l], p["b_router_e"][l],
         jnp.zeros((ROUTE_COLS - NGRP - NEXP,), F32)])[None, :]
    return {
        "n1g": p["norm1_g"][l][None, :], "n2g": p["norm2_g"][l][None, :],
        "w_in": p["w_in"][l].astype(BF16),
        "a_dw_w": p["a_dw_w"][l], "a_dw_b": p["a_dw_b"][l][None, :],
        "a_norm_g": p["a_norm_g"][l][None, :], "a_norm_b": p["a_norm_b"][l][None, :],
        "b_norm_g": p["b_norm_g"][l][None, :], "b_norm_b": p["b_norm_b"][l][None, :],
        "w_sp": jnp.concatenate([w_sp[0::2], w_sp[1::2]], axis=2).astype(BF16),
        "b_sp": jnp.repeat(p["b_spatial_b"][l].T, hd, axis=1),
        "c_conv_w": p["c_conv_w"][l],
        "w_branch": p["w_branch"][l].astype(BF16), "w_out": p["w_out"][l].astype(BF16),
        "gavg": gavg, "w_router": w_router.astype(BF16), "b_router": b_router,
    }


def _sample_spatial(l, p, ts):
    hd = W // HEADS
    w = p["b_spatial_w"][l][:, :ts, :ts]
    w = jnp.where(jnp.tril(jnp.ones((ts, ts), dtype=bool))[None], w, 0.0)
    w4 = jnp.repeat(jnp.transpose(w, (1, 2, 0)).reshape(ts * ts, HEADS), hd, axis=1)
    b4 = jnp.repeat(p["b_spatial_b"][l][:, :ts].T, hd, axis=1)
    return w4, b4


def kernel(x_prompt, x_sample, c_prompt, c_sample, state_conv_a, state_conv_c, norm1_g, norm2_g, w_ada, b_ada, w_in, a_dw_w, a_dw_b, a_norm_g, a_norm_b, b_norm_g, b_norm_b, b_spatial_w, b_spatial_b, c_conv_w, w_branch, w_out, w_router_g, b_router_g, w_router_e, b_router_e, w1, w3, w2, final_norm_g):
    p = dict(norm1_g=norm1_g, norm2_g=norm2_g, w_in=w_in, a_dw_w=a_dw_w, a_dw_b=a_dw_b,
             a_norm_g=a_norm_g, a_norm_b=a_norm_b, b_norm_g=b_norm_g, b_norm_b=b_norm_b,
             b_spatial_w=b_spatial_w, b_spatial_b=b_spatial_b, c_conv_w=c_conv_w,
             w_branch=w_branch, w_out=w_out, w_router_g=w_router_g, b_router_g=b_router_g,
             w_router_e=w_router_e, b_router_e=b_router_e)
    depth = w_in.shape[0]
    nb_, seq, _ = x_prompt.shape
    ns, ts, _ = x_sample.shape
    n_tok = nb_ * seq + ns * ts
    nblocks = -(-(2 * n_tok) // BM) + NEXP
    nsplit = 2
    nbs = ns // nsplit
    tps = ts * nbs
    assert seq % TT == 0 and TT % CHUNK == 0 and seq % TP == 0 and (nb_ * seq) % TPP == 0
    assert (ts * ns) % TP == 0 and ns % SUBLANES == 0

    mod = _adaln(jnp.concatenate([c_sample, c_prompt], axis=0), w_ada, b_ada)

    xp = x_prompt.reshape(nb_ * seq, D)
    xs = jnp.transpose(x_sample.reshape(nsplit, nbs, ts, D), (0, 2, 1, 3)).reshape(ts * ns, D)
    fng = final_norm_g[None, :]
    pa, pc, sa, sc, sv = [], [], [], [], []
    for l in range(depth):
        lw = _layer_weights(l, p)
        lw["w_sp4"], lw["b_sp4"] = _sample_spatial(l, p, ts)

        x1p, h2p, route_p, ha, hc = _mixer_prompt(xp, mod, l, nb_, ns, lw)
        pa.append(ha)
        pc.append(hc)
        x1s, h2s, route_s, ha_s, hc_s, v_s = _mixer_sample(
            xs, mod, l, jnp.transpose(state_conv_a[l], (1, 0, 2)),
            jnp.transpose(state_conv_c[l], (1, 0, 2)), lw, nsplit)
        sa.append(jnp.transpose(ha_s, (1, 0, 2)))
        sc.append(jnp.transpose(hc_s, (1, 0, 2)))
        sv.append(jnp.transpose(v_s, (1, 0, 2)))

        dest_p, dest_s, tbl = _plan(route_p, route_s)
        dest_p, dest_s = dest_p.reshape(-1), dest_s.reshape(-1)
        rows_buf = _scatter_rows(dest_p, h2p, None, nblocks * BM, TP)
        rows_buf = _scatter_rows(dest_s, h2s, rows_buf, nblocks * BM, tps)
        y_rows = _experts(tbl, rows_buf, w1, w3, w2, l)

        last = l == depth - 1
        xp = _combine(dest_p, x1p, route_p, mod, l, fng, y_rows, TP, final_norm=last,
                      g2_row0=ns, tiles_per_g2=seq // TP)
        xs = _combine(dest_s, x1s, route_s, mod, l, fng, y_rows, tps, final_norm=last,
                      seqs_per_tile=nbs)

    y_prompt = xp.reshape(nb_, seq, D)
    y_sample = jnp.transpose(xs.reshape(nsplit, ts, nbs, D), (0, 2, 1, 3)).reshape(ns, ts, D)
    return (y_prompt, y_sample, jnp.stack(pa), jnp.stack(pc), jnp.stack(sa), jnp.stack(sc),
            jnp.stack(sv))
```

```python
import functools

import jax
import jax.numpy as jnp
from jax import lax
from jax.experimental import pallas as pl
from jax.experimental.pallas import tpu as pltpu

F32 = jnp.float32
BF16 = jnp.bfloat16

D = 1024
W = D // 2
KA = 31
KC = 3
NG_A = 8
CHUNK = 128
HEADS = 8
NGRP = 4
EPG = 8
NEXP = NGRP * EPG
FF = D // 2
EPS = 1e-6
IN_COLS = 7 * W + 3 * D
LANES = 128
ROUTE_COLS = LANES

TT = 512
SUBLANES = 8
A_HDR = 32
A_PAD = SUBLANES
A_ROWS = 64
C_HDR = 8
TP = 512
TPP = 2048
BM = 256
VMEM_LIMIT = 56 * 1024 * 1024

NEG = -1e30


def _sigmoid(x):
    return 1.0 / (1.0 + jnp.exp(-x))


def _silu(x):
    return x * _sigmoid(x)


def _gelu_tanh(x):
    return 0.5 * x * (1.0 + jnp.tanh(0.7978845608028654 * (x + 0.044715 * x * x * x)))


def _rms(x, g):
    return x * lax.rsqrt(jnp.mean(x * x, axis=-1, keepdims=True) + EPS) * g


def _dot(a, b):
    return jnp.dot(a, b, preferred_element_type=F32)


def _dot_hi(a, b):
    return jnp.dot(a, b, preferred_element_type=F32, precision=lax.Precision.HIGHEST)


U32 = jnp.uint32
ROW_TILE = D // (2 * LANES)


def _pack_bf16_pair(lo, hi):
    def rne(x):
        b = pltpu.bitcast(x, U32)
        return b + (U32(0x7FFF) + ((b >> 16) & U32(1)))
    return (rne(lo) >> 16) | (rne(hi) & U32(0xFFFF0000))


def _store_row_tiles(ref, val):
    r = val.shape[0]
    for j in range(ROW_TILE):
        lo = val[:, 2 * j * LANES:(2 * j + 1) * LANES]
        hi = val[:, (2 * j + 1) * LANES:(2 * j + 2) * LANES]
        ref[pl.ds(j, r, stride=ROW_TILE), :] = _pack_bf16_pair(lo, hi)


def _load_row_tiles(ref):
    r = ref.shape[0] // ROW_TILE
    parts = []
    for j in range(ROW_TILE):
        w = ref[pl.ds(j, r, stride=ROW_TILE), :]
        parts.append(pltpu.bitcast(w << 16, F32))
        parts.append(pltpu.bitcast(w & U32(0xFFFF0000), F32))
    return jnp.concatenate(parts, axis=1)


def _row_tile(ref, r):
    return ref.at[pl.ds(pl.multiple_of(r * ROW_TILE, ROW_TILE), ROW_TILE)]


def _mod_kernel(c_ref, w_ref, b_ref, o_ref):
    c = c_ref[...]
    s = _silu(c).astype(BF16)
    o_ref[0, 0] = _dot(s, w_ref[0].astype(BF16)) + b_ref[0]


def _adaln(c_all, w_ada, b_ada):
    nl = w_ada.shape[0]
    r = c_all.shape[0]
    return pl.pallas_call(
        _mod_kernel,
        grid=(nl, 6),
        in_specs=[
            pl.BlockSpec((r, D), lambda l, j: (0, 0)),
            pl.BlockSpec((1, D, D), lambda l, j: (l, 0, j)),
            pl.BlockSpec((1, 1, D), lambda l, j: (l, 0, j)),
        ],
        out_specs=pl.BlockSpec((1, 1, r, D), lambda l, j: (l, j, 0, 0)),
        out_shape=jax.ShapeDtypeStruct((nl, 6, r, D), F32),
        compiler_params=pltpu.CompilerParams(
            dimension_semantics=("arbitrary", "arbitrary"), vmem_limit_bytes=VMEM_LIMIT),
        name="adaln_mod",
    )(c_all, w_ada, b_ada.reshape(nl, 1, 6 * D))


def _group_norm_silu(a, gavg_ref, g, b):
    mu = _dot(a.astype(BF16), gavg_ref[...])
    xc = a - mu
    var = _dot((xc * xc).astype(BF16), gavg_ref[...])
    y = xc * lax.rsqrt(var + EPS) * g + b
    return _silu(y)


def _layer_norm(x, g, b):
    mu = jnp.mean(x, axis=-1, keepdims=True)
    xc = x - mu
    var = jnp.mean(xc * xc, axis=-1, keepdims=True)
    return xc * lax.rsqrt(var + EPS) * g + b


def _merge_and_route(x, a, bb, cc, gates_fn, mod, n2g, wbr_ref, wout_ref, wr_ref, br_ref,
                     x1_ref, h2_ref, route_ref):
    g1, sh2, sc2 = mod
    acc = None
    for gi, br_in in enumerate((a, bb, cc)):
        br = _dot(br_in.astype(BF16), wbr_ref[gi])
        term = _sigmoid(gates_fn(gi)) * br
        acc = term if acc is None else acc + term
    mixed = _dot(acc.astype(BF16), wout_ref[...])
    x1 = x + g1 * mixed
    x1_ref[...] = x1
    h2 = _rms(x1, n2g) * (1.0 + sc2) + sh2
    _store_row_tiles(h2_ref, h2)

    logits = _dot(h2.astype(BF16), wr_ref[...]) + br_ref[...]
    rows = logits.shape[0]
    lane = lax.broadcasted_iota(jnp.int32, (rows, ROUTE_COLS), 1).astype(F32)
    is_g = lane < NGRP
    gl = jnp.where(is_g, logits, NEG)
    gmax = jnp.max(gl, axis=-1, keepdims=True)
    gsel = jnp.min(jnp.where(gl == gmax, lane, float(ROUTE_COLS)), axis=-1, keepdims=True)
    gden = jnp.sum(jnp.where(is_g, jnp.exp(gl - gmax), 0.0), axis=-1, keepdims=True)
    gprob = 1.0 / gden
    lo = NGRP + EPG * gsel
    in_grp = jnp.logical_and(lane >= lo, lane < lo + EPG)
    el = jnp.where(in_grp, logits, NEG)
    v1 = jnp.max(el, axis=-1, keepdims=True)
    i1 = jnp.min(jnp.where(el == v1, lane, float(ROUTE_COLS)), axis=-1, keepdims=True)
    el2 = jnp.where(lane == i1, NEG, el)
    v2 = jnp.max(el2, axis=-1, keepdims=True)
    i2 = jnp.min(jnp.where(el2 == v2, lane, float(ROUTE_COLS)), axis=-1, keepdims=True)
    p2 = jnp.exp(v2 - v1)
    w_a = gprob / (1.0 + p2)
    w_b = gprob * p2 / (1.0 + p2)
    route = jnp.where(lane == 0.0, i1 - NGRP,
                      jnp.where(lane == 1.0, i2 - NGRP,
                                jnp.where(lane == 2.0, w_a,
                                          jnp.where(lane == 3.0, w_b, 0.0))))
    route_ref[...] = route


def _mixer_prompt_kernel(x_ref, mod_ref, n1g_ref, n2g_ref, win_ref, adw_ref, adb_ref, ang_ref,
                         anb_ref, bng_ref, bnb_ref, wsp_ref, bsp_ref, ccw_ref, wbr_ref, wout_ref,
                         gavg_ref, wr_ref, br_ref,
                         x1_ref, h2_ref, route_ref, ha_ref, hc_ref,
                         abuf, cbuf, aconv, *, mod_row0):
    first = pl.program_id(1) == 0
    abuf[0:A_HDR, :] = jnp.where(first, 0.0, abuf[TT:TT + A_HDR, :])
    abuf[A_HDR + TT:A_HDR + TT + A_PAD, :] = jnp.zeros((A_PAD, W), F32)
    cbuf[0:C_HDR, :] = jnp.where(first, 0.0, cbuf[TT:TT + C_HDR, :])

    x = x_ref[...]
    mrow = mod_row0 + pl.program_id(0)
    sh1, sc1, g1, sh2, sc2 = (mod_ref[0, s, pl.ds(mrow, 1), :] for s in range(5))
    h = (_rms(x, n1g_ref[...]) * (1.0 + sc1) + sh1).astype(BF16)

    def zsec(lo, hi):
        return _dot(h, win_ref[:, lo:hi])

    a_glu = zsec(0, W) * _sigmoid(zsec(W, 2 * W))
    abuf[A_HDR:A_HDR + TT, :] = a_glu
    off = A_HDR - (KA - 1)
    rc = A_ROWS

    def conv_pass(p):
        lt, half = p // 2, p % 2
        ls = slice(lt * LANES, (lt + 1) * LANES)
        for r0 in range(half * (TT // 2), (half + 1) * (TT // 2), rc):
            y = None
            for r in range(SUBLANES):
                z = None
                for q in range((KA - 1 + off) // SUBLANES + 1):
                    k = SUBLANES * q + r - off
                    if 0 <= k < KA:
                        rows = slice(r0 + SUBLANES * q, r0 + SUBLANES * q + rc + SUBLANES)
                        term = adw_ref[k:k + 1, ls] * abuf[rows, ls]
                        z = term if z is None else z + term
                part = z[r:r + rc, :]
                y = part if y is None else y + part
            aconv[r0:r0 + rc, ls] = y + adb_ref[:, ls]

    conv_pass(0)
    u = _gelu_tanh(zsec(2 * W, 3 * W))
    conv_pass(1)
    v = _layer_norm(_gelu_tanh(zsec(3 * W, 4 * W)), bng_ref[...], bnb_ref[...])
    conv_pass(2)

    cch = zsec(5 * W, 6 * W) * zsec(6 * W, 7 * W)
    cbuf[C_HDR:C_HDR + TT, :] = cch
    coff = C_HDR - (KC - 1)
    ych = (ccw_ref[0:1, :] * cbuf[coff:coff + TT, :]
           + ccw_ref[1:2, :] * cbuf[coff + 1:coff + 1 + TT, :]
           + ccw_ref[2:3, :] * cbuf[coff + 2:coff + 2 + TT, :])
    conv_pass(3)
    cc = zsec(4 * W, 5 * W) * ych
    hc_ref[0] = cbuf[C_HDR + TT - (KC - 1):C_HDR + TT, :]
    conv_pass(4)
    gate_z = [zsec(7 * W, 7 * W + D)]
    conv_pass(5)
    gate_z.append(zsec(7 * W + D, 7 * W + 2 * D))
    conv_pass(6)
    gate_z.append(zsec(7 * W + 2 * D, 7 * W + 3 * D))
    conv_pass(7)
    ha_ref[0] = abuf[A_HDR + TT - (KA - 1):A_HDR + TT, :]

    vb = v.astype(BF16)
    low_head = lax.broadcasted_iota(jnp.int32, (CHUNK, LANES), 1) < (W // HEADS)
    zero_b = jnp.zeros((CHUNK, LANES), BF16)
    s_parts = []
    for c in range(TT // CHUNK):
        tiles = []
        for lt in range(W // LANES):
            vt = vb[c * CHUNK:(c + 1) * CHUNK, lt * LANES:(lt + 1) * LANES]
            stacked = jnp.concatenate([jnp.where(low_head, vt, zero_b),
                                       jnp.where(low_head, zero_b, vt)], axis=0)
            tiles.append(_dot(wsp_ref[lt], stacked))
        s_parts.append(jnp.concatenate(tiles, axis=1) + bsp_ref[...])
    bb = u * jnp.concatenate(s_parts, axis=0)
    a = _group_norm_silu(aconv[...], gavg_ref, ang_ref[...], anb_ref[...])

    _merge_and_route(x, a, bb, cc, lambda gi: gate_z[gi], (g1, sh2, sc2), n2g_ref[...], wbr_ref,
                     wout_ref, wr_ref, br_ref, x1_ref, h2_ref, route_ref)


def _const_spec(shape):
    nd = len(shape)
    return pl.BlockSpec(shape, lambda *_: (0,) * nd, pipeline_mode=pl.Buffered(1))


def _mixer_prompt(x, mod, layer, n, mod_row0, lw):
    t = x.shape[0] // n
    nt = t // TT
    mod_rows = mod.shape[2]
    row_spec = lambda cols: pl.BlockSpec((TT, cols), lambda i, j: (i * nt + j, 0))
    consts = [lw["n1g"], lw["n2g"], lw["w_in"], lw["a_dw_w"], lw["a_dw_b"], lw["a_norm_g"],
              lw["a_norm_b"], lw["b_norm_g"], lw["b_norm_b"], lw["w_sp"], lw["b_sp"], lw["c_conv_w"],
              lw["w_branch"], lw["w_out"], lw["gavg"], lw["w_router"], lw["b_router"]]
    return pl.pallas_call(
        functools.partial(_mixer_prompt_kernel, mod_row0=mod_row0),
        grid=(n, nt),
        in_specs=[row_spec(D),
                  pl.BlockSpec((1, 6, mod_rows, D), lambda i, j: (layer, 0, 0, 0),
                               pipeline_mode=pl.Buffered(1))]
        + [_const_spec(c.shape) for c in consts],
        out_specs=[row_spec(D),
                   pl.BlockSpec((TT * ROW_TILE, LANES), lambda i, j: (i * nt + j, 0)),
                   row_spec(ROUTE_COLS),
                   pl.BlockSpec((1, KA - 1, W), lambda i, j: (i, 0, 0)),
                   pl.BlockSpec((1, KC - 1, W), lambda i, j: (i, 0, 0))],
        out_shape=[jax.ShapeDtypeStruct((n * t, D), F32),
                   jax.ShapeDtypeStruct((n * t * ROW_TILE, LANES), U32),
                   jax.ShapeDtypeStruct((n * t, ROUTE_COLS), F32),
                   jax.ShapeDtypeStruct((n, KA - 1, W), F32),
                   jax.ShapeDtypeStruct((n, KC - 1, W), F32)],
        scratch_shapes=[pltpu.VMEM((A_HDR + TT + A_PAD, W), F32),
                        pltpu.VMEM((C_HDR + TT, W), F32),
                        pltpu.VMEM((TT, W), F32)],
        compiler_params=pltpu.CompilerParams(
            dimension_semantics=("arbitrary", "arbitrary"), vmem_limit_bytes=VMEM_LIMIT),
        name="mixer_prompt",
    )(x, mod, *consts)


def _mixer_sample_kernel(x_ref, mod_ref, hista_ref, histc_ref, n1g_ref, n2g_ref, win_ref, adw_ref,
                         adb_ref, ang_ref, anb_ref, bng_ref, bnb_ref, wsp4_ref, bsp4_ref,
                         ccw_ref, wbr_ref, wout_ref, gavg_ref, wr_ref, br_ref,
                         x1_ref, h2_ref, route_ref, ha_ref, hc_ref, v_ref):
    ts, nb = v_ref.shape[0], v_ref.shape[1]
    x = x_ref[...]

    def modrow(i):
        return jnp.concatenate([mod_ref[0, i]] * ts, axis=0)

    h = (_rms(x, n1g_ref[...]) * (1.0 + modrow(1)) + modrow(0)).astype(BF16)

    def zsec(lo, hi):
        return _dot(h, win_ref[:, lo:hi])

    def tslab(arr, j):
        return arr[j * nb:(j + 1) * nb, :]

    a_glu = zsec(0, W) * _sigmoid(zsec(W, 2 * W))
    kh = KA - 1
    a_conv = []
    for tq in range(ts):
        acc = jnp.zeros((nb, W), F32) + adb_ref[...]
        for r in range(tq, kh):
            acc = acc + adw_ref[r - tq:r - tq + 1, :] * hista_ref[r]
        for j in range(tq + 1):
            acc = acc + adw_ref[kh + j - tq:kh + j - tq + 1, :] * tslab(a_glu, j)
        a_conv.append(acc)
    a = _group_norm_silu(jnp.concatenate(a_conv, axis=0), gavg_ref, ang_ref[...], anb_ref[...])
    for r in range(kh - ts):
        ha_ref[r] = hista_ref[r + ts]
    for j in range(ts):
        ha_ref[kh - ts + j] = tslab(a_glu, j)

    u = _gelu_tanh(zsec(2 * W, 3 * W))
    v = _layer_norm(_gelu_tanh(zsec(3 * W, 4 * W)), bng_ref[...], bnb_ref[...])
    for j in range(ts):
        v_ref[j] = tslab(v, j)
    s_rows = []
    for tq in range(ts):
        s = jnp.zeros((nb, W), F32) + bsp4_ref[tq:tq + 1, :]
        for sq in range(tq + 1):
            s = s + wsp4_ref[tq * ts + sq:tq * ts + sq + 1, :] * tslab(v, sq)
        s_rows.append(s)
    bb = u * jnp.concatenate(s_rows, axis=0)

    cch = zsec(5 * W, 6 * W) * zsec(6 * W, 7 * W)
    xp = [histc_ref[r] for r in range(KC - 1)] + [tslab(cch, j) for j in range(ts)]
    ych = jnp.concatenate(
        [sum(ccw_ref[k:k + 1, :] * xp[tq + k] for k in range(KC)) for tq in range(ts)], axis=0)
    cc = zsec(4 * W, 5 * W) * ych
    for r in range(KC - 1):
        hc_ref[r] = xp[ts + r]

    def gates_fn(gi):
        return zsec(7 * W + gi * D, 7 * W + (gi + 1) * D)

    _merge_and_route(x, a, bb, cc, gates_fn, (modrow(2), modrow(3), modrow(4)), n2g_ref[...],
                     wbr_ref, wout_ref, wr_ref, br_ref, x1_ref, h2_ref, route_ref)


def _mixer_sample(x_rows, mod, layer, hist_a_tm, hist_c_tm, lw, nsplit):
    n = hist_a_tm.shape[1]
    ts = x_rows.shape[0] // n
    nb = n // nsplit
    rows = ts * nb
    consts = [lw["n1g"], lw["n2g"], lw["w_in"], lw["a_dw_w"], lw["a_dw_b"],
              lw["a_norm_g"], lw["a_norm_b"], lw["b_norm_g"], lw["b_norm_b"], lw["w_sp4"], lw["b_sp4"],
              lw["c_conv_w"], lw["w_branch"], lw["w_out"], lw["gavg"], lw["w_router"], lw["b_router"]]
    seq3 = lambda k, cols: pl.BlockSpec((k, nb, cols), lambda i: (0, i, 0))
    row_spec = lambda cols: pl.BlockSpec((rows, cols), lambda i: (i, 0))
    return pl.pallas_call(
        _mixer_sample_kernel,
        grid=(nsplit,),
        in_specs=[row_spec(D), pl.BlockSpec((1, 6, nb, D), lambda i: (layer, 0, i, 0)),
                  seq3(KA - 1, W), seq3(KC - 1, W)]
        + [_const_spec(c.shape) for c in consts],
        out_specs=[row_spec(D),
                   pl.BlockSpec((rows * ROW_TILE, LANES), lambda i: (i, 0)),
                   row_spec(ROUTE_COLS),
                   seq3(KA - 1, W), seq3(KC - 1, W), seq3(ts, W)],
        out_shape=[jax.ShapeDtypeStruct((ts * n, D), F32),
                   jax.ShapeDtypeStruct((ts * n * ROW_TILE, LANES), U32),
                   jax.ShapeDtypeStruct((ts * n, ROUTE_COLS), F32),
                   jax.ShapeDtypeStruct((KA - 1, n, W), F32),
                   jax.ShapeDtypeStruct((KC - 1, n, W), F32),
                   jax.ShapeDtypeStruct((ts, n, W), F32)],
        compiler_params=pltpu.CompilerParams(
            dimension_semantics=("arbitrary",), vmem_limit_bytes=VMEM_LIMIT),
        name="mixer_sample",
    )(x_rows, mod, hist_a_tm, hist_c_tm, *consts)


def _plan_kernel(rp_ref, rs_ref, tri_ref, destp_ref, dests_ref, tbl_ref, carry, pstart, *, ntp):
    ph = pl.program_id(0)
    i = pl.program_id(1)
    nt = pl.num_programs(1)
    lane = lax.broadcasted_iota(jnp.int32, (TP, ROUTE_COLS), 1).astype(F32)

    @pl.when(jnp.logical_and(ph == 0, i == 0))
    def _():
        carry[...] = jnp.zeros_like(carry)

    def sub_tile(r, dest_ref, row0):
        hot_a = lane == r[:, 0:1]
        hot_b = lane == r[:, 1:2]
        s = jnp.where(jnp.logical_or(hot_a, hot_b), 1.0, 0.0)

        @pl.when(ph == 1)
        def _():
            pre = _dot(tri_ref[...], s.astype(BF16)) + carry[...] + pstart[...]
            d_a = jnp.sum(jnp.where(hot_a, pre, 0.0), axis=-1, keepdims=True)
            d_b = jnp.sum(jnp.where(hot_b, pre, 0.0), axis=-1, keepdims=True)
            l2 = lax.broadcasted_iota(jnp.int32, (TP, 2), 1)
            dest_ref[row0:row0 + TP, :] = jnp.where(l2 == 0, d_a, d_b).astype(jnp.int32)

        carry[...] = carry[...] + jnp.sum(s, axis=0, keepdims=True)

    @pl.when(i < ntp)
    def _():
        for sub in range(rp_ref.shape[0] // TP):
            sub_tile(rp_ref[sub * TP:(sub + 1) * TP, :], destp_ref, sub * TP)

    @pl.when(i == ntp)
    def _():
        for sub in range(rs_ref.shape[0] // TP):
            sub_tile(rs_ref[sub * TP:(sub + 1) * TP, :], dests_ref, sub * TP)

    @pl.when(jnp.logical_and(ph == 0, i == nt - 1))
    def _():
        cnt = carry[...]
        nblk = jnp.floor((cnt + (BM - 1)) * (1.0 / BM))
        ri = lax.broadcasted_iota(jnp.int32, (ROUTE_COLS, ROUTE_COLS), 0)
        ci = lax.broadcasted_iota(jnp.int32, (ROUTE_COLS, ROUTE_COLS), 1)
        upper = jnp.where(ri <= ci, 1.0, 0.0)
        cum = _dot_hi(jnp.broadcast_to(nblk, (8, ROUTE_COLS)), upper)[0:1, :]
        cum_ex = cum - nblk
        pstart[...] = cum_ex * BM
        row = lax.broadcasted_iota(jnp.int32, (SUBLANES, ROUTE_COLS), 0)
        tbl = jnp.where(row == 0, cum_ex, jnp.where(row == 1, cnt, 0.0))
        tbl_ref[...] = tbl.astype(jnp.int32)
        carry[...] = jnp.zeros_like(carry)


def _plan(route_p, route_s):
    rows_p, rows_s = route_p.shape[0], route_s.shape[0]
    ntp = rows_p // TPP
    nbp = SUBLANES
    ri = lax.broadcasted_iota(jnp.int32, (TP, TP), 0)
    ci = lax.broadcasted_iota(jnp.int32, (TP, TP), 1)
    tri = jnp.where(ci < ri, 1.0, 0.0).astype(BF16)
    return pl.pallas_call(
        functools.partial(_plan_kernel, ntp=ntp),
        grid=(2, ntp + 1),
        in_specs=[pl.BlockSpec((TPP, ROUTE_COLS), lambda p, i: (jnp.minimum(i, ntp - 1), 0)),
                  pl.BlockSpec((rows_s, ROUTE_COLS), lambda p, i: (0, 0)),
                  pl.BlockSpec((TP, TP), lambda p, i: (0, 0))],
        out_specs=[pl.BlockSpec((TPP, 2), lambda p, i: (jnp.minimum(i, ntp - 1) * p, 0)),
                   pl.BlockSpec((rows_s, 2), lambda p, i: (0, 0)),
                   pl.BlockSpec((nbp, ROUTE_COLS), lambda p, i: (0, 0))],
        out_shape=[jax.ShapeDtypeStruct((rows_p, 2), jnp.int32),
                   jax.ShapeDtypeStruct((rows_s, 2), jnp.int32),
                   jax.ShapeDtypeStruct((nbp, ROUTE_COLS), jnp.int32)],
        scratch_shapes=[pltpu.VMEM((1, ROUTE_COLS), F32), pltpu.VMEM((1, ROUTE_COLS), F32)],
        compiler_params=pltpu.CompilerParams(
            dimension_semantics=("arbitrary", "arbitrary"), vmem_limit_bytes=VMEM_LIMIT),
        name="moe_plan",
    )(route_p, route_s, tri)


def _scatter_kernel(dest_ref, h_ref, *rest):
    rows_ref, sem = rest[-2:]
    tp = h_ref.shape[0] // ROW_TILE

    def row_copy(r, k):
        d = dest_ref[2 * r + k]
        return pltpu.make_async_copy(_row_tile(h_ref, r), _row_tile(rows_ref, d), sem)

    def issue(r, c):
        row_copy(r, 0).start(priority=0)
        row_copy(r, 1).start(priority=1)
        return c

    lax.fori_loop(0, tp, issue, 0, unroll=8)

    def drain(r, c):
        row_copy(r, 0).wait()
        row_copy(r, 1).wait()
        return c

    lax.fori_loop(0, tp, drain, 0, unroll=8)


def _scatter_rows(dest_flat, h2, rows_buf, n_rows, tp):
    nt = h2.shape[0] // (tp * ROW_TILE)
    in_specs = [pl.BlockSpec((2 * tp,), lambda i: (i,), memory_space=pltpu.SMEM),
                pl.BlockSpec((tp * ROW_TILE, LANES), lambda i: (i, 0))]
    args = [dest_flat, h2]
    aliases = {}
    if rows_buf is not None:
        in_specs.append(pl.BlockSpec(memory_space=pl.ANY))
        args.append(rows_buf)
        aliases = {2: 0}
    return pl.pallas_call(
        _scatter_kernel,
        grid=(nt,),
        in_specs=in_specs,
        out_specs=pl.BlockSpec(memory_space=pl.ANY),
        out_shape=jax.ShapeDtypeStruct((n_rows * ROW_TILE, LANES), U32),
        scratch_shapes=[pltpu.SemaphoreType.DMA(())],
        input_output_aliases=aliases,
        compiler_params=pltpu.CompilerParams(
            dimension_semantics=("arbitrary",), vmem_limit_bytes=VMEM_LIMIT,
            has_side_effects=True),
        name="moe_scatter",
    )(*args)


ROW_DMA_PRIORITY = 1
ROW_SLOTS = 3


def _expert_kernel(blk0_ref, cnt_ref, x_hbm, w1_ref, w3_ref, w2_ref, y_hbm,
                   xbuf, ybuf, w1b, w3b, w2b, sem_in, sem_out):
    e = pl.program_id(0)
    cnt = cnt_ref[e]
    nblk = lax.shift_right_logical(cnt + (BM - 1), BM.bit_length() - 1)
    blk0 = blk0_ref[e]
    last_e = NEXP - 1
    n_used = blk0_ref[last_e] + lax.shift_right_logical(cnt_ref[last_e] + (BM - 1),
                                                        BM.bit_length() - 1)
    blk_rows = BM * ROW_TILE

    def block_rows(ref, g):
        return ref.at[pl.ds(pl.multiple_of(g * blk_rows, blk_rows), blk_rows)]

    def in_copy(g):
        slot = lax.rem(g, ROW_SLOTS)
        return pltpu.make_async_copy(block_rows(x_hbm, g), xbuf.at[slot], sem_in.at[slot])

    def out_copy(g):
        slot = lax.rem(g, ROW_SLOTS)
        return pltpu.make_async_copy(ybuf.at[slot], block_rows(y_hbm, g), sem_out.at[slot])

    @pl.when(nblk > 0)
    def _():
        @pl.when(blk0 == 0)
        def _():
            for g0 in range(ROW_SLOTS - 1):
                @pl.when(g0 < n_used)
                def _():
                    in_copy(g0).start(priority=ROW_DMA_PRIORITY)

        w1b[...] = w1_ref[0, 0].astype(BF16)
        w3b[...] = w3_ref[0, 0].astype(BF16)
        w2b[...] = w2_ref[0, 0].astype(BF16)

        def body(j, carry):
            g = blk0 + j
            slot = lax.rem(g, ROW_SLOTS)

            @pl.when(g + (ROW_SLOTS - 1) < n_used)
            def _():
                in_copy(g + (ROW_SLOTS - 1)).start(priority=ROW_DMA_PRIORITY)

            in_copy(g).wait()

            @pl.when(g >= ROW_SLOTS)
            def _():
                out_copy(g - ROW_SLOTS).wait()

            row = lax.broadcasted_iota(jnp.int32, (BM, D), 0)
            x = jnp.where(row < cnt - j * BM, _load_row_tiles(xbuf.at[slot]), 0.0).astype(BF16)
            h1 = _dot(x, w1b[...])
            h3 = _dot(x, w3b[...])
            act = (_silu(h1) * h3).astype(BF16)
            _store_row_tiles(ybuf.at[slot], _dot(act, w2b[...]))
            out_copy(g).start(priority=ROW_DMA_PRIORITY)
            return carry

        lax.fori_loop(0, nblk, body, 0)

    @pl.when(e == last_e)
    def _():
        for back in range(ROW_SLOTS, 0, -1):
            @pl.when(n_used >= back)
            def _():
                out_copy(n_used - back).wait()


def _experts(tbl, rows_buf, w1, w3, w2, layer):
    blk0, cnt = tbl[0, :NEXP], tbl[1, :NEXP]
    wspec = lambda shape: pl.BlockSpec((1, 1) + shape, lambda e, blk0, cnt: (layer, e, 0, 0))
    return pl.pallas_call(
        _expert_kernel,
        grid_spec=pltpu.PrefetchScalarGridSpec(
            num_scalar_prefetch=2,
            grid=(NEXP,),
            in_specs=[pl.BlockSpec(memory_space=pl.ANY),
                      wspec((D, FF)), wspec((D, FF)), wspec((FF, D))],
            out_specs=pl.BlockSpec(memory_space=pl.ANY),
            scratch_shapes=[pltpu.VMEM((ROW_SLOTS, BM * ROW_TILE, LANES), U32),
                            pltpu.VMEM((ROW_SLOTS, BM * ROW_TILE, LANES), U32),
                            pltpu.VMEM((D, FF), BF16), pltpu.VMEM((D, FF), BF16),
                            pltpu.VMEM((FF, D), BF16),
                            pltpu.SemaphoreType.DMA((ROW_SLOTS,)),
                            pltpu.SemaphoreType.DMA((ROW_SLOTS,))],
        ),
        out_shape=jax.ShapeDtypeStruct(rows_buf.shape, U32),
        compiler_params=pltpu.CompilerParams(
            dimension_semantics=("arbitrary",), vmem_limit_bytes=VMEM_LIMIT,
            has_side_effects=True),
        name="moe_experts",
    )(blk0, cnt, rows_buf, w1, w3, w2)


def _combine_kernel(dest_ref, dest_next_ref, x1_ref, route_ref, g2_ref, fng_ref, y_ref, o_ref,
                    ybuf, sem, *, final_norm, g2_row0, tiles_per_g2):
    tp = x1_ref.shape[0]
    i = pl.program_id(0)
    n = pl.num_programs(0)
    slot = lax.rem(i, 2)

    def gather(d_ref, s):
        def row_copy(r, k):
            return pltpu.make_async_copy(_row_tile(y_ref, d_ref[2 * r + k]),
                                         _row_tile(ybuf.at[s, k], r), sem.at[s])

        def issue(r, c):
            row_copy(r, 0).start(priority=0)
            row_copy(r, 1).start(priority=1)
            return c

        lax.fori_loop(0, tp, issue, 0, unroll=8)

    @pl.when(i == 0)
    def _():
        gather(dest_ref, 0)

    @pl.when(i + 1 < n)
    def _():
        gather(dest_next_ref, 1 - slot)

    for k in range(2):
        pltpu.make_async_copy(y_ref.at[pl.ds(0, tp * ROW_TILE)], ybuf.at[slot, k], sem.at[slot]).wait()

    route = route_ref[...]
    w_a = route[:, 2:3]
    w_b = route[:, 3:4]
    if tiles_per_g2:
        g2 = g2_ref[0, 0, pl.ds(g2_row0 + i // tiles_per_g2, 1), :]
    else:
        g2 = jnp.concatenate([g2_ref[0, 0]] * (tp // g2_ref.shape[2]), axis=0)
    y = (w_a * _load_row_tiles(ybuf.at[slot, 0]) + w_b * _load_row_tiles(ybuf.at[slot, 1]))
    x2 = x1_ref[...] + g2 * y
    if final_norm:
        x2 = _rms(x2, fng_ref[...])
    o_ref[...] = x2


def _combine(dest_flat, x1, route, mod, layer, fng, y_rows, tp, *, final_norm, g2_row0=0,
             tiles_per_g2=0, seqs_per_tile=0):
    nt = x1.shape[0] // tp
    if tiles_per_g2:
        g2_spec = pl.BlockSpec((1, 1, mod.shape[2], D), lambda i: (layer, 5, 0, 0))
    else:
        g2_spec = pl.BlockSpec((1, 1, seqs_per_tile, D), lambda i: (layer, 5, i, 0))
    return pl.pallas_call(
        functools.partial(_combine_kernel, final_norm=final_norm, g2_row0=g2_row0,
                          tiles_per_g2=tiles_per_g2),
        grid=(nt,),
        in_specs=[pl.BlockSpec((2 * tp,), lambda i: (i,), memory_space=pltpu.SMEM),
                  pl.BlockSpec((2 * tp,), lambda i: (jnp.minimum(i + 1, nt - 1),),
                               memory_space=pltpu.SMEM),
                  pl.BlockSpec((tp, D), lambda i: (i, 0)),
                  pl.BlockSpec((tp, ROUTE_COLS), lambda i: (i, 0)),
                  g2_spec,
                  pl.BlockSpec((1, D), lambda i: (0, 0)),
                  pl.BlockSpec(memory_space=pl.ANY)],
        out_specs=pl.BlockSpec((tp, D), lambda i: (i, 0)),
        out_shape=jax.ShapeDtypeStruct(x1.shape, F32),
        scratch_shapes=[pltpu.VMEM((2, 2, tp * ROW_TILE, LANES), U32),
                        pltpu.SemaphoreType.DMA((2,))],
        compiler_params=pltpu.CompilerParams(
            dimension_semantics=("arbitrary",), vmem_limit_bytes=VMEM_LIMIT),
        name="moe_combine",
    )(dest_flat, dest_flat, x1, route, mod, fng, y_rows)


def _layer_weights(l, p):
    hd = W // HEADS
    gi = jnp.arange(W) // (W // NG_A)
    gavg = jnp.where(gi[:, None] == gi[None, :], 1.0 / (W // NG_A), 0.0).astype(BF16)
    tril = jnp.tril(jnp.ones((CHUNK, CHUNK), dtype=bool))
    w_sp = jnp.where(tril[None], p["b_spatial_w"][l], 0.0)
    w_router = jnp.concatenate(
        [p["w_router_g"][l], p["w_router_e"][l],
         jnp.zeros((D, ROUTE_COLS - NGRP - NEXP), F32)], axis=1)
    b_router = jnp.concatenate(
        [p["b_router_g"][l], p["b_router_e"][l],
         jnp.zeros((ROUTE_COLS - NGRP - NEXP,), F32)])[None, :]
    return {
        "n1g": p["norm1_g"][l][None, :], "n2g": p["norm2_g"][l][None, :],
        "w_in": p["w_in"][l].astype(BF16),
        "a_dw_w": p["a_dw_w"][l], "a_dw_b": p["a_dw_b"][l][None, :],
        "a_norm_g": p["a_norm_g"][l][None, :], "a_norm_b": p["a_norm_b"][l][None, :],
        "b_norm_g": p["b_norm_g"][l][None, :], "b_norm_b": p["b_norm_b"][l][None, :],
        "w_sp": jnp.concatenate([w_sp[0::2], w_sp[1::2]], axis=2).astype(BF16),
        "b_sp": jnp.repeat(p["b_spatial_b"][l].T, hd, axis=1),
        "c_conv_w": p["c_conv_w"][l],
        "w_branch": p["w_branch"][l].astype(BF16), "w_out": p["w_out"][l].astype(BF16),
        "gavg": gavg, "w_router": w_router.astype(BF16), "b_router": b_router,
    }


def _sample_spatial(l, p, ts):
    hd = W // HEADS
    w = p["b_spatial_w"][l][:, :ts, :ts]
    w = jnp.where(jnp.tril(jnp.ones((ts, ts), dtype=bool))[None], w, 0.0)
    w4 = jnp.repeat(jnp.transpose(w, (1, 2, 0)).reshape(ts * ts, HEADS), hd, axis=1)
    b4 = jnp.repeat(p["b_spatial_b"][l][:, :ts].T, hd, axis=1)
    return w4, b4


def kernel(x_prompt, x_sample, c_prompt, c_sample, state_conv_a, state_conv_c, norm1_g, norm2_g, w_ada, b_ada, w_in, a_dw_w, a_dw_b, a_norm_g, a_norm_b, b_norm_g, b_norm_b, b_spatial_w, b_spatial_b, c_conv_w, w_branch, w_out, w_router_g, b_router_g, w_router_e, b_router_e, w1, w3, w2, final_norm_g):
    p = dict(norm1_g=norm1_g, norm2_g=norm2_g, w_in=w_in, a_dw_w=a_dw_w, a_dw_b=a_dw_b,
             a_norm_g=a_norm_g, a_norm_b=a_norm_b, b_norm_g=b_norm_g, b_norm_b=b_norm_b,
             b_spatial_w=b_spatial_w, b_spatial_b=b_spatial_b, c_conv_w=c_conv_w,
             w_branch=w_branch, w_out=w_out, w_router_g=w_router_g, b_router_g=b_router_g,
             w_router_e=w_router_e, b_router_e=b_router_e)
    depth = w_in.shape[0]
    nb_, seq, _ = x_prompt.shape
    ns, ts, _ = x_sample.shape
    n_tok = nb_ * seq + ns * ts
    nblocks = -(-(2 * n_tok) // BM) + NEXP
    nsplit = 2
    nbs = ns // nsplit
    tps = ts * nbs
    assert seq % TT == 0 and TT % CHUNK == 0 and seq % TP == 0 and (nb_ * seq) % TPP == 0
    assert (ts * ns) % TP == 0 and ns % SUBLANES == 0

    mod = _adaln(jnp.concatenate([c_sample, c_prompt], axis=0), w_ada, b_ada)

    xp = x_prompt.reshape(nb_ * seq, D)
    xs = jnp.transpose(x_sample.reshape(nsplit, nbs, ts, D), (0, 2, 1, 3)).reshape(ts * ns, D)
    fng = final_norm_g[None, :]
    pa, pc, sa, sc, sv = [], [], [], [], []
    for l in range(depth):
        lw = _layer_weights(l, p)
        lw["w_sp4"], lw["b_sp4"] = _sample_spatial(l, p, ts)

        x1p, h2p, route_p, ha, hc = _mixer_prompt(xp, mod, l, nb_, ns, lw)
        pa.append(ha)
        pc.append(hc)
        x1s, h2s, route_s, ha_s, hc_s, v_s = _mixer_sample(
            xs, mod, l, jnp.transpose(state_conv_a[l], (1, 0, 2)),
            jnp.transpose(state_conv_c[l], (1, 0, 2)), lw, nsplit)
        sa.append(jnp.transpose(ha_s, (1, 0, 2)))
        sc.append(jnp.transpose(hc_s, (1, 0, 2)))
        sv.append(jnp.transpose(v_s, (1, 0, 2)))

        dest_p, dest_s, tbl = _plan(route_p, route_s)
        dest_p, dest_s = dest_p.reshape(-1), dest_s.reshape(-1)
        rows_buf = _scatter_rows(dest_p, h2p, None, nblocks * BM, TP)
        rows_buf = _scatter_rows(dest_s, h2s, rows_buf, nblocks * BM, tps)
        y_rows = _experts(tbl, rows_buf, w1, w3, w2, l)

        last = l == depth - 1
        xp = _combine(dest_p, x1p, route_p, mod, l, fng, y_rows, TP, final_norm=last,
                      g2_row0=ns, tiles_per_g2=seq // TP)
        xs = _combine(dest_s, x1s, route_s, mod, l, fng, y_rows, tps, final_norm=last,
                      seqs_per_tile=nbs)

    y_prompt = xp.reshape(nb_, seq, D)
    y_sample = jnp.transpose(xs.reshape(nsplit, ts, nbs, D), (0, 2, 1, 3)).reshape(ns, ts, D)
    return (y_prompt, y_sample, jnp.stack(pa), jnp.stack(pc), jnp.stack(sa), jnp.stack(sc),
            jnp.stack(sv))
```

```python
import functools

import jax
import jax.numpy as jnp
from jax import lax
from jax.experimental import pallas as pl
from jax.experimental.pallas import tpu as pltpu

F32 = jnp.float32
BF16 = jnp.bfloat16

D = 1024
W = D // 2
KA = 31
KC = 3
NG_A = 8
CHUNK = 128
HEADS = 8
NGRP = 4
EPG = 8
NEXP = NGRP * EPG
FF = D // 2
EPS = 1e-6
IN_COLS = 7 * W + 3 * D
LANES = 128
ROUTE_COLS = LANES

TT = 512
SUBLANES = 8
A_HDR = 32
A_PAD = SUBLANES
A_ROWS = 64
C_HDR = 8
TP = 512
TPP = 2048
BM = 256
CG = 32
VMEM_LIMIT = 56 * 1024 * 1024

NEG = -1e30


def _sigmoid(x):
    return 1.0 / (1.0 + jnp.exp(-x))


def _silu(x):
    return x * _sigmoid(x)


def _gelu_tanh(x):
    return 0.5 * x * (1.0 + jnp.tanh(0.7978845608028654 * (x + 0.044715 * x * x * x)))


def _rms(x, g):
    return x * lax.rsqrt(jnp.mean(x * x, axis=-1, keepdims=True) + EPS) * g


def _dot(a, b):
    return jnp.dot(a, b, preferred_element_type=F32)


def _dot_hi(a, b):
    return jnp.dot(a, b, preferred_element_type=F32, precision=lax.Precision.HIGHEST)


U32 = jnp.uint32
ROW_TILE = D // (2 * LANES)


def _pack_bf16_pair(lo, hi):
    def rne(x):
        b = pltpu.bitcast(x, U32)
        return b + (U32(0x7FFF) + ((b >> 16) & U32(1)))
    return (rne(lo) >> 16) | (rne(hi) & U32(0xFFFF0000))


def _store_row_tiles(ref, val):
    r = val.shape[0]
    for j in range(ROW_TILE):
        lo = val[:, 2 * j * LANES:(2 * j + 1) * LANES]
        hi = val[:, (2 * j + 1) * LANES:(2 * j + 2) * LANES]
        ref[pl.ds(j, r, stride=ROW_TILE), :] = _pack_bf16_pair(lo, hi)


def _load_row_tiles(ref):
    r = ref.shape[0] // ROW_TILE
    parts = []
    for j in range(ROW_TILE):
        w = ref[pl.ds(j, r, stride=ROW_TILE), :]
        parts.append(pltpu.bitcast(w << 16, F32))
        parts.append(pltpu.bitcast(w & U32(0xFFFF0000), F32))
    return jnp.concatenate(parts, axis=1)


def _row_tile(ref, r):
    return ref.at[pl.ds(pl.multiple_of(r * ROW_TILE, ROW_TILE), ROW_TILE)]


def _mod_kernel(c_ref, w_ref, b_ref, o_ref):
    c = c_ref[...]
    s = _silu(c).astype(BF16)
    o_ref[0, 0] = _dot(s, w_ref[0].astype(BF16)) + b_ref[0]


def _adaln(c_all, w_ada, b_ada):
    nl = w_ada.shape[0]
    r = c_all.shape[0]
    return pl.pallas_call(
        _mod_kernel,
        grid=(nl, 6),
        in_specs=[
            pl.BlockSpec((r, D), lambda l, j: (0, 0)),
            pl.BlockSpec((1, D, D), lambda l, j: (l, 0, j)),
            pl.BlockSpec((1, 1, D), lambda l, j: (l, 0, j)),
        ],
        out_specs=pl.BlockSpec((1, 1, r, D), lambda l, j: (l, j, 0, 0)),
        out_shape=jax.ShapeDtypeStruct((nl, 6, r, D), F32),
        compiler_params=pltpu.CompilerParams(
            dimension_semantics=("arbitrary", "arbitrary"), vmem_limit_bytes=VMEM_LIMIT),
        name="adaln_mod",
    )(c_all, w_ada, b_ada.reshape(nl, 1, 6 * D))


def _group_norm_silu(a, gavg_ref, g, b):
    mu = _dot(a.astype(BF16), gavg_ref[...])
    xc = a - mu
    var = _dot((xc * xc).astype(BF16), gavg_ref[...])
    y = xc * lax.rsqrt(var + EPS) * g + b
    return _silu(y)


def _layer_norm(x, g, b):
    mu = jnp.mean(x, axis=-1, keepdims=True)
    xc = x - mu
    var = jnp.mean(xc * xc, axis=-1, keepdims=True)
    return xc * lax.rsqrt(var + EPS) * g + b


def _merge_and_route(x, a, bb, cc, gates_fn, mod, n2g, wbr_ref, wout_ref, wr_ref, br_ref,
                     x1_ref, h2_ref, route_ref):
    g1, sh2, sc2 = mod
    acc = None
    for gi, br_in in enumerate((a, bb, cc)):
        br = _dot(br_in.astype(BF16), wbr_ref[gi])
        term = _sigmoid(gates_fn(gi)) * br
        acc = term if acc is None else acc + term
    mixed = _dot(acc.astype(BF16), wout_ref[...])
    x1 = x + g1 * mixed
    x1_ref[...] = x1
    h2 = _rms(x1, n2g) * (1.0 + sc2) + sh2
    _store_row_tiles(h2_ref, h2)

    logits = _dot(h2.astype(BF16), wr_ref[...]) + br_ref[...]
    rows = logits.shape[0]
    lane = lax.broadcasted_iota(jnp.int32, (rows, ROUTE_COLS), 1).astype(F32)
    is_g = lane < NGRP
    gl = jnp.where(is_g, logits, NEG)
    gmax = jnp.max(gl, axis=-1, keepdims=True)
    gsel = jnp.min(jnp.where(gl == gmax, lane, float(ROUTE_COLS)), axis=-1, keepdims=True)
    gden = jnp.sum(jnp.where(is_g, jnp.exp(gl - gmax), 0.0), axis=-1, keepdims=True)
    gprob = 1.0 / gden
    lo = NGRP + EPG * gsel
    in_grp = jnp.logical_and(lane >= lo, lane < lo + EPG)
    el = jnp.where(in_grp, logits, NEG)
    v1 = jnp.max(el, axis=-1, keepdims=True)
    i1 = jnp.min(jnp.where(el == v1, lane, float(ROUTE_COLS)), axis=-1, keepdims=True)
    el2 = jnp.where(lane == i1, NEG, el)
    v2 = jnp.max(el2, axis=-1, keepdims=True)
    i2 = jnp.min(jnp.where(el2 == v2, lane, float(ROUTE_COLS)), axis=-1, keepdims=True)
    p2 = jnp.exp(v2 - v1)
    w_a = gprob / (1.0 + p2)
    w_b = gprob * p2 / (1.0 + p2)
    route = jnp.where(lane == 0.0, i1 - NGRP,
                      jnp.where(lane == 1.0, i2 - NGRP,
                                jnp.where(lane == 2.0, w_a,
                                          jnp.where(lane == 3.0, w_b, 0.0))))
    route_ref[...] = route


def _mixer_prompt_kernel(x_ref, mod_ref, n1g_ref, n2g_ref, win_ref, adw_ref, adb_ref, ang_ref,
                         anb_ref, bng_ref, bnb_ref, wsp_ref, bsp_ref, ccw_ref, wbr_ref, wout_ref,
                         gavg_ref, wr_ref, br_ref,
                         x1_ref, h2_ref, route_ref, ha_ref, hc_ref,
                         abuf, cbuf, aconv, *, mod_row0):
    first = pl.program_id(1) == 0
    abuf[0:A_HDR, :] = jnp.where(first, 0.0, abuf[TT:TT + A_HDR, :])
    abuf[A_HDR + TT:A_HDR + TT + A_PAD, :] = jnp.zeros((A_PAD, W), F32)
    cbuf[0:C_HDR, :] = jnp.where(first, 0.0, cbuf[TT:TT + C_HDR, :])

    x = x_ref[...]
    mrow = mod_row0 + pl.program_id(0)
    sh1, sc1, g1, sh2, sc2 = (mod_ref[0, s, pl.ds(mrow, 1), :] for s in range(5))
    h = (_rms(x, n1g_ref[...]) * (1.0 + sc1) + sh1).astype(BF16)

    def zsec(lo, hi):
        return _dot(h, win_ref[:, lo:hi])

    a_glu = zsec(0, W) * _sigmoid(zsec(W, 2 * W))
    abuf[A_HDR:A_HDR + TT, :] = a_glu
    off = A_HDR - (KA - 1)
    rc = A_ROWS

    def conv_pass(p):
        lt, half = p // 2, p % 2
        ls = slice(lt * LANES, (lt + 1) * LANES)
        for r0 in range(half * (TT // 2), (half + 1) * (TT // 2), rc):
            y = None
            for r in range(SUBLANES):
                z = None
                for q in range((KA - 1 + off) // SUBLANES + 1):
                    k = SUBLANES * q + r - off
                    if 0 <= k < KA:
                        rows = slice(r0 + SUBLANES * q, r0 + SUBLANES * q + rc + SUBLANES)
                        term = adw_ref[k:k + 1, ls] * abuf[rows, ls]
                        z = term if z is None else z + term
                part = z[r:r + rc, :]
                y = part if y is None else y + part
            aconv[r0:r0 + rc, ls] = y + adb_ref[:, ls]

    conv_pass(0)
    u = _gelu_tanh(zsec(2 * W, 3 * W))
    conv_pass(1)
    v = _layer_norm(_gelu_tanh(zsec(3 * W, 4 * W)), bng_ref[...], bnb_ref[...])
    conv_pass(2)

    cch = zsec(5 * W, 6 * W) * zsec(6 * W, 7 * W)
    cbuf[C_HDR:C_HDR + TT, :] = cch
    coff = C_HDR - (KC - 1)
    ych = (ccw_ref[0:1, :] * cbuf[coff:coff + TT, :]
           + ccw_ref[1:2, :] * cbuf[coff + 1:coff + 1 + TT, :]
           + ccw_ref[2:3, :] * cbuf[coff + 2:coff + 2 + TT, :])
    conv_pass(3)
    cc = zsec(4 * W, 5 * W) * ych
    hc_ref[0] = cbuf[C_HDR + TT - (KC - 1):C_HDR + TT, :]
    conv_pass(4)
    gate_z = [zsec(7 * W, 7 * W + D)]
    conv_pass(5)
    gate_z.append(zsec(7 * W + D, 7 * W + 2 * D))
    conv_pass(6)
    gate_z.append(zsec(7 * W + 2 * D, 7 * W + 3 * D))
    conv_pass(7)
    ha_ref[0] = abuf[A_HDR + TT - (KA - 1):A_HDR + TT, :]

    vb = v.astype(BF16)
    low_head = lax.broadcasted_iota(jnp.int32, (CHUNK, LANES), 1) < (W // HEADS)
    zero_b = jnp.zeros((CHUNK, LANES), BF16)
    s_parts = []
    for c in range(TT // CHUNK):
        tiles = []
        for lt in range(W // LANES):
            vt = vb[c * CHUNK:(c + 1) * CHUNK, lt * LANES:(lt + 1) * LANES]
            stacked = jnp.concatenate([jnp.where(low_head, vt, zero_b),
                                       jnp.where(low_head, zero_b, vt)], axis=0)
            tiles.append(_dot(wsp_ref[lt], stacked))
        s_parts.append(jnp.concatenate(tiles, axis=1) + bsp_ref[...])
    bb = u * jnp.concatenate(s_parts, axis=0)
    a = _group_norm_silu(aconv[...], gavg_ref, ang_ref[...], anb_ref[...])

    _merge_and_route(x, a, bb, cc, lambda gi: gate_z[gi], (g1, sh2, sc2), n2g_ref[...], wbr_ref,
                     wout_ref, wr_ref, br_ref, x1_ref, h2_ref, route_ref)


def _const_spec(shape):
    nd = len(shape)
    return pl.BlockSpec(shape, lambda *_: (0,) * nd, pipeline_mode=pl.Buffered(1))


def _mixer_prompt(x, mod, layer, n, mod_row0, lw):
    t = x.shape[0] // n
    nt = t // TT
    mod_rows = mod.shape[2]
    row_spec = lambda cols: pl.BlockSpec((TT, cols), lambda i, j: (i * nt + j, 0))
    consts = [lw["n1g"], lw["n2g"], lw["w_in"], lw["a_dw_w"], lw["a_dw_b"], lw["a_norm_g"],
              lw["a_norm_b"], lw["b_norm_g"], lw["b_norm_b"], lw["w_sp"], lw["b_sp"], lw["c_conv_w"],
              lw["w_branch"], lw["w_out"], lw["gavg"], lw["w_router"], lw["b_router"]]
    return pl.pallas_call(
        functools.partial(_mixer_prompt_kernel, mod_row0=mod_row0),
        grid=(n, nt),
        in_specs=[row_spec(D),
                  pl.BlockSpec((1, 6, mod_rows, D), lambda i, j: (layer, 0, 0, 0),
                               pipeline_mode=pl.Buffered(1))]
        + [_const_spec(c.shape) for c in consts],
        out_specs=[row_spec(D),
                   pl.BlockSpec((TT * ROW_TILE, LANES), lambda i, j: (i * nt + j, 0)),
                   row_spec(ROUTE_COLS),
                   pl.BlockSpec((1, KA - 1, W), lambda i, j: (i, 0, 0)),
                   pl.BlockSpec((1, KC - 1, W), lambda i, j: (i, 0, 0))],
        out_shape=[jax.ShapeDtypeStruct((n * t, D), F32),
                   jax.ShapeDtypeStruct((n * t * ROW_TILE, LANES), U32),
                   jax.ShapeDtypeStruct((n * t, ROUTE_COLS), F32),
                   jax.ShapeDtypeStruct((n, KA - 1, W), F32),
                   jax.ShapeDtypeStruct((n, KC - 1, W), F32)],
        scratch_shapes=[pltpu.VMEM((A_HDR + TT + A_PAD, W), F32),
                        pltpu.VMEM((C_HDR + TT, W), F32),
                        pltpu.VMEM((TT, W), F32)],
        compiler_params=pltpu.CompilerParams(
            dimension_semantics=("arbitrary", "arbitrary"), vmem_limit_bytes=VMEM_LIMIT),
        name="mixer_prompt",
    )(x, mod, *consts)


def _mixer_sample_kernel(x_ref, mod_ref, hista_ref, histc_ref, n1g_ref, n2g_ref, win_ref, adw_ref,
                         adb_ref, ang_ref, anb_ref, bng_ref, bnb_ref, wsp4_ref, bsp4_ref,
                         ccw_ref, wbr_ref, wout_ref, gavg_ref, wr_ref, br_ref,
                         x1_ref, h2_ref, route_ref, ha_ref, hc_ref, v_ref):
    ts, nb = v_ref.shape[0], v_ref.shape[1]
    x = x_ref[...]

    def modrow(i):
        return jnp.concatenate([mod_ref[0, i]] * ts, axis=0)

    h = (_rms(x, n1g_ref[...]) * (1.0 + modrow(1)) + modrow(0)).astype(BF16)

    def zsec(lo, hi):
        return _dot(h, win_ref[:, lo:hi])

    def tslab(arr, j):
        return arr[j * nb:(j + 1) * nb, :]

    a_glu = zsec(0, W) * _sigmoid(zsec(W, 2 * W))
    kh = KA - 1
    a_conv = []
    for tq in range(ts):
        acc = jnp.zeros((nb, W), F32) + adb_ref[...]
        for r in range(tq, kh):
            acc = acc + adw_ref[r - tq:r - tq + 1, :] * hista_ref[r]
        for j in range(tq + 1):
            acc = acc + adw_ref[kh + j - tq:kh + j - tq + 1, :] * tslab(a_glu, j)
        a_conv.append(acc)
    a = _group_norm_silu(jnp.concatenate(a_conv, axis=0), gavg_ref, ang_ref[...], anb_ref[...])
    for r in range(kh - ts):
        ha_ref[r] = hista_ref[r + ts]
    for j in range(ts):
        ha_ref[kh - ts + j] = tslab(a_glu, j)

    u = _gelu_tanh(zsec(2 * W, 3 * W))
    v = _layer_norm(_gelu_tanh(zsec(3 * W, 4 * W)), bng_ref[...], bnb_ref[...])
    for j in range(ts):
        v_ref[j] = tslab(v, j)
    s_rows = []
    for tq in range(ts):
        s = jnp.zeros((nb, W), F32) + bsp4_ref[tq:tq + 1, :]
        for sq in range(tq + 1):
            s = s + wsp4_ref[tq * ts + sq:tq * ts + sq + 1, :] * tslab(v, sq)
        s_rows.append(s)
    bb = u * jnp.concatenate(s_rows, axis=0)

    cch = zsec(5 * W, 6 * W) * zsec(6 * W, 7 * W)
    xp = [histc_ref[r] for r in range(KC - 1)] + [tslab(cch, j) for j in range(ts)]
    ych = jnp.concatenate(
        [sum(ccw_ref[k:k + 1, :] * xp[tq + k] for k in range(KC)) for tq in range(ts)], axis=0)
    cc = zsec(4 * W, 5 * W) * ych
    for r in range(KC - 1):
        hc_ref[r] = xp[ts + r]

    def gates_fn(gi):
        return zsec(7 * W + gi * D, 7 * W + (gi + 1) * D)

    _merge_and_route(x, a, bb, cc, gates_fn, (modrow(2), modrow(3), modrow(4)), n2g_ref[...],
                     wbr_ref, wout_ref, wr_ref, br_ref, x1_ref, h2_ref, route_ref)


def _mixer_sample(x_rows, mod, layer, hist_a_tm, hist_c_tm, lw, nsplit):
    n = hist_a_tm.shape[1]
    ts = x_rows.shape[0] // n
    nb = n // nsplit
    rows = ts * nb
    consts = [lw["n1g"], lw["n2g"], lw["w_in"], lw["a_dw_w"], lw["a_dw_b"],
              lw["a_norm_g"], lw["a_norm_b"], lw["b_norm_g"], lw["b_norm_b"], lw["w_sp4"], lw["b_sp4"],
              lw["c_conv_w"], lw["w_branch"], lw["w_out"], lw["gavg"], lw["w_router"], lw["b_router"]]
    seq3 = lambda k, cols: pl.BlockSpec((k, nb, cols), lambda i: (0, i, 0))
    row_spec = lambda cols: pl.BlockSpec((rows, cols), lambda i: (i, 0))
    return pl.pallas_call(
        _mixer_sample_kernel,
        grid=(nsplit,),
        in_specs=[row_spec(D), pl.BlockSpec((1, 6, nb, D), lambda i: (layer, 0, i, 0)),
                  seq3(KA - 1, W), seq3(KC - 1, W)]
        + [_const_spec(c.shape) for c in consts],
        out_specs=[row_spec(D),
                   pl.BlockSpec((rows * ROW_TILE, LANES), lambda i: (i, 0)),
                   row_spec(ROUTE_COLS),
                   seq3(KA - 1, W), seq3(KC - 1, W), seq3(ts, W)],
        out_shape=[jax.ShapeDtypeStruct((ts * n, D), F32),
                   jax.ShapeDtypeStruct((ts * n * ROW_TILE, LANES), U32),
                   jax.ShapeDtypeStruct((ts * n, ROUTE_COLS), F32),
                   jax.ShapeDtypeStruct((KA - 1, n, W), F32),
                   jax.ShapeDtypeStruct((KC - 1, n, W), F32),
                   jax.ShapeDtypeStruct((ts, n, W), F32)],
        compiler_params=pltpu.CompilerParams(
            dimension_semantics=("arbitrary",), vmem_limit_bytes=VMEM_LIMIT),
        name="mixer_sample",
    )(x_rows, mod, hist_a_tm, hist_c_tm, *consts)


def _plan_kernel(rp_ref, rs_ref, tri_ref, destp_ref, dests_ref, tbl_ref, carry, pstart, *, ntp):
    ph = pl.program_id(0)
    i = pl.program_id(1)
    nt = pl.num_programs(1)
    lane = lax.broadcasted_iota(jnp.int32, (TP, ROUTE_COLS), 1).astype(F32)

    @pl.when(jnp.logical_and(ph == 0, i == 0))
    def _():
        carry[...] = jnp.zeros_like(carry)

    def sub_tile(r, dest_ref, row0):
        hot_a = lane == r[:, 0:1]
        hot_b = lane == r[:, 1:2]
        s = jnp.where(jnp.logical_or(hot_a, hot_b), 1.0, 0.0)

        @pl.when(ph == 1)
        def _():
            pre = _dot(tri_ref[...], s.astype(BF16)) + carry[...] + pstart[...]
            d_a = jnp.sum(jnp.where(hot_a, pre, 0.0), axis=-1, keepdims=True)
            d_b = jnp.sum(jnp.where(hot_b, pre, 0.0), axis=-1, keepdims=True)
            l2 = lax.broadcasted_iota(jnp.int32, (TP, 2), 1)
            dest_ref[row0:row0 + TP, :] = jnp.where(l2 == 0, d_a, d_b).astype(jnp.int32)

        carry[...] = carry[...] + jnp.sum(s, axis=0, keepdims=True)

    @pl.when(i < ntp)
    def _():
        for sub in range(rp_ref.shape[0] // TP):
            sub_tile(rp_ref[sub * TP:(sub + 1) * TP, :], destp_ref, sub * TP)

    @pl.when(i == ntp)
    def _():
        for sub in range(rs_ref.shape[0] // TP):
            sub_tile(rs_ref[sub * TP:(sub + 1) * TP, :], dests_ref, sub * TP)

    @pl.when(jnp.logical_and(ph == 0, i == nt - 1))
    def _():
        cnt = carry[...]
        nblk = jnp.floor((cnt + (BM - 1)) * (1.0 / BM))
        ri = lax.broadcasted_iota(jnp.int32, (ROUTE_COLS, ROUTE_COLS), 0)
        ci = lax.broadcasted_iota(jnp.int32, (ROUTE_COLS, ROUTE_COLS), 1)
        upper = jnp.where(ri <= ci, 1.0, 0.0)
        cum = _dot_hi(jnp.broadcast_to(nblk, (8, ROUTE_COLS)), upper)[0:1, :]
        cum_ex = cum - nblk
        pstart[...] = cum_ex * BM
        row = lax.broadcasted_iota(jnp.int32, (SUBLANES, ROUTE_COLS), 0)
        tbl = jnp.where(row == 0, cum_ex, jnp.where(row == 1, cnt, 0.0))
        tbl_ref[...] = tbl.astype(jnp.int32)
        carry[...] = jnp.zeros_like(carry)


def _plan(route_p, route_s):
    rows_p, rows_s = route_p.shape[0], route_s.shape[0]
    ntp = rows_p // TPP
    nbp = SUBLANES
    ri = lax.broadcasted_iota(jnp.int32, (TP, TP), 0)
    ci = lax.broadcasted_iota(jnp.int32, (TP, TP), 1)
    tri = jnp.where(ci < ri, 1.0, 0.0).astype(BF16)
    return pl.pallas_call(
        functools.partial(_plan_kernel, ntp=ntp),
        grid=(2, ntp + 1),
        in_specs=[pl.BlockSpec((TPP, ROUTE_COLS), lambda p, i: (jnp.minimum(i, ntp - 1), 0)),
                  pl.BlockSpec((rows_s, ROUTE_COLS), lambda p, i: (0, 0)),
                  pl.BlockSpec((TP, TP), lambda p, i: (0, 0))],
        out_specs=[pl.BlockSpec((TPP, 2), lambda p, i: (jnp.minimum(i, ntp - 1) * p, 0)),
                   pl.BlockSpec((rows_s, 2), lambda p, i: (0, 0)),
                   pl.BlockSpec((nbp, ROUTE_COLS), lambda p, i: (0, 0))],
        out_shape=[jax.ShapeDtypeStruct((rows_p, 2), jnp.int32),
                   jax.ShapeDtypeStruct((rows_s, 2), jnp.int32),
                   jax.ShapeDtypeStruct((nbp, ROUTE_COLS), jnp.int32)],
        scratch_shapes=[pltpu.VMEM((1, ROUTE_COLS), F32), pltpu.VMEM((1, ROUTE_COLS), F32)],
        compiler_params=pltpu.CompilerParams(
            dimension_semantics=("arbitrary", "arbitrary"), vmem_limit_bytes=VMEM_LIMIT),
        name="moe_plan",
    )(route_p, route_s, tri)


def _scatter_kernel(dest_ref, h_ref, *rest):
    rows_ref, sem = rest[-2:]
    tp = h_ref.shape[0] // ROW_TILE

    def row_copy(r, k):
        d = dest_ref[2 * r + k]
        return pltpu.make_async_copy(_row_tile(h_ref, r), _row_tile(rows_ref, d), sem)

    def issue(r, c):
        row_copy(r, 0).start(priority=0)
        row_copy(r, 1).start(priority=1)
        return c

    lax.fori_loop(0, tp, issue, 0, unroll=8)

    def drain(r, c):
        row_copy(r, 0).wait()
        row_copy(r, 1).wait()
        return c

    lax.fori_loop(0, tp, drain, 0, unroll=8)


def _scatter_rows(dest_flat, h2, rows_buf, n_rows, tp):
    nt = h2.shape[0] // (tp * ROW_TILE)
    in_specs = [pl.BlockSpec((2 * tp,), lambda i: (i,), memory_space=pltpu.SMEM),
                pl.BlockSpec((tp * ROW_TILE, LANES), lambda i: (i, 0))]
    args = [dest_flat, h2]
    aliases = {}
    if rows_buf is not None:
        in_specs.append(pl.BlockSpec(memory_space=pl.ANY))
        args.append(rows_buf)
        aliases = {2: 0}
    return pl.pallas_call(
        _scatter_kernel,
        grid=(nt,),
        in_specs=in_specs,
        out_specs=pl.BlockSpec(memory_space=pl.ANY),
        out_shape=jax.ShapeDtypeStruct((n_rows * ROW_TILE, LANES), U32),
        scratch_shapes=[pltpu.SemaphoreType.DMA(())],
        input_output_aliases=aliases,
        compiler_params=pltpu.CompilerParams(
            dimension_semantics=("arbitrary",), vmem_limit_bytes=VMEM_LIMIT,
            has_side_effects=True),
        name="moe_scatter",
    )(*args)


ROW_DMA_PRIORITY = 1
ROW_SLOTS = 3


def _expert_kernel(blk0_ref, cnt_ref, x_hbm, w1_ref, w3_ref, w2_ref, y_hbm,
                   xbuf, ybuf, w1b, w3b, w2b, sem_in, sem_out):
    e = pl.program_id(0)
    cnt = cnt_ref[e]
    nblk = lax.shift_right_logical(cnt + (BM - 1), BM.bit_length() - 1)
    blk0 = blk0_ref[e]
    last_e = NEXP - 1
    n_used = blk0_ref[last_e] + lax.shift_right_logical(cnt_ref[last_e] + (BM - 1),
                                                        BM.bit_length() - 1)
    blk_rows = BM * ROW_TILE

    def block_rows(ref, g):
        return ref.at[pl.ds(pl.multiple_of(g * blk_rows, blk_rows), blk_rows)]

    def in_copy(g):
        slot = lax.rem(g, ROW_SLOTS)
        return pltpu.make_async_copy(block_rows(x_hbm, g), xbuf.at[slot], sem_in.at[slot])

    def out_copy(g):
        slot = lax.rem(g, ROW_SLOTS)
        return pltpu.make_async_copy(ybuf.at[slot], block_rows(y_hbm, g), sem_out.at[slot])

    @pl.when(nblk > 0)
    def _():
        @pl.when(blk0 == 0)
        def _():
            for g0 in range(ROW_SLOTS - 1):
                @pl.when(g0 < n_used)
                def _():
                    in_copy(g0).start(priority=ROW_DMA_PRIORITY)

        w1b[...] = w1_ref[0, 0].astype(BF16)
        w3b[...] = w3_ref[0, 0].astype(BF16)
        w2b[...] = w2_ref[0, 0].astype(BF16)

        def body(j, carry):
            g = blk0 + j
            slot = lax.rem(g, ROW_SLOTS)

            @pl.when(g + (ROW_SLOTS - 1) < n_used)
            def _():
                in_copy(g + (ROW_SLOTS - 1)).start(priority=ROW_DMA_PRIORITY)

            in_copy(g).wait()

            @pl.when(g >= ROW_SLOTS)
            def _():
                out_copy(g - ROW_SLOTS).wait()

            row = lax.broadcasted_iota(jnp.int32, (BM, D), 0)
            x = jnp.where(row < cnt - j * BM, _load_row_tiles(xbuf.at[slot]), 0.0).astype(BF16)
            h1 = _dot(x, w1b[...])
            h3 = _dot(x, w3b[...])
            act = (_silu(h1) * h3).astype(BF16)
            _store_row_tiles(ybuf.at[slot], _dot(act, w2b[...]))
            out_copy(g).start(priority=ROW_DMA_PRIORITY)
            return carry

        lax.fori_loop(0, nblk, body, 0)

    @pl.when(e == last_e)
    def _():
        for back in range(ROW_SLOTS, 0, -1):
            @pl.when(n_used >= back)
            def _():
                out_copy(n_used - back).wait()


def _experts(tbl, rows_buf, w1, w3, w2, layer):
    blk0, cnt = tbl[0, :NEXP], tbl[1, :NEXP]
    wspec = lambda shape: pl.BlockSpec((1, 1) + shape, lambda e, blk0, cnt: (layer, e, 0, 0))
    return pl.pallas_call(
        _expert_kernel,
        grid_spec=pltpu.PrefetchScalarGridSpec(
            num_scalar_prefetch=2,
            grid=(NEXP,),
            in_specs=[pl.BlockSpec(memory_space=pl.ANY),
                      wspec((D, FF)), wspec((D, FF)), wspec((FF, D))],
            out_specs=pl.BlockSpec(memory_space=pl.ANY),
            scratch_shapes=[pltpu.VMEM((ROW_SLOTS, BM * ROW_TILE, LANES), U32),
                            pltpu.VMEM((ROW_SLOTS, BM * ROW_TILE, LANES), U32),
                            pltpu.VMEM((D, FF), BF16), pltpu.VMEM((D, FF), BF16),
                            pltpu.VMEM((FF, D), BF16),
                            pltpu.SemaphoreType.DMA((ROW_SLOTS,)),
                            pltpu.SemaphoreType.DMA((ROW_SLOTS,))],
        ),
        out_shape=jax.ShapeDtypeStruct(rows_buf.shape, U32),
        compiler_params=pltpu.CompilerParams(
            dimension_semantics=("arbitrary",), vmem_limit_bytes=VMEM_LIMIT,
            has_side_effects=True),
        name="moe_experts",
    )(blk0, cnt, rows_buf, w1, w3, w2)


def _combine_kernel(dest_ref, dest_next_ref, x1_ref, route_ref, g2_ref, fng_ref, y_ref, o_ref,
                    ybuf, sem, *, final_norm, g2_row0, tiles_per_g2):
    tp = x1_ref.shape[0]
    i = pl.program_id(0)
    n = pl.num_programs(0)
    slot = lax.rem(i, 2)

    def start_rows(d_ref, s, r0, count):
        for q in range(count):
            for k in range(2):
                pltpu.make_async_copy(_row_tile(y_ref, d_ref[2 * (r0 + q) + k]),
                                      _row_tile(ybuf.at[s, k], r0 + q), sem.at[s]).start(priority=k)

    @pl.when(i == 0)
    def _():
        def issue(r, c):
            start_rows(dest_ref, 0, r, 1)
            return c

        lax.fori_loop(0, tp, issue, 0, unroll=8)

    for k in range(2):
        pltpu.make_async_copy(y_ref.at[pl.ds(0, tp * ROW_TILE)], ybuf.at[slot, k], sem.at[slot]).wait()

    def row_group(g, prefetch):
        r0 = pl.multiple_of(g * CG, CG)
        if prefetch:
            start_rows(dest_next_ref, 1 - slot, r0, CG)
        rows = pl.ds(r0, CG)
        route = route_ref[rows, :]
        if tiles_per_g2:
            g2 = g2_ref[0, 0, pl.ds(g2_row0 + i // tiles_per_g2, 1), :]
        else:
            g2 = g2_ref[0, 0, pl.ds(pl.multiple_of(lax.rem(r0, g2_ref.shape[2]), CG), CG), :]
        tiles = pl.ds(pl.multiple_of(r0 * ROW_TILE, CG * ROW_TILE), CG * ROW_TILE)
        y = (route[:, 2:3] * _load_row_tiles(ybuf.at[slot, 0, tiles])
             + route[:, 3:4] * _load_row_tiles(ybuf.at[slot, 1, tiles]))
        x2 = x1_ref[rows, :] + g2 * y
        if final_norm:
            x2 = _rms(x2, fng_ref[...])
        o_ref[rows, :] = x2

    @pl.when(i + 1 < n)
    def _():
        lax.fori_loop(0, tp // CG, lambda g, c: (row_group(g, True), c)[1], 0)

    @pl.when(i + 1 >= n)
    def _():
        lax.fori_loop(0, tp // CG, lambda g, c: (row_group(g, False), c)[1], 0)


def _combine(dest_flat, x1, route, mod, layer, fng, y_rows, tp, *, final_norm, g2_row0=0,
             tiles_per_g2=0, seqs_per_tile=0):
    nt = x1.shape[0] // tp
    if tiles_per_g2:
        g2_spec = pl.BlockSpec((1, 1, mod.shape[2], D), lambda i: (layer, 5, 0, 0))
    else:
        g2_spec = pl.BlockSpec((1, 1, seqs_per_tile, D), lambda i: (layer, 5, i, 0))
    return pl.pallas_call(
        functools.partial(_combine_kernel, final_norm=final_norm, g2_row0=g2_row0,
                          tiles_per_g2=tiles_per_g2),
        grid=(nt,),
        in_specs=[pl.BlockSpec((2 * tp,), lambda i: (i,), memory_space=pltpu.SMEM),
                  pl.BlockSpec((2 * tp,), lambda i: (jnp.minimum(i + 1, nt - 1),),
                               memory_space=pltpu.SMEM),
                  pl.BlockSpec((tp, D), lambda i: (i, 0)),
                  pl.BlockSpec((tp, ROUTE_COLS), lambda i: (i, 0)),
                  g2_spec,
                  pl.BlockSpec((1, D), lambda i: (0, 0)),
                  pl.BlockSpec(memory_space=pl.ANY)],
        out_specs=pl.BlockSpec((tp, D), lambda i: (i, 0)),
        out_shape=jax.ShapeDtypeStruct(x1.shape, F32),
        scratch_shapes=[pltpu.VMEM((2, 2, tp * ROW_TILE, LANES), U32),
                        pltpu.SemaphoreType.DMA((2,))],
        compiler_params=pltpu.CompilerParams(
            dimension_semantics=("arbitrary",), vmem_limit_bytes=VMEM_LIMIT),
        name="moe_combine",
    )(dest_flat, dest_flat, x1, route, mod, fng, y_rows)


def _layer_weights(l, p):
    hd = W // HEADS
    gi = jnp.arange(W) // (W // NG_A)
    gavg = jnp.where(gi[:, None] == gi[None, :], 1.0 / (W // NG_A), 0.0).astype(BF16)
    tril = jnp.tril(jnp.ones((CHUNK, CHUNK), dtype=bool))
    w_sp = jnp.where(tril[None], p["b_spatial_w"][l], 0.0)
    w_router = jnp.concatenate(
        [p["w_router_g"][l], p["w_router_e"][l],
         jnp.zeros((D, ROUTE_COLS - NGRP - NEXP), F32)], axis=1)
    b_router = jnp.concatenate(
        [p["b_router_g"][l], p["b_router_e"][l],
         jnp.zeros((ROUTE_COLS - NGRP - NEXP,), F32)])[None, :]
    return {
        "n1g": p["norm1_g"][l][None, :], "n2g": p["norm2_g"][l][None, :],
        "w_in": p["w_in"][l].astype(BF16),
        "a_dw_w": p["a_dw_w"][l], "a_dw_b": p["a_dw_b"][l][None, :],
        "a_norm_g": p["a_norm_g"][l][None, :], "a_norm_b": p["a_norm_b"][l][None, :],
        "b_norm_g": p["b_norm_g"][l][None, :], "b_norm_b": p["b_norm_b"][l][None, :],
        "w_sp": jnp.concatenate([w_sp[0::2], w_sp[1::2]], axis=2).astype(BF16),
        "b_sp": jnp.repeat(p["b_spatial_b"][l].T, hd, axis=1),
        "c_conv_w": p["c_conv_w"][l],
        "w_branch": p["w_branch"][l].astype(BF16), "w_out": p["w_out"][l].astype(BF16),
        "gavg": gavg, "w_router": w_router.astype(BF16), "b_router": b_router,
    }


def _sample_spatial(l, p, ts):
    hd = W // HEADS
    w = p["b_spatial_w"][l][:, :ts, :ts]
    w = jnp.where(jnp.tril(jnp.ones((ts, ts), dtype=bool))[None], w, 0.0)
    w4 = jnp.repeat(jnp.transpose(w, (1, 2, 0)).reshape(ts * ts, HEADS), hd, axis=1)
    b4 = jnp.repeat(p["b_spatial_b"][l][:, :ts].T, hd, axis=1)
    return w4, b4


def kernel(x_prompt, x_sample, c_prompt, c_sample, state_conv_a, state_conv_c, norm1_g, norm2_g, w_ada, b_ada, w_in, a_dw_w, a_dw_b, a_norm_g, a_norm_b, b_norm_g, b_norm_b, b_spatial_w, b_spatial_b, c_conv_w, w_branch, w_out, w_router_g, b_router_g, w_router_e, b_router_e, w1, w3, w2, final_norm_g):
    p = dict(norm1_g=norm1_g, norm2_g=norm2_g, w_in=w_in, a_dw_w=a_dw_w, a_dw_b=a_dw_b,
             a_norm_g=a_norm_g, a_norm_b=a_norm_b, b_norm_g=b_norm_g, b_norm_b=b_norm_b,
             b_spatial_w=b_spatial_w, b_spatial_b=b_spatial_b, c_conv_w=c_conv_w,
             w_branch=w_branch, w_out=w_out, w_router_g=w_router_g, b_router_g=b_router_g,
             w_router_e=w_router_e, b_router_e=b_router_e)
    depth = w_in.shape[0]
    nb_, seq, _ = x_prompt.shape
    ns, ts, _ = x_sample.shape
    n_tok = nb_ * seq + ns * ts
    nblocks = -(-(2 * n_tok) // BM) + NEXP
    nsplit = 2
    nbs = ns // nsplit
    tps = ts * nbs
    assert seq % TT == 0 and TT % CHUNK == 0 and seq % TP == 0 and (nb_ * seq) % TPP == 0
    assert (ts * ns) % TP == 0 and ns % SUBLANES == 0

    mod = _adaln(jnp.concatenate([c_sample, c_prompt], axis=0), w_ada, b_ada)

    xp = x_prompt.reshape(nb_ * seq, D)
    xs = jnp.transpose(x_sample.reshape(nsplit, nbs, ts, D), (0, 2, 1, 3)).reshape(ts * ns, D)
    fng = final_norm_g[None, :]
    pa, pc, sa, sc, sv = [], [], [], [], []
    for l in range(depth):
        lw = _layer_weights(l, p)
        lw["w_sp4"], lw["b_sp4"] = _sample_spatial(l, p, ts)

        x1p, h2p, route_p, ha, hc = _mixer_prompt(xp, mod, l, nb_, ns, lw)
        pa.append(ha)
        pc.append(hc)
        x1s, h2s, route_s, ha_s, hc_s, v_s = _mixer_sample(
            xs, mod, l, jnp.transpose(state_conv_a[l], (1, 0, 2)),
            jnp.transpose(state_conv_c[l], (1, 0, 2)), lw, nsplit)
        sa.append(jnp.transpose(ha_s, (1, 0, 2)))
        sc.append(jnp.transpose(hc_s, (1, 0, 2)))
        sv.append(jnp.transpose(v_s, (1, 0, 2)))

        dest_p, dest_s, tbl = _plan(route_p, route_s)
        dest_p, dest_s = dest_p.reshape(-1), dest_s.reshape(-1)
        rows_buf = _scatter_rows(dest_p, h2p, None, nblocks * BM, TP)
        rows_buf = _scatter_rows(dest_s, h2s, rows_buf, nblocks * BM, tps)
        y_rows = _experts(tbl, rows_buf, w1, w3, w2, l)

        last = l == depth - 1
        xp = _combine(dest_p, x1p, route_p, mod, l, fng, y_rows, TP, final_norm=last,
                      g2_row0=ns, tiles_per_g2=seq // TP)
        xs = _combine(dest_s, x1s, route_s, mod, l, fng, y_rows, tps, final_norm=last,
                      seqs_per_tile=nbs)

    y_prompt = xp.reshape(nb_, seq, D)
    y_sample = jnp.transpose(xs.reshape(nsplit, ts, nbs, D), (0, 2, 1, 3)).reshape(ns, ts, D)
    return (y_prompt, y_sample, jnp.stack(pa), jnp.stack(pc), jnp.stack(sa), jnp.stack(sc),
            jnp.stack(sv))
```

```python
import functools

import jax
import jax.numpy as jnp
from jax import lax
from jax.experimental import pallas as pl
from jax.experimental.pallas import tpu as pltpu

F32 = jnp.float32
BF16 = jnp.bfloat16

D = 1024
W = D // 2
KA = 31
KC = 3
NG_A = 8
CHUNK = 128
HEADS = 8
NGRP = 4
EPG = 8
NEXP = NGRP * EPG
FF = D // 2
EPS = 1e-6
IN_COLS = 7 * W + 3 * D
LANES = 128
ROUTE_COLS = LANES

TT = 512
SUBLANES = 8
A_HDR = 32
A_PAD = SUBLANES
A_ROWS = 64
C_HDR = 8
TP = 512
TPP = 2048
BM = 256
VMEM_LIMIT = 56 * 1024 * 1024

NEG = -1e30


def _sigmoid(x):
    return 1.0 / (1.0 + jnp.exp(-x))


def _silu(x):
    return x * _sigmoid(x)


def _gelu_tanh(x):
    return 0.5 * x * (1.0 + jnp.tanh(0.7978845608028654 * (x + 0.044715 * x * x * x)))


def _rms(x, g):
    return x * lax.rsqrt(jnp.mean(x * x, axis=-1, keepdims=True) + EPS) * g


def _dot(a, b):
    return jnp.dot(a, b, preferred_element_type=F32)


def _dot_hi(a, b):
    return jnp.dot(a, b, preferred_element_type=F32, precision=lax.Precision.HIGHEST)


U32 = jnp.uint32
ROW_TILE = D // (2 * LANES)


def _pack_bf16_pair(lo, hi):
    def rne(x):
        b = pltpu.bitcast(x, U32)
        return b + (U32(0x7FFF) + ((b >> 16) & U32(1)))
    return (rne(lo) >> 16) | (rne(hi) & U32(0xFFFF0000))


def _store_row_tiles(ref, val):
    r = val.shape[0]
    for j in range(ROW_TILE):
        lo = val[:, 2 * j * LANES:(2 * j + 1) * LANES]
        hi = val[:, (2 * j + 1) * LANES:(2 * j + 2) * LANES]
        ref[pl.ds(j, r, stride=ROW_TILE), :] = _pack_bf16_pair(lo, hi)


def _load_row_tiles(ref):
    r = ref.shape[0] // ROW_TILE
    parts = []
    for j in range(ROW_TILE):
        w = ref[pl.ds(j, r, stride=ROW_TILE), :]
        parts.append(pltpu.bitcast(w << 16, F32))
        parts.append(pltpu.bitcast(w & U32(0xFFFF0000), F32))
    return jnp.concatenate(parts, axis=1)


def _row_tile(ref, r):
    return ref.at[pl.ds(pl.multiple_of(r * ROW_TILE, ROW_TILE), ROW_TILE)]


def _mod_kernel(c_ref, w_ref, b_ref, o_ref):
    c = c_ref[...]
    s = _silu(c).astype(BF16)
    o_ref[0, 0] = _dot(s, w_ref[0].astype(BF16)) + b_ref[0]


def _adaln(c_all, w_ada, b_ada):
    nl = w_ada.shape[0]
    r = c_all.shape[0]
    return pl.pallas_call(
        _mod_kernel,
        grid=(nl, 6),
        in_specs=[
            pl.BlockSpec((r, D), lambda l, j: (0, 0)),
            pl.BlockSpec((1, D, D), lambda l, j: (l, 0, j)),
            pl.BlockSpec((1, 1, D), lambda l, j: (l, 0, j)),
        ],
        out_specs=pl.BlockSpec((1, 1, r, D), lambda l, j: (l, j, 0, 0)),
        out_shape=jax.ShapeDtypeStruct((nl, 6, r, D), F32),
        compiler_params=pltpu.CompilerParams(
            dimension_semantics=("arbitrary", "arbitrary"), vmem_limit_bytes=VMEM_LIMIT),
        name="adaln_mod",
    )(c_all, w_ada, b_ada.reshape(nl, 1, 6 * D))


def _group_norm_silu(a, gavg_ref, g, b):
    mu = _dot(a.astype(BF16), gavg_ref[...])
    xc = a - mu
    var = _dot((xc * xc).astype(BF16), gavg_ref[...])
    y = xc * lax.rsqrt(var + EPS) * g + b
    return _silu(y)


def _layer_norm(x, g, b):
    mu = jnp.mean(x, axis=-1, keepdims=True)
    xc = x - mu
    var = jnp.mean(xc * xc, axis=-1, keepdims=True)
    return xc * lax.rsqrt(var + EPS) * g + b


def _merge_and_route(x, a, bb, cc, gates_fn, mod, n2g, wbr_ref, wout_ref, wr_ref, br_ref,
                     x1_ref, h2_ref, route_ref):
    g1, sh2, sc2 = mod
    acc = None
    for gi, br_in in enumerate((a, bb, cc)):
        br = _dot(br_in.astype(BF16), wbr_ref[gi])
        term = _sigmoid(gates_fn(gi)) * br
        acc = term if acc is None else acc + term
    mixed = _dot(acc.astype(BF16), wout_ref[...])
    x1 = x + g1 * mixed
    x1_ref[...] = x1
    h2 = _rms(x1, n2g) * (1.0 + sc2) + sh2
    _store_row_tiles(h2_ref, h2)

    logits = _dot(h2.astype(BF16), wr_ref[...]) + br_ref[...]
    rows = logits.shape[0]
    lane = lax.broadcasted_iota(jnp.int32, (rows, ROUTE_COLS), 1).astype(F32)
    is_g = lane < NGRP
    gl = jnp.where(is_g, logits, NEG)
    gmax = jnp.max(gl, axis=-1, keepdims=True)
    gsel = jnp.min(jnp.where(gl == gmax, lane, float(ROUTE_COLS)), axis=-1, keepdims=True)
    gden = jnp.sum(jnp.where(is_g, jnp.exp(gl - gmax), 0.0), axis=-1, keepdims=True)
    gprob = 1.0 / gden
    lo = NGRP + EPG * gsel
    in_grp = jnp.logical_and(lane >= lo, lane < lo + EPG)
    el = jnp.where(in_grp, logits, NEG)
    v1 = jnp.max(el, axis=-1, keepdims=True)
    i1 = jnp.min(jnp.where(el == v1, lane, float(ROUTE_COLS)), axis=-1, keepdims=True)
    el2 = jnp.where(lane == i1, NEG, el)
    v2 = jnp.max(el2, axis=-1, keepdims=True)
    i2 = jnp.min(jnp.where(el2 == v2, lane, float(ROUTE_COLS)), axis=-1, keepdims=True)
    p2 = jnp.exp(v2 - v1)
    w_a = gprob / (1.0 + p2)
    w_b = gprob * p2 / (1.0 + p2)
    route = jnp.where(lane == 0.0, i1 - NGRP,
                      jnp.where(lane == 1.0, i2 - NGRP,
                                jnp.where(lane == 2.0, w_a,
                                          jnp.where(lane == 3.0, w_b, 0.0))))
    route_ref[...] = route


def _mixer_prompt_kernel(x_ref, mod_ref, n1g_ref, n2g_ref, win_ref, adw_ref, adb_ref, ang_ref,
                         anb_ref, bng_ref, bnb_ref, wsp_ref, bsp_ref, ccw_ref, wbr_ref, wout_ref,
                         gavg_ref, wr_ref, br_ref,
                         x1_ref, h2_ref, route_ref, ha_ref, hc_ref,
                         abuf, cbuf, aconv, *, mod_row0):
    first = pl.program_id(1) == 0
    abuf[0:A_HDR, :] = jnp.where(first, 0.0, abuf[TT:TT + A_HDR, :])
    abuf[A_HDR + TT:A_HDR + TT + A_PAD, :] = jnp.zeros((A_PAD, W), F32)
    cbuf[0:C_HDR, :] = jnp.where(first, 0.0, cbuf[TT:TT + C_HDR, :])

    x = x_ref[...]
    mrow = mod_row0 + pl.program_id(0)
    sh1, sc1, g1, sh2, sc2 = (mod_ref[0, s, pl.ds(mrow, 1), :] for s in range(5))
    h = (_rms(x, n1g_ref[...]) * (1.0 + sc1) + sh1).astype(BF16)

    def zsec(lo, hi):
        return _dot(h, win_ref[:, lo:hi])

    a_glu = zsec(0, W) * _sigmoid(zsec(W, 2 * W))
    abuf[A_HDR:A_HDR + TT, :] = a_glu
    off = A_HDR - (KA - 1)
    rc = A_ROWS

    def conv_pass(p):
        lt, half = p // 2, p % 2
        ls = slice(lt * LANES, (lt + 1) * LANES)
        for r0 in range(half * (TT // 2), (half + 1) * (TT // 2), rc):
            y = None
            for r in range(SUBLANES):
                z = None
                for q in range((KA - 1 + off) // SUBLANES + 1):
                    k = SUBLANES * q + r - off
                    if 0 <= k < KA:
                        rows = slice(r0 + SUBLANES * q, r0 + SUBLANES * q + rc + SUBLANES)
                        term = adw_ref[k:k + 1, ls] * abuf[rows, ls]
                        z = term if z is None else z + term
                part = z[r:r + rc, :]
                y = part if y is None else y + part
            aconv[r0:r0 + rc, ls] = y + adb_ref[:, ls]

    conv_pass(0)
    u = _gelu_tanh(zsec(2 * W, 3 * W))
    conv_pass(1)
    v = _layer_norm(_gelu_tanh(zsec(3 * W, 4 * W)), bng_ref[...], bnb_ref[...])
    conv_pass(2)

    cch = zsec(5 * W, 6 * W) * zsec(6 * W, 7 * W)
    cbuf[C_HDR:C_HDR + TT, :] = cch
    coff = C_HDR - (KC - 1)
    ych = (ccw_ref[0:1, :] * cbuf[coff:coff + TT, :]
           + ccw_ref[1:2, :] * cbuf[coff + 1:coff + 1 + TT, :]
           + ccw_ref[2:3, :] * cbuf[coff + 2:coff + 2 + TT, :])
    conv_pass(3)
    cc = zsec(4 * W, 5 * W) * ych
    hc_ref[0] = cbuf[C_HDR + TT - (KC - 1):C_HDR + TT, :]
    conv_pass(4)
    gate_z = [zsec(7 * W, 7 * W + D)]
    conv_pass(5)
    gate_z.append(zsec(7 * W + D, 7 * W + 2 * D))
    conv_pass(6)
    gate_z.append(zsec(7 * W + 2 * D, 7 * W + 3 * D))
    conv_pass(7)
    ha_ref[0] = abuf[A_HDR + TT - (KA - 1):A_HDR + TT, :]

    vb = v.astype(BF16)
    low_head = lax.broadcasted_iota(jnp.int32, (CHUNK, LANES), 1) < (W // HEADS)
    zero_b = jnp.zeros((CHUNK, LANES), BF16)
    s_parts = []
    for c in range(TT // CHUNK):
        tiles = []
        for lt in range(W // LANES):
            vt = vb[c * CHUNK:(c + 1) * CHUNK, lt * LANES:(lt + 1) * LANES]
            stacked = jnp.concatenate([jnp.where(low_head, vt, zero_b),
                                       jnp.where(low_head, zero_b, vt)], axis=0)
            tiles.append(_dot(wsp_ref[lt], stacked))
        s_parts.append(jnp.concatenate(tiles, axis=1) + bsp_ref[...])
    bb = u * jnp.concatenate(s_parts, axis=0)
    a = _group_norm_silu(aconv[...], gavg_ref, ang_ref[...], anb_ref[...])

    _merge_and_route(x, a, bb, cc, lambda gi: gate_z[gi], (g1, sh2, sc2), n2g_ref[...], wbr_ref,
                     wout_ref, wr_ref, br_ref, x1_ref, h2_ref, route_ref)


def _const_spec(shape):
    nd = len(shape)
    return pl.BlockSpec(shape, lambda *_: (0,) * nd, pipeline_mode=pl.Buffered(1))


def _mixer_prompt(x, mod, layer, n, mod_row0, lw):
    t = x.shape[0] // n
    nt = t // TT
    mod_rows = mod.shape[2]
    row_spec = lambda cols: pl.BlockSpec((TT, cols), lambda i, j: (i * nt + j, 0))
    consts = [lw["n1g"], lw["n2g"], lw["w_in"], lw["a_dw_w"], lw["a_dw_b"], lw["a_norm_g"],
              lw["a_norm_b"], lw["b_norm_g"], lw["b_norm_b"], lw["w_sp"], lw["b_sp"], lw["c_conv_w"],
              lw["w_branch"], lw["w_out"], lw["gavg"], lw["w_router"], lw["b_router"]]
    return pl.pallas_call(
        functools.partial(_mixer_prompt_kernel, mod_row0=mod_row0),
        grid=(n, nt),
        in_specs=[row_spec(D),
                  pl.BlockSpec((1, 6, mod_rows, D), lambda i, j: (layer, 0, 0, 0),
                               pipeline_mode=pl.Buffered(1))]
        + [_const_spec(c.shape) for c in consts],
        out_specs=[row_spec(D),
                   pl.BlockSpec((TT * ROW_TILE, LANES), lambda i, j: (i * nt + j, 0)),
                   row_spec(ROUTE_COLS),
                   pl.BlockSpec((1, KA - 1, W), lambda i, j: (i, 0, 0)),
                   pl.BlockSpec((1, KC - 1, W), lambda i, j: (i, 0, 0))],
        out_shape=[jax.ShapeDtypeStruct((n * t, D), F32),
                   jax.ShapeDtypeStruct((n * t * ROW_TILE, LANES), U32),
                   jax.ShapeDtypeStruct((n * t, ROUTE_COLS), F32),
                   jax.ShapeDtypeStruct((n, KA - 1, W), F32),
                   jax.ShapeDtypeStruct((n, KC - 1, W), F32)],
        scratch_shapes=[pltpu.VMEM((A_HDR + TT + A_PAD, W), F32),
                        pltpu.VMEM((C_HDR + TT, W), F32),
                        pltpu.VMEM((TT, W), F32)],
        compiler_params=pltpu.CompilerParams(
            dimension_semantics=("arbitrary", "arbitrary"), vmem_limit_bytes=VMEM_LIMIT),
        name="mixer_prompt",
    )(x, mod, *consts)


def _mixer_sample_kernel(x_ref, mod_ref, hista_ref, histc_ref, n1g_ref, n2g_ref, win_ref, adw_ref,
                         adb_ref, ang_ref, anb_ref, bng_ref, bnb_ref, wsp4_ref, bsp4_ref,
                         ccw_ref, wbr_ref, wout_ref, gavg_ref, wr_ref, br_ref,
                         x1_ref, h2_ref, route_ref, ha_ref, hc_ref, v_ref):
    ts, nb = v_ref.shape[0], v_ref.shape[1]
    x = x_ref[...]

    def modrow(i):
        return jnp.concatenate([mod_ref[0, i]] * ts, axis=0)

    h = (_rms(x, n1g_ref[...]) * (1.0 + modrow(1)) + modrow(0)).astype(BF16)

    def zsec(lo, hi):
        return _dot(h, win_ref[:, lo:hi])

    def tslab(arr, j):
        return arr[j * nb:(j + 1) * nb, :]

    a_glu = zsec(0, W) * _sigmoid(zsec(W, 2 * W))
    kh = KA - 1
    a_conv = []
    for tq in range(ts):
        acc = jnp.zeros((nb, W), F32) + adb_ref[...]
        for r in range(tq, kh):
            acc = acc + adw_ref[r - tq:r - tq + 1, :] * hista_ref[r]
        for j in range(tq + 1):
            acc = acc + adw_ref[kh + j - tq:kh + j - tq + 1, :] * tslab(a_glu, j)
        a_conv.append(acc)
    a = _group_norm_silu(jnp.concatenate(a_conv, axis=0), gavg_ref, ang_ref[...], anb_ref[...])
    for r in range(kh - ts):
        ha_ref[r] = hista_ref[r + ts]
    for j in range(ts):
        ha_ref[kh - ts + j] = tslab(a_glu, j)

    u = _gelu_tanh(zsec(2 * W, 3 * W))
    v = _layer_norm(_gelu_tanh(zsec(3 * W, 4 * W)), bng_ref[...], bnb_ref[...])
    for j in range(ts):
        v_ref[j] = tslab(v, j)
    s_rows = []
    for tq in range(ts):
        s = jnp.zeros((nb, W), F32) + bsp4_ref[tq:tq + 1, :]
        for sq in range(tq + 1):
            s = s + wsp4_ref[tq * ts + sq:tq * ts + sq + 1, :] * tslab(v, sq)
        s_rows.append(s)
    bb = u * jnp.concatenate(s_rows, axis=0)

    cch = zsec(5 * W, 6 * W) * zsec(6 * W, 7 * W)
    xp = [histc_ref[r] for r in range(KC - 1)] + [tslab(cch, j) for j in range(ts)]
    ych = jnp.concatenate(
        [sum(ccw_ref[k:k + 1, :] * xp[tq + k] for k in range(KC)) for tq in range(ts)], axis=0)
    cc = zsec(4 * W, 5 * W) * ych
    for r in range(KC - 1):
        hc_ref[r] = xp[ts + r]

    def gates_fn(gi):
        return zsec(7 * W + gi * D, 7 * W + (gi + 1) * D)

    _merge_and_route(x, a, bb, cc, gates_fn, (modrow(2), modrow(3), modrow(4)), n2g_ref[...],
                     wbr_ref, wout_ref, wr_ref, br_ref, x1_ref, h2_ref, route_ref)


def _mixer_sample(x_rows, mod, layer, hist_a_tm, hist_c_tm, lw, nsplit):
    n = hist_a_tm.shape[1]
    ts = x_rows.shape[0] // n
    nb = n // nsplit
    rows = ts * nb
    consts = [lw["n1g"], lw["n2g"], lw["w_in"], lw["a_dw_w"], lw["a_dw_b"],
              lw["a_norm_g"], lw["a_norm_b"], lw["b_norm_g"], lw["b_norm_b"], lw["w_sp4"], lw["b_sp4"],
              lw["c_conv_w"], lw["w_branch"], lw["w_out"], lw["gavg"], lw["w_router"], lw["b_router"]]
    seq3 = lambda k, cols: pl.BlockSpec((k, nb, cols), lambda i: (0, i, 0))
    row_spec = lambda cols: pl.BlockSpec((rows, cols), lambda i: (i, 0))
    return pl.pallas_call(
        _mixer_sample_kernel,
        grid=(nsplit,),
        in_specs=[row_spec(D), pl.BlockSpec((1, 6, nb, D), lambda i: (layer, 0, i, 0)),
                  seq3(KA - 1, W), seq3(KC - 1, W)]
        + [_const_spec(c.shape) for c in consts],
        out_specs=[row_spec(D),
                   pl.BlockSpec((rows * ROW_TILE, LANES), lambda i: (i, 0)),
                   row_spec(ROUTE_COLS),
                   seq3(KA - 1, W), seq3(KC - 1, W), seq3(ts, W)],
        out_shape=[jax.ShapeDtypeStruct((ts * n, D), F32),
                   jax.ShapeDtypeStruct((ts * n * ROW_TILE, LANES), U32),
                   jax.ShapeDtypeStruct((ts * n, ROUTE_COLS), F32),
                   jax.ShapeDtypeStruct((KA - 1, n, W), F32),
                   jax.ShapeDtypeStruct((KC - 1, n, W), F32),
                   jax.ShapeDtypeStruct((ts, n, W), F32)],
        compiler_params=pltpu.CompilerParams(
            dimension_semantics=("arbitrary",), vmem_limit_bytes=VMEM_LIMIT),
        name="mixer_sample",
    )(x_rows, mod, hist_a_tm, hist_c_tm, *consts)


def _plan_kernel(rp_ref, rs_ref, tri_ref, destp_ref, dests_ref, tbl_ref, carry, pstart, *, ntp):
    ph = pl.program_id(0)
    i = pl.program_id(1)
    nt = pl.num_programs(1)
    lane = lax.broadcasted_iota(jnp.int32, (TP, ROUTE_COLS), 1).astype(F32)

    @pl.when(jnp.logical_and(ph == 0, i == 0))
    def _():
        carry[...] = jnp.zeros_like(carry)

    def sub_tile(r, dest_ref, row0):
        hot_a = lane == r[:, 0:1]
        hot_b = lane == r[:, 1:2]
        s = jnp.where(jnp.logical_or(hot_a, hot_b), 1.0, 0.0)

        @pl.when(ph == 1)
        def _():
            pre = _dot(tri_ref[...], s.astype(BF16)) + carry[...] + pstart[...]
            d_a = jnp.sum(jnp.where(hot_a, pre, 0.0), axis=-1, keepdims=True)
            d_b = jnp.sum(jnp.where(hot_b, pre, 0.0), axis=-1, keepdims=True)
            l2 = lax.broadcasted_iota(jnp.int32, (TP, 2), 1)
            dest_ref[row0:row0 + TP, :] = jnp.where(l2 == 0, d_a, d_b).astype(jnp.int32)

        carry[...] = carry[...] + jnp.sum(s, axis=0, keepdims=True)

    @pl.when(i < ntp)
    def _():
        for sub in range(rp_ref.shape[0] // TP):
            sub_tile(rp_ref[sub * TP:(sub + 1) * TP, :], destp_ref, sub * TP)

    @pl.when(i == ntp)
    def _():
        for sub in range(rs_ref.shape[0] // TP):
            sub_tile(rs_ref[sub * TP:(sub + 1) * TP, :], dests_ref, sub * TP)

    @pl.when(jnp.logical_and(ph == 0, i == nt - 1))
    def _():
        cnt = carry[...]
        nblk = jnp.floor((cnt + (BM - 1)) * (1.0 / BM))
        ri = lax.broadcasted_iota(jnp.int32, (ROUTE_COLS, ROUTE_COLS), 0)
        ci = lax.broadcasted_iota(jnp.int32, (ROUTE_COLS, ROUTE_COLS), 1)
        upper = jnp.where(ri <= ci, 1.0, 0.0)
        cum = _dot_hi(jnp.broadcast_to(nblk, (8, ROUTE_COLS)), upper)[0:1, :]
        cum_ex = cum - nblk
        pstart[...] = cum_ex * BM
        row = lax.broadcasted_iota(jnp.int32, (SUBLANES, ROUTE_COLS), 0)
        tbl = jnp.where(row == 0, cum_ex, jnp.where(row == 1, cnt, 0.0))
        tbl_ref[...] = tbl.astype(jnp.int32)
        carry[...] = jnp.zeros_like(carry)


def _plan(route_p, route_s):
    rows_p, rows_s = route_p.shape[0], route_s.shape[0]
    ntp = rows_p // TPP
    nbp = SUBLANES
    ri = lax.broadcasted_iota(jnp.int32, (TP, TP), 0)
    ci = lax.broadcasted_iota(jnp.int32, (TP, TP), 1)
    tri = jnp.where(ci < ri, 1.0, 0.0).astype(BF16)
    return pl.pallas_call(
        functools.partial(_plan_kernel, ntp=ntp),
        grid=(2, ntp + 1),
        in_specs=[pl.BlockSpec((TPP, ROUTE_COLS), lambda p, i: (jnp.minimum(i, ntp - 1), 0)),
                  pl.BlockSpec((rows_s, ROUTE_COLS), lambda p, i: (0, 0)),
                  pl.BlockSpec((TP, TP), lambda p, i: (0, 0))],
        out_specs=[pl.BlockSpec((TPP, 2), lambda p, i: (jnp.minimum(i, ntp - 1) * p, 0)),
                   pl.BlockSpec((rows_s, 2), lambda p, i: (0, 0)),
                   pl.BlockSpec((nbp, ROUTE_COLS), lambda p, i: (0, 0))],
        out_shape=[jax.ShapeDtypeStruct((rows_p, 2), jnp.int32),
                   jax.ShapeDtypeStruct((rows_s, 2), jnp.int32),
                   jax.ShapeDtypeStruct((nbp, ROUTE_COLS), jnp.int32)],
        scratch_shapes=[pltpu.VMEM((1, ROUTE_COLS), F32), pltpu.VMEM((1, ROUTE_COLS), F32)],
        compiler_params=pltpu.CompilerParams(
            dimension_semantics=("arbitrary", "arbitrary"), vmem_limit_bytes=VMEM_LIMIT),
        name="moe_plan",
    )(route_p, route_s, tri)


def _scatter_kernel(dest_ref, h_ref, *rest):
    rows_ref, sem = rest[-2:]
    tp = h_ref.shape[0] // ROW_TILE

    def row_copy(r, k):
        d = dest_ref[2 * r + k]
        return pltpu.make_async_copy(_row_tile(h_ref, r), _row_tile(rows_ref, d), sem)

    def issue(r, c):
        row_copy(r, 0).start(priority=0)
        row_copy(r, 1).start(priority=1)
        return c

    lax.fori_loop(0, tp, issue, 0, unroll=8)

    def drain(r, c):
        row_copy(r, 0).wait()
        row_copy(r, 1).wait()
        return c

    lax.fori_loop(0, tp, drain, 0, unroll=8)


def _scatter_rows(dest_flat, h2, rows_buf, n_rows, tp):
    nt = h2.shape[0] // (tp * ROW_TILE)
    in_specs = [pl.BlockSpec((2 * tp,), lambda i: (i,), memory_space=pltpu.SMEM),
                pl.BlockSpec((tp * ROW_TILE, LANES), lambda i: (i, 0))]
    args = [dest_flat, h2]
    aliases = {}
    if rows_buf is not None:
        in_specs.append(pl.BlockSpec(memory_space=pl.ANY))
        args.append(rows_buf)
        aliases = {2: 0}
    return pl.pallas_call(
        _scatter_kernel,
        grid=(nt,),
        in_specs=in_specs,
        out_specs=pl.BlockSpec(memory_space=pl.ANY),
        out_shape=jax.ShapeDtypeStruct((n_rows * ROW_TILE, LANES), U32),
        scratch_shapes=[pltpu.SemaphoreType.DMA(())],
        input_output_aliases=aliases,
        compiler_params=pltpu.CompilerParams(
            dimension_semantics=("arbitrary",), vmem_limit_bytes=VMEM_LIMIT,
            has_side_effects=True),
        name="moe_scatter",
    )(*args)


ROW_DMA_PRIORITY = 1
READ_AHEAD = 2
ROW_SLOTS = 4


def _expert_kernel(blk0_ref, cnt_ref, x_hbm, w1_ref, w3_ref, w2_ref, y_hbm,
                   xbuf, ybuf, w1b, w3b, w2b, sem_in, sem_out):
    e = pl.program_id(0)
    cnt = cnt_ref[e]
    nblk = lax.shift_right_logical(cnt + (BM - 1), BM.bit_length() - 1)
    blk0 = blk0_ref[e]
    last_e = NEXP - 1
    n_used = blk0_ref[last_e] + lax.shift_right_logical(cnt_ref[last_e] + (BM - 1),
                                                        BM.bit_length() - 1)
    blk_rows = BM * ROW_TILE

    def block_rows(ref, g):
        return ref.at[pl.ds(pl.multiple_of(g * blk_rows, blk_rows), blk_rows)]

    def in_copy(g):
        slot = lax.rem(g, ROW_SLOTS)
        return pltpu.make_async_copy(block_rows(x_hbm, g), xbuf.at[slot], sem_in.at[slot])

    def out_copy(g):
        slot = lax.rem(g, ROW_SLOTS)
        return pltpu.make_async_copy(ybuf.at[slot], block_rows(y_hbm, g), sem_out.at[slot])

    @pl.when(nblk > 0)
    def _():
        @pl.when(blk0 == 0)
        def _():
            for g0 in range(READ_AHEAD):
                @pl.when(g0 < n_used)
                def _():
                    in_copy(g0).start(priority=ROW_DMA_PRIORITY)

        w1b[...] = w1_ref[0, 0].astype(BF16)
        w3b[...] = w3_ref[0, 0].astype(BF16)
        w2b[...] = w2_ref[0, 0].astype(BF16)

        def chain(j, nb):
            g = blk0 + j
            for q in range(nb):
                @pl.when(g + READ_AHEAD + q < n_used)
                def _():
                    in_copy(g + READ_AHEAD + q).start(priority=ROW_DMA_PRIORITY)
            for q in range(nb):
                in_copy(g + q).wait()
            for q in range(nb):
                @pl.when(g + q >= ROW_SLOTS)
                def _():
                    out_copy(g + q - ROW_SLOTS).wait()

            row = lax.broadcasted_iota(jnp.int32, (BM, D), 0)
            xs = [jnp.where(row < cnt - (j + q) * BM,
                            _load_row_tiles(xbuf.at[lax.rem(g + q, ROW_SLOTS)]), 0.0).astype(BF16)
                  for q in range(nb)]
            x = xs[0] if nb == 1 else jnp.concatenate(xs, axis=0)
            h1 = _dot(x, w1b[...])
            h3 = _dot(x, w3b[...])
            act = (_silu(h1) * h3).astype(BF16)
            y = _dot(act, w2b[...])
            for q in range(nb):
                _store_row_tiles(ybuf.at[lax.rem(g + q, ROW_SLOTS)], y[q * BM:(q + 1) * BM, :])
                out_copy(g + q).start(priority=ROW_DMA_PRIORITY)

        def pair(p, carry):
            chain(2 * p, 2)
            return carry

        lax.fori_loop(0, lax.shift_right_logical(nblk, 1), pair, 0)

        @pl.when(lax.rem(nblk, 2) == 1)
        def _():
            chain(nblk - 1, 1)

    @pl.when(e == last_e)
    def _():
        for back in range(ROW_SLOTS, 0, -1):
            @pl.when(n_used >= back)
            def _():
                out_copy(n_used - back).wait()


def _experts(tbl, rows_buf, w1, w3, w2, layer):
    blk0, cnt = tbl[0, :NEXP], tbl[1, :NEXP]
    wspec = lambda shape: pl.BlockSpec((1, 1) + shape, lambda e, blk0, cnt: (layer, e, 0, 0))
    return pl.pallas_call(
        _expert_kernel,
        grid_spec=pltpu.PrefetchScalarGridSpec(
            num_scalar_prefetch=2,
            grid=(NEXP,),
            in_specs=[pl.BlockSpec(memory_space=pl.ANY),
                      wspec((D, FF)), wspec((D, FF)), wspec((FF, D))],
            out_specs=pl.BlockSpec(memory_space=pl.ANY),
            scratch_shapes=[pltpu.VMEM((ROW_SLOTS, BM * ROW_TILE, LANES), U32),
                            pltpu.VMEM((ROW_SLOTS, BM * ROW_TILE, LANES), U32),
                            pltpu.VMEM((D, FF), BF16), pltpu.VMEM((D, FF), BF16),
                            pltpu.VMEM((FF, D), BF16),
                            pltpu.SemaphoreType.DMA((ROW_SLOTS,)),
                            pltpu.SemaphoreType.DMA((ROW_SLOTS,))],
        ),
        out_shape=jax.ShapeDtypeStruct(rows_buf.shape, U32),
        compiler_params=pltpu.CompilerParams(
            dimension_semantics=("arbitrary",), vmem_limit_bytes=VMEM_LIMIT,
            has_side_effects=True),
        name="moe_experts",
    )(blk0, cnt, rows_buf, w1, w3, w2)


def _combine_kernel(dest_ref, dest_next_ref, x1_ref, route_ref, g2_ref, fng_ref, y_ref, o_ref,
                    ybuf, sem, *, final_norm, g2_row0, tiles_per_g2):
    tp = x1_ref.shape[0]
    i = pl.program_id(0)
    n = pl.num_programs(0)
    slot = lax.rem(i, 2)

    def gather(d_ref, s):
        def row_copy(r, k):
            return pltpu.make_async_copy(_row_tile(y_ref, d_ref[2 * r + k]),
                                         _row_tile(ybuf.at[s, k], r), sem.at[s])

        def issue(r, c):
            row_copy(r, 0).start(priority=0)
            row_copy(r, 1).start(priority=1)
            return c

        lax.fori_loop(0, tp, issue, 0, unroll=8)

    @pl.when(i == 0)
    def _():
        gather(dest_ref, 0)

    @pl.when(i + 1 < n)
    def _():
        gather(dest_next_ref, 1 - slot)

    for k in range(2):
        pltpu.make_async_copy(y_ref.at[pl.ds(0, tp * ROW_TILE)], ybuf.at[slot, k], sem.at[slot]).wait()

    route = route_ref[...]
    w_a = route[:, 2:3]
    w_b = route[:, 3:4]
    if tiles_per_g2:
        g2 = g2_ref[0, 0, pl.ds(g2_row0 + i // tiles_per_g2, 1), :]
    else:
        g2 = jnp.concatenate([g2_ref[0, 0]] * (tp // g2_ref.shape[2]), axis=0)
    y = (w_a * _load_row_tiles(ybuf.at[slot, 0]) + w_b * _load_row_tiles(ybuf.at[slot, 1]))
    x2 = x1_ref[...] + g2 * y
    if final_norm:
        x2 = _rms(x2, fng_ref[...])
    o_ref[...] = x2


def _combine(dest_flat, x1, route, mod, layer, fng, y_rows, tp, *, final_norm, g2_row0=0,
             tiles_per_g2=0, seqs_per_tile=0):
    nt = x1.shape[0] // tp
    if tiles_per_g2:
        g2_spec = pl.BlockSpec((1, 1, mod.shape[2], D), lambda i: (layer, 5, 0, 0))
    else:
        g2_spec = pl.BlockSpec((1, 1, seqs_per_tile, D), lambda i: (layer, 5, i, 0))
    return pl.pallas_call(
        functools.partial(_combine_kernel, final_norm=final_norm, g2_row0=g2_row0,
                          tiles_per_g2=tiles_per_g2),
        grid=(nt,),
        in_specs=[pl.BlockSpec((2 * tp,), lambda i: (i,), memory_space=pltpu.SMEM),
                  pl.BlockSpec((2 * tp,), lambda i: (jnp.minimum(i + 1, nt - 1),),
                               memory_space=pltpu.SMEM),
                  pl.BlockSpec((tp, D), lambda i: (i, 0)),
                  pl.BlockSpec((tp, ROUTE_COLS), lambda i: (i, 0)),
                  g2_spec,
                  pl.BlockSpec((1, D), lambda i: (0, 0)),
                  pl.BlockSpec(memory_space=pl.ANY)],
        out_specs=pl.BlockSpec((tp, D), lambda i: (i, 0)),
        out_shape=jax.ShapeDtypeStruct(x1.shape, F32),
        scratch_shapes=[pltpu.VMEM((2, 2, tp * ROW_TILE, LANES), U32),
                        pltpu.SemaphoreType.DMA((2,))],
        compiler_params=pltpu.CompilerParams(
            dimension_semantics=("arbitrary",), vmem_limit_bytes=VMEM_LIMIT),
        name="moe_combine",
    )(dest_flat, dest_flat, x1, route, mod, fng, y_rows)


def _layer_weights(l, p):
    hd = W // HEADS
    gi = jnp.arange(W) // (W // NG_A)
    gavg = jnp.where(gi[:, None] == gi[None, :], 1.0 / (W // NG_A), 0.0).astype(BF16)
    tril = jnp.tril(jnp.ones((CHUNK, CHUNK), dtype=bool))
    w_sp = jnp.where(tril[None], p["b_spatial_w"][l], 0.0)
    w_router = jnp.concatenate(
        [p["w_router_g"][l], p["w_router_e"][l],
         jnp.zeros((D, ROUTE_COLS - NGRP - NEXP), F32)], axis=1)
    b_router = jnp.concatenate(
        [p["b_router_g"][l], p["b_router_e"][l],
         jnp.zeros((ROUTE_COLS - NGRP - NEXP,), F32)])[None, :]
    return {
        "n1g": p["norm1_g"][l][None, :], "n2g": p["norm2_g"][l][None, :],
        "w_in": p["w_in"][l].astype(BF16),
        "a_dw_w": p["a_dw_w"][l], "a_dw_b": p["a_dw_b"][l][None, :],
        "a_norm_g": p["a_norm_g"][l][None, :], "a_norm_b": p["a_norm_b"][l][None, :],
        "b_norm_g": p["b_norm_g"][l][None, :], "b_norm_b": p["b_norm_b"][l][None, :],
        "w_sp": jnp.concatenate([w_sp[0::2], w_sp[1::2]], axis=2).astype(BF16),
        "b_sp": jnp.repeat(p["b_spatial_b"][l].T, hd, axis=1),
        "c_conv_w": p["c_conv_w"][l],
        "w_branch": p["w_branch"][l].astype(BF16), "w_out": p["w_out"][l].astype(BF16),
        "gavg": gavg, "w_router": w_router.astype(BF16), "b_router": b_router,
    }


def _sample_spatial(l, p, ts):
    hd = W // HEADS
    w = p["b_spatial_w"][l][:, :ts, :ts]
    w = jnp.where(jnp.tril(jnp.ones((ts, ts), dtype=bool))[None], w, 0.0)
    w4 = jnp.repeat(jnp.transpose(w, (1, 2, 0)).reshape(ts * ts, HEADS), hd, axis=1)
    b4 = jnp.repeat(p["b_spatial_b"][l][:, :ts].T, hd, axis=1)
    return w4, b4


def kernel(x_prompt, x_sample, c_prompt, c_sample, state_conv_a, state_conv_c, norm1_g, norm2_g, w_ada, b_ada, w_in, a_dw_w, a_dw_b, a_norm_g, a_norm_b, b_norm_g, b_norm_b, b_spatial_w, b_spatial_b, c_conv_w, w_branch, w_out, w_router_g, b_router_g, w_router_e, b_router_e, w1, w3, w2, final_norm_g):
    p = dict(norm1_g=norm1_g, norm2_g=norm2_g, w_in=w_in, a_dw_w=a_dw_w, a_dw_b=a_dw_b,
             a_norm_g=a_norm_g, a_norm_b=a_norm_b, b_norm_g=b_norm_g, b_norm_b=b_norm_b,
             b_spatial_w=b_spatial_w, b_spatial_b=b_spatial_b, c_conv_w=c_conv_w,
             w_branch=w_branch, w_out=w_out, w_router_g=w_router_g, b_router_g=b_router_g,
             w_router_e=w_router_e, b_router_e=b_router_e)
    depth = w_in.shape[0]
    nb_, seq, _ = x_prompt.shape
    ns, ts, _ = x_sample.shape
    n_tok = nb_ * seq + ns * ts
    nblocks = -(-(2 * n_tok) // BM) + NEXP
    nsplit = 2
    nbs = ns // nsplit
    tps = ts * nbs
    assert seq % TT == 0 and TT % CHUNK == 0 and seq % TP == 0 and (nb_ * seq) % TPP == 0
    assert (ts * ns) % TP == 0 and ns % SUBLANES == 0

    mod = _adaln(jnp.concatenate([c_sample, c_prompt], axis=0), w_ada, b_ada)

    xp = x_prompt.reshape(nb_ * seq, D)
    xs = jnp.transpose(x_sample.reshape(nsplit, nbs, ts, D), (0, 2, 1, 3)).reshape(ts * ns, D)
    fng = final_norm_g[None, :]
    pa, pc, sa, sc, sv = [], [], [], [], []
    for l in range(depth):
        lw = _layer_weights(l, p)
        lw["w_sp4"], lw["b_sp4"] = _sample_spatial(l, p, ts)

        x1p, h2p, route_p, ha, hc = _mixer_prompt(xp, mod, l, nb_, ns, lw)
        pa.append(ha)
        pc.append(hc)
        x1s, h2s, route_s, ha_s, hc_s, v_s = _mixer_sample(
            xs, mod, l, jnp.transpose(state_conv_a[l], (1, 0, 2)),
            jnp.transpose(state_conv_c[l], (1, 0, 2)), lw, nsplit)
        sa.append(jnp.transpose(ha_s, (1, 0, 2)))
        sc.append(jnp.transpose(hc_s, (1, 0, 2)))
        sv.append(jnp.transpose(v_s, (1, 0, 2)))

        dest_p, dest_s, tbl = _plan(route_p, route_s)
        dest_p, dest_s = dest_p.reshape(-1), dest_s.reshape(-1)
        rows_buf = _scatter_rows(dest_p, h2p, None, nblocks * BM, TP)
        rows_buf = _scatter_rows(dest_s, h2s, rows_buf, nblocks * BM, tps)
        y_rows = _experts(tbl, rows_buf, w1, w3, w2, l)

        last = l == depth - 1
        xp = _combine(dest_p, x1p, route_p, mod, l, fng, y_rows, TP, final_norm=last,
                      g2_row0=ns, tiles_per_g2=seq // TP)
        xs = _combine(dest_s, x1s, route_s, mod, l, fng, y_rows, tps, final_norm=last,
                      seqs_per_tile=nbs)

    y_prompt = xp.reshape(nb_, seq, D)
    y_sample = jnp.transpose(xs.reshape(nsplit, ts, nbs, D), (0, 2, 1, 3)).reshape(ns, ts, D)
    return (y_prompt, y_sample, jnp.stack(pa), jnp.stack(pc), jnp.stack(sa), jnp.stack(sc),
            jnp.stack(sv))
```

```python
import functools

import jax
import jax.numpy as jnp
from jax import lax
from jax.experimental import pallas as pl
from jax.experimental.pallas import tpu as pltpu

F32 = jnp.float32
BF16 = jnp.bfloat16

D = 1024
W = D // 2
KA = 31
KC = 3
NG_A = 8
CHUNK = 128
HEADS = 8
NGRP = 4
EPG = 8
NEXP = NGRP * EPG
FF = D // 2
EPS = 1e-6
IN_COLS = 7 * W + 3 * D
LANES = 128
ROUTE_COLS = LANES

TT = 512
SUBLANES = 8
A_HDR = 32
A_PAD = SUBLANES
A_ROWS = 128
C_HDR = 8
TP = 512
TPP = 2048
BM = 256
VMEM_LIMIT = 56 * 1024 * 1024

NEG = -1e30


def _sigmoid(x):
    return 0.5 * jnp.tanh(0.5 * x) + 0.5


def _silu(x):
    return x * _sigmoid(x)


def _gelu_tanh(x):
    return 0.5 * x * (1.0 + jnp.tanh(0.7978845608028654 * (x + 0.044715 * x * x * x)))


def _rms(x, g):
    return x * lax.rsqrt(jnp.mean(x * x, axis=-1, keepdims=True) + EPS) * g


def _rms_mod(x, g, scale, shift):
    inv = lax.rsqrt(jnp.mean(x * x, axis=-1, keepdims=True) + EPS)
    return x * inv * (g * (1.0 + scale)) + shift


def _dot(a, b):
    return jnp.dot(a, b, preferred_element_type=F32)


def _dot_hi(a, b):
    return jnp.dot(a, b, preferred_element_type=F32, precision=lax.Precision.HIGHEST)


U32 = jnp.uint32
ROW_TILE = D // (2 * LANES)


def _pack_bf16_pair(lo, hi):
    def rne(x):
        b = pltpu.bitcast(x, U32)
        return b + (U32(0x7FFF) + ((b >> 16) & U32(1)))
    return (rne(lo) >> 16) | (rne(hi) & U32(0xFFFF0000))


def _store_row_tiles(ref, val):
    r = val.shape[0]
    for j in range(ROW_TILE):
        lo = val[:, 2 * j * LANES:(2 * j + 1) * LANES]
        hi = val[:, (2 * j + 1) * LANES:(2 * j + 2) * LANES]
        ref[pl.ds(j, r, stride=ROW_TILE), :] = _pack_bf16_pair(lo, hi)


def _load_row_tiles(ref):
    r = ref.shape[0] // ROW_TILE
    parts = []
    for j in range(ROW_TILE):
        w = ref[pl.ds(j, r, stride=ROW_TILE), :]
        parts.append(pltpu.bitcast(w << 16, F32))
        parts.append(pltpu.bitcast(w & U32(0xFFFF0000), F32))
    return jnp.concatenate(parts, axis=1)


def _row_tile(ref, r):
    return ref.at[pl.ds(pl.multiple_of(r * ROW_TILE, ROW_TILE), ROW_TILE)]


def _mod_kernel(c_ref, w_ref, b_ref, o_ref):
    c = c_ref[...]
    s = _silu(c).astype(BF16)
    o_ref[0, 0] = _dot(s, w_ref[0].astype(BF16)) + b_ref[0]


def _adaln(c_all, w_ada, b_ada):
    nl = w_ada.shape[0]
    r = c_all.shape[0]
    return pl.pallas_call(
        _mod_kernel,
        grid=(nl, 6),
        in_specs=[
            pl.BlockSpec((r, D), lambda l, j: (0, 0)),
            pl.BlockSpec((1, D, D), lambda l, j: (l, 0, j)),
            pl.BlockSpec((1, 1, D), lambda l, j: (l, 0, j)),
        ],
        out_specs=pl.BlockSpec((1, 1, r, D), lambda l, j: (l, j, 0, 0)),
        out_shape=jax.ShapeDtypeStruct((nl, 6, r, D), F32),
        compiler_params=pltpu.CompilerParams(
            dimension_semantics=("arbitrary", "arbitrary"), vmem_limit_bytes=VMEM_LIMIT),
        name="adaln_mod",
    )(c_all, w_ada, b_ada.reshape(nl, 1, 6 * D))


def _group_norm_silu(a, gavg_ref, g, b):
    mu = _dot(a.astype(BF16), gavg_ref[...])
    xc = a - mu
    var = _dot((xc * xc).astype(BF16), gavg_ref[...])
    y = xc * lax.rsqrt(var + EPS) * g + b
    return _silu(y)


def _layer_norm(x, g, b):
    mu = jnp.mean(x, axis=-1, keepdims=True)
    xc = x - mu
    var = jnp.mean(xc * xc, axis=-1, keepdims=True)
    return xc * lax.rsqrt(var + EPS) * g + b


def _merge_and_route(x, a, bb, cc, gates_fn, mod, n2g, wbr_ref, wout_ref, wr_ref, br_ref,
                     x1_ref, h2_ref, route_ref):
    g1, sh2, sc2 = mod
    acc = None
    for gi, br_in in enumerate((a, bb, cc)):
        br = _dot(br_in.astype(BF16), wbr_ref[gi])
        term = _sigmoid(gates_fn(gi)) * br
        acc = term if acc is None else acc + term
    mixed = _dot(acc.astype(BF16), wout_ref[...])
    x1 = x + g1 * mixed
    x1_ref[...] = x1
    h2 = _rms_mod(x1, n2g, sc2, sh2)
    _store_row_tiles(h2_ref, h2)

    logits = _dot(h2.astype(BF16), wr_ref[...]) + br_ref[...]
    rows = logits.shape[0]
    lane = lax.broadcasted_iota(jnp.int32, (rows, ROUTE_COLS), 1).astype(F32)
    is_g = lane < NGRP
    gl = jnp.where(is_g, logits, NEG)
    gmax = jnp.max(gl, axis=-1, keepdims=True)
    gsel = jnp.min(jnp.where(gl == gmax, lane, float(ROUTE_COLS)), axis=-1, keepdims=True)
    gden = jnp.sum(jnp.where(is_g, jnp.exp(gl - gmax), 0.0), axis=-1, keepdims=True)
    gprob = 1.0 / gden
    lo = NGRP + EPG * gsel
    in_grp = jnp.logical_and(lane >= lo, lane < lo + EPG)
    el = jnp.where(in_grp, logits, NEG)
    v1 = jnp.max(el, axis=-1, keepdims=True)
    i1 = jnp.min(jnp.where(el == v1, lane, float(ROUTE_COLS)), axis=-1, keepdims=True)
    el2 = jnp.where(lane == i1, NEG, el)
    v2 = jnp.max(el2, axis=-1, keepdims=True)
    i2 = jnp.min(jnp.where(el2 == v2, lane, float(ROUTE_COLS)), axis=-1, keepdims=True)
    p2 = jnp.exp(v2 - v1)
    w_a = gprob / (1.0 + p2)
    w_b = gprob * p2 / (1.0 + p2)
    route = jnp.where(lane == 0.0, i1 - NGRP,
                      jnp.where(lane == 1.0, i2 - NGRP,
                                jnp.where(lane == 2.0, w_a,
                                          jnp.where(lane == 3.0, w_b, 0.0))))
    route_ref[...] = route


def _mixer_prompt_kernel(x_ref, mod_ref, n1g_ref, n2g_ref, win_ref, adw_ref, adb_ref, ang_ref,
                         anb_ref, bng_ref, bnb_ref, wsp_ref, bsp_ref, ccw_ref, wbr_ref, wout_ref,
                         gavg_ref, wr_ref, br_ref,
                         x1_ref, h2_ref, route_ref, ha_ref, hc_ref,
                         abuf, cbuf, aconv, *, mod_row0):
    first = pl.program_id(1) == 0
    abuf[0:A_HDR, :] = jnp.where(first, 0.0, abuf[TT:TT + A_HDR, :])
    abuf[A_HDR + TT:A_HDR + TT + A_PAD, :] = jnp.zeros((A_PAD, W), F32)
    cbuf[0:C_HDR, :] = jnp.where(first, 0.0, cbuf[TT:TT + C_HDR, :])

    x = x_ref[...]
    mrow = mod_row0 + pl.program_id(0)
    sh1, sc1, g1, sh2, sc2 = (mod_ref[0, s, pl.ds(mrow, 1), :] for s in range(5))
    h = _rms_mod(x, n1g_ref[...], sc1, sh1).astype(BF16)

    def zsec(lo, hi):
        return _dot(h, win_ref[:, lo:hi])

    a_glu = zsec(0, W) * _sigmoid(zsec(W, 2 * W))
    abuf[A_HDR:A_HDR + TT, :] = a_glu
    off = A_HDR - (KA - 1)
    rc = A_ROWS

    def conv_pass(p):
        lt, half = p // 2, p % 2
        ls = slice(lt * LANES, (lt + 1) * LANES)
        for r0 in range(half * (TT // 2), (half + 1) * (TT // 2), rc):
            y = None
            for r in range(SUBLANES):
                z = None
                for q in range((KA - 1 + off) // SUBLANES + 1):
                    k = SUBLANES * q + r - off
                    if 0 <= k < KA:
                        rows = slice(r0 + SUBLANES * q, r0 + SUBLANES * q + rc + SUBLANES)
                        term = adw_ref[k:k + 1, ls] * abuf[rows, ls]
                        z = term if z is None else z + term
                part = z[r:r + rc, :]
                y = part if y is None else y + part
            aconv[r0:r0 + rc, ls] = y + adb_ref[:, ls]

    conv_pass(0)
    u = _gelu_tanh(zsec(2 * W, 3 * W))
    conv_pass(1)
    v = _layer_norm(_gelu_tanh(zsec(3 * W, 4 * W)), bng_ref[...], bnb_ref[...])
    conv_pass(2)

    cch = zsec(5 * W, 6 * W) * zsec(6 * W, 7 * W)
    cbuf[C_HDR:C_HDR + TT, :] = cch
    coff = C_HDR - (KC - 1)
    ych = (ccw_ref[0:1, :] * cbuf[coff:coff + TT, :]
           + ccw_ref[1:2, :] * cbuf[coff + 1:coff + 1 + TT, :]
           + ccw_ref[2:3, :] * cbuf[coff + 2:coff + 2 + TT, :])
    conv_pass(3)
    cc = zsec(4 * W, 5 * W) * ych
    hc_ref[0] = cbuf[C_HDR + TT - (KC - 1):C_HDR + TT, :]
    conv_pass(4)
    gate_z = [zsec(7 * W, 7 * W + D)]
    conv_pass(5)
    gate_z.append(zsec(7 * W + D, 7 * W + 2 * D))
    conv_pass(6)
    gate_z.append(zsec(7 * W + 2 * D, 7 * W + 3 * D))
    conv_pass(7)
    ha_ref[0] = abuf[A_HDR + TT - (KA - 1):A_HDR + TT, :]

    vb = v.astype(BF16)
    low_head = lax.broadcasted_iota(jnp.int32, (CHUNK, LANES), 1) < (W // HEADS)
    zero_b = jnp.zeros((CHUNK, LANES), BF16)
    s_parts = []
    for c in range(TT // CHUNK):
        tiles = []
        for lt in range(W // LANES):
            vt = vb[c * CHUNK:(c + 1) * CHUNK, lt * LANES:(lt + 1) * LANES]
            stacked = jnp.concatenate([jnp.where(low_head, vt, zero_b),
                                       jnp.where(low_head, zero_b, vt)], axis=0)
            tiles.append(_dot(wsp_ref[lt], stacked))
        s_parts.append(jnp.concatenate(tiles, axis=1) + bsp_ref[...])
    bb = u * jnp.concatenate(s_parts, axis=0)
    a = _group_norm_silu(aconv[...], gavg_ref, ang_ref[...], anb_ref[...])

    _merge_and_route(x, a, bb, cc, lambda gi: gate_z[gi], (g1, sh2, sc2), n2g_ref[...], wbr_ref,
                     wout_ref, wr_ref, br_ref, x1_ref, h2_ref, route_ref)


def _const_spec(shape):
    nd = len(shape)
    return pl.BlockSpec(shape, lambda *_: (0,) * nd, pipeline_mode=pl.Buffered(1))


def _mixer_prompt(x, mod, layer, n, mod_row0, lw):
    t = x.shape[0] // n
    nt = t // TT
    mod_rows = mod.shape[2]
    row_spec = lambda cols: pl.BlockSpec((TT, cols), lambda i, j: (i * nt + j, 0))
    consts = [lw["n1g"], lw["n2g"], lw["w_in"], lw["a_dw_w"], lw["a_dw_b"], lw["a_norm_g"],
              lw["a_norm_b"], lw["b_norm_g"], lw["b_norm_b"], lw["w_sp"], lw["b_sp"], lw["c_conv_w"],
              lw["w_branch"], lw["w_out"], lw["gavg"], lw["w_router"], lw["b_router"]]
    return pl.pallas_call(
        functools.partial(_mixer_prompt_kernel, mod_row0=mod_row0),
        grid=(n, nt),
        in_specs=[row_spec(D),
                  pl.BlockSpec((1, 6, mod_rows, D), lambda i, j: (layer, 0, 0, 0),
                               pipeline_mode=pl.Buffered(1))]
        + [_const_spec(c.shape) for c in consts],
        out_specs=[row_spec(D),
                   pl.BlockSpec((TT * ROW_TILE, LANES), lambda i, j: (i * nt + j, 0)),
                   row_spec(ROUTE_COLS),
                   pl.BlockSpec((1, KA - 1, W), lambda i, j: (i, 0, 0)),
                   pl.BlockSpec((1, KC - 1, W), lambda i, j: (i, 0, 0))],
        out_shape=[jax.ShapeDtypeStruct((n * t, D), F32),
                   jax.ShapeDtypeStruct((n * t * ROW_TILE, LANES), U32),
                   jax.ShapeDtypeStruct((n * t, ROUTE_COLS), F32),
                   jax.ShapeDtypeStruct((n, KA - 1, W), F32),
                   jax.ShapeDtypeStruct((n, KC - 1, W), F32)],
        scratch_shapes=[pltpu.VMEM((A_HDR + TT + A_PAD, W), F32),
                        pltpu.VMEM((C_HDR + TT, W), F32),
                        pltpu.VMEM((TT, W), F32)],
        compiler_params=pltpu.CompilerParams(
            dimension_semantics=("arbitrary", "arbitrary"), vmem_limit_bytes=VMEM_LIMIT),
        name="mixer_prompt",
    )(x, mod, *consts)


def _mixer_sample_kernel(x_ref, mod_ref, hista_ref, histc_ref, n1g_ref, n2g_ref, win_ref, adw_ref,
                         adb_ref, ang_ref, anb_ref, bng_ref, bnb_ref, wsp4_ref, bsp4_ref,
                         ccw_ref, wbr_ref, wout_ref, gavg_ref, wr_ref, br_ref,
                         x1_ref, h2_ref, route_ref, ha_ref, hc_ref, v_ref):
    ts, nb = v_ref.shape[0], v_ref.shape[1]
    x = x_ref[...]

    def modrow(i):
        return jnp.concatenate([mod_ref[0, i]] * ts, axis=0)

    h = (_rms(x, n1g_ref[...]) * (1.0 + modrow(1)) + modrow(0)).astype(BF16)

    def zsec(lo, hi):
        return _dot(h, win_ref[:, lo:hi])

    def tslab(arr, j):
        return arr[j * nb:(j + 1) * nb, :]

    a_glu = zsec(0, W) * _sigmoid(zsec(W, 2 * W))
    kh = KA - 1
    a_conv = []
    for tq in range(ts):
        acc = jnp.zeros((nb, W), F32) + adb_ref[...]
        for r in range(tq, kh):
            acc = acc + adw_ref[r - tq:r - tq + 1, :] * hista_ref[r]
        for j in range(tq + 1):
            acc = acc + adw_ref[kh + j - tq:kh + j - tq + 1, :] * tslab(a_glu, j)
        a_conv.append(acc)
    a = _group_norm_silu(jnp.concatenate(a_conv, axis=0), gavg_ref, ang_ref[...], anb_ref[...])
    for r in range(kh - ts):
        ha_ref[r] = hista_ref[r + ts]
    for j in range(ts):
        ha_ref[kh - ts + j] = tslab(a_glu, j)

    u = _gelu_tanh(zsec(2 * W, 3 * W))
    v = _layer_norm(_gelu_tanh(zsec(3 * W, 4 * W)), bng_ref[...], bnb_ref[...])
    for j in range(ts):
        v_ref[j] = tslab(v, j)
    s_rows = []
    for tq in range(ts):
        s = jnp.zeros((nb, W), F32) + bsp4_ref[tq:tq + 1, :]
        for sq in range(tq + 1):
            s = s + wsp4_ref[tq * ts + sq:tq * ts + sq + 1, :] * tslab(v, sq)
        s_rows.append(s)
    bb = u * jnp.concatenate(s_rows, axis=0)

    cch = zsec(5 * W, 6 * W) * zsec(6 * W, 7 * W)
    xp = [histc_ref[r] for r in range(KC - 1)] + [tslab(cch, j) for j in range(ts)]
    ych = jnp.concatenate(
        [sum(ccw_ref[k:k + 1, :] * xp[tq + k] for k in range(KC)) for tq in range(ts)], axis=0)
    cc = zsec(4 * W, 5 * W) * ych
    for r in range(KC - 1):
        hc_ref[r] = xp[ts + r]

    def gates_fn(gi):
        return zsec(7 * W + gi * D, 7 * W + (gi + 1) * D)

    _merge_and_route(x, a, bb, cc, gates_fn, (modrow(2), modrow(3), modrow(4)), n2g_ref[...],
                     wbr_ref, wout_ref, wr_ref, br_ref, x1_ref, h2_ref, route_ref)


def _mixer_sample(x_rows, mod, layer, hist_a_tm, hist_c_tm, lw, nsplit):
    n = hist_a_tm.shape[1]
    ts = x_rows.shape[0] // n
    nb = n // nsplit
    rows = ts * nb
    consts = [lw["n1g"], lw["n2g"], lw["w_in"], lw["a_dw_w"], lw["a_dw_b"],
              lw["a_norm_g"], lw["a_norm_b"], lw["b_norm_g"], lw["b_norm_b"], lw["w_sp4"], lw["b_sp4"],
              lw["c_conv_w"], lw["w_branch"], lw["w_out"], lw["gavg"], lw["w_router"], lw["b_router"]]
    seq3 = lambda k, cols: pl.BlockSpec((k, nb, cols), lambda i: (0, i, 0))
    row_spec = lambda cols: pl.BlockSpec((rows, cols), lambda i: (i, 0))
    return pl.pallas_call(
        _mixer_sample_kernel,
        grid=(nsplit,),
        in_specs=[row_spec(D), pl.BlockSpec((1, 6, nb, D), lambda i: (layer, 0, i, 0)),
                  seq3(KA - 1, W), seq3(KC - 1, W)]
        + [_const_spec(c.shape) for c in consts],
        out_specs=[row_spec(D),
                   pl.BlockSpec((rows * ROW_TILE, LANES), lambda i: (i, 0)),
                   row_spec(ROUTE_COLS),
                   seq3(KA - 1, W), seq3(KC - 1, W), seq3(ts, W)],
        out_shape=[jax.ShapeDtypeStruct((ts * n, D), F32),
                   jax.ShapeDtypeStruct((ts * n * ROW_TILE, LANES), U32),
                   jax.ShapeDtypeStruct((ts * n, ROUTE_COLS), F32),
                   jax.ShapeDtypeStruct((KA - 1, n, W), F32),
                   jax.ShapeDtypeStruct((KC - 1, n, W), F32),
                   jax.ShapeDtypeStruct((ts, n, W), F32)],
        compiler_params=pltpu.CompilerParams(
            dimension_semantics=("arbitrary",), vmem_limit_bytes=VMEM_LIMIT),
        name="mixer_sample",
    )(x_rows, mod, hist_a_tm, hist_c_tm, *consts)


def _plan_kernel(rp_ref, rs_ref, tri_ref, destp_ref, dests_ref, tbl_ref, carry, pstart, *, ntp):
    ph = pl.program_id(0)
    i = pl.program_id(1)
    nt = pl.num_programs(1)
    lane = lax.broadcasted_iota(jnp.int32, (TP, ROUTE_COLS), 1).astype(F32)

    @pl.when(jnp.logical_and(ph == 0, i == 0))
    def _():
        carry[...] = jnp.zeros_like(carry)

    def sub_tile(r, dest_ref, row0):
        hot_a = lane == r[:, 0:1]
        hot_b = lane == r[:, 1:2]
        s = jnp.where(jnp.logical_or(hot_a, hot_b), 1.0, 0.0)

        @pl.when(ph == 1)
        def _():
            pre = _dot(tri_ref[...], s.astype(BF16)) + carry[...] + pstart[...]
            d_a = jnp.sum(jnp.where(hot_a, pre, 0.0), axis=-1, keepdims=True)
            d_b = jnp.sum(jnp.where(hot_b, pre, 0.0), axis=-1, keepdims=True)
            l2 = lax.broadcasted_iota(jnp.int32, (TP, 2), 1)
            dest_ref[row0:row0 + TP, :] = jnp.where(l2 == 0, d_a, d_b).astype(jnp.int32)

        carry[...] = carry[...] + jnp.sum(s, axis=0, keepdims=True)

    @pl.when(i < ntp)
    def _():
        for sub in range(rp_ref.shape[0] // TP):
            sub_tile(rp_ref[sub * TP:(sub + 1) * TP, :], destp_ref, sub * TP)

    @pl.when(i == ntp)
    def _():
        for sub in range(rs_ref.shape[0] // TP):
            sub_tile(rs_ref[sub * TP:(sub + 1) * TP, :], dests_ref, sub * TP)

    @pl.when(jnp.logical_and(ph == 0, i == nt - 1))
    def _():
        cnt = carry[...]
        nblk = jnp.floor((cnt + (BM - 1)) * (1.0 / BM))
        ri = lax.broadcasted_iota(jnp.int32, (ROUTE_COLS, ROUTE_COLS), 0)
        ci = lax.broadcasted_iota(jnp.int32, (ROUTE_COLS, ROUTE_COLS), 1)
        upper = jnp.where(ri <= ci, 1.0, 0.0)
        cum = _dot_hi(jnp.broadcast_to(nblk, (8, ROUTE_COLS)), upper)[0:1, :]
        cum_ex = cum - nblk
        pstart[...] = cum_ex * BM
        row = lax.broadcasted_iota(jnp.int32, (SUBLANES, ROUTE_COLS), 0)
        tbl = jnp.where(row == 0, cum_ex, jnp.where(row == 1, cnt, 0.0))
        tbl_ref[...] = tbl.astype(jnp.int32)
        carry[...] = jnp.zeros_like(carry)


def _plan(route_p, route_s):
    rows_p, rows_s = route_p.shape[0], route_s.shape[0]
    ntp = rows_p // TPP
    nbp = SUBLANES
    ri = lax.broadcasted_iota(jnp.int32, (TP, TP), 0)
    ci = lax.broadcasted_iota(jnp.int32, (TP, TP), 1)
    tri = jnp.where(ci < ri, 1.0, 0.0).astype(BF16)
    return pl.pallas_call(
        functools.partial(_plan_kernel, ntp=ntp),
        grid=(2, ntp + 1),
        in_specs=[pl.BlockSpec((TPP, ROUTE_COLS), lambda p, i: (jnp.minimum(i, ntp - 1), 0)),
                  pl.BlockSpec((rows_s, ROUTE_COLS), lambda p, i: (0, 0)),
                  pl.BlockSpec((TP, TP), lambda p, i: (0, 0))],
        out_specs=[pl.BlockSpec((TPP, 2), lambda p, i: (jnp.minimum(i, ntp - 1) * p, 0)),
                   pl.BlockSpec((rows_s, 2), lambda p, i: (0, 0)),
                   pl.BlockSpec((nbp, ROUTE_COLS), lambda p, i: (0, 0))],
        out_shape=[jax.ShapeDtypeStruct((rows_p, 2), jnp.int32),
                   jax.ShapeDtypeStruct((rows_s, 2), jnp.int32),
                   jax.ShapeDtypeStruct((nbp, ROUTE_COLS), jnp.int32)],
        scratch_shapes=[pltpu.VMEM((1, ROUTE_COLS), F32), pltpu.VMEM((1, ROUTE_COLS), F32)],
        compiler_params=pltpu.CompilerParams(
            dimension_semantics=("arbitrary", "arbitrary"), vmem_limit_bytes=VMEM_LIMIT),
        name="moe_plan",
    )(route_p, route_s, tri)


def _scatter_kernel(dest_ref, h_ref, *rest):
    rows_ref, sem = rest[-2:]
    tp = h_ref.shape[0] // ROW_TILE

    def row_copy(r, k):
        d = dest_ref[2 * r + k]
        return pltpu.make_async_copy(_row_tile(h_ref, r), _row_tile(rows_ref, d), sem)

    def issue(r, c):
        row_copy(r, 0).start(priority=0)
        row_copy(r, 1).start(priority=1)
        return c

    lax.fori_loop(0, tp, issue, 0, unroll=8)

    def drain(r, c):
        row_copy(r, 0).wait()
        row_copy(r, 1).wait()
        return c

    lax.fori_loop(0, tp, drain, 0, unroll=8)


def _scatter_rows(dest_flat, h2, rows_buf, n_rows, tp):
    nt = h2.shape[0] // (tp * ROW_TILE)
    in_specs = [pl.BlockSpec((2 * tp,), lambda i: (i,), memory_space=pltpu.SMEM),
                pl.BlockSpec((tp * ROW_TILE, LANES), lambda i: (i, 0))]
    args = [dest_flat, h2]
    aliases = {}
    if rows_buf is not None:
        in_specs.append(pl.BlockSpec(memory_space=pl.ANY))
        args.append(rows_buf)
        aliases = {2: 0}
    return pl.pallas_call(
        _scatter_kernel,
        grid=(nt,),
        in_specs=in_specs,
        out_specs=pl.BlockSpec(memory_space=pl.ANY),
        out_shape=jax.ShapeDtypeStruct((n_rows * ROW_TILE, LANES), U32),
        scratch_shapes=[pltpu.SemaphoreType.DMA(())],
        input_output_aliases=aliases,
        compiler_params=pltpu.CompilerParams(
            dimension_semantics=("arbitrary",), vmem_limit_bytes=VMEM_LIMIT,
            has_side_effects=True),
        name="moe_scatter",
    )(*args)


ROW_DMA_PRIORITY = 1
READ_AHEAD = 2
ROW_SLOTS = 4


def _expert_kernel(blk0_ref, cnt_ref, x_hbm, w1_ref, w3_ref, w2_ref, y_hbm,
                   xbuf, ybuf, w1b, w3b, w2b, sem_in, sem_out):
    e = pl.program_id(0)
    cnt = cnt_ref[e]
    nblk = lax.shift_right_logical(cnt + (BM - 1), BM.bit_length() - 1)
    blk0 = blk0_ref[e]
    last_e = NEXP - 1
    n_used = blk0_ref[last_e] + lax.shift_right_logical(cnt_ref[last_e] + (BM - 1),
                                                        BM.bit_length() - 1)
    blk_rows = BM * ROW_TILE

    def block_rows(ref, g):
        return ref.at[pl.ds(pl.multiple_of(g * blk_rows, blk_rows), blk_rows)]

    def in_copy(g):
        slot = lax.rem(g, ROW_SLOTS)
        return pltpu.make_async_copy(block_rows(x_hbm, g), xbuf.at[slot], sem_in.at[slot])

    def out_copy(g):
        slot = lax.rem(g, ROW_SLOTS)
        return pltpu.make_async_copy(ybuf.at[slot], block_rows(y_hbm, g), sem_out.at[slot])

    @pl.when(nblk > 0)
    def _():
        @pl.when(blk0 == 0)
        def _():
            for g0 in range(READ_AHEAD):
                @pl.when(g0 < n_used)
                def _():
                    in_copy(g0).start(priority=ROW_DMA_PRIORITY)

        w1b[...] = w1_ref[0, 0].astype(BF16)
        w3b[...] = w3_ref[0, 0].astype(BF16)
        w2b[...] = w2_ref[0, 0].astype(BF16)

        def chain(j, nb):
            g = blk0 + j
            for q in range(nb):
                @pl.when(g + READ_AHEAD + q < n_used)
                def _():
                    in_copy(g + READ_AHEAD + q).start(priority=ROW_DMA_PRIORITY)
            for q in range(nb):
                in_copy(g + q).wait()
            for q in range(nb):
                @pl.when(g + q >= ROW_SLOTS)
                def _():
                    out_copy(g + q - ROW_SLOTS).wait()

            row = lax.broadcasted_iota(jnp.int32, (BM, D), 0)
            xs = [jnp.where(row < cnt - (j + q) * BM,
                            _load_row_tiles(xbuf.at[lax.rem(g + q, ROW_SLOTS)]), 0.0).astype(BF16)
                  for q in range(nb)]
            x = xs[0] if nb == 1 else jnp.concatenate(xs, axis=0)
            h1 = _dot(x, w1b[...])
            h3 = _dot(x, w3b[...])
            act = (_silu(h1) * h3).astype(BF16)
            y = _dot(act, w2b[...])
            for q in range(nb):
                _store_row_tiles(ybuf.at[lax.rem(g + q, ROW_SLOTS)], y[q * BM:(q + 1) * BM, :])
                out_copy(g + q).start(priority=ROW_DMA_PRIORITY)

        def pair(p, carry):
            chain(2 * p, 2)
            return carry

        lax.fori_loop(0, lax.shift_right_logical(nblk, 1), pair, 0)

        @pl.when(lax.rem(nblk, 2) == 1)
        def _():
            chain(nblk - 1, 1)

    @pl.when(e == last_e)
    def _():
        for back in range(ROW_SLOTS, 0, -1):
            @pl.when(n_used >= back)
            def _():
                out_copy(n_used - back).wait()


def _experts(tbl, rows_buf, w1, w3, w2, layer):
    blk0, cnt = tbl[0, :NEXP], tbl[1, :NEXP]
    wspec = lambda shape: pl.BlockSpec((1, 1) + shape, lambda e, blk0, cnt: (layer, e, 0, 0))
    return pl.pallas_call(
        _expert_kernel,
        grid_spec=pltpu.PrefetchScalarGridSpec(
            num_scalar_prefetch=2,
            grid=(NEXP,),
            in_specs=[pl.BlockSpec(memory_space=pl.ANY),
                      wspec((D, FF)), wspec((D, FF)), wspec((FF, D))],
            out_specs=pl.BlockSpec(memory_space=pl.ANY),
            scratch_shapes=[pltpu.VMEM((ROW_SLOTS, BM * ROW_TILE, LANES), U32),
                            pltpu.VMEM((ROW_SLOTS, BM * ROW_TILE, LANES), U32),
                            pltpu.VMEM((D, FF), BF16), pltpu.VMEM((D, FF), BF16),
                            pltpu.VMEM((FF, D), BF16),
                            pltpu.SemaphoreType.DMA((ROW_SLOTS,)),
                            pltpu.SemaphoreType.DMA((ROW_SLOTS,))],
        ),
        out_shape=jax.ShapeDtypeStruct(rows_buf.shape, U32),
        compiler_params=pltpu.CompilerParams(
            dimension_semantics=("arbitrary",), vmem_limit_bytes=VMEM_LIMIT,
            has_side_effects=True),
        name="moe_experts",
    )(blk0, cnt, rows_buf, w1, w3, w2)


def _combine_kernel(dest_ref, dest_next_ref, x1_ref, route_ref, g2_ref, fng_ref, y_ref, o_ref,
                    ybuf, sem, *, final_norm, g2_row0, tiles_per_g2):
    tp = x1_ref.shape[0]
    i = pl.program_id(0)
    n = pl.num_programs(0)
    slot = lax.rem(i, 2)

    def gather(d_ref, s):
        def row_copy(r, k):
            return pltpu.make_async_copy(_row_tile(y_ref, d_ref[2 * r + k]),
                                         _row_tile(ybuf.at[s, k], r), sem.at[s])

        def issue(r, c):
            row_copy(r, 0).start(priority=0)
            row_copy(r, 1).start(priority=1)
            return c

        lax.fori_loop(0, tp, issue, 0, unroll=8)

    @pl.when(i == 0)
    def _():
        gather(dest_ref, 0)

    @pl.when(i + 1 < n)
    def _():
        gather(dest_next_ref, 1 - slot)

    for k in range(2):
        pltpu.make_async_copy(y_ref.at[pl.ds(0, tp * ROW_TILE)], ybuf.at[slot, k], sem.at[slot]).wait()

    route = route_ref[...]
    w_a = route[:, 2:3]
    w_b = route[:, 3:4]
    if tiles_per_g2:
        g2 = g2_ref[0, 0, pl.ds(g2_row0 + i // tiles_per_g2, 1), :]
    else:
        g2 = jnp.concatenate([g2_ref[0, 0]] * (tp // g2_ref.shape[2]), axis=0)
    y = (w_a * _load_row_tiles(ybuf.at[slot, 0]) + w_b * _load_row_tiles(ybuf.at[slot, 1]))
    x2 = x1_ref[...] + g2 * y
    if final_norm:
        x2 = _rms(x2, fng_ref[...])
    o_ref[...] = x2


def _combine(dest_flat, x1, route, mod, layer, fng, y_rows, tp, *, final_norm, g2_row0=0,
             tiles_per_g2=0, seqs_per_tile=0):
    nt = x1.shape[0] // tp
    if tiles_per_g2:
        g2_spec = pl.BlockSpec((1, 1, mod.shape[2], D), lambda i: (layer, 5, 0, 0))
    else:
        g2_spec = pl.BlockSpec((1, 1, seqs_per_tile, D), lambda i: (layer, 5, i, 0))
    return pl.pallas_call(
        functools.partial(_combine_kernel, final_norm=final_norm, g2_row0=g2_row0,
                          tiles_per_g2=tiles_per_g2),
        grid=(nt,),
        in_specs=[pl.BlockSpec((2 * tp,), lambda i: (i,), memory_space=pltpu.SMEM),
                  pl.BlockSpec((2 * tp,), lambda i: (jnp.minimum(i + 1, nt - 1),),
                               memory_space=pltpu.SMEM),
                  pl.BlockSpec((tp, D), lambda i: (i, 0)),
                  pl.BlockSpec((tp, ROUTE_COLS), lambda i: (i, 0)),
                  g2_spec,
                  pl.BlockSpec((1, D), lambda i: (0, 0)),
                  pl.BlockSpec(memory_space=pl.ANY)],
        out_specs=pl.BlockSpec((tp, D), lambda i: (i, 0)),
        out_shape=jax.ShapeDtypeStruct(x1.shape, F32),
        scratch_shapes=[pltpu.VMEM((2, 2, tp * ROW_TILE, LANES), U32),
                        pltpu.SemaphoreType.DMA((2,))],
        compiler_params=pltpu.CompilerParams(
            dimension_semantics=("arbitrary",), vmem_limit_bytes=VMEM_LIMIT),
        name="moe_combine",
    )(dest_flat, dest_flat, x1, route, mod, fng, y_rows)


def _layer_weights(l, p):
    hd = W // HEADS
    gi = jnp.arange(W) // (W // NG_A)
    gavg = jnp.where(gi[:, None] == gi[None, :], 1.0 / (W // NG_A), 0.0).astype(BF16)
    tril = jnp.tril(jnp.ones((CHUNK, CHUNK), dtype=bool))
    w_sp = jnp.where(tril[None], p["b_spatial_w"][l], 0.0)
    w_router = jnp.concatenate(
        [p["w_router_g"][l], p["w_router_e"][l],
         jnp.zeros((D, ROUTE_COLS - NGRP - NEXP), F32)], axis=1)
    b_router = jnp.concatenate(
        [p["b_router_g"][l], p["b_router_e"][l],
         jnp.zeros((ROUTE_COLS - NGRP - NEXP,), F32)])[None, :]
    return {
        "n1g": p["norm1_g"][l][None, :], "n2g": p["norm2_g"][l][None, :],
        "w_in": p["w_in"][l].astype(BF16),
        "a_dw_w": p["a_dw_w"][l], "a_dw_b": p["a_dw_b"][l][None, :],
        "a_norm_g": p["a_norm_g"][l][None, :], "a_norm_b": p["a_norm_b"][l][None, :],
        "b_norm_g": p["b_norm_g"][l][None, :], "b_norm_b": p["b_norm_b"][l][None, :],
        "w_sp": jnp.concatenate([w_sp[0::2], w_sp[1::2]], axis=2).astype(BF16),
        "b_sp": jnp.repeat(p["b_spatial_b"][l].T, hd, axis=1),
        "c_conv_w": p["c_conv_w"][l],
        "w_branch": p["w_branch"][l].astype(BF16), "w_out": p["w_out"][l].astype(BF16),
        "gavg": gavg, "w_router": w_router.astype(BF16), "b_router": b_router,
    }


def _sample_spatial(l, p, ts):
    hd = W // HEADS
    w = p["b_spatial_w"][l][:, :ts, :ts]
    w = jnp.where(jnp.tril(jnp.ones((ts, ts), dtype=bool))[None], w, 0.0)
    w4 = jnp.repeat(jnp.transpose(w, (1, 2, 0)).reshape(ts * ts, HEADS), hd, axis=1)
    b4 = jnp.repeat(p["b_spatial_b"][l][:, :ts].T, hd, axis=1)
    return w4, b4


def kernel(x_prompt, x_sample, c_prompt, c_sample, state_conv_a, state_conv_c, norm1_g, norm2_g, w_ada, b_ada, w_in, a_dw_w, a_dw_b, a_norm_g, a_norm_b, b_norm_g, b_norm_b, b_spatial_w, b_spatial_b, c_conv_w, w_branch, w_out, w_router_g, b_router_g, w_router_e, b_router_e, w1, w3, w2, final_norm_g):
    p = dict(norm1_g=norm1_g, norm2_g=norm2_g, w_in=w_in, a_dw_w=a_dw_w, a_dw_b=a_dw_b,
             a_norm_g=a_norm_g, a_norm_b=a_norm_b, b_norm_g=b_norm_g, b_norm_b=b_norm_b,
             b_spatial_w=b_spatial_w, b_spatial_b=b_spatial_b, c_conv_w=c_conv_w,
             w_branch=w_branch, w_out=w_out, w_router_g=w_router_g, b_router_g=b_router_g,
             w_router_e=w_router_e, b_router_e=b_router_e)
    depth = w_in.shape[0]
    nb_, seq, _ = x_prompt.shape
    ns, ts, _ = x_sample.shape
    n_tok = nb_ * seq + ns * ts
    nblocks = -(-(2 * n_tok) // BM) + NEXP
    nsplit = 2
    nbs = ns // nsplit
    tps = ts * nbs
    assert seq % TT == 0 and TT % CHUNK == 0 and seq % TP == 0 and (nb_ * seq) % TPP == 0
    assert (ts * ns) % TP == 0 and ns % SUBLANES == 0

    mod = _adaln(jnp.concatenate([c_sample, c_prompt], axis=0), w_ada, b_ada)

    xp = x_prompt.reshape(nb_ * seq, D)
    xs = jnp.transpose(x_sample.reshape(nsplit, nbs, ts, D), (0, 2, 1, 3)).reshape(ts * ns, D)
    fng = final_norm_g[None, :]
    pa, pc, sa, sc, sv = [], [], [], [], []
    for l in range(depth):
        lw = _layer_weights(l, p)
        lw["w_sp4"], lw["b_sp4"] = _sample_spatial(l, p, ts)

        x1p, h2p, route_p, ha, hc = _mixer_prompt(xp, mod, l, nb_, ns, lw)
        pa.append(ha)
        pc.append(hc)
        x1s, h2s, route_s, ha_s, hc_s, v_s = _mixer_sample(
            xs, mod, l, jnp.transpose(state_conv_a[l], (1, 0, 2)),
            jnp.transpose(state_conv_c[l], (1, 0, 2)), lw, nsplit)
        sa.append(jnp.transpose(ha_s, (1, 0, 2)))
        sc.append(jnp.transpose(hc_s, (1, 0, 2)))
        sv.append(jnp.transpose(v_s, (1, 0, 2)))

        dest_p, dest_s, tbl = _plan(route_p, route_s)
        dest_p, dest_s = dest_p.reshape(-1), dest_s.reshape(-1)
        rows_buf = _scatter_rows(dest_p, h2p, None, nblocks * BM, TP)
        rows_buf = _scatter_rows(dest_s, h2s, rows_buf, nblocks * BM, tps)
        y_rows = _experts(tbl, rows_buf, w1, w3, w2, l)

        last = l == depth - 1
        xp = _combine(dest_p, x1p, route_p, mod, l, fng, y_rows, TP, final_norm=last,
                      g2_row0=ns, tiles_per_g2=seq // TP)
        xs = _combine(dest_s, x1s, route_s, mod, l, fng, y_rows, tps, final_norm=last,
                      seqs_per_tile=nbs)

    y_prompt = xp.reshape(nb_, seq, D)
    y_sample = jnp.transpose(xs.reshape(nsplit, ts, nbs, D), (0, 2, 1, 3)).reshape(ns, ts, D)
    return (y_prompt, y_sample, jnp.stack(pa), jnp.stack(pc), jnp.stack(sa), jnp.stack(sc),
            jnp.stack(sv))
```

```python
import functools

import jax
import jax.numpy as jnp
from jax import lax
from jax.experimental import pallas as pl
from jax.experimental.pallas import tpu as pltpu

F32 = jnp.float32
BF16 = jnp.bfloat16

D = 1024
W = D // 2
KA = 31
KC = 3
NG_A = 8
CHUNK = 128
HEADS = 8
NGRP = 4
EPG = 8
NEXP = NGRP * EPG
FF = D // 2
EPS = 1e-6
IN_COLS = 7 * W + 3 * D
LANES = 128
ROUTE_COLS = LANES

TT = 512
SUBLANES = 8
A_HDR = 32
A_PAD = SUBLANES
A_ROWS = 128
C_HDR = 8
TP = 512
TPP = 2048
BM = 256
VMEM_LIMIT = 56 * 1024 * 1024

NEG = -1e30


def _sigmoid(x):
    return 0.5 * jnp.tanh(0.5 * x) + 0.5


def _silu(x):
    h = 0.5 * x
    return h + h * jnp.tanh(h)


GELU_C = 0.7978845608028654


def _gelu_tanh(x):
    h = 0.5 * x
    return h + h * jnp.tanh(x * (GELU_C + (GELU_C * 0.044715) * (x * x)))


def _rms(x, g):
    return x * lax.rsqrt(jnp.mean(x * x, axis=-1, keepdims=True) + EPS) * g


def _rms_mod(x, g, scale, shift):
    inv = lax.rsqrt(jnp.mean(x * x, axis=-1, keepdims=True) + EPS)
    return x * inv * (g * (1.0 + scale)) + shift


def _dot(a, b):
    return jnp.dot(a, b, preferred_element_type=F32)


def _dot_hi(a, b):
    return jnp.dot(a, b, preferred_element_type=F32, precision=lax.Precision.HIGHEST)


U32 = jnp.uint32
ROW_TILE = D // (2 * LANES)


def _pack_bf16_pair(lo, hi):
    def rne(x):
        b = pltpu.bitcast(x, U32)
        return b + (U32(0x7FFF) + ((b >> 16) & U32(1)))
    return (rne(lo) >> 16) | (rne(hi) & U32(0xFFFF0000))


def _store_row_tiles(ref, val):
    r = val.shape[0]
    for j in range(ROW_TILE):
        lo = val[:, 2 * j * LANES:(2 * j + 1) * LANES]
        hi = val[:, (2 * j + 1) * LANES:(2 * j + 2) * LANES]
        ref[pl.ds(j, r, stride=ROW_TILE), :] = _pack_bf16_pair(lo, hi)


def _load_row_tiles(ref):
    r = ref.shape[0] // ROW_TILE
    parts = []
    for j in range(ROW_TILE):
        w = ref[pl.ds(j, r, stride=ROW_TILE), :]
        parts.append(pltpu.bitcast(w << 16, F32))
        parts.append(pltpu.bitcast(w & U32(0xFFFF0000), F32))
    return jnp.concatenate(parts, axis=1)


def _row_tile(ref, r):
    return ref.at[pl.ds(pl.multiple_of(r * ROW_TILE, ROW_TILE), ROW_TILE)]


def _mod_kernel(c_ref, w_ref, b_ref, o_ref):
    c = c_ref[...]
    s = _silu(c).astype(BF16)
    o_ref[0, 0] = _dot(s, w_ref[0].astype(BF16)) + b_ref[0]


def _adaln(c_all, w_ada, b_ada):
    nl = w_ada.shape[0]
    r = c_all.shape[0]
    return pl.pallas_call(
        _mod_kernel,
        grid=(nl, 6),
        in_specs=[
            pl.BlockSpec((r, D), lambda l, j: (0, 0)),
            pl.BlockSpec((1, D, D), lambda l, j: (l, 0, j)),
            pl.BlockSpec((1, 1, D), lambda l, j: (l, 0, j)),
        ],
        out_specs=pl.BlockSpec((1, 1, r, D), lambda l, j: (l, j, 0, 0)),
        out_shape=jax.ShapeDtypeStruct((nl, 6, r, D), F32),
        compiler_params=pltpu.CompilerParams(
            dimension_semantics=("arbitrary", "arbitrary"), vmem_limit_bytes=VMEM_LIMIT),
        name="adaln_mod",
    )(c_all, w_ada, b_ada.reshape(nl, 1, 6 * D))


def _group_norm_silu(a, gavg_ref, g, b):
    mu = _dot(a.astype(BF16), gavg_ref[...])
    xc = a - mu
    var = _dot((xc * xc).astype(BF16), gavg_ref[...])
    y = xc * lax.rsqrt(var + EPS) * g + b
    return _silu(y)


def _layer_norm(x, g, b):
    mu = jnp.mean(x, axis=-1, keepdims=True)
    xc = x - mu
    var = jnp.mean(xc * xc, axis=-1, keepdims=True)
    return xc * lax.rsqrt(var + EPS) * g + b


def _merge_and_route(x, a, bb, cc, gates_fn, mod, n2g, wbr_ref, wout_ref, wr_ref, br_ref,
                     x1_ref, h2_ref, route_ref):
    g1, sh2, sc2 = mod
    acc = None
    for gi, br_in in enumerate((a, bb, cc)):
        br = _dot(br_in.astype(BF16), wbr_ref[gi])
        term = _sigmoid(gates_fn(gi)) * br
        acc = term if acc is None else acc + term
    mixed = _dot(acc.astype(BF16), wout_ref[...])
    x1 = x + g1 * mixed
    x1_ref[...] = x1
    h2 = _rms_mod(x1, n2g, sc2, sh2)
    _store_row_tiles(h2_ref, h2)

    logits = _dot(h2.astype(BF16), wr_ref[...]) + br_ref[...]
    rows = logits.shape[0]
    lane = lax.broadcasted_iota(jnp.int32, (rows, ROUTE_COLS), 1).astype(F32)
    is_g = lane < NGRP
    gl = jnp.where(is_g, logits, NEG)
    gmax = jnp.max(gl, axis=-1, keepdims=True)
    gsel = jnp.min(jnp.where(gl == gmax, lane, float(ROUTE_COLS)), axis=-1, keepdims=True)
    gden = jnp.sum(jnp.where(is_g, jnp.exp(gl - gmax), 0.0), axis=-1, keepdims=True)
    gprob = 1.0 / gden
    lo = NGRP + EPG * gsel
    in_grp = jnp.logical_and(lane >= lo, lane < lo + EPG)
    el = jnp.where(in_grp, logits, NEG)
    v1 = jnp.max(el, axis=-1, keepdims=True)
    i1 = jnp.min(jnp.where(el == v1, lane, float(ROUTE_COLS)), axis=-1, keepdims=True)
    el2 = jnp.where(lane == i1, NEG, el)
    v2 = jnp.max(el2, axis=-1, keepdims=True)
    i2 = jnp.min(jnp.where(el2 == v2, lane, float(ROUTE_COLS)), axis=-1, keepdims=True)
    p2 = jnp.exp(v2 - v1)
    w_a = gprob / (1.0 + p2)
    w_b = gprob * p2 / (1.0 + p2)
    route = jnp.where(lane == 0.0, i1 - NGRP,
                      jnp.where(lane == 1.0, i2 - NGRP,
                                jnp.where(lane == 2.0, w_a,
                                          jnp.where(lane == 3.0, w_b, 0.0))))
    route_ref[...] = route


def _mixer_prompt_kernel(x_ref, mod_ref, n1g_ref, n2g_ref, win_ref, adw_ref, adb_ref, ang_ref,
                         anb_ref, bng_ref, bnb_ref, wsp_ref, bsp_ref, ccw_ref, wbr_ref, wout_ref,
                         gavg_ref, wr_ref, br_ref,
                         x1_ref, h2_ref, route_ref, ha_ref, hc_ref,
                         abuf, cbuf, aconv, *, mod_row0):
    first = pl.program_id(1) == 0
    abuf[0:A_HDR, :] = jnp.where(first, 0.0, abuf[TT:TT + A_HDR, :])
    abuf[A_HDR + TT:A_HDR + TT + A_PAD, :] = jnp.zeros((A_PAD, W), F32)
    cbuf[0:C_HDR, :] = jnp.where(first, 0.0, cbuf[TT:TT + C_HDR, :])

    x = x_ref[...]
    mrow = mod_row0 + pl.program_id(0)
    sh1, sc1, g1, sh2, sc2 = (mod_ref[0, s, pl.ds(mrow, 1), :] for s in range(5))
    h = _rms_mod(x, n1g_ref[...], sc1, sh1).astype(BF16)

    def zsec(lo, hi):
        return _dot(h, win_ref[:, lo:hi])

    a_glu = zsec(0, W) * _sigmoid(zsec(W, 2 * W))
    abuf[A_HDR:A_HDR + TT, :] = a_glu
    off = A_HDR - (KA - 1)
    rc = A_ROWS

    def conv_pass(p):
        lt, half = p // 2, p % 2
        ls = slice(lt * LANES, (lt + 1) * LANES)
        for r0 in range(half * (TT // 2), (half + 1) * (TT // 2), rc):
            y = None
            for r in range(SUBLANES):
                z = None
                for q in range((KA - 1 + off) // SUBLANES + 1):
                    k = SUBLANES * q + r - off
                    if 0 <= k < KA:
                        rows = slice(r0 + SUBLANES * q, r0 + SUBLANES * q + rc + SUBLANES)
                        term = adw_ref[k:k + 1, ls] * abuf[rows, ls]
                        z = term if z is None else z + term
                part = z[r:r + rc, :]
                y = part if y is None else y + part
            aconv[r0:r0 + rc, ls] = y + adb_ref[:, ls]

    conv_pass(0)
    u = _gelu_tanh(zsec(2 * W, 3 * W))
    conv_pass(1)
    v = _layer_norm(_gelu_tanh(zsec(3 * W, 4 * W)), bng_ref[...], bnb_ref[...])
    conv_pass(2)

    cch = zsec(5 * W, 6 * W) * zsec(6 * W, 7 * W)
    cbuf[C_HDR:C_HDR + TT, :] = cch
    coff = C_HDR - (KC - 1)
    ych = (ccw_ref[0:1, :] * cbuf[coff:coff + TT, :]
           + ccw_ref[1:2, :] * cbuf[coff + 1:coff + 1 + TT, :]
           + ccw_ref[2:3, :] * cbuf[coff + 2:coff + 2 + TT, :])
    conv_pass(3)
    cc = zsec(4 * W, 5 * W) * ych
    hc_ref[0] = cbuf[C_HDR + TT - (KC - 1):C_HDR + TT, :]
    conv_pass(4)
    gate_z = [zsec(7 * W, 7 * W + D)]
    conv_pass(5)
    gate_z.append(zsec(7 * W + D, 7 * W + 2 * D))
    conv_pass(6)
    gate_z.append(zsec(7 * W + 2 * D, 7 * W + 3 * D))
    conv_pass(7)
    ha_ref[0] = abuf[A_HDR + TT - (KA - 1):A_HDR + TT, :]

    vb = v.astype(BF16)
    low_head = lax.broadcasted_iota(jnp.int32, (CHUNK, LANES), 1) < (W // HEADS)
    zero_b = jnp.zeros((CHUNK, LANES), BF16)
    s_parts = []
    for c in range(TT // CHUNK):
        tiles = []
        for lt in range(W // LANES):
            vt = vb[c * CHUNK:(c + 1) * CHUNK, lt * LANES:(lt + 1) * LANES]
            stacked = jnp.concatenate([jnp.where(low_head, vt, zero_b),
                                       jnp.where(low_head, zero_b, vt)], axis=0)
            tiles.append(_dot(wsp_ref[lt], stacked))
        s_parts.append(jnp.concatenate(tiles, axis=1) + bsp_ref[...])
    bb = u * jnp.concatenate(s_parts, axis=0)
    a = _group_norm_silu(aconv[...], gavg_ref, ang_ref[...], anb_ref[...])

    _merge_and_route(x, a, bb, cc, lambda gi: gate_z[gi], (g1, sh2, sc2), n2g_ref[...], wbr_ref,
                     wout_ref, wr_ref, br_ref, x1_ref, h2_ref, route_ref)


def _const_spec(shape):
    nd = len(shape)
    return pl.BlockSpec(shape, lambda *_: (0,) * nd, pipeline_mode=pl.Buffered(1))


def _mixer_prompt(x, mod, layer, n, mod_row0, lw):
    t = x.shape[0] // n
    nt = t // TT
    mod_rows = mod.shape[2]
    row_spec = lambda cols: pl.BlockSpec((TT, cols), lambda i, j: (i * nt + j, 0))
    consts = [lw["n1g"], lw["n2g"], lw["w_in"], lw["a_dw_w"], lw["a_dw_b"], lw["a_norm_g"],
              lw["a_norm_b"], lw["b_norm_g"], lw["b_norm_b"], lw["w_sp"], lw["b_sp"], lw["c_conv_w"],
              lw["w_branch"], lw["w_out"], lw["gavg"], lw["w_router"], lw["b_router"]]
    return pl.pallas_call(
        functools.partial(_mixer_prompt_kernel, mod_row0=mod_row0),
        grid=(n, nt),
        in_specs=[row_spec(D),
                  pl.BlockSpec((1, 6, mod_rows, D), lambda i, j: (layer, 0, 0, 0),
                               pipeline_mode=pl.Buffered(1))]
        + [_const_spec(c.shape) for c in consts],
        out_specs=[row_spec(D),
                   pl.BlockSpec((TT * ROW_TILE, LANES), lambda i, j: (i * nt + j, 0)),
                   row_spec(ROUTE_COLS),
                   pl.BlockSpec((1, KA - 1, W), lambda i, j: (i, 0, 0)),
                   pl.BlockSpec((1, KC - 1, W), lambda i, j: (i, 0, 0))],
        out_shape=[jax.ShapeDtypeStruct((n * t, D), F32),
                   jax.ShapeDtypeStruct((n * t * ROW_TILE, LANES), U32),
                   jax.ShapeDtypeStruct((n * t, ROUTE_COLS), F32),
                   jax.ShapeDtypeStruct((n, KA - 1, W), F32),
                   jax.ShapeDtypeStruct((n, KC - 1, W), F32)],
        scratch_shapes=[pltpu.VMEM((A_HDR + TT + A_PAD, W), F32),
                        pltpu.VMEM((C_HDR + TT, W), F32),
                        pltpu.VMEM((TT, W), F32)],
        compiler_params=pltpu.CompilerParams(
            dimension_semantics=("arbitrary", "arbitrary"), vmem_limit_bytes=VMEM_LIMIT),
        name="mixer_prompt",
    )(x, mod, *consts)


def _mixer_sample_kernel(x_ref, mod_ref, hista_ref, histc_ref, n1g_ref, n2g_ref, win_ref, adw_ref,
                         adb_ref, ang_ref, anb_ref, bng_ref, bnb_ref, wsp4_ref, bsp4_ref,
                         ccw_ref, wbr_ref, wout_ref, gavg_ref, wr_ref, br_ref,
                         x1_ref, h2_ref, route_ref, ha_ref, hc_ref, v_ref):
    ts, nb = v_ref.shape[0], v_ref.shape[1]
    x = x_ref[...]

    def modrow(i):
        return jnp.concatenate([mod_ref[0, i]] * ts, axis=0)

    h = (_rms(x, n1g_ref[...]) * (1.0 + modrow(1)) + modrow(0)).astype(BF16)

    def zsec(lo, hi):
        return _dot(h, win_ref[:, lo:hi])

    def tslab(arr, j):
        return arr[j * nb:(j + 1) * nb, :]

    a_glu = zsec(0, W) * _sigmoid(zsec(W, 2 * W))
    kh = KA - 1
    a_conv = [jnp.zeros((nb, W), F32) + adb_ref[...] for _ in range(ts)]
    for r in range(kh):
        slab = hista_ref[0, :, r, :]
        for tq in range(min(r, ts - 1) + 1):
            a_conv[tq] = a_conv[tq] + adw_ref[r - tq:r - tq + 1, :] * slab
    for tq in range(ts):
        for j in range(tq + 1):
            a_conv[tq] = a_conv[tq] + adw_ref[kh + j - tq:kh + j - tq + 1, :] * tslab(a_glu, j)
    a = _group_norm_silu(jnp.concatenate(a_conv, axis=0), gavg_ref, ang_ref[...], anb_ref[...])
    ha_ref[:, 0:kh - ts, :] = hista_ref[0, :, ts:kh, :]
    for j in range(ts):
        ha_ref[:, kh - ts + j, :] = tslab(a_glu, j)

    u = _gelu_tanh(zsec(2 * W, 3 * W))
    v = _layer_norm(_gelu_tanh(zsec(3 * W, 4 * W)), bng_ref[...], bnb_ref[...])
    for j in range(ts):
        v_ref[j] = tslab(v, j)
    s_rows = []
    for tq in range(ts):
        s = jnp.zeros((nb, W), F32) + bsp4_ref[tq:tq + 1, :]
        for sq in range(tq + 1):
            s = s + wsp4_ref[tq * ts + sq:tq * ts + sq + 1, :] * tslab(v, sq)
        s_rows.append(s)
    bb = u * jnp.concatenate(s_rows, axis=0)

    cch = zsec(5 * W, 6 * W) * zsec(6 * W, 7 * W)
    xp = [histc_ref[r] for r in range(KC - 1)] + [tslab(cch, j) for j in range(ts)]
    ych = jnp.concatenate(
        [sum(ccw_ref[k:k + 1, :] * xp[tq + k] for k in range(KC)) for tq in range(ts)], axis=0)
    cc = zsec(4 * W, 5 * W) * ych
    for r in range(KC - 1):
        hc_ref[r] = xp[ts + r]

    def gates_fn(gi):
        return zsec(7 * W + gi * D, 7 * W + (gi + 1) * D)

    _merge_and_route(x, a, bb, cc, gates_fn, (modrow(2), modrow(3), modrow(4)), n2g_ref[...],
                     wbr_ref, wout_ref, wr_ref, br_ref, x1_ref, h2_ref, route_ref)


def _mixer_sample(x_rows, mod, layer, state_a, hist_c_tm, lw, nsplit):
    n = state_a.shape[1]
    ts = x_rows.shape[0] // n
    nb = n // nsplit
    rows = ts * nb
    consts = [lw["n1g"], lw["n2g"], lw["w_in"], lw["a_dw_w"], lw["a_dw_b"],
              lw["a_norm_g"], lw["a_norm_b"], lw["b_norm_g"], lw["b_norm_b"], lw["w_sp4"], lw["b_sp4"],
              lw["c_conv_w"], lw["w_branch"], lw["w_out"], lw["gavg"], lw["w_router"], lw["b_router"]]
    seq3 = lambda k, cols: pl.BlockSpec((k, nb, cols), lambda i: (0, i, 0))
    row_spec = lambda cols: pl.BlockSpec((rows, cols), lambda i: (i, 0))
    return pl.pallas_call(
        _mixer_sample_kernel,
        grid=(nsplit,),
        in_specs=[row_spec(D), pl.BlockSpec((1, 6, nb, D), lambda i: (layer, 0, i, 0)),
                  pl.BlockSpec((1, nb, KA - 1, W), lambda i: (layer, i, 0, 0)), seq3(KC - 1, W)]
        + [_const_spec(c.shape) for c in consts],
        out_specs=[row_spec(D),
                   pl.BlockSpec((rows * ROW_TILE, LANES), lambda i: (i, 0)),
                   row_spec(ROUTE_COLS),
                   pl.BlockSpec((nb, KA - 1, W), lambda i: (i, 0, 0)), seq3(KC - 1, W), seq3(ts, W)],
        out_shape=[jax.ShapeDtypeStruct((ts * n, D), F32),
                   jax.ShapeDtypeStruct((ts * n * ROW_TILE, LANES), U32),
                   jax.ShapeDtypeStruct((ts * n, ROUTE_COLS), F32),
                   jax.ShapeDtypeStruct((n, KA - 1, W), F32),
                   jax.ShapeDtypeStruct((KC - 1, n, W), F32),
                   jax.ShapeDtypeStruct((ts, n, W), F32)],
        compiler_params=pltpu.CompilerParams(
            dimension_semantics=("arbitrary",), vmem_limit_bytes=VMEM_LIMIT),
        name="mixer_sample",
    )(x_rows, mod, state_a, hist_c_tm, *consts)


def _plan_kernel(rp_ref, rs_ref, tri_ref, destp_ref, dests_ref, tbl_ref, carry, pstart, *, ntp):
    ph = pl.program_id(0)
    i = pl.program_id(1)
    nt = pl.num_programs(1)
    lane = lax.broadcasted_iota(jnp.int32, (TP, ROUTE_COLS), 1).astype(F32)

    @pl.when(jnp.logical_and(ph == 0, i == 0))
    def _():
        carry[...] = jnp.zeros_like(carry)

    def sub_tile(r, dest_ref, row0):
        hot_a = lane == r[:, 0:1]
        hot_b = lane == r[:, 1:2]
        s = jnp.where(jnp.logical_or(hot_a, hot_b), 1.0, 0.0)

        @pl.when(ph == 1)
        def _():
            pre = _dot(tri_ref[...], s.astype(BF16)) + carry[...] + pstart[...]
            d_a = jnp.sum(jnp.where(hot_a, pre, 0.0), axis=-1, keepdims=True)
            d_b = jnp.sum(jnp.where(hot_b, pre, 0.0), axis=-1, keepdims=True)
            l2 = lax.broadcasted_iota(jnp.int32, (TP, 2), 1)
            dest_ref[row0:row0 + TP, :] = jnp.where(l2 == 0, d_a, d_b).astype(jnp.int32)

        carry[...] = carry[...] + jnp.sum(s, axis=0, keepdims=True)

    @pl.when(i < ntp)
    def _():
        for sub in range(rp_ref.shape[0] // TP):
            sub_tile(rp_ref[sub * TP:(sub + 1) * TP, :], destp_ref, sub * TP)

    @pl.when(i == ntp)
    def _():
        for sub in range(rs_ref.shape[0] // TP):
            sub_tile(rs_ref[sub * TP:(sub + 1) * TP, :], dests_ref, sub * TP)

    @pl.when(jnp.logical_and(ph == 0, i == nt - 1))
    def _():
        cnt = carry[...]
        nblk = jnp.floor((cnt + (BM - 1)) * (1.0 / BM))
        ri = lax.broadcasted_iota(jnp.int32, (ROUTE_COLS, ROUTE_COLS), 0)
        ci = lax.broadcasted_iota(jnp.int32, (ROUTE_COLS, ROUTE_COLS), 1)
        upper = jnp.where(ri <= ci, 1.0, 0.0)
        cum = _dot_hi(jnp.broadcast_to(nblk, (8, ROUTE_COLS)), upper)[0:1, :]
        cum_ex = cum - nblk
        pstart[...] = cum_ex * BM
        row = lax.broadcasted_iota(jnp.int32, (SUBLANES, ROUTE_COLS), 0)
        tbl = jnp.where(row == 0, cum_ex, jnp.where(row == 1, cnt, 0.0))
        tbl_ref[...] = tbl.astype(jnp.int32)
        carry[...] = jnp.zeros_like(carry)


def _plan(route_p, route_s):
    rows_p, rows_s = route_p.shape[0], route_s.shape[0]
    ntp = rows_p // TPP
    nbp = SUBLANES
    ri = lax.broadcasted_iota(jnp.int32, (TP, TP), 0)
    ci = lax.broadcasted_iota(jnp.int32, (TP, TP), 1)
    tri = jnp.where(ci < ri, 1.0, 0.0).astype(BF16)
    return pl.pallas_call(
        functools.partial(_plan_kernel, ntp=ntp),
        grid=(2, ntp + 1),
        in_specs=[pl.BlockSpec((TPP, ROUTE_COLS), lambda p, i: (jnp.minimum(i, ntp - 1), 0)),
                  pl.BlockSpec((rows_s, ROUTE_COLS), lambda p, i: (0, 0)),
                  pl.BlockSpec((TP, TP), lambda p, i: (0, 0))],
        out_specs=[pl.BlockSpec((TPP, 2), lambda p, i: (jnp.minimum(i, ntp - 1) * p, 0)),
                   pl.BlockSpec((rows_s, 2), lambda p, i: (0, 0)),
                   pl.BlockSpec((nbp, ROUTE_COLS), lambda p, i: (0, 0))],
        out_shape=[jax.ShapeDtypeStruct((rows_p, 2), jnp.int32),
                   jax.ShapeDtypeStruct((rows_s, 2), jnp.int32),
                   jax.ShapeDtypeStruct((nbp, ROUTE_COLS), jnp.int32)],
        scratch_shapes=[pltpu.VMEM((1, ROUTE_COLS), F32), pltpu.VMEM((1, ROUTE_COLS), F32)],
        compiler_params=pltpu.CompilerParams(
            dimension_semantics=("arbitrary", "arbitrary"), vmem_limit_bytes=VMEM_LIMIT),
        name="moe_plan",
    )(route_p, route_s, tri)


def _scatter_kernel(dest_ref, h_ref, *rest):
    rows_ref, sem = rest[-2:]
    tp = h_ref.shape[0] // ROW_TILE

    def row_copy(r, k):
        d = dest_ref[2 * r + k]
        return pltpu.make_async_copy(_row_tile(h_ref, r), _row_tile(rows_ref, d), sem)

    def issue(r, c):
        row_copy(r, 0).start(priority=0)
        row_copy(r, 1).start(priority=1)
        return c

    lax.fori_loop(0, tp, issue, 0, unroll=8)

    def drain(r, c):
        row_copy(r, 0).wait()
        row_copy(r, 1).wait()
        return c

    lax.fori_loop(0, tp, drain, 0, unroll=8)


def _scatter_rows(dest_flat, h2, rows_buf, n_rows, tp):
    nt = h2.shape[0] // (tp * ROW_TILE)
    in_specs = [pl.BlockSpec((2 * tp,), lambda i: (i,), memory_space=pltpu.SMEM),
                pl.BlockSpec((tp * ROW_TILE, LANES), lambda i: (i, 0))]
    args = [dest_flat, h2]
    aliases = {}
    if rows_buf is not None:
        in_specs.append(pl.BlockSpec(memory_space=pl.ANY))
        args.append(rows_buf)
        aliases = {2: 0}
    return pl.pallas_call(
        _scatter_kernel,
        grid=(nt,),
        in_specs=in_specs,
        out_specs=pl.BlockSpec(memory_space=pl.ANY),
        out_shape=jax.ShapeDtypeStruct((n_rows * ROW_TILE, LANES), U32),
        scratch_shapes=[pltpu.SemaphoreType.DMA(())],
        input_output_aliases=aliases,
        compiler_params=pltpu.CompilerParams(
            dimension_semantics=("arbitrary",), vmem_limit_bytes=VMEM_LIMIT,
            has_side_effects=True),
        name="moe_scatter",
    )(*args)


ROW_DMA_PRIORITY = 1
READ_AHEAD = 3
ROW_SLOTS = 5


def _expert_kernel(blk0_ref, cnt_ref, x_hbm, w1_ref, w3_ref, w2_ref, y_hbm,
                   xbuf, ybuf, w1b, w3b, w2b, sem_in, sem_out):
    e = pl.program_id(0)
    cnt = cnt_ref[e]
    nblk = lax.shift_right_logical(cnt + (BM - 1), BM.bit_length() - 1)
    blk0 = blk0_ref[e]
    last_e = NEXP - 1
    n_used = blk0_ref[last_e] + lax.shift_right_logical(cnt_ref[last_e] + (BM - 1),
                                                        BM.bit_length() - 1)
    blk_rows = BM * ROW_TILE

    def block_rows(ref, g):
        return ref.at[pl.ds(pl.multiple_of(g * blk_rows, blk_rows), blk_rows)]

    def in_copy(g):
        slot = lax.rem(g, ROW_SLOTS)
        return pltpu.make_async_copy(block_rows(x_hbm, g), xbuf.at[slot], sem_in.at[slot])

    def out_copy(g):
        slot = lax.rem(g, ROW_SLOTS)
        return pltpu.make_async_copy(ybuf.at[slot], block_rows(y_hbm, g), sem_out.at[slot])

    @pl.when(nblk > 0)
    def _():
        @pl.when(blk0 == 0)
        def _():
            for g0 in range(READ_AHEAD):
                @pl.when(g0 < n_used)
                def _():
                    in_copy(g0).start(priority=ROW_DMA_PRIORITY)

        w1b[...] = w1_ref[0, 0].astype(BF16)
        w3b[...] = w3_ref[0, 0].astype(BF16)
        w2b[...] = w2_ref[0, 0].astype(BF16)

        def chain(j, nb):
            g = blk0 + j
            for q in range(nb):
                @pl.when(g + READ_AHEAD + q < n_used)
                def _():
                    in_copy(g + READ_AHEAD + q).start(priority=ROW_DMA_PRIORITY)
            for q in range(nb):
                in_copy(g + q).wait()
            for q in range(nb):
                @pl.when(g + q >= ROW_SLOTS)
                def _():
                    out_copy(g + q - ROW_SLOTS).wait()

            row = lax.broadcasted_iota(jnp.int32, (BM, D), 0)
            xs = [jnp.where(row < cnt - (j + q) * BM,
                            _load_row_tiles(xbuf.at[lax.rem(g + q, ROW_SLOTS)]), 0.0).astype(BF16)
                  for q in range(nb)]
            x = xs[0] if nb == 1 else jnp.concatenate(xs, axis=0)
            h1 = _dot(x, w1b[...])
            h3 = _dot(x, w3b[...])
            act = (_silu(h1) * h3).astype(BF16)
            y = _dot(act, w2b[...])
            for q in range(nb):
                _store_row_tiles(ybuf.at[lax.rem(g + q, ROW_SLOTS)], y[q * BM:(q + 1) * BM, :])
                out_copy(g + q).start(priority=ROW_DMA_PRIORITY)

        def pair(p, carry):
            chain(2 * p, 2)
            return carry

        lax.fori_loop(0, lax.shift_right_logical(nblk, 1), pair, 0)

        @pl.when(lax.rem(nblk, 2) == 1)
        def _():
            chain(nblk - 1, 1)

    @pl.when(e == last_e)
    def _():
        for back in range(ROW_SLOTS, 0, -1):
            @pl.when(n_used >= back)
            def _():
                out_copy(n_used - back).wait()


def _experts(tbl, rows_buf, w1, w3, w2, layer):
    blk0, cnt = tbl[0, :NEXP], tbl[1, :NEXP]
    wspec = lambda shape: pl.BlockSpec((1, 1) + shape, lambda e, blk0, cnt: (layer, e, 0, 0))
    return pl.pallas_call(
        _expert_kernel,
        grid_spec=pltpu.PrefetchScalarGridSpec(
            num_scalar_prefetch=2,
            grid=(NEXP,),
            in_specs=[pl.BlockSpec(memory_space=pl.ANY),
                      wspec((D, FF)), wspec((D, FF)), wspec((FF, D))],
            out_specs=pl.BlockSpec(memory_space=pl.ANY),
            scratch_shapes=[pltpu.VMEM((ROW_SLOTS, BM * ROW_TILE, LANES), U32),
                            pltpu.VMEM((ROW_SLOTS, BM * ROW_TILE, LANES), U32),
                            pltpu.VMEM((D, FF), BF16), pltpu.VMEM((D, FF), BF16),
                            pltpu.VMEM((FF, D), BF16),
                            pltpu.SemaphoreType.DMA((ROW_SLOTS,)),
                            pltpu.SemaphoreType.DMA((ROW_SLOTS,))],
        ),
        out_shape=jax.ShapeDtypeStruct(rows_buf.shape, U32),
        compiler_params=pltpu.CompilerParams(
            dimension_semantics=("arbitrary",), vmem_limit_bytes=VMEM_LIMIT,
            has_side_effects=True),
        name="moe_experts",
    )(blk0, cnt, rows_buf, w1, w3, w2)


def _combine_kernel(dest_ref, dest_next_ref, x1_ref, route_ref, g2_ref, fng_ref, y_ref, o_ref,
                    ybuf, sem, *, final_norm, g2_row0, tiles_per_g2):
    tp = x1_ref.shape[0]
    i = pl.program_id(0)
    n = pl.num_programs(0)
    slot = lax.rem(i, 2)

    def gather(d_ref, s):
        def row_copy(r, k):
            return pltpu.make_async_copy(_row_tile(y_ref, d_ref[2 * r + k]),
                                         _row_tile(ybuf.at[s, k], r), sem.at[s])

        def issue(r, c):
            row_copy(r, 0).start(priority=0)
            row_copy(r, 1).start(priority=1)
            return c

        lax.fori_loop(0, tp, issue, 0, unroll=8)

    @pl.when(i == 0)
    def _():
        gather(dest_ref, 0)

    @pl.when(i + 1 < n)
    def _():
        gather(dest_next_ref, 1 - slot)

    for k in range(2):
        pltpu.make_async_copy(y_ref.at[pl.ds(0, tp * ROW_TILE)], ybuf.at[slot, k], sem.at[slot]).wait()

    route = route_ref[...]
    w_a = route[:, 2:3]
    w_b = route[:, 3:4]
    if tiles_per_g2:
        g2 = g2_ref[0, 0, pl.ds(g2_row0 + i // tiles_per_g2, 1), :]
    else:
        g2 = jnp.concatenate([g2_ref[0, 0]] * (tp // g2_ref.shape[2]), axis=0)
    y = (w_a * _load_row_tiles(ybuf.at[slot, 0]) + w_b * _load_row_tiles(ybuf.at[slot, 1]))
    x2 = x1_ref[...] + g2 * y
    if final_norm:
        x2 = _rms(x2, fng_ref[...])
    o_ref[...] = x2


def _combine(dest_flat, x1, route, mod, layer, fng, y_rows, tp, *, final_norm, g2_row0=0,
             tiles_per_g2=0, seqs_per_tile=0):
    nt = x1.shape[0] // tp
    if tiles_per_g2:
        g2_spec = pl.BlockSpec((1, 1, mod.shape[2], D), lambda i: (layer, 5, 0, 0))
    else:
        g2_spec = pl.BlockSpec((1, 1, seqs_per_tile, D), lambda i: (layer, 5, i, 0))
    return pl.pallas_call(
        functools.partial(_combine_kernel, final_norm=final_norm, g2_row0=g2_row0,
                          tiles_per_g2=tiles_per_g2),
        grid=(nt,),
        in_specs=[pl.BlockSpec((2 * tp,), lambda i: (i,), memory_space=pltpu.SMEM),
                  pl.BlockSpec((2 * tp,), lambda i: (jnp.minimum(i + 1, nt - 1),),
                               memory_space=pltpu.SMEM),
                  pl.BlockSpec((tp, D), lambda i: (i, 0)),
                  pl.BlockSpec((tp, ROUTE_COLS), lambda i: (i, 0)),
                  g2_spec,
                  pl.BlockSpec((1, D), lambda i: (0, 0)),
                  pl.BlockSpec(memory_space=pl.ANY)],
        out_specs=pl.BlockSpec((tp, D), lambda i: (i, 0)),
        out_shape=jax.ShapeDtypeStruct(x1.shape, F32),
        scratch_shapes=[pltpu.VMEM((2, 2, tp * ROW_TILE, LANES), U32),
                        pltpu.SemaphoreType.DMA((2,))],
        compiler_params=pltpu.CompilerParams(
            dimension_semantics=("arbitrary",), vmem_limit_bytes=VMEM_LIMIT),
        name="moe_combine",
    )(dest_flat, dest_flat, x1, route, mod, fng, y_rows)


def _layer_weights(l, p):
    hd = W // HEADS
    gi = jnp.arange(W) // (W // NG_A)
    gavg = jnp.where(gi[:, None] == gi[None, :], 1.0 / (W // NG_A), 0.0).astype(BF16)
    tril = jnp.tril(jnp.ones((CHUNK, CHUNK), dtype=bool))
    w_sp = jnp.where(tril[None], p["b_spatial_w"][l], 0.0)
    w_router = jnp.concatenate(
        [p["w_router_g"][l], p["w_router_e"][l],
         jnp.zeros((D, ROUTE_COLS - NGRP - NEXP), F32)], axis=1)
    b_router = jnp.concatenate(
        [p["b_router_g"][l], p["b_router_e"][l],
         jnp.zeros((ROUTE_COLS - NGRP - NEXP,), F32)])[None, :]
    return {
        "n1g": p["norm1_g"][l][None, :], "n2g": p["norm2_g"][l][None, :],
        "w_in": p["w_in"][l].astype(BF16),
        "a_dw_w": p["a_dw_w"][l], "a_dw_b": p["a_dw_b"][l][None, :],
        "a_norm_g": p["a_norm_g"][l][None, :], "a_norm_b": p["a_norm_b"][l][None, :],
        "b_norm_g": p["b_norm_g"][l][None, :], "b_norm_b": p["b_norm_b"][l][None, :],
        "w_sp": jnp.concatenate([w_sp[0::2], w_sp[1::2]], axis=2).astype(BF16),
        "b_sp": jnp.repeat(p["b_spatial_b"][l].T, hd, axis=1),
        "c_conv_w": p["c_conv_w"][l],
        "w_branch": p["w_branch"][l].astype(BF16), "w_out": p["w_out"][l].astype(BF16),
        "gavg": gavg, "w_router": w_router.astype(BF16), "b_router": b_router,
    }


def _sample_spatial(l, p, ts):
    hd = W // HEADS
    w = p["b_spatial_w"][l][:, :ts, :ts]
    w = jnp.where(jnp.tril(jnp.ones((ts, ts), dtype=bool))[None], w, 0.0)
    w4 = jnp.repeat(jnp.transpose(w, (1, 2, 0)).reshape(ts * ts, HEADS), hd, axis=1)
    b4 = jnp.repeat(p["b_spatial_b"][l][:, :ts].T, hd, axis=1)
    return w4, b4


def kernel(x_prompt, x_sample, c_prompt, c_sample, state_conv_a, state_conv_c, norm1_g, norm2_g, w_ada, b_ada, w_in, a_dw_w, a_dw_b, a_norm_g, a_norm_b, b_norm_g, b_norm_b, b_spatial_w, b_spatial_b, c_conv_w, w_branch, w_out, w_router_g, b_router_g, w_router_e, b_router_e, w1, w3, w2, final_norm_g):
    p = dict(norm1_g=norm1_g, norm2_g=norm2_g, w_in=w_in, a_dw_w=a_dw_w, a_dw_b=a_dw_b,
             a_norm_g=a_norm_g, a_norm_b=a_norm_b, b_norm_g=b_norm_g, b_norm_b=b_norm_b,
             b_spatial_w=b_spatial_w, b_spatial_b=b_spatial_b, c_conv_w=c_conv_w,
             w_branch=w_branch, w_out=w_out, w_router_g=w_router_g, b_router_g=b_router_g,
             w_router_e=w_router_e, b_router_e=b_router_e)
    depth = w_in.shape[0]
    nb_, seq, _ = x_prompt.shape
    ns, ts, _ = x_sample.shape
    n_tok = nb_ * seq + ns * ts
    nblocks = -(-(2 * n_tok) // BM) + NEXP
    nsplit = 2
    nbs = ns // nsplit
    tps = ts * nbs
    assert seq % TT == 0 and TT % CHUNK == 0 and seq % TP == 0 and (nb_ * seq) % TPP == 0
    assert (ts * ns) % TP == 0 and ns % SUBLANES == 0

    mod = _adaln(jnp.concatenate([c_sample, c_prompt], axis=0), w_ada, b_ada)

    xp = x_prompt.reshape(nb_ * seq, D)
    xs = jnp.transpose(x_sample.reshape(nsplit, nbs, ts, D), (0, 2, 1, 3)).reshape(ts * ns, D)
    fng = final_norm_g[None, :]
    pa, pc, sa, sc, sv = [], [], [], [], []
    for l in range(depth):
        lw = _layer_weights(l, p)
        lw["w_sp4"], lw["b_sp4"] = _sample_spatial(l, p, ts)

        x1p, h2p, route_p, ha, hc = _mixer_prompt(xp, mod, l, nb_, ns, lw)
        pa.append(ha)
        pc.append(hc)
        x1s, h2s, route_s, ha_s, hc_s, v_s = _mixer_sample(
            xs, mod, l, state_conv_a, jnp.transpose(state_conv_c[l], (1, 0, 2)), lw, nsplit)
        sa.append(ha_s)
        sc.append(jnp.transpose(hc_s, (1, 0, 2)))
        sv.append(jnp.transpose(v_s, (1, 0, 2)))

        dest_p, dest_s, tbl = _plan(route_p, route_s)
        dest_p, dest_s = dest_p.reshape(-1), dest_s.reshape(-1)
        rows_buf = _scatter_rows(dest_p, h2p, None, nblocks * BM, TP)
        rows_buf = _scatter_rows(dest_s, h2s, rows_buf, nblocks * BM, tps)
        y_rows = _experts(tbl, rows_buf, w1, w3, w2, l)

        last = l == depth - 1
        xp = _combine(dest_p, x1p, route_p, mod, l, fng, y_rows, TP, final_norm=last,
                      g2_row0=ns, tiles_per_g2=seq // TP)
        xs = _combine(dest_s, x1s, route_s, mod, l, fng, y_rows, tps, final_norm=last,
                      seqs_per_tile=nbs)

    y_prompt = xp.reshape(nb_, seq, D)
    y_sample = jnp.transpose(xs.reshape(nsplit, ts, nbs, D), (0, 2, 1, 3)).reshape(ns, ts, D)
    return (y_prompt, y_sample, jnp.stack(pa), jnp.stack(pc), jnp.stack(sa), jnp.stack(sc),
            jnp.stack(sv))
```

```python
import functools

import jax
import jax.numpy as jnp
from jax import lax
from jax.experimental import pallas as pl
from jax.experimental.pallas import tpu as pltpu

F32 = jnp.float32
BF16 = jnp.bfloat16

D = 1024
W = D // 2
KA = 31
KC = 3
NG_A = 8
CHUNK = 128
HEADS = 8
NGRP = 4
EPG = 8
NEXP = NGRP * EPG
FF = D // 2
EPS = 1e-6
IN_COLS = 7 * W + 3 * D
LANES = 128
ROUTE_COLS = LANES

TT = 512
SUBLANES = 8
A_HDR = 32
A_PAD = SUBLANES
A_ROWS = 128
C_HDR = 8
TP = 512
TPP = 2048
BM = 256
VMEM_LIMIT = 56 * 1024 * 1024

NEG = -1e30


def _sigmoid(x):
    return 0.5 * jnp.tanh(0.5 * x) + 0.5


def _silu(x):
    h = 0.5 * x
    return h + h * jnp.tanh(h)


GELU_C = 0.7978845608028654


def _gelu_tanh(x):
    h = 0.5 * x
    return h + h * jnp.tanh(x * (GELU_C + (GELU_C * 0.044715) * (x * x)))


def _rms(x, g):
    return x * lax.rsqrt(jnp.mean(x * x, axis=-1, keepdims=True) + EPS) * g


def _rms_mod(x, g, scale, shift):
    inv = lax.rsqrt(jnp.mean(x * x, axis=-1, keepdims=True) + EPS)
    return x * inv * (g * (1.0 + scale)) + shift


def _dot(a, b):
    return jnp.dot(a, b, preferred_element_type=F32)


def _dot_hi(a, b):
    return jnp.dot(a, b, preferred_element_type=F32, precision=lax.Precision.HIGHEST)


U32 = jnp.uint32
ROW_TILE = D // (2 * LANES)


def _pack_bf16_pair(lo, hi):
    def rne(x):
        b = pltpu.bitcast(x, U32)
        return b + (U32(0x7FFF) + ((b >> 16) & U32(1)))
    return (rne(lo) >> 16) | (rne(hi) & U32(0xFFFF0000))


def _store_row_tiles(ref, val):
    r = val.shape[0]
    for j in range(ROW_TILE):
        lo = val[:, 2 * j * LANES:(2 * j + 1) * LANES]
        hi = val[:, (2 * j + 1) * LANES:(2 * j + 2) * LANES]
        ref[pl.ds(j, r, stride=ROW_TILE), :] = _pack_bf16_pair(lo, hi)


def _load_row_tiles(ref):
    r = ref.shape[0] // ROW_TILE
    parts = []
    for j in range(ROW_TILE):
        w = ref[pl.ds(j, r, stride=ROW_TILE), :]
        parts.append(pltpu.bitcast(w << 16, F32))
        parts.append(pltpu.bitcast(w & U32(0xFFFF0000), F32))
    return jnp.concatenate(parts, axis=1)


def _row_tile(ref, r):
    return ref.at[pl.ds(pl.multiple_of(r * ROW_TILE, ROW_TILE), ROW_TILE)]


def _mod_kernel(c_ref, w_ref, b_ref, o_ref):
    c = c_ref[...]
    s = _silu(c).astype(BF16)
    o_ref[0, 0] = _dot(s, w_ref[0].astype(BF16)) + b_ref[0]


def _adaln(c_all, w_ada, b_ada):
    nl = w_ada.shape[0]
    r = c_all.shape[0]
    return pl.pallas_call(
        _mod_kernel,
        grid=(nl, 6),
        in_specs=[
            pl.BlockSpec((r, D), lambda l, j: (0, 0)),
            pl.BlockSpec((1, D, D), lambda l, j: (l, 0, j)),
            pl.BlockSpec((1, 1, D), lambda l, j: (l, 0, j)),
        ],
        out_specs=pl.BlockSpec((1, 1, r, D), lambda l, j: (l, j, 0, 0)),
        out_shape=jax.ShapeDtypeStruct((nl, 6, r, D), F32),
        compiler_params=pltpu.CompilerParams(
            dimension_semantics=("arbitrary", "arbitrary"), vmem_limit_bytes=VMEM_LIMIT),
        name="adaln_mod",
    )(c_all, w_ada, b_ada.reshape(nl, 1, 6 * D))


def _group_norm_silu(a, gavg_ref, g, b):
    mu = _dot(a.astype(BF16), gavg_ref[...])
    xc = a - mu
    var = _dot((xc * xc).astype(BF16), gavg_ref[...])
    y = xc * lax.rsqrt(var + EPS) * g + b
    return _silu(y)


def _layer_norm(x, g, b):
    mu = jnp.mean(x, axis=-1, keepdims=True)
    xc = x - mu
    var = jnp.mean(xc * xc, axis=-1, keepdims=True)
    return xc * lax.rsqrt(var + EPS) * g + b


def _merge_and_route(x, a, bb, cc, gates_fn, mod, n2g, wbr_ref, wout_ref, wr_ref, br_ref,
                     x1_ref, h2_ref, route_ref):
    g1, sh2, sc2 = mod
    acc = None
    for gi, br_in in enumerate((a, bb, cc)):
        br = _dot(br_in.astype(BF16), wbr_ref[gi])
        term = _sigmoid(gates_fn(gi)) * br
        acc = term if acc is None else acc + term
    mixed = _dot(acc.astype(BF16), wout_ref[...])
    x1 = x + g1 * mixed
    x1_ref[...] = x1
    h2 = _rms_mod(x1, n2g, sc2, sh2)
    _store_row_tiles(h2_ref, h2)

    logits = _dot(h2.astype(BF16), wr_ref[...]) + br_ref[...]
    rows = logits.shape[0]
    lane = lax.broadcasted_iota(jnp.int32, (rows, ROUTE_COLS), 1).astype(F32)
    is_g = lane < NGRP
    gl = jnp.where(is_g, logits, NEG)
    gmax = jnp.max(gl, axis=-1, keepdims=True)
    gsel = jnp.min(jnp.where(gl == gmax, lane, float(ROUTE_COLS)), axis=-1, keepdims=True)
    gden = jnp.sum(jnp.where(is_g, jnp.exp(gl - gmax), 0.0), axis=-1, keepdims=True)
    gprob = 1.0 / gden
    lo = NGRP + EPG * gsel
    in_grp = jnp.logical_and(lane >= lo, lane < lo + EPG)
    el = jnp.where(in_grp, logits, NEG)
    v1 = jnp.max(el, axis=-1, keepdims=True)
    i1 = jnp.min(jnp.where(el == v1, lane, float(ROUTE_COLS)), axis=-1, keepdims=True)
    el2 = jnp.where(lane == i1, NEG, el)
    v2 = jnp.max(el2, axis=-1, keepdims=True)
    i2 = jnp.min(jnp.where(el2 == v2, lane, float(ROUTE_COLS)), axis=-1, keepdims=True)
    p2 = jnp.exp(v2 - v1)
    w_a = gprob / (1.0 + p2)
    w_b = gprob * p2 / (1.0 + p2)
    route = jnp.where(lane == 0.0, i1 - NGRP,
                      jnp.where(lane == 1.0, i2 - NGRP,
                                jnp.where(lane == 2.0, w_a,
                                          jnp.where(lane == 3.0, w_b, 0.0))))
    route_ref[...] = route


def _mixer_prompt_kernel(x_ref, mod_ref, n1g_ref, n2g_ref, win_ref, adw_ref, adb_ref, ang_ref,
                         anb_ref, bng_ref, bnb_ref, wsp_ref, bsp_ref, ccw_ref, wbr_ref, wout_ref,
                         gavg_ref, wr_ref, br_ref,
                         x1_ref, h2_ref, route_ref, ha_ref, hc_ref,
                         abuf, cbuf, aconv, *, mod_row0):
    first = pl.program_id(1) == 0
    abuf[0:A_HDR, :] = jnp.where(first, 0.0, abuf[TT:TT + A_HDR, :])
    abuf[A_HDR + TT:A_HDR + TT + A_PAD, :] = jnp.zeros((A_PAD, W), F32)
    cbuf[0:C_HDR, :] = jnp.where(first, 0.0, cbuf[TT:TT + C_HDR, :])

    x = x_ref[...]
    mrow = mod_row0 + pl.program_id(0)
    sh1, sc1, g1, sh2, sc2 = (mod_ref[0, s, pl.ds(mrow, 1), :] for s in range(5))
    h = _rms_mod(x, n1g_ref[...], sc1, sh1).astype(BF16)

    def zsec(lo, hi):
        return _dot(h, win_ref[:, lo:hi])

    a_glu = zsec(0, W) * _sigmoid(zsec(W, 2 * W))
    abuf[A_HDR:A_HDR + TT, :] = a_glu
    off = A_HDR - (KA - 1)
    rc = A_ROWS

    def conv_pass(p):
        lt, half = p // 2, p % 2
        ls = slice(lt * LANES, (lt + 1) * LANES)
        for r0 in range(half * (TT // 2), (half + 1) * (TT // 2), rc):
            y = None
            for r in range(SUBLANES):
                z = None
                for q in range((KA - 1 + off) // SUBLANES + 1):
                    k = SUBLANES * q + r - off
                    if 0 <= k < KA:
                        rows = slice(r0 + SUBLANES * q, r0 + SUBLANES * q + rc + SUBLANES)
                        term = adw_ref[k:k + 1, ls] * abuf[rows, ls]
                        z = term if z is None else z + term
                part = z[r:r + rc, :]
                y = part if y is None else y + part
            aconv[r0:r0 + rc, ls] = y + adb_ref[:, ls]

    conv_pass(0)
    u = _gelu_tanh(zsec(2 * W, 3 * W))
    conv_pass(1)
    v = _layer_norm(_gelu_tanh(zsec(3 * W, 4 * W)), bng_ref[...], bnb_ref[...])
    conv_pass(2)

    cch = zsec(5 * W, 6 * W) * zsec(6 * W, 7 * W)
    cbuf[C_HDR:C_HDR + TT, :] = cch
    coff = C_HDR - (KC - 1)
    ych = (ccw_ref[0:1, :] * cbuf[coff:coff + TT, :]
           + ccw_ref[1:2, :] * cbuf[coff + 1:coff + 1 + TT, :]
           + ccw_ref[2:3, :] * cbuf[coff + 2:coff + 2 + TT, :])
    conv_pass(3)
    cc = zsec(4 * W, 5 * W) * ych
    hc_ref[0] = cbuf[C_HDR + TT - (KC - 1):C_HDR + TT, :]
    conv_pass(4)
    gate_z = [zsec(7 * W, 7 * W + D)]
    conv_pass(5)
    gate_z.append(zsec(7 * W + D, 7 * W + 2 * D))
    conv_pass(6)
    gate_z.append(zsec(7 * W + 2 * D, 7 * W + 3 * D))
    conv_pass(7)
    ha_ref[0] = abuf[A_HDR + TT - (KA - 1):A_HDR + TT, :]

    vb = v.astype(BF16)
    low_head = lax.broadcasted_iota(jnp.int32, (CHUNK, LANES), 1) < (W // HEADS)
    zero_b = jnp.zeros((CHUNK, LANES), BF16)
    s_parts = []
    for c in range(TT // CHUNK):
        tiles = []
        for lt in range(W // LANES):
            vt = vb[c * CHUNK:(c + 1) * CHUNK, lt * LANES:(lt + 1) * LANES]
            stacked = jnp.concatenate([jnp.where(low_head, vt, zero_b),
                                       jnp.where(low_head, zero_b, vt)], axis=0)
            tiles.append(_dot(wsp_ref[lt], stacked))
        s_parts.append(jnp.concatenate(tiles, axis=1) + bsp_ref[...])
    bb = u * jnp.concatenate(s_parts, axis=0)
    a = _group_norm_silu(aconv[...], gavg_ref, ang_ref[...], anb_ref[...])

    _merge_and_route(x, a, bb, cc, lambda gi: gate_z[gi], (g1, sh2, sc2), n2g_ref[...], wbr_ref,
                     wout_ref, wr_ref, br_ref, x1_ref, h2_ref, route_ref)


def _const_spec(shape):
    nd = len(shape)
    return pl.BlockSpec(shape, lambda *_: (0,) * nd, pipeline_mode=pl.Buffered(1))


def _mixer_prompt(x, mod, layer, n, mod_row0, lw):
    t = x.shape[0] // n
    nt = t // TT
    mod_rows = mod.shape[2]
    row_spec = lambda cols: pl.BlockSpec((TT, cols), lambda i, j: (i * nt + j, 0))
    consts = [lw["n1g"], lw["n2g"], lw["w_in"], lw["a_dw_w"], lw["a_dw_b"], lw["a_norm_g"],
              lw["a_norm_b"], lw["b_norm_g"], lw["b_norm_b"], lw["w_sp"], lw["b_sp"], lw["c_conv_w"],
              lw["w_branch"], lw["w_out"], lw["gavg"], lw["w_router"], lw["b_router"]]
    return pl.pallas_call(
        functools.partial(_mixer_prompt_kernel, mod_row0=mod_row0),
        grid=(n, nt),
        in_specs=[row_spec(D),
                  pl.BlockSpec((1, 6, mod_rows, D), lambda i, j: (layer, 0, 0, 0),
                               pipeline_mode=pl.Buffered(1))]
        + [_const_spec(c.shape) for c in consts],
        out_specs=[row_spec(D),
                   pl.BlockSpec((TT * ROW_TILE, LANES), lambda i, j: (i * nt + j, 0)),
                   row_spec(ROUTE_COLS),
                   pl.BlockSpec((1, KA - 1, W), lambda i, j: (i, 0, 0)),
                   pl.BlockSpec((1, KC - 1, W), lambda i, j: (i, 0, 0))],
        out_shape=[jax.ShapeDtypeStruct((n * t, D), F32),
                   jax.ShapeDtypeStruct((n * t * ROW_TILE, LANES), U32),
                   jax.ShapeDtypeStruct((n * t, ROUTE_COLS), F32),
                   jax.ShapeDtypeStruct((n, KA - 1, W), F32),
                   jax.ShapeDtypeStruct((n, KC - 1, W), F32)],
        scratch_shapes=[pltpu.VMEM((A_HDR + TT + A_PAD, W), F32),
                        pltpu.VMEM((C_HDR + TT, W), F32),
                        pltpu.VMEM((TT, W), F32)],
        compiler_params=pltpu.CompilerParams(
            dimension_semantics=("arbitrary", "arbitrary"), vmem_limit_bytes=VMEM_LIMIT),
        name="mixer_prompt",
    )(x, mod, *consts)


def _mixer_sample_kernel(x_ref, mod_ref, hista_ref, histc_ref, n1g_ref, n2g_ref, win_ref, adw_ref,
                         adb_ref, ang_ref, anb_ref, bng_ref, bnb_ref, wsp4_ref, bsp4_ref,
                         ccw_ref, wbr_ref, wout_ref, gavg_ref, wr_ref, br_ref,
                         x1_ref, h2_ref, route_ref, ha_ref, hc_ref, v_ref):
    ts, nb = v_ref.shape[0], v_ref.shape[1]
    x = x_ref[...]

    def modrow(i):
        return jnp.concatenate([mod_ref[0, i]] * ts, axis=0)

    h = (_rms(x, n1g_ref[...]) * (1.0 + modrow(1)) + modrow(0)).astype(BF16)

    def zsec(lo, hi):
        return _dot(h, win_ref[:, lo:hi])

    def tslab(arr, j):
        return arr[j * nb:(j + 1) * nb, :]

    a_glu = zsec(0, W) * _sigmoid(zsec(W, 2 * W))
    kh = KA - 1
    a_conv = []
    for tq in range(ts):
        acc = jnp.zeros((nb, W), F32) + adb_ref[...]
        for r in range(tq, kh):
            acc = acc + adw_ref[r - tq:r - tq + 1, :] * hista_ref[r]
        for j in range(tq + 1):
            acc = acc + adw_ref[kh + j - tq:kh + j - tq + 1, :] * tslab(a_glu, j)
        a_conv.append(acc)
    a = _group_norm_silu(jnp.concatenate(a_conv, axis=0), gavg_ref, ang_ref[...], anb_ref[...])
    for r in range(kh - ts):
        ha_ref[r] = hista_ref[r + ts]
    for j in range(ts):
        ha_ref[kh - ts + j] = tslab(a_glu, j)

    u = _gelu_tanh(zsec(2 * W, 3 * W))
    v = _layer_norm(_gelu_tanh(zsec(3 * W, 4 * W)), bng_ref[...], bnb_ref[...])
    for j in range(ts):
        v_ref[j] = tslab(v, j)
    s_rows = []
    for tq in range(ts):
        s = jnp.zeros((nb, W), F32) + bsp4_ref[tq:tq + 1, :]
        for sq in range(tq + 1):
            s = s + wsp4_ref[tq * ts + sq:tq * ts + sq + 1, :] * tslab(v, sq)
        s_rows.append(s)
    bb = u * jnp.concatenate(s_rows, axis=0)

    cch = zsec(5 * W, 6 * W) * zsec(6 * W, 7 * W)
    xp = [histc_ref[r] for r in range(KC - 1)] + [tslab(cch, j) for j in range(ts)]
    ych = jnp.concatenate(
        [sum(ccw_ref[k:k + 1, :] * xp[tq + k] for k in range(KC)) for tq in range(ts)], axis=0)
    cc = zsec(4 * W, 5 * W) * ych
    for r in range(KC - 1):
        hc_ref[r] = xp[ts + r]

    def gates_fn(gi):
        return zsec(7 * W + gi * D, 7 * W + (gi + 1) * D)

    _merge_and_route(x, a, bb, cc, gates_fn, (modrow(2), modrow(3), modrow(4)), n2g_ref[...],
                     wbr_ref, wout_ref, wr_ref, br_ref, x1_ref, h2_ref, route_ref)


def _mixer_sample(x_rows, mod, layer, hist_a_tm, hist_c_tm, lw, nsplit):
    n = hist_a_tm.shape[1]
    ts = x_rows.shape[0] // n
    nb = n // nsplit
    rows = ts * nb
    consts = [lw["n1g"], lw["n2g"], lw["w_in"], lw["a_dw_w"], lw["a_dw_b"],
              lw["a_norm_g"], lw["a_norm_b"], lw["b_norm_g"], lw["b_norm_b"], lw["w_sp4"], lw["b_sp4"],
              lw["c_conv_w"], lw["w_branch"], lw["w_out"], lw["gavg"], lw["w_router"], lw["b_router"]]
    seq3 = lambda k, cols: pl.BlockSpec((k, nb, cols), lambda i: (0, i, 0))
    row_spec = lambda cols: pl.BlockSpec((rows, cols), lambda i: (i, 0))
    return pl.pallas_call(
        _mixer_sample_kernel,
        grid=(nsplit,),
        in_specs=[row_spec(D), pl.BlockSpec((1, 6, nb, D), lambda i: (layer, 0, i, 0)),
                  seq3(KA - 1, W), seq3(KC - 1, W)]
        + [_const_spec(c.shape) for c in consts],
        out_specs=[row_spec(D),
                   pl.BlockSpec((rows * ROW_TILE, LANES), lambda i: (i, 0)),
                   row_spec(ROUTE_COLS),
                   seq3(KA - 1, W), seq3(KC - 1, W), seq3(ts, W)],
        out_shape=[jax.ShapeDtypeStruct((ts * n, D), F32),
                   jax.ShapeDtypeStruct((ts * n * ROW_TILE, LANES), U32),
                   jax.ShapeDtypeStruct((ts * n, ROUTE_COLS), F32),
                   jax.ShapeDtypeStruct((KA - 1, n, W), F32),
                   jax.ShapeDtypeStruct((KC - 1, n, W), F32),
                   jax.ShapeDtypeStruct((ts, n, W), F32)],
        compiler_params=pltpu.CompilerParams(
            dimension_semantics=("arbitrary",), vmem_limit_bytes=VMEM_LIMIT),
        name="mixer_sample",
    )(x_rows, mod, hist_a_tm, hist_c_tm, *consts)


def _plan_kernel(rp_ref, rs_ref, tri_ref, destp_ref, dests_ref, tbl_ref, carry, pstart, *, ntp):
    ph = pl.program_id(0)
    i = pl.program_id(1)
    nt = pl.num_programs(1)
    lane = lax.broadcasted_iota(jnp.int32, (TP, ROUTE_COLS), 1).astype(F32)

    @pl.when(jnp.logical_and(ph == 0, i == 0))
    def _():
        carry[...] = jnp.zeros_like(carry)

    def sub_tile(r, dest_ref, row0):
        hot_a = lane == r[:, 0:1]
        hot_b = lane == r[:, 1:2]
        s = jnp.where(jnp.logical_or(hot_a, hot_b), 1.0, 0.0)

        @pl.when(ph == 1)
        def _():
            pre = _dot(tri_ref[...], s.astype(BF16)) + carry[...] + pstart[...]
            d_a = jnp.sum(jnp.where(hot_a, pre, 0.0), axis=-1, keepdims=True)
            d_b = jnp.sum(jnp.where(hot_b, pre, 0.0), axis=-1, keepdims=True)
            l2 = lax.broadcasted_iota(jnp.int32, (TP, 2), 1)
            dest_ref[row0:row0 + TP, :] = jnp.where(l2 == 0, d_a, d_b).astype(jnp.int32)

        carry[...] = carry[...] + jnp.sum(s, axis=0, keepdims=True)

    @pl.when(i < ntp)
    def _():
        for sub in range(rp_ref.shape[0] // TP):
            sub_tile(rp_ref[sub * TP:(sub + 1) * TP, :], destp_ref, sub * TP)

    @pl.when(i == ntp)
    def _():
        for sub in range(rs_ref.shape[0] // TP):
            sub_tile(rs_ref[sub * TP:(sub + 1) * TP, :], dests_ref, sub * TP)

    @pl.when(jnp.logical_and(ph == 0, i == nt - 1))
    def _():
        cnt = carry[...]
        nblk = jnp.floor((cnt + (BM - 1)) * (1.0 / BM))
        ri = lax.broadcasted_iota(jnp.int32, (ROUTE_COLS, ROUTE_COLS), 0)
        ci = lax.broadcasted_iota(jnp.int32, (ROUTE_COLS, ROUTE_COLS), 1)
        upper = jnp.where(ri <= ci, 1.0, 0.0)
        cum = _dot_hi(jnp.broadcast_to(nblk, (8, ROUTE_COLS)), upper)[0:1, :]
        cum_ex = cum - nblk
        pstart[...] = cum_ex * BM
        row = lax.broadcasted_iota(jnp.int32, (SUBLANES, ROUTE_COLS), 0)
        tbl = jnp.where(row == 0, cum_ex, jnp.where(row == 1, cnt, 0.0))
        tbl_ref[...] = tbl.astype(jnp.int32)
        carry[...] = jnp.zeros_like(carry)


def _plan(route_p, route_s):
    rows_p, rows_s = route_p.shape[0], route_s.shape[0]
    ntp = rows_p // TPP
    nbp = SUBLANES
    ri = lax.broadcasted_iota(jnp.int32, (TP, TP), 0)
    ci = lax.broadcasted_iota(jnp.int32, (TP, TP), 1)
    tri = jnp.where(ci < ri, 1.0, 0.0).astype(BF16)
    return pl.pallas_call(
        functools.partial(_plan_kernel, ntp=ntp),
        grid=(2, ntp + 1),
        in_specs=[pl.BlockSpec((TPP, ROUTE_COLS), lambda p, i: (jnp.minimum(i, ntp - 1), 0)),
                  pl.BlockSpec((rows_s, ROUTE_COLS), lambda p, i: (0, 0)),
                  pl.BlockSpec((TP, TP), lambda p, i: (0, 0))],
        out_specs=[pl.BlockSpec((TPP, 2), lambda p, i: (jnp.minimum(i, ntp - 1) * p, 0)),
                   pl.BlockSpec((rows_s, 2), lambda p, i: (0, 0)),
                   pl.BlockSpec((nbp, ROUTE_COLS), lambda p, i: (0, 0))],
        out_shape=[jax.ShapeDtypeStruct((rows_p, 2), jnp.int32),
                   jax.ShapeDtypeStruct((rows_s, 2), jnp.int32),
                   jax.ShapeDtypeStruct((nbp, ROUTE_COLS), jnp.int32)],
        scratch_shapes=[pltpu.VMEM((1, ROUTE_COLS), F32), pltpu.VMEM((1, ROUTE_COLS), F32)],
        compiler_params=pltpu.CompilerParams(
            dimension_semantics=("arbitrary", "arbitrary"), vmem_limit_bytes=VMEM_LIMIT),
        name="moe_plan",
    )(route_p, route_s, tri)


def _scatter_kernel(dest_ref, h_ref, *rest):
    rows_ref, sem = rest[-2:]
    tp = h_ref.shape[0] // ROW_TILE

    def row_copy(r, k):
        d = dest_ref[2 * r + k]
        return pltpu.make_async_copy(_row_tile(h_ref, r), _row_tile(rows_ref, d), sem)

    def issue(r, c):
        row_copy(r, 0).start(priority=0)
        row_copy(r, 1).start(priority=1)
        return c

    lax.fori_loop(0, tp, issue, 0, unroll=8)

    def drain(r, c):
        row_copy(r, 0).wait()
        row_copy(r, 1).wait()
        return c

    lax.fori_loop(0, tp, drain, 0, unroll=8)


def _scatter_rows(dest_flat, h2, rows_buf, n_rows, tp):
    nt = h2.shape[0] // (tp * ROW_TILE)
    in_specs = [pl.BlockSpec((2 * tp,), lambda i: (i,), memory_space=pltpu.SMEM),
                pl.BlockSpec((tp * ROW_TILE, LANES), lambda i: (i, 0))]
    args = [dest_flat, h2]
    aliases = {}
    if rows_buf is not None:
        in_specs.append(pl.BlockSpec(memory_space=pl.ANY))
        args.append(rows_buf)
        aliases = {2: 0}
    return pl.pallas_call(
        _scatter_kernel,
        grid=(nt,),
        in_specs=in_specs,
        out_specs=pl.BlockSpec(memory_space=pl.ANY),
        out_shape=jax.ShapeDtypeStruct((n_rows * ROW_TILE, LANES), U32),
        scratch_shapes=[pltpu.SemaphoreType.DMA(())],
        input_output_aliases=aliases,
        compiler_params=pltpu.CompilerParams(
            dimension_semantics=("arbitrary",), vmem_limit_bytes=VMEM_LIMIT,
            has_side_effects=True),
        name="moe_scatter",
    )(*args)


ROW_DMA_PRIORITY = 1
READ_AHEAD = 3
ROW_SLOTS = 5


def _expert_kernel(blk0_ref, cnt_ref, x_hbm, w1_ref, w3_ref, w2_ref, y_hbm,
                   xbuf, ybuf, w1b, w3b, w2b, sem_in, sem_out):
    e = pl.program_id(0)
    cnt = cnt_ref[e]
    nblk = lax.shift_right_logical(cnt + (BM - 1), BM.bit_length() - 1)
    blk0 = blk0_ref[e]
    last_e = NEXP - 1
    n_used = blk0_ref[last_e] + lax.shift_right_logical(cnt_ref[last_e] + (BM - 1),
                                                        BM.bit_length() - 1)
    blk_rows = BM * ROW_TILE

    def block_rows(ref, g):
        return ref.at[pl.ds(pl.multiple_of(g * blk_rows, blk_rows), blk_rows)]

    def in_copy(g):
        slot = lax.rem(g, ROW_SLOTS)
        return pltpu.make_async_copy(block_rows(x_hbm, g), xbuf.at[slot], sem_in.at[slot])

    def out_copy(g):
        slot = lax.rem(g, ROW_SLOTS)
        return pltpu.make_async_copy(ybuf.at[slot], block_rows(y_hbm, g), sem_out.at[slot])

    @pl.when(nblk > 0)
    def _():
        @pl.when(blk0 == 0)
        def _():
            for g0 in range(READ_AHEAD):
                @pl.when(g0 < n_used)
                def _():
                    in_copy(g0).start(priority=ROW_DMA_PRIORITY)

        w1b[...] = w1_ref[0, 0].astype(BF16)
        w3b[...] = w3_ref[0, 0].astype(BF16)
        w2b[...] = w2_ref[0, 0].astype(BF16)

        def chain(j, nb):
            g = blk0 + j
            for q in range(nb):
                @pl.when(g + READ_AHEAD + q < n_used)
                def _():
                    in_copy(g + READ_AHEAD + q).start(priority=ROW_DMA_PRIORITY)
            for q in range(nb):
                in_copy(g + q).wait()
            for q in range(nb):
                @pl.when(g + q >= ROW_SLOTS)
                def _():
                    out_copy(g + q - ROW_SLOTS).wait()

            row = lax.broadcasted_iota(jnp.int32, (BM, D), 0)
            xs = [jnp.where(row < cnt - (j + q) * BM,
                            _load_row_tiles(xbuf.at[lax.rem(g + q, ROW_SLOTS)]), 0.0).astype(BF16)
                  for q in range(nb)]
            x = xs[0] if nb == 1 else jnp.concatenate(xs, axis=0)
            h1 = _dot(x, w1b[...])
            h3 = _dot(x, w3b[...])
            act = (_silu(h1) * h3).astype(BF16)
            y = _dot(act, w2b[...])
            for q in range(nb):
                _store_row_tiles(ybuf.at[lax.rem(g + q, ROW_SLOTS)], y[q * BM:(q + 1) * BM, :])
                out_copy(g + q).start(priority=ROW_DMA_PRIORITY)

        def pair(p, carry):
            chain(2 * p, 2)
            return carry

        lax.fori_loop(0, lax.shift_right_logical(nblk, 1), pair, 0)

        @pl.when(lax.rem(nblk, 2) == 1)
        def _():
            chain(nblk - 1, 1)

    @pl.when(e == last_e)
    def _():
        for back in range(ROW_SLOTS, 0, -1):
            @pl.when(n_used >= back)
            def _():
                out_copy(n_used - back).wait()


def _experts(tbl, rows_buf, w1, w3, w2, layer):
    blk0, cnt = tbl[0, :NEXP], tbl[1, :NEXP]
    wspec = lambda shape: pl.BlockSpec((1, 1) + shape, lambda e, blk0, cnt: (layer, e, 0, 0))
    return pl.pallas_call(
        _expert_kernel,
        grid_spec=pltpu.PrefetchScalarGridSpec(
            num_scalar_prefetch=2,
            grid=(NEXP,),
            in_specs=[pl.BlockSpec(memory_space=pl.ANY),
                      wspec((D, FF)), wspec((D, FF)), wspec((FF, D))],
            out_specs=pl.BlockSpec(memory_space=pl.ANY),
            scratch_shapes=[pltpu.VMEM((ROW_SLOTS, BM * ROW_TILE, LANES), U32),
                            pltpu.VMEM((ROW_SLOTS, BM * ROW_TILE, LANES), U32),
                            pltpu.VMEM((D, FF), BF16), pltpu.VMEM((D, FF), BF16),
                            pltpu.VMEM((FF, D), BF16),
                            pltpu.SemaphoreType.DMA((ROW_SLOTS,)),
                            pltpu.SemaphoreType.DMA((ROW_SLOTS,))],
        ),
        out_shape=jax.ShapeDtypeStruct(rows_buf.shape, U32),
        compiler_params=pltpu.CompilerParams(
            dimension_semantics=("arbitrary",), vmem_limit_bytes=VMEM_LIMIT,
            has_side_effects=True),
        name="moe_experts",
    )(blk0, cnt, rows_buf, w1, w3, w2)


def _combine_kernel(dest_ref, dest_next_ref, x1_ref, route_ref, g2_ref, fng_ref, y_ref, o_ref,
                    ybuf, sem, *, final_norm, g2_row0, tiles_per_g2):
    tp = x1_ref.shape[0]
    i = pl.program_id(0)
    n = pl.num_programs(0)
    slot = lax.rem(i, 2)

    def gather(d_ref, s):
        def row_copy(r, k):
            return pltpu.make_async_copy(_row_tile(y_ref, d_ref[2 * r + k]),
                                         _row_tile(ybuf.at[s, k], r), sem.at[s])

        def issue(r, c):
            row_copy(r, 0).start(priority=0)
            row_copy(r, 1).start(priority=1)
            return c

        lax.fori_loop(0, tp, issue, 0, unroll=8)

    @pl.when(i == 0)
    def _():
        gather(dest_ref, 0)

    @pl.when(i + 1 < n)
    def _():
        gather(dest_next_ref, 1 - slot)

    for k in range(2):
        pltpu.make_async_copy(y_ref.at[pl.ds(0, tp * ROW_TILE)], ybuf.at[slot, k], sem.at[slot]).wait()

    route = route_ref[...]
    w_a = route[:, 2:3]
    w_b = route[:, 3:4]
    if tiles_per_g2:
        g2 = g2_ref[0, 0, pl.ds(g2_row0 + i // tiles_per_g2, 1), :]
    else:
        g2 = jnp.concatenate([g2_ref[0, 0]] * (tp // g2_ref.shape[2]), axis=0)
    y = (w_a * _load_row_tiles(ybuf.at[slot, 0]) + w_b * _load_row_tiles(ybuf.at[slot, 1]))
    x2 = x1_ref[...] + g2 * y
    if final_norm:
        x2 = _rms(x2, fng_ref[...])
    o_ref[...] = x2


def _combine(dest_flat, x1, route, mod, layer, fng, y_rows, tp, *, final_norm, g2_row0=0,
             tiles_per_g2=0, seqs_per_tile=0):
    nt = x1.shape[0] // tp
    if tiles_per_g2:
        g2_spec = pl.BlockSpec((1, 1, mod.shape[2], D), lambda i: (layer, 5, 0, 0))
    else:
        g2_spec = pl.BlockSpec((1, 1, seqs_per_tile, D), lambda i: (layer, 5, i, 0))
    return pl.pallas_call(
        functools.partial(_combine_kernel, final_norm=final_norm, g2_row0=g2_row0,
                          tiles_per_g2=tiles_per_g2),
        grid=(nt,),
        in_specs=[pl.BlockSpec((2 * tp,), lambda i: (i,), memory_space=pltpu.SMEM),
                  pl.BlockSpec((2 * tp,), lambda i: (jnp.minimum(i + 1, nt - 1),),
                               memory_space=pltpu.SMEM),
                  pl.BlockSpec((tp, D), lambda i: (i, 0)),
                  pl.BlockSpec((tp, ROUTE_COLS), lambda i: (i, 0)),
                  g2_spec,
                  pl.BlockSpec((1, D), lambda i: (0, 0)),
                  pl.BlockSpec(memory_space=pl.ANY)],
        out_specs=pl.BlockSpec((tp, D), lambda i: (i, 0)),
        out_shape=jax.ShapeDtypeStruct(x1.shape, F32),
        scratch_shapes=[pltpu.VMEM((2, 2, tp * ROW_TILE, LANES), U32),
                        pltpu.SemaphoreType.DMA((2,))],
        compiler_params=pltpu.CompilerParams(
            dimension_semantics=("arbitrary",), vmem_limit_bytes=VMEM_LIMIT),
        name="moe_combine",
    )(dest_flat, dest_flat, x1, route, mod, fng, y_rows)


def _layer_weights(l, p):
    hd = W // HEADS
    gi = jnp.arange(W) // (W // NG_A)
    gavg = jnp.where(gi[:, None] == gi[None, :], 1.0 / (W // NG_A), 0.0).astype(BF16)
    tril = jnp.tril(jnp.ones((CHUNK, CHUNK), dtype=bool))
    w_sp = jnp.where(tril[None], p["b_spatial_w"][l], 0.0)
    w_router = jnp.concatenate(
        [p["w_router_g"][l], p["w_router_e"][l],
         jnp.zeros((D, ROUTE_COLS - NGRP - NEXP), F32)], axis=1)
    b_router = jnp.concatenate(
        [p["b_router_g"][l], p["b_router_e"][l],
         jnp.zeros((ROUTE_COLS - NGRP - NEXP,), F32)])[None, :]
    return {
        "n1g": p["norm1_g"][l][None, :], "n2g": p["norm2_g"][l][None, :],
        "w_in": p["w_in"][l].astype(BF16),
        "a_dw_w": p["a_dw_w"][l], "a_dw_b": p["a_dw_b"][l][None, :],
        "a_norm_g": p["a_norm_g"][l][None, :], "a_norm_b": p["a_norm_b"][l][None, :],
        "b_norm_g": p["b_norm_g"][l][None, :], "b_norm_b": p["b_norm_b"][l][None, :],
        "w_sp": jnp.concatenate([w_sp[0::2], w_sp[1::2]], axis=2).astype(BF16),
        "b_sp": jnp.repeat(p["b_spatial_b"][l].T, hd, axis=1),
        "c_conv_w": p["c_conv_w"][l],
        "w_branch": p["w_branch"][l].astype(BF16), "w_out": p["w_out"][l].astype(BF16),
        "gavg": gavg, "w_router": w_router.astype(BF16), "b_router": b_router,
    }


def _sample_spatial(l, p, ts):
    hd = W // HEADS
    w = p["b_spatial_w"][l][:, :ts, :ts]
    w = jnp.where(jnp.tril(jnp.ones((ts, ts), dtype=bool))[None], w, 0.0)
    w4 = jnp.repeat(jnp.transpose(w, (1, 2, 0)).reshape(ts * ts, HEADS), hd, axis=1)
    b4 = jnp.repeat(p["b_spatial_b"][l][:, :ts].T, hd, axis=1)
    return w4, b4


def kernel(x_prompt, x_sample, c_prompt, c_sample, state_conv_a, state_conv_c, norm1_g, norm2_g, w_ada, b_ada, w_in, a_dw_w, a_dw_b, a_norm_g, a_norm_b, b_norm_g, b_norm_b, b_spatial_w, b_spatial_b, c_conv_w, w_branch, w_out, w_router_g, b_router_g, w_router_e, b_router_e, w1, w3, w2, final_norm_g):
    p = dict(norm1_g=norm1_g, norm2_g=norm2_g, w_in=w_in, a_dw_w=a_dw_w, a_dw_b=a_dw_b,
             a_norm_g=a_norm_g, a_norm_b=a_norm_b, b_norm_g=b_norm_g, b_norm_b=b_norm_b,
             b_spatial_w=b_spatial_w, b_spatial_b=b_spatial_b, c_conv_w=c_conv_w,
             w_branch=w_branch, w_out=w_out, w_router_g=w_router_g, b_router_g=b_router_g,
             w_router_e=w_router_e, b_router_e=b_router_e)
    depth = w_in.shape[0]
    nb_, seq, _ = x_prompt.shape
    ns, ts, _ = x_sample.shape
    n_tok = nb_ * seq + ns * ts
    nblocks = -(-(2 * n_tok) // BM) + NEXP
    nsplit = 2
    nbs = ns // nsplit
    tps = ts * nbs
    assert seq % TT == 0 and TT % CHUNK == 0 and seq % TP == 0 and (nb_ * seq) % TPP == 0
    assert (ts * ns) % TP == 0 and ns % SUBLANES == 0

    mod = _adaln(jnp.concatenate([c_sample, c_prompt], axis=0), w_ada, b_ada)

    xp = x_prompt.reshape(nb_ * seq, D)
    xs = jnp.transpose(x_sample.reshape(nsplit, nbs, ts, D), (0, 2, 1, 3)).reshape(ts * ns, D)
    fng = final_norm_g[None, :]
    pa, pc, sa, sc, sv = [], [], [], [], []
    for l in range(depth):
        lw = _layer_weights(l, p)
        lw["w_sp4"], lw["b_sp4"] = _sample_spatial(l, p, ts)

        x1p, h2p, route_p, ha, hc = _mixer_prompt(xp, mod, l, nb_, ns, lw)
        pa.append(ha)
        pc.append(hc)
        x1s, h2s, route_s, ha_s, hc_s, v_s = _mixer_sample(
            xs, mod, l, jnp.transpose(state_conv_a[l], (1, 0, 2)),
            jnp.transpose(state_conv_c[l], (1, 0, 2)), lw, nsplit)
        sa.append(jnp.transpose(ha_s, (1, 0, 2)))
        sc.append(jnp.transpose(hc_s, (1, 0, 2)))
        sv.append(jnp.transpose(v_s, (1, 0, 2)))

        dest_p, dest_s, tbl = _plan(route_p, route_s)
        dest_p, dest_s = dest_p.reshape(-1), dest_s.reshape(-1)
        rows_buf = _scatter_rows(dest_p, h2p, None, nblocks * BM, TP)
        rows_buf = _scatter_rows(dest_s, h2s, rows_buf, nblocks * BM, tps)
        y_rows = _experts(tbl, rows_buf, w1, w3, w2, l)

        last = l == depth - 1
        xp = _combine(dest_p, x1p, route_p, mod, l, fng, y_rows, TP, final_norm=last,
                      g2_row0=ns, tiles_per_g2=seq // TP)
        xs = _combine(dest_s, x1s, route_s, mod, l, fng, y_rows, tps, final_norm=last,
                      seqs_per_tile=nbs)

    y_prompt = xp.reshape(nb_, seq, D)
    y_sample = jnp.transpose(xs.reshape(nsplit, ts, nbs, D), (0, 2, 1, 3)).reshape(ns, ts, D)
    return (y_prompt, y_sample, jnp.stack(pa), jnp.stack(pc), jnp.stack(sa), jnp.stack(sc),
            jnp.stack(sv))
```

```python
import functools

import jax
import jax.numpy as jnp
from jax import lax
from jax.experimental import pallas as pl
from jax.experimental.pallas import tpu as pltpu

F32 = jnp.float32
BF16 = jnp.bfloat16

D = 1024
W = D // 2
KA = 31
KC = 3
NG_A = 8
CHUNK = 128
HEADS = 8
NGRP = 4
EPG = 8
NEXP = NGRP * EPG
FF = D // 2
EPS = 1e-6
IN_COLS = 7 * W + 3 * D
LANES = 128
ROUTE_COLS = LANES

TT = 512
SUBLANES = 8
A_HDR = 32
A_PAD = SUBLANES
A_ROWS = 128
C_HDR = 8
TP = 512
TPP = 2048
BM = 256
VMEM_LIMIT = 56 * 1024 * 1024

NEG = -1e30


def _sigmoid(x):
    return 0.5 * jnp.tanh(0.5 * x) + 0.5


def _silu(x):
    h = 0.5 * x
    return h + h * jnp.tanh(h)


GELU_C = 0.7978845608028654


def _gelu_tanh(x):
    h = 0.5 * x
    return h + h * jnp.tanh(x * (GELU_C + (GELU_C * 0.044715) * (x * x)))


def _rms(x, g):
    return x * lax.rsqrt(jnp.mean(x * x, axis=-1, keepdims=True) + EPS) * g


def _rms_mod(x, g, scale, shift):
    inv = lax.rsqrt(jnp.mean(x * x, axis=-1, keepdims=True) + EPS)
    return x * inv * (g * (1.0 + scale)) + shift


def _dot(a, b):
    return jnp.dot(a, b, preferred_element_type=F32)


def _dot_hi(a, b):
    return jnp.dot(a, b, preferred_element_type=F32, precision=lax.Precision.HIGHEST)


U32 = jnp.uint32
ROW_TILE = D // (2 * LANES)


def _pack_bf16_pair(lo, hi):
    def rne(x):
        b = pltpu.bitcast(x, U32)
        return b + (U32(0x7FFF) + ((b >> 16) & U32(1)))
    return (rne(lo) >> 16) | (rne(hi) & U32(0xFFFF0000))


def _store_row_tiles(ref, val):
    r = val.shape[0]
    for j in range(ROW_TILE):
        lo = val[:, 2 * j * LANES:(2 * j + 1) * LANES]
        hi = val[:, (2 * j + 1) * LANES:(2 * j + 2) * LANES]
        ref[pl.ds(j, r, stride=ROW_TILE), :] = _pack_bf16_pair(lo, hi)


def _load_row_tiles(ref):
    r = ref.shape[0] // ROW_TILE
    parts = []
    for j in range(ROW_TILE):
        w = ref[pl.ds(j, r, stride=ROW_TILE), :]
        parts.append(pltpu.bitcast(w << 16, F32))
        parts.append(pltpu.bitcast(w & U32(0xFFFF0000), F32))
    return jnp.concatenate(parts, axis=1)


def _row_tile(ref, r):
    return ref.at[pl.ds(pl.multiple_of(r * ROW_TILE, ROW_TILE), ROW_TILE)]


def _mod_kernel(c_ref, w_ref, b_ref, o_ref):
    c = c_ref[...]
    s = _silu(c).astype(BF16)
    o_ref[0, 0] = _dot(s, w_ref[0].astype(BF16)) + b_ref[0]


def _adaln(c_all, w_ada, b_ada):
    nl = w_ada.shape[0]
    r = c_all.shape[0]
    return pl.pallas_call(
        _mod_kernel,
        grid=(nl, 6),
        in_specs=[
            pl.BlockSpec((r, D), lambda l, j: (0, 0)),
            pl.BlockSpec((1, D, D), lambda l, j: (l, 0, j)),
            pl.BlockSpec((1, 1, D), lambda l, j: (l, 0, j)),
        ],
        out_specs=pl.BlockSpec((1, 1, r, D), lambda l, j: (l, j, 0, 0)),
        out_shape=jax.ShapeDtypeStruct((nl, 6, r, D), F32),
        compiler_params=pltpu.CompilerParams(
            dimension_semantics=("arbitrary", "arbitrary"), vmem_limit_bytes=VMEM_LIMIT),
        name="adaln_mod",
    )(c_all, w_ada, b_ada.reshape(nl, 1, 6 * D))


def _group_norm_silu(a, gavg_ref, g, b):
    mu = _dot(a.astype(BF16), gavg_ref[...])
    xc = a - mu
    var = _dot((xc * xc).astype(BF16), gavg_ref[...])
    y = xc * lax.rsqrt(var + EPS) * g + b
    return _silu(y)


def _layer_norm(x, g, b):
    mu = jnp.mean(x, axis=-1, keepdims=True)
    xc = x - mu
    var = jnp.mean(xc * xc, axis=-1, keepdims=True)
    return xc * lax.rsqrt(var + EPS) * g + b


def _merge_and_route(x, a, bb, cc, gates_fn, mod, n2g, wbr_ref, wout_ref, wr_ref, br_ref,
                     x1_ref, h2_ref, route_ref):
    g1, sh2, sc2 = mod
    acc = None
    for gi, br_in in enumerate((a, bb, cc)):
        br_half = _dot(br_in.astype(BF16), wbr_ref[gi])
        term = (1.0 + jnp.tanh(gates_fn(gi))) * br_half
        acc = term if acc is None else acc + term
    mixed = _dot(acc.astype(BF16), wout_ref[...])
    x1 = x + g1 * mixed
    x1_ref[...] = x1
    h2 = _rms_mod(x1, n2g, sc2, sh2)
    _store_row_tiles(h2_ref, h2)

    logits = _dot(h2.astype(BF16), wr_ref[...]) + br_ref[...]
    rows = logits.shape[0]
    lane = lax.broadcasted_iota(jnp.int32, (rows, ROUTE_COLS), 1).astype(F32)
    is_g = lane < NGRP
    gl = jnp.where(is_g, logits, NEG)
    gmax = jnp.max(gl, axis=-1, keepdims=True)
    gsel = jnp.min(jnp.where(gl == gmax, lane, float(ROUTE_COLS)), axis=-1, keepdims=True)
    gden = jnp.sum(jnp.where(is_g, jnp.exp(gl - gmax), 0.0), axis=-1, keepdims=True)
    gprob = 1.0 / gden
    lo = NGRP + EPG * gsel
    in_grp = jnp.logical_and(lane >= lo, lane < lo + EPG)
    el = jnp.where(in_grp, logits, NEG)
    v1 = jnp.max(el, axis=-1, keepdims=True)
    i1 = jnp.min(jnp.where(el == v1, lane, float(ROUTE_COLS)), axis=-1, keepdims=True)
    el2 = jnp.where(lane == i1, NEG, el)
    v2 = jnp.max(el2, axis=-1, keepdims=True)
    i2 = jnp.min(jnp.where(el2 == v2, lane, float(ROUTE_COLS)), axis=-1, keepdims=True)
    p2 = jnp.exp(v2 - v1)
    w_a = gprob / (1.0 + p2)
    w_b = gprob * p2 / (1.0 + p2)
    route = jnp.where(lane == 0.0, i1 - NGRP,
                      jnp.where(lane == 1.0, i2 - NGRP,
                                jnp.where(lane == 2.0, w_a,
                                          jnp.where(lane == 3.0, w_b, 0.0))))
    route_ref[...] = route


def _mixer_prompt_kernel(x_ref, mod_ref, n1g_ref, n2g_ref, win_ref, adw_ref, adb_ref, ang_ref,
                         anb_ref, bng_ref, bnb_ref, wsp_ref, bsp_ref, ccw_ref, wbr_ref, wout_ref,
                         gavg_ref, wr_ref, br_ref,
                         x1_ref, h2_ref, route_ref, ha_ref, hc_ref,
                         abuf, cbuf, aconv, *, mod_row0):
    first = pl.program_id(1) == 0
    abuf[0:A_HDR, :] = jnp.where(first, 0.0, abuf[TT:TT + A_HDR, :])
    abuf[A_HDR + TT:A_HDR + TT + A_PAD, :] = jnp.zeros((A_PAD, W), F32)
    cbuf[0:C_HDR, :] = jnp.where(first, 0.0, cbuf[TT:TT + C_HDR, :])

    x = x_ref[...]
    mrow = mod_row0 + pl.program_id(0)
    sh1, sc1, g1, sh2, sc2 = (mod_ref[0, s, pl.ds(mrow, 1), :] for s in range(5))
    h = _rms_mod(x, n1g_ref[...], sc1, sh1).astype(BF16)

    def zsec(lo, hi):
        return _dot(h, win_ref[:, lo:hi])

    a_glu = zsec(0, W) * _sigmoid(zsec(W, 2 * W))
    abuf[A_HDR:A_HDR + TT, :] = a_glu
    off = A_HDR - (KA - 1)
    rc = A_ROWS

    def conv_pass(p):
        lt, half = p // 2, p % 2
        ls = slice(lt * LANES, (lt + 1) * LANES)
        for r0 in range(half * (TT // 2), (half + 1) * (TT // 2), rc):
            y = None
            for r in range(SUBLANES):
                z = None
                for q in range((KA - 1 + off) // SUBLANES + 1):
                    k = SUBLANES * q + r - off
                    if 0 <= k < KA:
                        rows = slice(r0 + SUBLANES * q, r0 + SUBLANES * q + rc + SUBLANES)
                        term = adw_ref[k:k + 1, ls] * abuf[rows, ls]
                        z = term if z is None else z + term
                part = z[r:r + rc, :]
                y = part if y is None else y + part
            aconv[r0:r0 + rc, ls] = y + adb_ref[:, ls]

    conv_pass(0)
    u = _gelu_tanh(zsec(2 * W, 3 * W))
    conv_pass(1)
    v = _layer_norm(_gelu_tanh(zsec(3 * W, 4 * W)), bng_ref[...], bnb_ref[...])
    conv_pass(2)

    cch = zsec(5 * W, 6 * W) * zsec(6 * W, 7 * W)
    cbuf[C_HDR:C_HDR + TT, :] = cch
    coff = C_HDR - (KC - 1)
    ych = (ccw_ref[0:1, :] * cbuf[coff:coff + TT, :]
           + ccw_ref[1:2, :] * cbuf[coff + 1:coff + 1 + TT, :]
           + ccw_ref[2:3, :] * cbuf[coff + 2:coff + 2 + TT, :])
    conv_pass(3)
    cc = zsec(4 * W, 5 * W) * ych
    hc_ref[0] = cbuf[C_HDR + TT - (KC - 1):C_HDR + TT, :]
    conv_pass(4)
    gate_z = [zsec(7 * W, 7 * W + D)]
    conv_pass(5)
    gate_z.append(zsec(7 * W + D, 7 * W + 2 * D))
    conv_pass(6)
    gate_z.append(zsec(7 * W + 2 * D, 7 * W + 3 * D))
    conv_pass(7)
    ha_ref[0] = abuf[A_HDR + TT - (KA - 1):A_HDR + TT, :]

    vb = v.astype(BF16)
    low_head = lax.broadcasted_iota(jnp.int32, (CHUNK, LANES), 1) < (W // HEADS)
    zero_b = jnp.zeros((CHUNK, LANES), BF16)
    s_parts = []
    for c in range(TT // CHUNK):
        tiles = []
        for lt in range(W // LANES):
            vt = vb[c * CHUNK:(c + 1) * CHUNK, lt * LANES:(lt + 1) * LANES]
            stacked = jnp.concatenate([jnp.where(low_head, vt, zero_b),
                                       jnp.where(low_head, zero_b, vt)], axis=0)
            tiles.append(_dot(wsp_ref[lt], stacked))
        s_parts.append(jnp.concatenate(tiles, axis=1) + bsp_ref[...])
    bb = u * jnp.concatenate(s_parts, axis=0)
    a = _group_norm_silu(aconv[...], gavg_ref, ang_ref[...], anb_ref[...])

    _merge_and_route(x, a, bb, cc, lambda gi: gate_z[gi], (g1, sh2, sc2), n2g_ref[...], wbr_ref,
                     wout_ref, wr_ref, br_ref, x1_ref, h2_ref, route_ref)


def _const_spec(shape):
    nd = len(shape)
    return pl.BlockSpec(shape, lambda *_: (0,) * nd, pipeline_mode=pl.Buffered(1))


def _mixer_prompt(x, mod, layer, n, mod_row0, lw):
    t = x.shape[0] // n
    nt = t // TT
    mod_rows = mod.shape[2]
    row_spec = lambda cols: pl.BlockSpec((TT, cols), lambda i, j: (i * nt + j, 0))
    consts = [lw["n1g"], lw["n2g"], lw["w_in"], lw["a_dw_w"], lw["a_dw_b"], lw["a_norm_g"],
              lw["a_norm_b"], lw["b_norm_g"], lw["b_norm_b"], lw["w_sp"], lw["b_sp"], lw["c_conv_w"],
              lw["w_branch"], lw["w_out"], lw["gavg"], lw["w_router"], lw["b_router"]]
    return pl.pallas_call(
        functools.partial(_mixer_prompt_kernel, mod_row0=mod_row0),
        grid=(n, nt),
        in_specs=[row_spec(D),
                  pl.BlockSpec((1, 6, mod_rows, D), lambda i, j: (layer, 0, 0, 0),
                               pipeline_mode=pl.Buffered(1))]
        + [_const_spec(c.shape) for c in consts],
        out_specs=[row_spec(D),
                   pl.BlockSpec((TT * ROW_TILE, LANES), lambda i, j: (i * nt + j, 0)),
                   row_spec(ROUTE_COLS),
                   pl.BlockSpec((1, KA - 1, W), lambda i, j: (i, 0, 0)),
                   pl.BlockSpec((1, KC - 1, W), lambda i, j: (i, 0, 0))],
        out_shape=[jax.ShapeDtypeStruct((n * t, D), F32),
                   jax.ShapeDtypeStruct((n * t * ROW_TILE, LANES), U32),
                   jax.ShapeDtypeStruct((n * t, ROUTE_COLS), F32),
                   jax.ShapeDtypeStruct((n, KA - 1, W), F32),
                   jax.ShapeDtypeStruct((n, KC - 1, W), F32)],
        scratch_shapes=[pltpu.VMEM((A_HDR + TT + A_PAD, W), F32),
                        pltpu.VMEM((C_HDR + TT, W), F32),
                        pltpu.VMEM((TT, W), F32)],
        compiler_params=pltpu.CompilerParams(
            dimension_semantics=("arbitrary", "arbitrary"), vmem_limit_bytes=VMEM_LIMIT),
        name="mixer_prompt",
    )(x, mod, *consts)


def _mixer_sample_kernel(x_ref, mod_ref, hista_ref, histc_ref, n1g_ref, n2g_ref, win_ref, adw_ref,
                         adb_ref, ang_ref, anb_ref, bng_ref, bnb_ref, wsp4_ref, bsp4_ref,
                         ccw_ref, wbr_ref, wout_ref, gavg_ref, wr_ref, br_ref,
                         x1_ref, h2_ref, route_ref, ha_ref, hc_ref, v_ref):
    ts, nb = v_ref.shape[0], v_ref.shape[1]
    x = x_ref[...]

    def modrow(i):
        return jnp.concatenate([mod_ref[0, i]] * ts, axis=0)

    h = (_rms(x, n1g_ref[...]) * (1.0 + modrow(1)) + modrow(0)).astype(BF16)

    def zsec(lo, hi):
        return _dot(h, win_ref[:, lo:hi])

    def tslab(arr, j):
        return arr[j * nb:(j + 1) * nb, :]

    a_glu = zsec(0, W) * _sigmoid(zsec(W, 2 * W))
    kh = KA - 1
    a_conv = []
    for tq in range(ts):
        acc = jnp.zeros((nb, W), F32) + adb_ref[...]
        for r in range(tq, kh):
            acc = acc + adw_ref[r - tq:r - tq + 1, :] * hista_ref[r]
        for j in range(tq + 1):
            acc = acc + adw_ref[kh + j - tq:kh + j - tq + 1, :] * tslab(a_glu, j)
        a_conv.append(acc)
    a = _group_norm_silu(jnp.concatenate(a_conv, axis=0), gavg_ref, ang_ref[...], anb_ref[...])
    for r in range(kh - ts):
        ha_ref[r] = hista_ref[r + ts]
    for j in range(ts):
        ha_ref[kh - ts + j] = tslab(a_glu, j)

    u = _gelu_tanh(zsec(2 * W, 3 * W))
    v = _layer_norm(_gelu_tanh(zsec(3 * W, 4 * W)), bng_ref[...], bnb_ref[...])
    for j in range(ts):
        v_ref[j] = tslab(v, j)
    s_rows = []
    for tq in range(ts):
        s = jnp.zeros((nb, W), F32) + bsp4_ref[tq:tq + 1, :]
        for sq in range(tq + 1):
            s = s + wsp4_ref[tq * ts + sq:tq * ts + sq + 1, :] * tslab(v, sq)
        s_rows.append(s)
    bb = u * jnp.concatenate(s_rows, axis=0)

    cch = zsec(5 * W, 6 * W) * zsec(6 * W, 7 * W)
    xp = [histc_ref[r] for r in range(KC - 1)] + [tslab(cch, j) for j in range(ts)]
    ych = jnp.concatenate(
        [sum(ccw_ref[k:k + 1, :] * xp[tq + k] for k in range(KC)) for tq in range(ts)], axis=0)
    cc = zsec(4 * W, 5 * W) * ych
    for r in range(KC - 1):
        hc_ref[r] = xp[ts + r]

    def gates_fn(gi):
        return zsec(7 * W + gi * D, 7 * W + (gi + 1) * D)

    _merge_and_route(x, a, bb, cc, gates_fn, (modrow(2), modrow(3), modrow(4)), n2g_ref[...],
                     wbr_ref, wout_ref, wr_ref, br_ref, x1_ref, h2_ref, route_ref)


def _mixer_sample(x_rows, mod, layer, hist_a_tm, hist_c_tm, lw, nsplit):
    n = hist_a_tm.shape[1]
    ts = x_rows.shape[0] // n
    nb = n // nsplit
    rows = ts * nb
    consts = [lw["n1g"], lw["n2g"], lw["w_in"], lw["a_dw_w"], lw["a_dw_b"],
              lw["a_norm_g"], lw["a_norm_b"], lw["b_norm_g"], lw["b_norm_b"], lw["w_sp4"], lw["b_sp4"],
              lw["c_conv_w"], lw["w_branch"], lw["w_out"], lw["gavg"], lw["w_router"], lw["b_router"]]
    seq3 = lambda k, cols: pl.BlockSpec((k, nb, cols), lambda i: (0, i, 0))
    row_spec = lambda cols: pl.BlockSpec((rows, cols), lambda i: (i, 0))
    return pl.pallas_call(
        _mixer_sample_kernel,
        grid=(nsplit,),
        in_specs=[row_spec(D), pl.BlockSpec((1, 6, nb, D), lambda i: (layer, 0, i, 0)),
                  seq3(KA - 1, W), seq3(KC - 1, W)]
        + [_const_spec(c.shape) for c in consts],
        out_specs=[row_spec(D),
                   pl.BlockSpec((rows * ROW_TILE, LANES), lambda i: (i, 0)),
                   row_spec(ROUTE_COLS),
                   seq3(KA - 1, W), seq3(KC - 1, W), seq3(ts, W)],
        out_shape=[jax.ShapeDtypeStruct((ts * n, D), F32),
                   jax.ShapeDtypeStruct((ts * n * ROW_TILE, LANES), U32),
                   jax.ShapeDtypeStruct((ts * n, ROUTE_COLS), F32),
                   jax.ShapeDtypeStruct((KA - 1, n, W), F32),
                   jax.ShapeDtypeStruct((KC - 1, n, W), F32),
                   jax.ShapeDtypeStruct((ts, n, W), F32)],
        compiler_params=pltpu.CompilerParams(
            dimension_semantics=("arbitrary",), vmem_limit_bytes=VMEM_LIMIT),
        name="mixer_sample",
    )(x_rows, mod, hist_a_tm, hist_c_tm, *consts)


def _plan_kernel(rp_ref, rs_ref, tri_ref, destp_ref, dests_ref, tbl_ref, carry, pstart, *, ntp):
    ph = pl.program_id(0)
    i = pl.program_id(1)
    nt = pl.num_programs(1)
    lane = lax.broadcasted_iota(jnp.int32, (TP, ROUTE_COLS), 1).astype(F32)

    @pl.when(jnp.logical_and(ph == 0, i == 0))
    def _():
        carry[...] = jnp.zeros_like(carry)

    def sub_tile(r, dest_ref, row0):
        hot_a = lane == r[:, 0:1]
        hot_b = lane == r[:, 1:2]
        s = jnp.where(jnp.logical_or(hot_a, hot_b), 1.0, 0.0)

        @pl.when(ph == 1)
        def _():
            pre = _dot(tri_ref[...], s.astype(BF16)) + carry[...] + pstart[...]
            d_a = jnp.sum(jnp.where(hot_a, pre, 0.0), axis=-1, keepdims=True)
            d_b = jnp.sum(jnp.where(hot_b, pre, 0.0), axis=-1, keepdims=True)
            l2 = lax.broadcasted_iota(jnp.int32, (TP, 2), 1)
            dest_ref[row0:row0 + TP, :] = jnp.where(l2 == 0, d_a, d_b).astype(jnp.int32)

        carry[...] = carry[...] + jnp.sum(s, axis=0, keepdims=True)

    @pl.when(i < ntp)
    def _():
        for sub in range(rp_ref.shape[0] // TP):
            sub_tile(rp_ref[sub * TP:(sub + 1) * TP, :], destp_ref, sub * TP)

    @pl.when(i == ntp)
    def _():
        for sub in range(rs_ref.shape[0] // TP):
            sub_tile(rs_ref[sub * TP:(sub + 1) * TP, :], dests_ref, sub * TP)

    @pl.when(jnp.logical_and(ph == 0, i == nt - 1))
    def _():
        cnt = carry[...]
        nblk = jnp.floor((cnt + (BM - 1)) * (1.0 / BM))
        ri = lax.broadcasted_iota(jnp.int32, (ROUTE_COLS, ROUTE_COLS), 0)
        ci = lax.broadcasted_iota(jnp.int32, (ROUTE_COLS, ROUTE_COLS), 1)
        upper = jnp.where(ri <= ci, 1.0, 0.0)
        cum = _dot_hi(jnp.broadcast_to(nblk, (8, ROUTE_COLS)), upper)[0:1, :]
        cum_ex = cum - nblk
        pstart[...] = cum_ex * BM
        row = lax.broadcasted_iota(jnp.int32, (SUBLANES, ROUTE_COLS), 0)
        tbl = jnp.where(row == 0, cum_ex, jnp.where(row == 1, cnt, 0.0))
        tbl_ref[...] = tbl.astype(jnp.int32)
        carry[...] = jnp.zeros_like(carry)


def _plan(route_p, route_s):
    rows_p, rows_s = route_p.shape[0], route_s.shape[0]
    ntp = rows_p // TPP
    nbp = SUBLANES
    ri = lax.broadcasted_iota(jnp.int32, (TP, TP), 0)
    ci = lax.broadcasted_iota(jnp.int32, (TP, TP), 1)
    tri = jnp.where(ci < ri, 1.0, 0.0).astype(BF16)
    return pl.pallas_call(
        functools.partial(_plan_kernel, ntp=ntp),
        grid=(2, ntp + 1),
        in_specs=[pl.BlockSpec((TPP, ROUTE_COLS), lambda p, i: (jnp.minimum(i, ntp - 1), 0)),
                  pl.BlockSpec((rows_s, ROUTE_COLS), lambda p, i: (0, 0)),
                  pl.BlockSpec((TP, TP), lambda p, i: (0, 0))],
        out_specs=[pl.BlockSpec((TPP, 2), lambda p, i: (jnp.minimum(i, ntp - 1) * p, 0)),
                   pl.BlockSpec((rows_s, 2), lambda p, i: (0, 0)),
                   pl.BlockSpec((nbp, ROUTE_COLS), lambda p, i: (0, 0))],
        out_shape=[jax.ShapeDtypeStruct((rows_p, 2), jnp.int32),
                   jax.ShapeDtypeStruct((rows_s, 2), jnp.int32),
                   jax.ShapeDtypeStruct((nbp, ROUTE_COLS), jnp.int32)],
        scratch_shapes=[pltpu.VMEM((1, ROUTE_COLS), F32), pltpu.VMEM((1, ROUTE_COLS), F32)],
        compiler_params=pltpu.CompilerParams(
            dimension_semantics=("arbitrary", "arbitrary"), vmem_limit_bytes=VMEM_LIMIT),
        name="moe_plan",
    )(route_p, route_s, tri)


def _scatter_kernel(dest_ref, h_ref, *rest):
    rows_ref, sem = rest[-2:]
    tp = h_ref.shape[0] // ROW_TILE

    def row_copy(r, k):
        d = dest_ref[2 * r + k]
        return pltpu.make_async_copy(_row_tile(h_ref, r), _row_tile(rows_ref, d), sem)

    def issue(r, c):
        row_copy(r, 0).start(priority=0)
        row_copy(r, 1).start(priority=1)
        return c

    lax.fori_loop(0, tp, issue, 0, unroll=8)

    def drain(r, c):
        row_copy(r, 0).wait()
        row_copy(r, 1).wait()
        return c

    lax.fori_loop(0, tp, drain, 0, unroll=8)


def _scatter_rows(dest_flat, h2, rows_buf, n_rows, tp):
    nt = h2.shape[0] // (tp * ROW_TILE)
    in_specs = [pl.BlockSpec((2 * tp,), lambda i: (i,), memory_space=pltpu.SMEM),
                pl.BlockSpec((tp * ROW_TILE, LANES), lambda i: (i, 0))]
    args = [dest_flat, h2]
    aliases = {}
    if rows_buf is not None:
        in_specs.append(pl.BlockSpec(memory_space=pl.ANY))
        args.append(rows_buf)
        aliases = {2: 0}
    return pl.pallas_call(
        _scatter_kernel,
        grid=(nt,),
        in_specs=in_specs,
        out_specs=pl.BlockSpec(memory_space=pl.ANY),
        out_shape=jax.ShapeDtypeStruct((n_rows * ROW_TILE, LANES), U32),
        scratch_shapes=[pltpu.SemaphoreType.DMA(())],
        input_output_aliases=aliases,
        compiler_params=pltpu.CompilerParams(
            dimension_semantics=("arbitrary",), vmem_limit_bytes=VMEM_LIMIT,
            has_side_effects=True),
        name="moe_scatter",
    )(*args)


ROW_DMA_PRIORITY = 1
READ_AHEAD = 4
ROW_SLOTS = 6


def _expert_kernel(blk0_ref, cnt_ref, x_hbm, w1_ref, w3_ref, w2_ref, y_hbm,
                   xbuf, ybuf, w1b, w3b, w2b, sem_in, sem_out):
    e = pl.program_id(0)
    cnt = cnt_ref[e]
    nblk = lax.shift_right_logical(cnt + (BM - 1), BM.bit_length() - 1)
    blk0 = blk0_ref[e]
    last_e = NEXP - 1
    n_used = blk0_ref[last_e] + lax.shift_right_logical(cnt_ref[last_e] + (BM - 1),
                                                        BM.bit_length() - 1)
    blk_rows = BM * ROW_TILE

    def block_rows(ref, g):
        return ref.at[pl.ds(pl.multiple_of(g * blk_rows, blk_rows), blk_rows)]

    def in_copy(g):
        slot = lax.rem(g, ROW_SLOTS)
        return pltpu.make_async_copy(block_rows(x_hbm, g), xbuf.at[slot], sem_in.at[slot])

    def out_copy(g):
        slot = lax.rem(g, ROW_SLOTS)
        return pltpu.make_async_copy(ybuf.at[slot], block_rows(y_hbm, g), sem_out.at[slot])

    @pl.when(nblk > 0)
    def _():
        @pl.when(blk0 == 0)
        def _():
            for g0 in range(READ_AHEAD):
                @pl.when(g0 < n_used)
                def _():
                    in_copy(g0).start(priority=ROW_DMA_PRIORITY)

        w1b[...] = w1_ref[0, 0].astype(BF16)
        w3b[...] = w3_ref[0, 0].astype(BF16)
        w2b[...] = w2_ref[0, 0].astype(BF16)

        def chain(j, nb):
            g = blk0 + j
            for q in range(nb):
                @pl.when(g + READ_AHEAD + q < n_used)
                def _():
                    in_copy(g + READ_AHEAD + q).start(priority=ROW_DMA_PRIORITY)
            for q in range(nb):
                in_copy(g + q).wait()
            for q in range(nb):
                @pl.when(g + q >= ROW_SLOTS)
                def _():
                    out_copy(g + q - ROW_SLOTS).wait()

            row = lax.broadcasted_iota(jnp.int32, (BM, D), 0)
            xs = [jnp.where(row < cnt - (j + q) * BM,
                            _load_row_tiles(xbuf.at[lax.rem(g + q, ROW_SLOTS)]), 0.0).astype(BF16)
                  for q in range(nb)]
            x = xs[0] if nb == 1 else jnp.concatenate(xs, axis=0)
            h1 = _dot(x, w1b[...])
            h3 = _dot(x, w3b[...])
            act = (_silu(h1) * h3).astype(BF16)
            y = _dot(act, w2b[...])
            for q in range(nb):
                _store_row_tiles(ybuf.at[lax.rem(g + q, ROW_SLOTS)], y[q * BM:(q + 1) * BM, :])
                out_copy(g + q).start(priority=ROW_DMA_PRIORITY)

        def pair(p, carry):
            chain(2 * p, 2)
            return carry

        lax.fori_loop(0, lax.shift_right_logical(nblk, 1), pair, 0)

        @pl.when(lax.rem(nblk, 2) == 1)
        def _():
            chain(nblk - 1, 1)

    @pl.when(e == last_e)
    def _():
        for back in range(ROW_SLOTS, 0, -1):
            @pl.when(n_used >= back)
            def _():
                out_copy(n_used - back).wait()


def _experts(tbl, rows_buf, w1, w3, w2, layer):
    blk0, cnt = tbl[0, :NEXP], tbl[1, :NEXP]
    wspec = lambda shape: pl.BlockSpec((1, 1) + shape, lambda e, blk0, cnt: (layer, e, 0, 0))
    return pl.pallas_call(
        _expert_kernel,
        grid_spec=pltpu.PrefetchScalarGridSpec(
            num_scalar_prefetch=2,
            grid=(NEXP,),
            in_specs=[pl.BlockSpec(memory_space=pl.ANY),
                      wspec((D, FF)), wspec((D, FF)), wspec((FF, D))],
            out_specs=pl.BlockSpec(memory_space=pl.ANY),
            scratch_shapes=[pltpu.VMEM((ROW_SLOTS, BM * ROW_TILE, LANES), U32),
                            pltpu.VMEM((ROW_SLOTS, BM * ROW_TILE, LANES), U32),
                            pltpu.VMEM((D, FF), BF16), pltpu.VMEM((D, FF), BF16),
                            pltpu.VMEM((FF, D), BF16),
                            pltpu.SemaphoreType.DMA((ROW_SLOTS,)),
                            pltpu.SemaphoreType.DMA((ROW_SLOTS,))],
        ),
        out_shape=jax.ShapeDtypeStruct(rows_buf.shape, U32),
        compiler_params=pltpu.CompilerParams(
            dimension_semantics=("arbitrary",), vmem_limit_bytes=VMEM_LIMIT,
            has_side_effects=True),
        name="moe_experts",
    )(blk0, cnt, rows_buf, w1, w3, w2)


def _combine_kernel(dest_ref, dest_next_ref, x1_ref, route_ref, g2_ref, fng_ref, y_ref, o_ref,
                    ybuf, sem, *, final_norm, g2_row0, tiles_per_g2):
    tp = x1_ref.shape[0]
    i = pl.program_id(0)
    n = pl.num_programs(0)
    slot = lax.rem(i, 2)

    def gather(d_ref, s):
        def row_copy(r, k):
            return pltpu.make_async_copy(_row_tile(y_ref, d_ref[2 * r + k]),
                                         _row_tile(ybuf.at[s, k], r), sem.at[s])

        def issue(r, c):
            row_copy(r, 0).start(priority=0)
            row_copy(r, 1).start(priority=1)
            return c

        lax.fori_loop(0, tp, issue, 0, unroll=8)

    @pl.when(i == 0)
    def _():
        gather(dest_ref, 0)

    @pl.when(i + 1 < n)
    def _():
        gather(dest_next_ref, 1 - slot)

    for k in range(2):
        pltpu.make_async_copy(y_ref.at[pl.ds(0, tp * ROW_TILE)], ybuf.at[slot, k], sem.at[slot]).wait()

    route = route_ref[...]
    w_a = route[:, 2:3]
    w_b = route[:, 3:4]
    if tiles_per_g2:
        g2 = g2_ref[0, 0, pl.ds(g2_row0 + i // tiles_per_g2, 1), :]
    else:
        g2 = jnp.concatenate([g2_ref[0, 0]] * (tp // g2_ref.shape[2]), axis=0)
    y = (w_a * _load_row_tiles(ybuf.at[slot, 0]) + w_b * _load_row_tiles(ybuf.at[slot, 1]))
    x2 = x1_ref[...] + g2 * y
    if final_norm:
        x2 = _rms(x2, fng_ref[...])
    o_ref[...] = x2


def _combine(dest_flat, x1, route, mod, layer, fng, y_rows, tp, *, final_norm, g2_row0=0,
             tiles_per_g2=0, seqs_per_tile=0):
    nt = x1.shape[0] // tp
    if tiles_per_g2:
        g2_spec = pl.BlockSpec((1, 1, mod.shape[2], D), lambda i: (layer, 5, 0, 0))
    else:
        g2_spec = pl.BlockSpec((1, 1, seqs_per_tile, D), lambda i: (layer, 5, i, 0))
    return pl.pallas_call(
        functools.partial(_combine_kernel, final_norm=final_norm, g2_row0=g2_row0,
                          tiles_per_g2=tiles_per_g2),
        grid=(nt,),
        in_specs=[pl.BlockSpec((2 * tp,), lambda i: (i,), memory_space=pltpu.SMEM),
                  pl.BlockSpec((2 * tp,), lambda i: (jnp.minimum(i + 1, nt - 1),),
                               memory_space=pltpu.SMEM),
                  pl.BlockSpec((tp, D), lambda i: (i, 0)),
                  pl.BlockSpec((tp, ROUTE_COLS), lambda i: (i, 0)),
                  g2_spec,
                  pl.BlockSpec((1, D), lambda i: (0, 0)),
                  pl.BlockSpec(memory_space=pl.ANY)],
        out_specs=pl.BlockSpec((tp, D), lambda i: (i, 0)),
        out_shape=jax.ShapeDtypeStruct(x1.shape, F32),
        scratch_shapes=[pltpu.VMEM((2, 2, tp * ROW_TILE, LANES), U32),
                        pltpu.SemaphoreType.DMA((2,))],
        compiler_params=pltpu.CompilerParams(
            dimension_semantics=("arbitrary",), vmem_limit_bytes=VMEM_LIMIT),
        name="moe_combine",
    )(dest_flat, dest_flat, x1, route, mod, fng, y_rows)


def _layer_weights(l, p):
    hd = W // HEADS
    gi = jnp.arange(W) // (W // NG_A)
    gavg = jnp.where(gi[:, None] == gi[None, :], 1.0 / (W // NG_A), 0.0).astype(BF16)
    tril = jnp.tril(jnp.ones((CHUNK, CHUNK), dtype=bool))
    w_sp = jnp.where(tril[None], p["b_spatial_w"][l], 0.0)
    w_router = jnp.concatenate(
        [p["w_router_g"][l], p["w_router_e"][l],
         jnp.zeros((D, ROUTE_COLS - NGRP - NEXP), F32)], axis=1)
    b_router = jnp.concatenate(
        [p["b_router_g"][l], p["b_router_e"][l],
         jnp.zeros((ROUTE_COLS - NGRP - NEXP,), F32)])[None, :]
    return {
        "n1g": p["norm1_g"][l][None, :], "n2g": p["norm2_g"][l][None, :],
        "w_in": jnp.concatenate([p["w_in"][l][:, :7 * W], 0.5 * p["w_in"][l][:, 7 * W:]],
                                axis=1).astype(BF16),
        "a_dw_w": p["a_dw_w"][l], "a_dw_b": p["a_dw_b"][l][None, :],
        "a_norm_g": p["a_norm_g"][l][None, :], "a_norm_b": p["a_norm_b"][l][None, :],
        "b_norm_g": p["b_norm_g"][l][None, :], "b_norm_b": p["b_norm_b"][l][None, :],
        "w_sp": jnp.concatenate([w_sp[0::2], w_sp[1::2]], axis=2).astype(BF16),
        "b_sp": jnp.repeat(p["b_spatial_b"][l].T, hd, axis=1),
        "c_conv_w": p["c_conv_w"][l],
        "w_branch": (0.5 * p["w_branch"][l]).astype(BF16), "w_out": p["w_out"][l].astype(BF16),
        "gavg": gavg, "w_router": w_router.astype(BF16), "b_router": b_router,
    }


def _sample_spatial(l, p, ts):
    hd = W // HEADS
    w = p["b_spatial_w"][l][:, :ts, :ts]
    w = jnp.where(jnp.tril(jnp.ones((ts, ts), dtype=bool))[None], w, 0.0)
    w4 = jnp.repeat(jnp.transpose(w, (1, 2, 0)).reshape(ts * ts, HEADS), hd, axis=1)
    b4 = jnp.repeat(p["b_spatial_b"][l][:, :ts].T, hd, axis=1)
    return w4, b4


def kernel(x_prompt, x_sample, c_prompt, c_sample, state_conv_a, state_conv_c, norm1_g, norm2_g, w_ada, b_ada, w_in, a_dw_w, a_dw_b, a_norm_g, a_norm_b, b_norm_g, b_norm_b, b_spatial_w, b_spatial_b, c_conv_w, w_branch, w_out, w_router_g, b_router_g, w_router_e, b_router_e, w1, w3, w2, final_norm_g):
    p = dict(norm1_g=norm1_g, norm2_g=norm2_g, w_in=w_in, a_dw_w=a_dw_w, a_dw_b=a_dw_b,
             a_norm_g=a_norm_g, a_norm_b=a_norm_b, b_norm_g=b_norm_g, b_norm_b=b_norm_b,
             b_spatial_w=b_spatial_w, b_spatial_b=b_spatial_b, c_conv_w=c_conv_w,
             w_branch=w_branch, w_out=w_out, w_router_g=w_router_g, b_router_g=b_router_g,
             w_router_e=w_router_e, b_router_e=b_router_e)
    depth = w_in.shape[0]
    nb_, seq, _ = x_prompt.shape
    ns, ts, _ = x_sample.shape
    n_tok = nb_ * seq + ns * ts
    nblocks = -(-(2 * n_tok) // BM) + NEXP
    nsplit = 2
    nbs = ns // nsplit
    tps = ts * nbs
    assert seq % TT == 0 and TT % CHUNK == 0 and seq % TP == 0 and (nb_ * seq) % TPP == 0
    assert (ts * ns) % TP == 0 and ns % SUBLANES == 0

    mod = _adaln(jnp.concatenate([c_sample, c_prompt], axis=0), w_ada, b_ada)

    xp = x_prompt.reshape(nb_ * seq, D)
    xs = jnp.transpose(x_sample.reshape(nsplit, nbs, ts, D), (0, 2, 1, 3)).reshape(ts * ns, D)
    fng = final_norm_g[None, :]
    pa, pc, sa, sc, sv = [], [], [], [], []
    for l in range(depth):
        lw = _layer_weights(l, p)
        lw["w_sp4"], lw["b_sp4"] = _sample_spatial(l, p, ts)

        x1p, h2p, route_p, ha, hc = _mixer_prompt(xp, mod, l, nb_, ns, lw)
        pa.append(ha)
        pc.append(hc)
        x1s, h2s, route_s, ha_s, hc_s, v_s = _mixer_sample(
            xs, mod, l, jnp.transpose(state_conv_a[l], (1, 0, 2)),
            jnp.transpose(state_conv_c[l], (1, 0, 2)), lw, nsplit)
        sa.append(jnp.transpose(ha_s, (1, 0, 2)))
        sc.append(jnp.transpose(hc_s, (1, 0, 2)))
        sv.append(jnp.transpose(v_s, (1, 0, 2)))

        dest_p, dest_s, tbl = _plan(route_p, route_s)
        dest_p, dest_s = dest_p.reshape(-1), dest_s.reshape(-1)
        rows_buf = _scatter_rows(dest_p, h2p, None, nblocks * BM, TP)
        rows_buf = _scatter_rows(dest_s, h2s, rows_buf, nblocks * BM, tps)
        y_rows = _experts(tbl, rows_buf, w1, w3, w2, l)

        last = l == depth - 1
        xp = _combine(dest_p, x1p, route_p, mod, l, fng, y_rows, TP, final_norm=last,
                      g2_row0=ns, tiles_per_g2=seq // TP)
        xs = _combine(dest_s, x1s, route_s, mod, l, fng, y_rows, tps, final_norm=last,
                      seqs_per_tile=nbs)

    y_prompt = xp.reshape(nb_, seq, D)
    y_sample = jnp.transpose(xs.reshape(nsplit, ts, nbs, D), (0, 2, 1, 3)).reshape(ns, ts, D)
    return (y_prompt, y_sample, jnp.stack(pa), jnp.stack(pc), jnp.stack(sa), jnp.stack(sc),
            jnp.stack(sv))
```

```python
import functools

import jax
import jax.numpy as jnp
from jax import lax
from jax.experimental import pallas as pl
from jax.experimental.pallas import tpu as pltpu

F32 = jnp.float32
BF16 = jnp.bfloat16

D = 1024
W = D // 2
KA = 31
KC = 3
NG_A = 8
CHUNK = 128
HEADS = 8
NGRP = 4
EPG = 8
NEXP = NGRP * EPG
FF = D // 2
EPS = 1e-6
IN_COLS = 7 * W + 3 * D
LANES = 128
ROUTE_COLS = LANES

TT = 512
SUBLANES = 8
A_HDR = 32
A_PAD = SUBLANES
A_ROWS = 128
C_HDR = 8
TP = 512
TPP = 2048
BM = 256
VMEM_LIMIT = 56 * 1024 * 1024

NEG = -1e30


def _sigmoid(x):
    return 0.5 * jnp.tanh(0.5 * x) + 0.5


def _silu(x):
    h = 0.5 * x
    return h + h * jnp.tanh(h)


GELU_C = 0.7978845608028654


def _gelu_tanh(x):
    h = 0.5 * x
    return h + h * jnp.tanh(x * (GELU_C + (GELU_C * 0.044715) * (x * x)))


def _rms(x, g):
    return x * lax.rsqrt(jnp.mean(x * x, axis=-1, keepdims=True) + EPS) * g


def _rms_mod(x, g, scale, shift):
    inv = lax.rsqrt(jnp.mean(x * x, axis=-1, keepdims=True) + EPS)
    return x * inv * (g * (1.0 + scale)) + shift


def _dot(a, b):
    return jnp.dot(a, b, preferred_element_type=F32)


def _dot_hi(a, b):
    return jnp.dot(a, b, preferred_element_type=F32, precision=lax.Precision.HIGHEST)


U32 = jnp.uint32
ROW_TILE = D // (2 * LANES)


def _pack_bf16_pair(lo, hi):
    def rne(x):
        b = pltpu.bitcast(x, U32)
        return b + (U32(0x7FFF) + ((b >> 16) & U32(1)))
    return (rne(lo) >> 16) | (rne(hi) & U32(0xFFFF0000))


def _store_row_tiles(ref, val):
    r = val.shape[0]
    for j in range(ROW_TILE):
        lo = val[:, 2 * j * LANES:(2 * j + 1) * LANES]
        hi = val[:, (2 * j + 1) * LANES:(2 * j + 2) * LANES]
        ref[pl.ds(j, r, stride=ROW_TILE), :] = _pack_bf16_pair(lo, hi)


def _load_row_tiles(ref):
    r = ref.shape[0] // ROW_TILE
    parts = []
    for j in range(ROW_TILE):
        w = ref[pl.ds(j, r, stride=ROW_TILE), :]
        parts.append(pltpu.bitcast(w << 16, F32))
        parts.append(pltpu.bitcast(w & U32(0xFFFF0000), F32))
    return jnp.concatenate(parts, axis=1)


def _row_tile(ref, r):
    return ref.at[pl.ds(pl.multiple_of(r * ROW_TILE, ROW_TILE), ROW_TILE)]


def _mod_kernel(c_ref, w_ref, b_ref, o_ref):
    c = c_ref[...]
    s = _silu(c).astype(BF16)
    o_ref[0, 0] = _dot(s, w_ref[0].astype(BF16)) + b_ref[0]


def _adaln(c_all, w_ada, b_ada):
    nl = w_ada.shape[0]
    r = c_all.shape[0]
    return pl.pallas_call(
        _mod_kernel,
        grid=(nl, 6),
        in_specs=[
            pl.BlockSpec((r, D), lambda l, j: (0, 0)),
            pl.BlockSpec((1, D, D), lambda l, j: (l, 0, j)),
            pl.BlockSpec((1, 1, D), lambda l, j: (l, 0, j)),
        ],
        out_specs=pl.BlockSpec((1, 1, r, D), lambda l, j: (l, j, 0, 0)),
        out_shape=jax.ShapeDtypeStruct((nl, 6, r, D), F32),
        compiler_params=pltpu.CompilerParams(
            dimension_semantics=("arbitrary", "arbitrary"), vmem_limit_bytes=VMEM_LIMIT),
        name="adaln_mod",
    )(c_all, w_ada, b_ada.reshape(nl, 1, 6 * D))


def _group_norm_silu(a, gavg_ref, g, b):
    mu = _dot(a.astype(BF16), gavg_ref[...])
    xc = a - mu
    var = _dot((xc * xc).astype(BF16), gavg_ref[...])
    y = xc * lax.rsqrt(var + EPS) * g + b
    return _silu(y)


def _layer_norm(x, g, b):
    mu = jnp.mean(x, axis=-1, keepdims=True)
    xc = x - mu
    var = jnp.mean(xc * xc, axis=-1, keepdims=True)
    return xc * lax.rsqrt(var + EPS) * g + b


def _merge_and_route(x, a, bb, cc, gates_fn, mod, n2g, wbr_ref, wout_ref, wr_ref, br_ref,
                     x1_ref, h2_ref, route_ref):
    g1, sh2, sc2 = mod
    acc = None
    for gi, br_in in enumerate((a, bb, cc)):
        br = _dot(br_in.astype(BF16), wbr_ref[gi])
        term = _sigmoid(gates_fn(gi)) * br
        acc = term if acc is None else acc + term
    mixed = _dot(acc.astype(BF16), wout_ref[...])
    x1 = x + g1 * mixed
    x1_ref[...] = x1
    h2 = _rms_mod(x1, n2g, sc2, sh2)
    _store_row_tiles(h2_ref, h2)

    logits = _dot(h2.astype(BF16), wr_ref[...]) + br_ref[...]
    rows = logits.shape[0]
    lane = lax.broadcasted_iota(jnp.int32, (rows, ROUTE_COLS), 1).astype(F32)
    is_g = lane < NGRP
    gl = jnp.where(is_g, logits, NEG)
    gmax = jnp.max(gl, axis=-1, keepdims=True)
    gsel = jnp.min(jnp.where(gl == gmax, lane, float(ROUTE_COLS)), axis=-1, keepdims=True)
    gden = jnp.sum(jnp.where(is_g, jnp.exp(gl - gmax), 0.0), axis=-1, keepdims=True)
    gprob = 1.0 / gden
    lo = NGRP + EPG * gsel
    in_grp = jnp.logical_and(lane >= lo, lane < lo + EPG)
    el = jnp.where(in_grp, logits, NEG)
    v1 = jnp.max(el, axis=-1, keepdims=True)
    i1 = jnp.min(jnp.where(el == v1, lane, float(ROUTE_COLS)), axis=-1, keepdims=True)
    el2 = jnp.where(lane == i1, NEG, el)
    v2 = jnp.max(el2, axis=-1, keepdims=True)
    i2 = jnp.min(jnp.where(el2 == v2, lane, float(ROUTE_COLS)), axis=-1, keepdims=True)
    p2 = jnp.exp(v2 - v1)
    w_a = gprob / (1.0 + p2)
    w_b = gprob * p2 / (1.0 + p2)
    route = jnp.where(lane == 0.0, i1 - NGRP,
                      jnp.where(lane == 1.0, i2 - NGRP,
                                jnp.where(lane == 2.0, w_a,
                                          jnp.where(lane == 3.0, w_b, 0.0))))
    route_ref[...] = route


def _mixer_prompt_kernel(x_ref, mod_ref, n1g_ref, n2g_ref, win_ref, adw_ref, adb_ref, ang_ref,
                         anb_ref, bng_ref, bnb_ref, wsp_ref, bsp_ref, ccw_ref, wbr_ref, wout_ref,
                         gavg_ref, wr_ref, br_ref,
                         x1_ref, h2_ref, route_ref, ha_ref, hc_ref,
                         abuf, cbuf, aconv, *, mod_row0):
    first = pl.program_id(1) == 0
    abuf[0:A_HDR, :] = jnp.where(first, 0.0, abuf[TT:TT + A_HDR, :])
    abuf[A_HDR + TT:A_HDR + TT + A_PAD, :] = jnp.zeros((A_PAD, W), F32)
    cbuf[0:C_HDR, :] = jnp.where(first, 0.0, cbuf[TT:TT + C_HDR, :])

    x = x_ref[...]
    mrow = mod_row0 + pl.program_id(0)
    sh1, sc1, g1, sh2, sc2 = (mod_ref[0, s, pl.ds(mrow, 1), :] for s in range(5))
    h = _rms_mod(x, n1g_ref[...], sc1, sh1).astype(BF16)

    def zsec(lo, hi):
        return _dot(h, win_ref[:, lo:hi])

    a_glu = zsec(0, W) * _sigmoid(zsec(W, 2 * W))
    abuf[A_HDR:A_HDR + TT, :] = a_glu
    off = A_HDR - (KA - 1)
    rc = A_ROWS

    def conv_pass(p):
        lt, half = p // 2, p % 2
        ls = slice(lt * LANES, (lt + 1) * LANES)
        for r0 in range(half * (TT // 2), (half + 1) * (TT // 2), rc):
            y = None
            for r in range(SUBLANES):
                z = None
                for q in range((KA - 1 + off) // SUBLANES + 1):
                    k = SUBLANES * q + r - off
                    if 0 <= k < KA:
                        rows = slice(r0 + SUBLANES * q, r0 + SUBLANES * q + rc + SUBLANES)
                        term = adw_ref[k:k + 1, ls] * abuf[rows, ls]
                        z = term if z is None else z + term
                part = z[r:r + rc, :]
                y = part if y is None else y + part
            aconv[r0:r0 + rc, ls] = y + adb_ref[:, ls]

    conv_pass(0)
    u = _gelu_tanh(zsec(2 * W, 3 * W))
    conv_pass(1)
    v = _layer_norm(_gelu_tanh(zsec(3 * W, 4 * W)), bng_ref[...], bnb_ref[...])
    conv_pass(2)

    cch = zsec(5 * W, 6 * W) * zsec(6 * W, 7 * W)
    cbuf[C_HDR:C_HDR + TT, :] = cch
    coff = C_HDR - (KC - 1)
    ych = (ccw_ref[0:1, :] * cbuf[coff:coff + TT, :]
           + ccw_ref[1:2, :] * cbuf[coff + 1:coff + 1 + TT, :]
           + ccw_ref[2:3, :] * cbuf[coff + 2:coff + 2 + TT, :])
    conv_pass(3)
    cc = zsec(4 * W, 5 * W) * ych
    hc_ref[0] = cbuf[C_HDR + TT - (KC - 1):C_HDR + TT, :]
    conv_pass(4)
    gate_z = [zsec(7 * W, 7 * W + D)]
    conv_pass(5)
    gate_z.append(zsec(7 * W + D, 7 * W + 2 * D))
    conv_pass(6)
    gate_z.append(zsec(7 * W + 2 * D, 7 * W + 3 * D))
    conv_pass(7)
    ha_ref[0] = abuf[A_HDR + TT - (KA - 1):A_HDR + TT, :]

    vb = v.astype(BF16)
    low_head = lax.broadcasted_iota(jnp.int32, (CHUNK, LANES), 1) < (W // HEADS)
    zero_b = jnp.zeros((CHUNK, LANES), BF16)
    s_parts = []
    for c in range(TT // CHUNK):
        tiles = []
        for lt in range(W // LANES):
            vt = vb[c * CHUNK:(c + 1) * CHUNK, lt * LANES:(lt + 1) * LANES]
            stacked = jnp.concatenate([jnp.where(low_head, vt, zero_b),
                                       jnp.where(low_head, zero_b, vt)], axis=0)
            tiles.append(_dot(wsp_ref[lt], stacked))
        s_parts.append(jnp.concatenate(tiles, axis=1) + bsp_ref[...])
    bb = u * jnp.concatenate(s_parts, axis=0)
    a = _group_norm_silu(aconv[...], gavg_ref, ang_ref[...], anb_ref[...])

    _merge_and_route(x, a, bb, cc, lambda gi: gate_z[gi], (g1, sh2, sc2), n2g_ref[...], wbr_ref,
                     wout_ref, wr_ref, br_ref, x1_ref, h2_ref, route_ref)


def _const_spec(shape):
    nd = len(shape)
    return pl.BlockSpec(shape, lambda *_: (0,) * nd, pipeline_mode=pl.Buffered(1))


def _mixer_prompt(x, mod, layer, n, mod_row0, lw):
    t = x.shape[0] // n
    nt = t // TT
    mod_rows = mod.shape[2]
    row_spec = lambda cols: pl.BlockSpec((TT, cols), lambda i, j: (i * nt + j, 0))
    consts = [lw["n1g"], lw["n2g"], lw["w_in"], lw["a_dw_w"], lw["a_dw_b"], lw["a_norm_g"],
              lw["a_norm_b"], lw["b_norm_g"], lw["b_norm_b"], lw["w_sp"], lw["b_sp"], lw["c_conv_w"],
              lw["w_branch"], lw["w_out"], lw["gavg"], lw["w_router"], lw["b_router"]]
    return pl.pallas_call(
        functools.partial(_mixer_prompt_kernel, mod_row0=mod_row0),
        grid=(n, nt),
        in_specs=[row_spec(D),
                  pl.BlockSpec((1, 6, mod_rows, D), lambda i, j: (layer, 0, 0, 0),
                               pipeline_mode=pl.Buffered(1))]
        + [_const_spec(c.shape) for c in consts],
        out_specs=[row_spec(D),
                   pl.BlockSpec((TT * ROW_TILE, LANES), lambda i, j: (i * nt + j, 0)),
                   row_spec(ROUTE_COLS),
                   pl.BlockSpec((1, KA - 1, W), lambda i, j: (i, 0, 0)),
                   pl.BlockSpec((1, KC - 1, W), lambda i, j: (i, 0, 0))],
        out_shape=[jax.ShapeDtypeStruct((n * t, D), F32),
                   jax.ShapeDtypeStruct((n * t * ROW_TILE, LANES), U32),
                   jax.ShapeDtypeStruct((n * t, ROUTE_COLS), F32),
                   jax.ShapeDtypeStruct((n, KA - 1, W), F32),
                   jax.ShapeDtypeStruct((n, KC - 1, W), F32)],
        scratch_shapes=[pltpu.VMEM((A_HDR + TT + A_PAD, W), F32),
                        pltpu.VMEM((C_HDR + TT, W), F32),
                        pltpu.VMEM((TT, W), F32)],
        compiler_params=pltpu.CompilerParams(
            dimension_semantics=("arbitrary", "arbitrary"), vmem_limit_bytes=VMEM_LIMIT),
        name="mixer_prompt",
    )(x, mod, *consts)


def _mixer_sample_kernel(x_ref, mod_ref, hista_ref, histc_ref, n1g_ref, n2g_ref, win_ref, adw_ref,
                         adb_ref, ang_ref, anb_ref, bng_ref, bnb_ref, wsp4_ref, bsp4_ref,
                         ccw_ref, wbr_ref, wout_ref, gavg_ref, wr_ref, br_ref,
                         x1_ref, h2_ref, route_ref, ha_ref, hc_ref, v_ref):
    ts, nb = v_ref.shape[0], v_ref.shape[1]
    x = x_ref[...]

    def modrow(i):
        return jnp.concatenate([mod_ref[0, i]] * ts, axis=0)

    h = (_rms(x, n1g_ref[...]) * (1.0 + modrow(1)) + modrow(0)).astype(BF16)

    def zsec(lo, hi):
        return _dot(h, win_ref[:, lo:hi])

    def tslab(arr, j):
        return arr[j * nb:(j + 1) * nb, :]

    a_glu = zsec(0, W) * _sigmoid(zsec(W, 2 * W))
    kh = KA - 1
    a_conv = []
    for tq in range(ts):
        acc = jnp.zeros((nb, W), F32) + adb_ref[...]
        for r in range(tq, kh):
            acc = acc + adw_ref[r - tq:r - tq + 1, :] * hista_ref[r]
        for j in range(tq + 1):
            acc = acc + adw_ref[kh + j - tq:kh + j - tq + 1, :] * tslab(a_glu, j)
        a_conv.append(acc)
    a = _group_norm_silu(jnp.concatenate(a_conv, axis=0), gavg_ref, ang_ref[...], anb_ref[...])
    for r in range(kh - ts):
        ha_ref[r] = hista_ref[r + ts]
    for j in range(ts):
        ha_ref[kh - ts + j] = tslab(a_glu, j)

    u = _gelu_tanh(zsec(2 * W, 3 * W))
    v = _layer_norm(_gelu_tanh(zsec(3 * W, 4 * W)), bng_ref[...], bnb_ref[...])
    for j in range(ts):
        v_ref[j] = tslab(v, j)
    s_rows = []
    for tq in range(ts):
        s = jnp.zeros((nb, W), F32) + bsp4_ref[tq:tq + 1, :]
        for sq in range(tq + 1):
            s = s + wsp4_ref[tq * ts + sq:tq * ts + sq + 1, :] * tslab(v, sq)
        s_rows.append(s)
    bb = u * jnp.concatenate(s_rows, axis=0)

    cch = zsec(5 * W, 6 * W) * zsec(6 * W, 7 * W)
    xp = [histc_ref[r] for r in range(KC - 1)] + [tslab(cch, j) for j in range(ts)]
    ych = jnp.concatenate(
        [sum(ccw_ref[k:k + 1, :] * xp[tq + k] for k in range(KC)) for tq in range(ts)], axis=0)
    cc = zsec(4 * W, 5 * W) * ych
    for r in range(KC - 1):
        hc_ref[r] = xp[ts + r]

    def gates_fn(gi):
        return zsec(7 * W + gi * D, 7 * W + (gi + 1) * D)

    _merge_and_route(x, a, bb, cc, gates_fn, (modrow(2), modrow(3), modrow(4)), n2g_ref[...],
                     wbr_ref, wout_ref, wr_ref, br_ref, x1_ref, h2_ref, route_ref)


def _mixer_sample(x_rows, mod, layer, hist_a_tm, hist_c_tm, lw, nsplit):
    n = hist_a_tm.shape[1]
    ts = x_rows.shape[0] // n
    nb = n // nsplit
    rows = ts * nb
    consts = [lw["n1g"], lw["n2g"], lw["w_in"], lw["a_dw_w"], lw["a_dw_b"],
              lw["a_norm_g"], lw["a_norm_b"], lw["b_norm_g"], lw["b_norm_b"], lw["w_sp4"], lw["b_sp4"],
              lw["c_conv_w"], lw["w_branch"], lw["w_out"], lw["gavg"], lw["w_router"], lw["b_router"]]
    seq3 = lambda k, cols: pl.BlockSpec((k, nb, cols), lambda i: (0, i, 0))
    row_spec = lambda cols: pl.BlockSpec((rows, cols), lambda i: (i, 0))
    return pl.pallas_call(
        _mixer_sample_kernel,
        grid=(nsplit,),
        in_specs=[row_spec(D), pl.BlockSpec((1, 6, nb, D), lambda i: (layer, 0, i, 0)),
                  seq3(KA - 1, W), seq3(KC - 1, W)]
        + [_const_spec(c.shape) for c in consts],
        out_specs=[row_spec(D),
                   pl.BlockSpec((rows * ROW_TILE, LANES), lambda i: (i, 0)),
                   row_spec(ROUTE_COLS),
                   seq3(KA - 1, W), seq3(KC - 1, W), seq3(ts, W)],
        out_shape=[jax.ShapeDtypeStruct((ts * n, D), F32),
                   jax.ShapeDtypeStruct((ts * n * ROW_TILE, LANES), U32),
                   jax.ShapeDtypeStruct((ts * n, ROUTE_COLS), F32),
                   jax.ShapeDtypeStruct((KA - 1, n, W), F32),
                   jax.ShapeDtypeStruct((KC - 1, n, W), F32),
                   jax.ShapeDtypeStruct((ts, n, W), F32)],
        compiler_params=pltpu.CompilerParams(
            dimension_semantics=("arbitrary",), vmem_limit_bytes=VMEM_LIMIT),
        name="mixer_sample",
    )(x_rows, mod, hist_a_tm, hist_c_tm, *consts)


def _plan_kernel(rp_ref, rs_ref, tri_ref, destp_ref, dests_ref, tbl_ref, carry, pstart, *, ntp):
    ph = pl.program_id(0)
    i = pl.program_id(1)
    nt = pl.num_programs(1)
    lane = lax.broadcasted_iota(jnp.int32, (TP, ROUTE_COLS), 1).astype(F32)

    @pl.when(jnp.logical_and(ph == 0, i == 0))
    def _():
        carry[...] = jnp.zeros_like(carry)

    def sub_tile(r, dest_ref, row0):
        hot_a = lane == r[:, 0:1]
        hot_b = lane == r[:, 1:2]
        s = jnp.where(jnp.logical_or(hot_a, hot_b), 1.0, 0.0)

        @pl.when(ph == 1)
        def _():
            pre = _dot(tri_ref[...], s.astype(BF16)) + carry[...] + pstart[...]
            d_a = jnp.sum(jnp.where(hot_a, pre, 0.0), axis=-1, keepdims=True)
            d_b = jnp.sum(jnp.where(hot_b, pre, 0.0), axis=-1, keepdims=True)
            l2 = lax.broadcasted_iota(jnp.int32, (TP, 2), 1)
            dest_ref[row0:row0 + TP, :] = jnp.where(l2 == 0, d_a, d_b).astype(jnp.int32)

        carry[...] = carry[...] + jnp.sum(s, axis=0, keepdims=True)

    @pl.when(i < ntp)
    def _():
        for sub in range(rp_ref.shape[0] // TP):
            sub_tile(rp_ref[sub * TP:(sub + 1) * TP, :], destp_ref, sub * TP)

    @pl.when(i == ntp)
    def _():
        for sub in range(rs_ref.shape[0] // TP):
            sub_tile(rs_ref[sub * TP:(sub + 1) * TP, :], dests_ref, sub * TP)

    @pl.when(jnp.logical_and(ph == 0, i == nt - 1))
    def _():
        cnt = carry[...]
        nblk = jnp.floor((cnt + (BM - 1)) * (1.0 / BM))
        ri = lax.broadcasted_iota(jnp.int32, (ROUTE_COLS, ROUTE_COLS), 0)
        ci = lax.broadcasted_iota(jnp.int32, (ROUTE_COLS, ROUTE_COLS), 1)
        upper = jnp.where(ri <= ci, 1.0, 0.0)
        cum = _dot_hi(jnp.broadcast_to(nblk, (8, ROUTE_COLS)), upper)[0:1, :]
        cum_ex = cum - nblk
        pstart[...] = cum_ex * BM
        row = lax.broadcasted_iota(jnp.int32, (SUBLANES, ROUTE_COLS), 0)
        tbl = jnp.where(row == 0, cum_ex, jnp.where(row == 1, cnt, 0.0))
        tbl_ref[...] = tbl.astype(jnp.int32)
        carry[...] = jnp.zeros_like(carry)


def _plan(route_p, route_s):
    rows_p, rows_s = route_p.shape[0], route_s.shape[0]
    ntp = rows_p // TPP
    nbp = SUBLANES
    ri = lax.broadcasted_iota(jnp.int32, (TP, TP), 0)
    ci = lax.broadcasted_iota(jnp.int32, (TP, TP), 1)
    tri = jnp.where(ci < ri, 1.0, 0.0).astype(BF16)
    return pl.pallas_call(
        functools.partial(_plan_kernel, ntp=ntp),
        grid=(2, ntp + 1),
        in_specs=[pl.BlockSpec((TPP, ROUTE_COLS), lambda p, i: (jnp.minimum(i, ntp - 1), 0)),
                  pl.BlockSpec((rows_s, ROUTE_COLS), lambda p, i: (0, 0)),
                  pl.BlockSpec((TP, TP), lambda p, i: (0, 0))],
        out_specs=[pl.BlockSpec((TPP, 2), lambda p, i: (jnp.minimum(i, ntp - 1) * p, 0)),
                   pl.BlockSpec((rows_s, 2), lambda p, i: (0, 0)),
                   pl.BlockSpec((nbp, ROUTE_COLS), lambda p, i: (0, 0))],
        out_shape=[jax.ShapeDtypeStruct((rows_p, 2), jnp.int32),
                   jax.ShapeDtypeStruct((rows_s, 2), jnp.int32),
                   jax.ShapeDtypeStruct((nbp, ROUTE_COLS), jnp.int32)],
        scratch_shapes=[pltpu.VMEM((1, ROUTE_COLS), F32), pltpu.VMEM((1, ROUTE_COLS), F32)],
        compiler_params=pltpu.CompilerParams(
            dimension_semantics=("arbitrary", "arbitrary"), vmem_limit_bytes=VMEM_LIMIT),
        name="moe_plan",
    )(route_p, route_s, tri)


def _scatter_kernel(dest_ref, h_ref, *rest):
    rows_ref, sem = rest[-2:]
    tp = h_ref.shape[0] // ROW_TILE

    def row_copy(r, k):
        d = dest_ref[2 * r + k]
        return pltpu.make_async_copy(_row_tile(h_ref, r), _row_tile(rows_ref, d), sem)

    def issue(r, c):
        row_copy(r, 0).start(priority=0)
        row_copy(r, 1).start(priority=1)
        return c

    lax.fori_loop(0, tp, issue, 0, unroll=8)

    def drain(r, c):
        row_copy(r, 0).wait()
        row_copy(r, 1).wait()
        return c

    lax.fori_loop(0, tp, drain, 0, unroll=8)


def _scatter_rows(dest_flat, h2, rows_buf, n_rows, tp):
    nt = h2.shape[0] // (tp * ROW_TILE)
    in_specs = [pl.BlockSpec((2 * tp,), lambda i: (i,), memory_space=pltpu.SMEM),
                pl.BlockSpec((tp * ROW_TILE, LANES), lambda i: (i, 0))]
    args = [dest_flat, h2]
    aliases = {}
    if rows_buf is not None:
        in_specs.append(pl.BlockSpec(memory_space=pl.ANY))
        args.append(rows_buf)
        aliases = {2: 0}
    return pl.pallas_call(
        _scatter_kernel,
        grid=(nt,),
        in_specs=in_specs,
        out_specs=pl.BlockSpec(memory_space=pl.ANY),
        out_shape=jax.ShapeDtypeStruct((n_rows * ROW_TILE, LANES), U32),
        scratch_shapes=[pltpu.SemaphoreType.DMA(())],
        input_output_aliases=aliases,
        compiler_params=pltpu.CompilerParams(
            dimension_semantics=("arbitrary",), vmem_limit_bytes=VMEM_LIMIT,
            has_side_effects=True),
        name="moe_scatter",
    )(*args)


ROW_DMA_PRIORITY = 1
READ_AHEAD = 4
ROW_SLOTS = 6


def _expert_kernel(blk0_ref, cnt_ref, x_hbm, w1_ref, w3_ref, w2_ref, y_hbm,
                   xbuf, ybuf, w1b, w3b, w2b, sem_in, sem_out):
    e = pl.program_id(0)
    cnt = cnt_ref[e]
    nblk = lax.shift_right_logical(cnt + (BM - 1), BM.bit_length() - 1)
    blk0 = blk0_ref[e]
    last_e = NEXP - 1
    n_used = blk0_ref[last_e] + lax.shift_right_logical(cnt_ref[last_e] + (BM - 1),
                                                        BM.bit_length() - 1)
    blk_rows = BM * ROW_TILE

    def block_rows(ref, g):
        return ref.at[pl.ds(pl.multiple_of(g * blk_rows, blk_rows), blk_rows)]

    def in_copy(g):
        slot = lax.rem(g, ROW_SLOTS)
        return pltpu.make_async_copy(block_rows(x_hbm, g), xbuf.at[slot], sem_in.at[slot])

    def out_copy(g):
        slot = lax.rem(g, ROW_SLOTS)
        return pltpu.make_async_copy(ybuf.at[slot], block_rows(y_hbm, g), sem_out.at[slot])

    @pl.when(nblk > 0)
    def _():
        @pl.when(blk0 == 0)
        def _():
            for g0 in range(READ_AHEAD):
                @pl.when(g0 < n_used)
                def _():
                    in_copy(g0).start(priority=ROW_DMA_PRIORITY)

        w1b[...] = w1_ref[0, 0].astype(BF16)
        w3b[...] = w3_ref[0, 0].astype(BF16)
        w2b[...] = w2_ref[0, 0].astype(BF16)

        def chain(j, nb):
            g = blk0 + j
            for q in range(nb):
                @pl.when(g + READ_AHEAD + q < n_used)
                def _():
                    in_copy(g + READ_AHEAD + q).start(priority=ROW_DMA_PRIORITY)
            for q in range(nb):
                in_copy(g + q).wait()
            for q in range(nb):
                @pl.when(g + q >= ROW_SLOTS)
                def _():
                    out_copy(g + q - ROW_SLOTS).wait()

            row = lax.broadcasted_iota(jnp.int32, (BM, D), 0)
            xs = [jnp.where(row < cnt - (j + q) * BM,
                            _load_row_tiles(xbuf.at[lax.rem(g + q, ROW_SLOTS)]), 0.0).astype(BF16)
                  for q in range(nb)]
            x = xs[0] if nb == 1 else jnp.concatenate(xs, axis=0)
            h1 = _dot(x, w1b[...])
            h3 = _dot(x, w3b[...])
            act = (_silu(h1) * h3).astype(BF16)
            y = _dot(act, w2b[...])
            for q in range(nb):
                _store_row_tiles(ybuf.at[lax.rem(g + q, ROW_SLOTS)], y[q * BM:(q + 1) * BM, :])
                out_copy(g + q).start(priority=ROW_DMA_PRIORITY)

        def pair(p, carry):
            chain(2 * p, 2)
            return carry

        lax.fori_loop(0, lax.shift_right_logical(nblk, 1), pair, 0)

        @pl.when(lax.rem(nblk, 2) == 1)
        def _():
            chain(nblk - 1, 1)

    @pl.when(e == last_e)
    def _():
        for back in range(ROW_SLOTS, 0, -1):
            @pl.when(n_used >= back)
            def _():
                out_copy(n_used - back).wait()


def _experts(tbl, rows_buf, w1, w3, w2, layer):
    blk0, cnt = tbl[0, :NEXP], tbl[1, :NEXP]
    wspec = lambda shape: pl.BlockSpec((1, 1) + shape, lambda e, blk0, cnt: (layer, e, 0, 0))
    return pl.pallas_call(
        _expert_kernel,
        grid_spec=pltpu.PrefetchScalarGridSpec(
            num_scalar_prefetch=2,
            grid=(NEXP,),
            in_specs=[pl.BlockSpec(memory_space=pl.ANY),
                      wspec((D, FF)), wspec((D, FF)), wspec((FF, D))],
            out_specs=pl.BlockSpec(memory_space=pl.ANY),
            scratch_shapes=[pltpu.VMEM((ROW_SLOTS, BM * ROW_TILE, LANES), U32),
                            pltpu.VMEM((ROW_SLOTS, BM * ROW_TILE, LANES), U32),
                            pltpu.VMEM((D, FF), BF16), pltpu.VMEM((D, FF), BF16),
                            pltpu.VMEM((FF, D), BF16),
                            pltpu.SemaphoreType.DMA((ROW_SLOTS,)),
                            pltpu.SemaphoreType.DMA((ROW_SLOTS,))],
        ),
        out_shape=jax.ShapeDtypeStruct(rows_buf.shape, U32),
        compiler_params=pltpu.CompilerParams(
            dimension_semantics=("arbitrary",), vmem_limit_bytes=VMEM_LIMIT,
            has_side_effects=True),
        name="moe_experts",
    )(blk0, cnt, rows_buf, w1, w3, w2)


def _combine_kernel(dest_ref, dest_next_ref, x1_ref, route_ref, g2_ref, fng_ref, y_ref, o_ref,
                    ybuf, sem, *, final_norm, g2_row0, tiles_per_g2):
    tp = x1_ref.shape[0]
    i = pl.program_id(0)
    n = pl.num_programs(0)
    slot = lax.rem(i, 2)

    def gather(d_ref, s):
        def row_copy(r, k):
            return pltpu.make_async_copy(_row_tile(y_ref, d_ref[2 * r + k]),
                                         _row_tile(ybuf.at[s, k], r), sem.at[s])

        def issue(r, c):
            row_copy(r, 0).start(priority=0)
            row_copy(r, 1).start(priority=1)
            return c

        lax.fori_loop(0, tp, issue, 0, unroll=8)

    @pl.when(i == 0)
    def _():
        gather(dest_ref, 0)

    @pl.when(i + 1 < n)
    def _():
        gather(dest_next_ref, 1 - slot)

    for k in range(2):
        pltpu.make_async_copy(y_ref.at[pl.ds(0, tp * ROW_TILE)], ybuf.at[slot, k], sem.at[slot]).wait()

    route = route_ref[...]
    w_a = route[:, 2:3]
    w_b = route[:, 3:4]
    if tiles_per_g2:
        g2 = g2_ref[0, 0, pl.ds(g2_row0 + i // tiles_per_g2, 1), :]
    else:
        g2 = jnp.concatenate([g2_ref[0, 0]] * (tp // g2_ref.shape[2]), axis=0)
    y = (w_a * _load_row_tiles(ybuf.at[slot, 0]) + w_b * _load_row_tiles(ybuf.at[slot, 1]))
    x2 = x1_ref[...] + g2 * y
    if final_norm:
        x2 = _rms(x2, fng_ref[...])
    o_ref[...] = x2


def _combine(dest_flat, x1, route, mod, layer, fng, y_rows, tp, *, final_norm, g2_row0=0,
             tiles_per_g2=0, seqs_per_tile=0):
    nt = x1.shape[0] // tp
    if tiles_per_g2:
        g2_spec = pl.BlockSpec((1, 1, mod.shape[2], D), lambda i: (layer, 5, 0, 0))
    else:
        g2_spec = pl.BlockSpec((1, 1, seqs_per_tile, D), lambda i: (layer, 5, i, 0))
    return pl.pallas_call(
        functools.partial(_combine_kernel, final_norm=final_norm, g2_row0=g2_row0,
                          tiles_per_g2=tiles_per_g2),
        grid=(nt,),
        in_specs=[pl.BlockSpec((2 * tp,), lambda i: (i,), memory_space=pltpu.SMEM),
                  pl.BlockSpec((2 * tp,), lambda i: (jnp.minimum(i + 1, nt - 1),),
                               memory_space=pltpu.SMEM),
                  pl.BlockSpec((tp, D), lambda i: (i, 0)),
                  pl.BlockSpec((tp, ROUTE_COLS), lambda i: (i, 0)),
                  g2_spec,
                  pl.BlockSpec((1, D), lambda i: (0, 0)),
                  pl.BlockSpec(memory_space=pl.ANY)],
        out_specs=pl.BlockSpec((tp, D), lambda i: (i, 0)),
        out_shape=jax.ShapeDtypeStruct(x1.shape, F32),
        scratch_shapes=[pltpu.VMEM((2, 2, tp * ROW_TILE, LANES), U32),
                        pltpu.SemaphoreType.DMA((2,))],
        compiler_params=pltpu.CompilerParams(
            dimension_semantics=("arbitrary",), vmem_limit_bytes=VMEM_LIMIT),
        name="moe_combine",
    )(dest_flat, dest_flat, x1, route, mod, fng, y_rows)


def _layer_weights(l, p):
    hd = W // HEADS
    gi = jnp.arange(W) // (W // NG_A)
    gavg = jnp.where(gi[:, None] == gi[None, :], 1.0 / (W // NG_A), 0.0).astype(BF16)
    tril = jnp.tril(jnp.ones((CHUNK, CHUNK), dtype=bool))
    w_sp = jnp.where(tril[None], p["b_spatial_w"][l], 0.0)
    w_router = jnp.concatenate(
        [p["w_router_g"][l], p["w_router_e"][l],
         jnp.zeros((D, ROUTE_COLS - NGRP - NEXP), F32)], axis=1)
    b_router = jnp.concatenate(
        [p["b_router_g"][l], p["b_router_e"][l],
         jnp.zeros((ROUTE_COLS - NGRP - NEXP,), F32)])[None, :]
    return {
        "n1g": p["norm1_g"][l][None, :], "n2g": p["norm2_g"][l][None, :],
        "w_in": p["w_in"][l].astype(BF16),
        "a_dw_w": p["a_dw_w"][l], "a_dw_b": p["a_dw_b"][l][None, :],
        "a_norm_g": p["a_norm_g"][l][None, :], "a_norm_b": p["a_norm_b"][l][None, :],
        "b_norm_g": p["b_norm_g"][l][None, :], "b_norm_b": p["b_norm_b"][l][None, :],
        "w_sp": jnp.concatenate([w_sp[0::2], w_sp[1::2]], axis=2).astype(BF16),
        "b_sp": jnp.repeat(p["b_spatial_b"][l].T, hd, axis=1),
        "c_conv_w": p["c_conv_w"][l],
        "w_branch": p["w_branch"][l].astype(BF16), "w_out": p["w_out"][l].astype(BF16),
        "gavg": gavg, "w_router": w_router.astype(BF16), "b_router": b_router,
    }


def _sample_spatial(l, p, ts):
    hd = W // HEADS
    w = p["b_spatial_w"][l][:, :ts, :ts]
    w = jnp.where(jnp.tril(jnp.ones((ts, ts), dtype=bool))[None], w, 0.0)
    w4 = jnp.repeat(jnp.transpose(w, (1, 2, 0)).reshape(ts * ts, HEADS), hd, axis=1)
    b4 = jnp.repeat(p["b_spatial_b"][l][:, :ts].T, hd, axis=1)
    return w4, b4


def kernel(x_prompt, x_sample, c_prompt, c_sample, state_conv_a, state_conv_c, norm1_g, norm2_g, w_ada, b_ada, w_in, a_dw_w, a_dw_b, a_norm_g, a_norm_b, b_norm_g, b_norm_b, b_spatial_w, b_spatial_b, c_conv_w, w_branch, w_out, w_router_g, b_router_g, w_router_e, b_router_e, w1, w3, w2, final_norm_g):
    p = dict(norm1_g=norm1_g, norm2_g=norm2_g, w_in=w_in, a_dw_w=a_dw_w, a_dw_b=a_dw_b,
             a_norm_g=a_norm_g, a_norm_b=a_norm_b, b_norm_g=b_norm_g, b_norm_b=b_norm_b,
             b_spatial_w=b_spatial_w, b_spatial_b=b_spatial_b, c_conv_w=c_conv_w,
             w_branch=w_branch, w_out=w_out, w_router_g=w_router_g, b_router_g=b_router_g,
             w_router_e=w_router_e, b_router_e=b_router_e)
    depth = w_in.shape[0]
    nb_, seq, _ = x_prompt.shape
    ns, ts, _ = x_sample.shape
    n_tok = nb_ * seq + ns * ts
    nblocks = -(-(2 * n_tok) // BM) + NEXP
    nsplit = 2
    nbs = ns // nsplit
    tps = ts * nbs
    assert seq % TT == 0 and TT % CHUNK == 0 and seq % TP == 0 and (nb_ * seq) % TPP == 0
    assert (ts * ns) % TP == 0 and ns % SUBLANES == 0

    mod = _adaln(jnp.concatenate([c_sample, c_prompt], axis=0), w_ada, b_ada)

    xp = x_prompt.reshape(nb_ * seq, D)
    xs = jnp.transpose(x_sample.reshape(nsplit, nbs, ts, D), (0, 2, 1, 3)).reshape(ts * ns, D)
    fng = final_norm_g[None, :]
    pa, pc, sa, sc, sv = [], [], [], [], []
    for l in range(depth):
        lw = _layer_weights(l, p)
        lw["w_sp4"], lw["b_sp4"] = _sample_spatial(l, p, ts)

        x1p, h2p, route_p, ha, hc = _mixer_prompt(xp, mod, l, nb_, ns, lw)
        pa.append(ha)
        pc.append(hc)
        x1s, h2s, route_s, ha_s, hc_s, v_s = _mixer_sample(
            xs, mod, l, jnp.transpose(state_conv_a[l], (1, 0, 2)),
            jnp.transpose(state_conv_c[l], (1, 0, 2)), lw, nsplit)
        sa.append(jnp.transpose(ha_s, (1, 0, 2)))
        sc.append(jnp.transpose(hc_s, (1, 0, 2)))
        sv.append(jnp.transpose(v_s, (1, 0, 2)))

        dest_p, dest_s, tbl = _plan(route_p, route_s)
        dest_p, dest_s = dest_p.reshape(-1), dest_s.reshape(-1)
        rows_buf = _scatter_rows(dest_p, h2p, None, nblocks * BM, TP)
        rows_buf = _scatter_rows(dest_s, h2s, rows_buf, nblocks * BM, tps)
        y_rows = _experts(tbl, rows_buf, w1, w3, w2, l)

        last = l == depth - 1
        xp = _combine(dest_p, x1p, route_p, mod, l, fng, y_rows, TP, final_norm=last,
                      g2_row0=ns, tiles_per_g2=seq // TP)
        xs = _combine(dest_s, x1s, route_s, mod, l, fng, y_rows, tps, final_norm=last,
                      seqs_per_tile=nbs)

    y_prompt = xp.reshape(nb_, seq, D)
    y_sample = jnp.transpose(xs.reshape(nsplit, ts, nbs, D), (0, 2, 1, 3)).reshape(ns, ts, D)
    return (y_prompt, y_sample, jnp.stack(pa), jnp.stack(pc), jnp.stack(sa), jnp.stack(sc),
            jnp.stack(sv))
```
